```python
import math
import jax, jax.numpy as jnp
from jax import lax
import numpy as np

D_MODEL = 2048
BATCH = 4
SEQ = 2048
DEPTH = 1
DEC_BATCH = 128
DEC_SEQ = 1
PAST_LEN = 16384
PAGE_SIZE = 128

D_MIX = D_MODEL
D_A = D_MIX // 2
D_B = D_MIX - D_A
H_A = 4
DK = D_A // H_A
DV = D_A // H_A
G_B = 8
CONV_W = 3
D_FF = 5504
CHUNK = 64
GATE_CAP = 15.0
EPS = 1e-6
D_IN_TOT = 4 * D_A + 2 * H_A + 3 * D_B

kernel_name = "hymba_mlstm_shortconv_macaron_step"


def rmsnorm(x, g):
    xf = x.astype(jnp.float32)
    return xf * lax.rsqrt(jnp.mean(xf * xf, axis=-1, keepdims=True) + EPS) * g.astype(jnp.float32)


def swiglu(x, wg, wu, wd):
    return (jax.nn.silu(x @ wg) * (x @ wu)) @ wd


def soft_cap(x):
    return GATE_CAP * jnp.tanh(x / GATE_CAP)


def mlstm_chunkwise(q, k, v, logi, logf, C0, n0, m0):
    Bsz, H, S, _ = q.shape
    L = math.gcd(S, CHUNK)
    NC = S // L

    def to_chunks(a):
        a = a.reshape((Bsz, H, NC, L) + a.shape[3:])
        return jnp.moveaxis(a, 2, 0)

    xs = (to_chunks(q), to_chunks(k), to_chunks(v), to_chunks(logi), to_chunks(logf))
    causal = jnp.tril(jnp.ones((L, L), dtype=bool))

    def step(carry, xc):
        C, n, m = carry
        qc, kc, vc, ic, fc = xc
        b = jnp.cumsum(fc, axis=-1)
        dmat = b[..., :, None] - b[..., None, :] + ic[..., None, :]
        dmat = jnp.where(causal, dmat, -jnp.inf)
        m_inter = b + m[..., None]
        m_t = jnp.maximum(m_inter, jnp.max(dmat, axis=-1))
        s = jnp.einsum('bhtd,bhsd->bhts', qc, kc) * jnp.exp(dmat - m_t[..., None])
        scale_inter = jnp.exp(m_inter - m_t)
        num = scale_inter[..., None] * jnp.einsum('bhtd,bhde->bhte', qc, C) + jnp.einsum('bhts,bhse->bhte', s, vc)
        den = scale_inter * jnp.einsum('bhtd,bhd->bht', qc, n) + jnp.sum(s, axis=-1)
        h = num / jnp.maximum(jnp.abs(den), jnp.exp(-m_t))[..., None]
        b_last = b[..., -1]
        g = b_last[..., None] - b + ic
        m_new = jnp.maximum(b_last + m, jnp.max(g, axis=-1))
        decay = jnp.exp(b_last + m - m_new)
        wk = jnp.exp(g - m_new[..., None])[..., None] * kc
        C_new = decay[..., None, None] * C + jnp.einsum('bhsd,bhse->bhde', wk, vc)
        n_new = decay[..., None] * n + jnp.sum(wk, axis=-2)
        return (C_new, n_new, m_new), h

    (C_f, n_f, m_f), hs = lax.scan(step, (C0, n0, m0), xs)
    h = jnp.moveaxis(hs, 0, 2).reshape(Bsz, H, S, DV)
    return h, C_f, n_f, m_f


def mixing_block(h, w_in, b_gates, conv_w, conv_b, norm_mlstm, norm_conv, w_out, C0, n0, m0, buf0):
    Bsz, S, _ = h.shape
    z = h @ w_in.astype(jnp.float32)
    offs = np.cumsum([D_A, D_A, D_A, D_A, H_A, H_A, D_B, D_B])
    q, k, v, o, ig, fg, gb, gc, xc = jnp.split(z, offs, axis=-1)

    def heads(a):
        return a.reshape(Bsz, S, H_A, -1).transpose(0, 2, 1, 3)

    qh, kh, vh = heads(q), heads(k) * (DK ** -0.5), heads(v)
    bg = b_gates.astype(jnp.float32)
    logi = soft_cap(ig + bg[:H_A]).transpose(0, 2, 1)
    logf = jax.nn.log_sigmoid(soft_cap(fg + bg[H_A:])).transpose(0, 2, 1)
    ha, C_f, n_f, m_f = mlstm_chunkwise(qh, kh, vh, logi, logf,
                                        C0.astype(jnp.float32), n0.astype(jnp.float32), m0.astype(jnp.float32))
    ha = ha * lax.rsqrt(jnp.mean(ha * ha, axis=-1, keepdims=True) + EPS)
    ha = ha.transpose(0, 2, 1, 3).reshape(Bsz, S, D_A) * norm_mlstm.astype(jnp.float32) * jax.nn.sigmoid(o)

    u = gc * xc
    u_ext = jnp.concatenate([buf0.astype(jnp.float32), u], axis=1)
    cw = conv_w.astype(jnp.float32)
    yc = sum(cw[j] * u_ext[:, j:j + S] for j in range(CONV_W)) + conv_b.astype(jnp.float32)
    yb = gb * yc
    yb = yb.reshape(Bsz, S, G_B, D_B // G_B)
    yb = (yb * lax.rsqrt(jnp.mean(yb * yb, axis=-1, keepdims=True) + EPS)).reshape(Bsz, S, D_B)
    yb = yb * norm_conv.astype(jnp.float32)
    new_buf = u_ext[:, -(CONV_W - 1):]

    out = jnp.concatenate([ha, yb], axis=-1) @ w_out.astype(jnp.float32)
    return out, C_f, n_f, m_f, new_buf


def run_trunk(x, C_in, n_in, m_in, conv_in,
              norm_ffn1, ffn1_gate, ffn1_up, ffn1_down, norm_mix, w_in, b_gates, conv_w, conv_b,
              norm_mlstm, norm_conv, w_out, norm_ffn2, ffn2_gate, ffn2_up, ffn2_down, norm_final):
    h = x.astype(jnp.float32)
    Cs, ns, ms, bufs = [], [], [], []
    for l in range(DEPTH):
        h = h + 0.5 * swiglu(rmsnorm(h, norm_ffn1[l]), ffn1_gate[l].astype(jnp.float32),
                             ffn1_up[l].astype(jnp.float32), ffn1_down[l].astype(jnp.float32))
        mix, C_f, n_f, m_f, buf = mixing_block(rmsnorm(h, norm_mix[l]), w_in[l], b_gates[l], conv_w[l], conv_b[l],
                                               norm_mlstm[l], norm_conv[l], w_out[l],
                                               C_in[l], n_in[l], m_in[l], conv_in[l])
        h = h + mix
        h = h + 0.5 * swiglu(rmsnorm(h, norm_ffn2[l]), ffn2_gate[l].astype(jnp.float32),
                             ffn2_up[l].astype(jnp.float32), ffn2_down[l].astype(jnp.float32))
        Cs.append(C_f); ns.append(n_f); ms.append(m_f); bufs.append(buf)
    y = rmsnorm(h, norm_final).astype(x.dtype)
    return y, jnp.stack(Cs), jnp.stack(ns), jnp.stack(ms), jnp.stack(bufs)


def setup_inputs(seed: int = 0) -> dict:
    key = jax.random.key(seed)
    ks = iter(jax.random.split(key, 32))
    f32 = jnp.float32

    def nrm(shape, scale):
        return jax.random.normal(next(ks), shape, f32) * scale

    def gain(shape):
        return 1.0 + nrm(shape, 0.05)

    b_in = jax.random.normal(next(ks), (DEPTH, H_A), f32) * 0.1
    b_fg = 3.0 + jax.random.normal(next(ks), (DEPTH, H_A), f32) * 0.1
    return {
        "x_prompt": nrm((BATCH, SEQ, D_MODEL), 1.0),
        "x_sample": nrm((DEC_BATCH, DEC_SEQ, D_MODEL), 1.0),
        "state_mlstm_C": nrm((DEPTH, DEC_BATCH, H_A, DK, DV), 0.1),
        "state_mlstm_n": nrm((DEPTH, DEC_BATCH, H_A, DK), 0.1),
        "state_mlstm_m": nrm((DEPTH, DEC_BATCH, H_A), 1.0),
        "state_conv": nrm((DEPTH, DEC_BATCH, CONV_W - 1, D_B), 1.0),
        "norm_ffn1": gain((DEPTH, D_MODEL)),
        "ffn1_gate": nrm((DEPTH, D_MODEL, D_FF), D_MODEL ** -0.5),
        "ffn1_up": nrm((DEPTH, D_MODEL, D_FF), D_MODEL ** -0.5),
        "ffn1_down": nrm((DEPTH, D_FF, D_MODEL), D_FF ** -0.5),
        "norm_mix": gain((DEPTH, D_MODEL)),
        "w_in": nrm((DEPTH, D_MODEL, D_IN_TOT), D_MODEL ** -0.5),
        "b_gates": jnp.concatenate([b_in, b_fg], axis=-1),
        "conv_w": nrm((DEPTH, CONV_W, D_B), CONV_W ** -0.5),
        "conv_b": nrm((DEPTH, D_B), 0.02),
        "norm_mlstm": gain((DEPTH, D_A)),
        "norm_conv": gain((DEPTH, D_B)),
        "w_out": nrm((DEPTH, D_MIX, D_MODEL), D_MIX ** -0.5),
        "norm_ffn2": gain((DEPTH, D_MODEL)),
        "ffn2_gate": nrm((DEPTH, D_MODEL, D_FF), D_MODEL ** -0.5),
        "ffn2_up": nrm((DEPTH, D_MODEL, D_FF), D_MODEL ** -0.5),
        "ffn2_down": nrm((DEPTH, D_FF, D_MODEL), D_FF ** -0.5),
        "norm_final": gain((D_MODEL,)),
    }


def reference(x_prompt, x_sample, state_mlstm_C, state_mlstm_n, state_mlstm_m, state_conv,
              norm_ffn1, ffn1_gate, ffn1_up, ffn1_down, norm_mix, w_in, b_gates, conv_w, conv_b,
              norm_mlstm, norm_conv, w_out, norm_ffn2, ffn2_gate, ffn2_up, ffn2_down, norm_final):
    weights = (norm_ffn1, ffn1_gate, ffn1_up, ffn1_down, norm_mix, w_in, b_gates, conv_w, conv_b,
               norm_mlstm, norm_conv, w_out, norm_ffn2, ffn2_gate, ffn2_up, ffn2_down, norm_final)
    Bp = x_prompt.shape[0]
    C0 = jnp.zeros((DEPTH, Bp, H_A, DK, DV), jnp.float32)
    n0 = jnp.zeros((DEPTH, Bp, H_A, DK), jnp.float32)
    m0 = jnp.zeros((DEPTH, Bp, H_A), jnp.float32)
    buf0 = jnp.zeros((DEPTH, Bp, CONV_W - 1, D_B), jnp.float32)
    y_prompt, C_p, n_p, m_p, conv_p = run_trunk(x_prompt, C0, n0, m0, buf0, *weights)
    y_sample, C_s, n_s, m_s, conv_s = run_trunk(x_sample, state_mlstm_C, state_mlstm_n, state_mlstm_m,
                                                state_conv, *weights)
    return (y_prompt, y_sample, C_p, n_p, m_p, conv_p, C_s, n_s, m_s, conv_s)
```

```python
import functools

import jax
import jax.numpy as jnp
from jax import lax
from jax.experimental import pallas as pl
from jax.experimental.pallas import tpu as pltpu

F32 = jnp.float32
BF16 = jnp.bfloat16

D_MODEL = 2048
D_A = 1024
D_B = 1024
H_A = 4
DK = 256
DV = 256
G_B = 8
D_FF = 5504
GATE_CAP = 15.0
EPS = 1e-6

N_TOK_BLOCKS = 8
TM_P = 1024
TM_S = 16
TM = TM_P + TM_S
TF = 256
LANE = 128
MLSTM_CHUNK = 256
SAMPLE_BS = 8
CONV_TS = 512
VMEM_LIMIT = 60 * 1024 * 1024


def _rms(x, g):
    return x * lax.rsqrt(jnp.mean(x * x, axis=-1, keepdims=True) + EPS) * g


def _soft_cap(x):
    return GATE_CAP * jnp.tanh(x / GATE_CAP)


def _log_sigmoid(x):
    return -jax.nn.softplus(-x)


def _bdot(a, b):
    return jnp.dot(a, b, preferred_element_type=F32)


def _ffn_kernel(xp_ref, xs_ref, g_ref, wg_ref, wu_ref, wd_ref, gfin_ref, op_ref, os_ref, xn_ref,
                *, n_f, last_valid, final_norm):
    f = pl.program_id(1)

    @pl.when(f == 0)
    def _():
        xp = xp_ref[...]
        xs = xs_ref[0]
        g = g_ref[...]
        xn_ref[:TM_P, :] = _rms(xp, g).astype(BF16)
        xn_ref[TM_P:, :] = _rms(xs, g).astype(BF16)
        op_ref[...] = xp
        os_ref[0] = xs

    def step(valid):
        xn = xn_ref[...]
        a = _bdot(xn, wg_ref[:, :valid].astype(BF16))
        b = _bdot(xn, wu_ref[:, :valid].astype(BF16))
        hid = (a * jax.nn.sigmoid(a) * b * 0.5).astype(BF16)
        r = _bdot(hid, wd_ref[:valid, :].astype(BF16))
        op_ref[...] += r[:TM_P]
        os_ref[0] += r[TM_P:]

    if last_valid == TF:
        step(TF)
    else:
        @pl.when(f < n_f - 1)
        def _():
            step(TF)

        @pl.when(f == n_f - 1)
        def _():
            step(last_valid)

    if final_norm:
        @pl.when(f == n_f - 1)
        def _():
            gf = gfin_ref[...]
            op_ref[...] = _rms(op_ref[...], gf)
            os_ref[0] = _rms(os_ref[0], gf)


def _ffn(xp, xs, g, wg, wu, wd, gfin, final_norm):
    n_f = pl.cdiv(D_FF, TF)
    last_valid = D_FF - (n_f - 1) * TF
    kern = functools.partial(_ffn_kernel, n_f=n_f, last_valid=last_valid, final_norm=final_norm)
    return pl.pallas_call(
        kern,
        grid=(N_TOK_BLOCKS, n_f),
        in_specs=[
            pl.BlockSpec((TM_P, D_MODEL), lambda i, f: (i, 0)),
            pl.BlockSpec((1, TM_S, D_MODEL), lambda i, f: (i, 0, 0)),
            pl.BlockSpec((1, D_MODEL), lambda i, f: (0, 0)),
            pl.BlockSpec((None, D_MODEL, TF), lambda i, f: (0, 0, f)),
            pl.BlockSpec((None, D_MODEL, TF), lambda i, f: (0, 0, f)),
            pl.BlockSpec((None, TF, D_MODEL), lambda i, f: (0, f, 0)),
            pl.BlockSpec((1, D_MODEL), lambda i, f: (0, 0)),
        ],
        out_specs=[
            pl.BlockSpec((TM_P, D_MODEL), lambda i, f: (i, 0)),
            pl.BlockSpec((1, TM_S, D_MODEL), lambda i, f: (i, 0, 0)),
        ],
        out_shape=[
            jax.ShapeDtypeStruct((N_TOK_BLOCKS * TM_P, D_MODEL), F32),
            jax.ShapeDtypeStruct((N_TOK_BLOCKS, TM_S, D_MODEL), F32),
        ],
        scratch_shapes=[pltpu.VMEM((TM, D_MODEL), BF16)],
        compiler_params=pltpu.CompilerParams(
            dimension_semantics=("parallel", "arbitrary"), vmem_limit_bytes=VMEM_LIMIT),
        name="ffn_final" if final_norm else "ffn",
    )(xp, xs, g, wg, wu, wd, gfin)


def _proj_kernel(hp_ref, hs_ref, g_ref, w_ref, zp_ref, zs_ref, xn_ref):
    @pl.when(pl.program_id(1) == 0)
    def _():
        g = g_ref[...]
        xn_ref[:TM_P, :] = _rms(hp_ref[...], g).astype(BF16)
        xn_ref[TM_P:, :] = _rms(hs_ref[0], g).astype(BF16)

    z = _bdot(xn_ref[...], w_ref[...].astype(BF16))
    zp_ref[...] = z[:TM_P]
    zs_ref[0] = z[TM_P:]


def _proj(hp, hs, g, w3, col_block0, n_cols, tn, name):
    n_n = n_cols // tn
    return pl.pallas_call(
        _proj_kernel,
        grid=(N_TOK_BLOCKS, n_n),
        in_specs=[
            pl.BlockSpec((TM_P, D_MODEL), lambda i, j: (i, 0)),
            pl.BlockSpec((1, TM_S, D_MODEL), lambda i, j: (i, 0, 0)),
            pl.BlockSpec((1, D_MODEL), lambda i, j: (0, 0)),
            pl.BlockSpec((None, D_MODEL, tn), lambda i, j: (0, 0, col_block0 + j)),
        ],
        out_specs=[
            pl.BlockSpec((TM_P, tn), lambda i, j: (i, j)),
            pl.BlockSpec((1, TM_S, tn), lambda i, j: (i, 0, j)),
        ],
        out_shape=[
            jax.ShapeDtypeStruct((N_TOK_BLOCKS * TM_P, n_cols), F32),
            jax.ShapeDtypeStruct((N_TOK_BLOCKS, TM_S, n_cols), F32),
        ],
        scratch_shapes=[pltpu.VMEM((TM, D_MODEL), BF16)],
        compiler_params=pltpu.CompilerParams(
            dimension_semantics=("parallel", "arbitrary"), vmem_limit_bytes=VMEM_LIMIT),
        name=name,
    )(hp, hs, g, w3)


def _mlstm_prompt_kernel(q_ref, k_ref, v_ref, o_ref, g_ref, bias_ref, nrm_ref,
                         ha_ref, c_ref, n_ref, m_ref, m_s, *, n_chunks):
    L = MLSTM_CHUNK
    c = pl.program_id(1)

    @pl.when(c == 0)
    def _():
        c_ref[...] = jnp.zeros_like(c_ref)
        n_ref[...] = jnp.zeros_like(n_ref)
        m_s[...] = jnp.zeros_like(m_s)

    gz = g_ref[...] + bias_ref[...]
    capped = _soft_cap(gz)
    li = capped
    lf = _log_sigmoid(capped)
    row = lax.broadcasted_iota(jnp.int32, (L, L), 0)
    col = lax.broadcasted_iota(jnp.int32, (L, L), 1)
    causal = row >= col
    tri = causal.astype(F32)
    bc = jnp.dot(tri, lf, precision=lax.Precision.HIGHEST, preferred_element_type=F32)
    bc_t = bc.T
    li_t = li.T

    for h in range(H_A):
        hs = slice(h * DK, (h + 1) * DK)
        b_col = bc[:, H_A + h:H_A + h + 1]
        b_row = bc_t[H_A + h:H_A + h + 1, :]
        i_col = li[:, h:h + 1]
        i_row = li_t[h:h + 1, :]
        m_prev = m_s[h][0:1, 0:1]
        c_prev = c_ref[0, 0, h]
        n_prev = n_ref[0, 0, h:h + 1, :]

        qf = q_ref[:, hs]
        kf = k_ref[:, hs] * (DK ** -0.5)
        qb = qf.astype(BF16)
        kb = kf.astype(BF16)
        vb = v_ref[:, hs].astype(BF16)

        dmat = jnp.where(causal, b_col - b_row + i_row, -jnp.inf)
        m_inter = b_col + m_prev
        m_t = jnp.maximum(m_inter, jnp.max(dmat, axis=-1, keepdims=True))
        s = lax.dot_general(qb, kb, (((1,), (1,)), ((), ())), preferred_element_type=F32)
        s = s * jnp.exp(dmat - m_t)
        sc = jnp.exp(m_inter - m_t)
        num = sc * _bdot(qb, c_prev.astype(BF16)) + _bdot(s.astype(BF16), vb)
        den = sc * jnp.sum(qf * n_prev, axis=-1, keepdims=True) + jnp.sum(s, axis=-1, keepdims=True)
        hh = num / jnp.maximum(jnp.abs(den), jnp.exp(-m_t))
        hn = hh * lax.rsqrt(jnp.mean(hh * hh, axis=-1, keepdims=True) + EPS)
        ha_ref[:, hs] = (hn * nrm_ref[:, hs] * jax.nn.sigmoid(o_ref[:, hs])).astype(ha_ref.dtype)

        b_last = b_row[:, L - 1:L]
        g_row = b_last - b_row + i_row
        m_new = jnp.maximum(b_last + m_prev, jnp.max(g_row, axis=-1, keepdims=True))
        decay = jnp.exp(b_last + m_prev - m_new)
        wk = jnp.exp(b_last - b_col + i_col - m_new) * kf
        c_ref[0, 0, h] = decay * c_prev + lax.dot_general(
            wk.astype(BF16), vb, (((0,), (0,)), ((), ())), preferred_element_type=F32)
        n_ref[0, 0, h:h + 1, :] = decay * n_prev + jnp.sum(wk, axis=0, keepdims=True)
        m_s[h] = jnp.broadcast_to(m_new, (8, LANE))

    @pl.when(c == n_chunks - 1)
    def _():
        m_ref[0] = m_s[:, 0, :]


def _mlstm_prompt(za_p, g_p, bias_row, nrm_row, batch, seq):
    L = MLSTM_CHUNK
    n_chunks = seq // L
    kern = functools.partial(_mlstm_prompt_kernel, n_chunks=n_chunks)
    rows = lambda b, c: b * n_chunks + c
    return pl.pallas_call(
        kern,
        grid=(batch, n_chunks),
        in_specs=[
            pl.BlockSpec((L, D_A), lambda b, c: (rows(b, c), 0)),
            pl.BlockSpec((L, D_A), lambda b, c: (rows(b, c), 1)),
            pl.BlockSpec((L, D_A), lambda b, c: (rows(b, c), 2)),
            pl.BlockSpec((L, D_A), lambda b, c: (rows(b, c), 3)),
            pl.BlockSpec((L, LANE), lambda b, c: (rows(b, c), 0)),
            pl.BlockSpec((1, LANE), lambda b, c: (0, 0)),
            pl.BlockSpec((1, D_A), lambda b, c: (0, 0)),
        ],
        out_specs=[
            pl.BlockSpec((L, D_A), lambda b, c: (rows(b, c), 0)),
            pl.BlockSpec((1, 1, H_A, DK, DV), lambda b, c: (0, b, 0, 0, 0)),
            pl.BlockSpec((1, 1, H_A, DK), lambda b, c: (0, b, 0, 0)),
            pl.BlockSpec((1, H_A, LANE), lambda b, c: (b, 0, 0)),
        ],
        out_shape=[
            jax.ShapeDtypeStruct((batch * seq, D_A), BF16),
            jax.ShapeDtypeStruct((1, batch, H_A, DK, DV), F32),
            jax.ShapeDtypeStruct((1, batch, H_A, DK), F32),
            jax.ShapeDtypeStruct((batch, H_A, LANE), F32),
        ],
        scratch_shapes=[pltpu.VMEM((H_A, 8, LANE), F32)],
        compiler_params=pltpu.CompilerParams(
            dimension_semantics=("parallel", "arbitrary"), vmem_limit_bytes=VMEM_LIMIT),
        name="mlstm_prompt",
    )(za_p, za_p, za_p, za_p, g_p, bias_row, nrm_row)


def _expand_heads(x, width):
    rows = x.shape[0]
    return jnp.concatenate([jnp.broadcast_to(x[:, h:h + 1], (rows, width)) for h in range(H_A)], axis=1)


def _head_sums(x):
    return jnp.concatenate(
        [jnp.sum(x[:, h * DK:(h + 1) * DK], axis=-1, keepdims=True) for h in range(H_A)], axis=1)


def _mlstm_sample_kernel(qt_ref, kt_ref, za_ref, g_ref, bias_ref, nrm_ref, m_ref, n_ref, c_ref,
                         ha_ref, m_out, n_out, c_out, qc_ref):
    bs = SAMPLE_BS
    gz = g_ref[...] + bias_ref[...]
    capped = _soft_cap(gz)
    logi = capped[:, 0:H_A]
    logf = _log_sigmoid(capped)[:, H_A:2 * H_A]
    m_prev = m_ref[...]
    m_inter = logf + m_prev
    m_t = jnp.maximum(m_inter, logi)
    sc = jnp.exp(m_inter - m_t)
    ei = jnp.exp(logi - m_t)
    emt = jnp.exp(-m_t)

    q = za_ref[:, 0:D_A]
    k = za_ref[:, D_A:2 * D_A] * (DK ** -0.5)
    v = za_ref[:, 2 * D_A:3 * D_A]
    o = za_ref[:, 3 * D_A:4 * D_A]
    n_prev = n_ref[...]

    for j in range(bs):
        for h in range(H_A):
            hs = slice(h * DK, (h + 1) * DK)
            c_prev = c_ref[0, j, h]
            q_col = qt_ref[0, hs, j:j + 1]
            k_col = kt_ref[0, hs, j:j + 1] * (DK ** -0.5)
            qc_ref[j:j + 1, hs] = jnp.sum(q_col * c_prev, axis=0, keepdims=True)
            dec = sc[j:j + 1, h:h + 1]
            e = ei[j:j + 1, h:h + 1]
            c_out[0, j, h] = dec * c_prev + (e * k_col) * v[j:j + 1, hs]

    s = _head_sums(q * k) * ei
    den = sc * _head_sums(q * n_prev) + s
    denom = jnp.maximum(jnp.abs(den), emt)
    num = _expand_heads(sc, DV) * qc_ref[...] + _expand_heads(s, DV) * v
    hh = num / _expand_heads(denom, DV)
    ms = _head_sums(hh * hh) * (1.0 / DV)
    hn = hh * lax.rsqrt(_expand_heads(ms, DV) + EPS)
    ha_ref[...] = (hn * nrm_ref[...] * jax.nn.sigmoid(o)).astype(ha_ref.dtype)
    n_out[...] = _expand_heads(sc, DK) * n_prev + _expand_heads(ei, DK) * k
    m_out[...] = m_t


def _mlstm_sample(za_s, g_s, bias_row, nrm_row, m0, n0, c0):
    nb = za_s.shape[0]
    bs = SAMPLE_BS
    steps = nb // bs
    qt = za_s[:, 0:D_A].reshape(steps, bs, D_A).transpose(0, 2, 1)
    kt = za_s[:, D_A:2 * D_A].reshape(steps, bs, D_A).transpose(0, 2, 1)
    return pl.pallas_call(
        _mlstm_sample_kernel,
        grid=(steps,),
        in_specs=[
            pl.BlockSpec((1, D_A, bs), lambda i: (i, 0, 0)),
            pl.BlockSpec((1, D_A, bs), lambda i: (i, 0, 0)),
            pl.BlockSpec((bs, 4 * D_A), lambda i: (i, 0)),
            pl.BlockSpec((bs, LANE), lambda i: (i, 0)),
            pl.BlockSpec((1, LANE), lambda i: (0, 0)),
            pl.BlockSpec((1, D_A), lambda i: (0, 0)),
            pl.BlockSpec((bs, H_A), lambda i: (i, 0)),
            pl.BlockSpec((bs, H_A * DK), lambda i: (i, 0)),
            pl.BlockSpec((1, bs, H_A, DK, DV), lambda i: (0, i, 0, 0, 0)),
        ],
        out_specs=[
            pl.BlockSpec((bs, D_A), lambda i: (i, 0)),
            pl.BlockSpec((bs, H_A), lambda i: (i, 0)),
            pl.BlockSpec((bs, H_A * DK), lambda i: (i, 0)),
            pl.BlockSpec((1, bs, H_A, DK, DV), lambda i: (0, i, 0, 0, 0)),
        ],
        out_shape=[
            jax.ShapeDtypeStruct((nb, D_A), BF16),
            jax.ShapeDtypeStruct((nb, H_A), F32),
            jax.ShapeDtypeStruct((nb, H_A * DK), F32),
            jax.ShapeDtypeStruct((1, nb, H_A, DK, DV), F32),
        ],
        scratch_shapes=[pltpu.VMEM((bs, D_A), F32)],
        compiler_params=pltpu.CompilerParams(
            dimension_semantics=("parallel",), vmem_limit_bytes=VMEM_LIMIT),
        name="mlstm_sample",
    )(qt, kt, za_s, g_s, bias_row, nrm_row, m0, n0, c0)


def _group_norm(yb, nrm):
    gw = D_B // G_B
    parts = []
    for g in range(G_B):
        seg = yb[:, g * gw:(g + 1) * gw]
        parts.append(seg * lax.rsqrt(jnp.mean(seg * seg, axis=-1, keepdims=True) + EPS))
    return jnp.concatenate(parts, axis=1) * nrm


def _conv_prompt_kernel(gb_ref, gc_ref, xc_ref, cw_ref, cb_ref, nrm_ref, yb_ref, buf_ref, carry):
    ts = CONV_TS
    c = pl.program_id(1)

    @pl.when(c == 0)
    def _():
        carry[...] = jnp.zeros_like(carry)

    u = gc_ref[...] * xc_ref[...]
    um2 = carry[0:1, :]
    um1 = carry[1:2, :]
    row = lax.broadcasted_iota(jnp.int32, (ts, D_B), 0)
    u1 = jnp.where(row == 0, um1, pltpu.roll(u, 1, 0))
    u2 = jnp.where(row == 0, um2, jnp.where(row == 1, um1, pltpu.roll(u, 2, 0)))
    yc = cw_ref[0:1, :] * u2 + cw_ref[1:2, :] * u1 + cw_ref[2:3, :] * u + cb_ref[...]
    yb_ref[...] = _group_norm(gb_ref[...] * yc, nrm_ref[...]).astype(yb_ref.dtype)
    last2 = u[ts - 2:ts, :]
    carry[0:2, :] = last2
    buf_ref[0, 0] = last2


def _conv_prompt(zc_p, cw, cb_row, nrm_row, batch, seq):
    ts = CONV_TS
    n_c = seq // ts
    rows = lambda b, c: b * n_c + c
    return pl.pallas_call(
        _conv_prompt_kernel,
        grid=(batch, n_c),
        in_specs=[
            pl.BlockSpec((ts, D_B), lambda b, c: (rows(b, c), 0)),
            pl.BlockSpec((ts, D_B), lambda b, c: (rows(b, c), 1)),
            pl.BlockSpec((ts, D_B), lambda b, c: (rows(b, c), 2)),
            pl.BlockSpec((3, D_B), lambda b, c: (0, 0)),
            pl.BlockSpec((1, D_B), lambda b, c: (0, 0)),
            pl.BlockSpec((1, D_B), lambda b, c: (0, 0)),
        ],
        out_specs=[
            pl.BlockSpec((ts, D_B), lambda b, c: (rows(b, c), 0)),
            pl.BlockSpec((1, 1, 2, D_B), lambda b, c: (0, b, 0, 0)),
        ],
        out_shape=[
            jax.ShapeDtypeStruct((batch * seq, D_B), BF16),
            jax.ShapeDtypeStruct((1, batch, 2, D_B), F32),
        ],
        scratch_shapes=[pltpu.VMEM((8, D_B), F32)],
        compiler_params=pltpu.CompilerParams(
            dimension_semantics=("parallel", "arbitrary"), vmem_limit_bytes=VMEM_LIMIT),
        name="conv_prompt",
    )(zc_p, zc_p, zc_p, cw, cb_row, nrm_row)


def _conv_sample_kernel(zc_ref, buf_ref, cw_ref, cb_ref, nrm_ref, yb_ref, nbuf_ref):
    gb = zc_ref[:, 0:D_B]
    u = zc_ref[:, D_B:2 * D_B] * zc_ref[:, 2 * D_B:3 * D_B]
    b0 = buf_ref[:, 0:D_B]
    b1 = buf_ref[:, D_B:2 * D_B]
    yc = cw_ref[0:1, :] * b0 + cw_ref[1:2, :] * b1 + cw_ref[2:3, :] * u + cb_ref[...]
    yb_ref[...] = _group_norm(gb * yc, nrm_ref[...]).astype(yb_ref.dtype)
    nbuf_ref[:, 0:D_B] = b1
    nbuf_ref[:, D_B:2 * D_B] = u


def _conv_sample(zc_s, buf, cw, cb_row, nrm_row):
    nb = zc_s.shape[0]
    return pl.pallas_call(
        _conv_sample_kernel,
        out_shape=[
            jax.ShapeDtypeStruct((nb, D_B), BF16),
            jax.ShapeDtypeStruct((nb, 2 * D_B), F32),
        ],
        compiler_params=pltpu.CompilerParams(vmem_limit_bytes=VMEM_LIMIT),
        name="conv_sample",
    )(zc_s, buf, cw, cb_row, nrm_row)


def _outproj_kernel(hp_ref, hs_ref, ap_ref, as_ref, bp_ref, bs_ref, w_ref, op_ref, os_ref):
    wa = w_ref[:D_A, :].astype(BF16)
    wb = w_ref[D_A:, :].astype(BF16)
    op_ref[...] = hp_ref[...] + _bdot(ap_ref[...], wa) + _bdot(bp_ref[...], wb)
    os_ref[0] = hs_ref[0] + _bdot(as_ref[0], wa) + _bdot(bs_ref[0], wb)


def _outproj(hp, hs, ap, a_s, bp, b_s, w3, tn):
    n_n = D_MODEL // tn
    return pl.pallas_call(
        _outproj_kernel,
        grid=(N_TOK_BLOCKS, n_n),
        in_specs=[
            pl.BlockSpec((TM_P, tn), lambda i, j: (i, j)),
            pl.BlockSpec((1, TM_S, tn), lambda i, j: (i, 0, j)),
            pl.BlockSpec((TM_P, D_A), lambda i, j: (i, 0)),
            pl.BlockSpec((1, TM_S, D_A), lambda i, j: (i, 0, 0)),
            pl.BlockSpec((TM_P, D_B), lambda i, j: (i, 0)),
            pl.BlockSpec((1, TM_S, D_B), lambda i, j: (i, 0, 0)),
            pl.BlockSpec((None, D_A + D_B, tn), lambda i, j: (0, 0, j)),
        ],
        out_specs=[
            pl.BlockSpec((TM_P, tn), lambda i, j: (i, j)),
            pl.BlockSpec((1, TM_S, tn), lambda i, j: (i, 0, j)),
        ],
        out_shape=[
            jax.ShapeDtypeStruct((N_TOK_BLOCKS * TM_P, D_MODEL), F32),
            jax.ShapeDtypeStruct((N_TOK_BLOCKS, TM_S, D_MODEL), F32),
        ],
        compiler_params=pltpu.CompilerParams(
            dimension_semantics=("parallel", "arbitrary"), vmem_limit_bytes=VMEM_LIMIT),
        name="outproj",
    )(hp, hs, ap, a_s, bp, b_s, w3)


def kernel(x_prompt, x_sample, state_mlstm_C, state_mlstm_n, state_mlstm_m, state_conv, norm_ffn1, ffn1_gate, ffn1_up, ffn1_down, norm_mix, w_in, b_gates, conv_w, conv_b, norm_mlstm, norm_conv, w_out, norm_ffn2, ffn2_gate, ffn2_up, ffn2_down, norm_final):
    batch, seq, _ = x_prompt.shape
    nb = x_sample.shape[0]
    assert batch * seq == N_TOK_BLOCKS * TM_P and nb == N_TOK_BLOCKS * TM_S
    assert norm_ffn1.shape[0] == 1, "single-layer trunk"

    xp = x_prompt.reshape(batch * seq, D_MODEL)
    xs = x_sample.reshape(N_TOK_BLOCKS, TM_S, D_MODEL)
    gfin = norm_final.reshape(1, D_MODEL)

    h1p, h1s = _ffn(xp, xs, norm_ffn1, ffn1_gate, ffn1_up, ffn1_down, gfin, final_norm=False)

    za_p, za_s = _proj(h1p, h1s, norm_mix, w_in, 0, 4 * D_A, 512, "proj_qkvo")
    g_p, g_s = _proj(h1p, h1s, norm_mix, w_in, (4 * D_A) // LANE, LANE, LANE, "proj_gates")
    w_conv = w_in[:, :, 4 * D_A + 2 * H_A:]
    zc_p, zc_s = _proj(h1p, h1s, norm_mix, w_conv, 0, 3 * D_B, 512, "proj_conv")

    bias_row = jnp.zeros((1, LANE), F32).at[0, :2 * H_A].set(b_gates[0].astype(F32))
    ha_p, c_p, n_p, m_p = _mlstm_prompt(za_p, g_p, bias_row, norm_mlstm, batch, seq)
    ha_s, m_s, n_s, c_s = _mlstm_sample(
        za_s.reshape(nb, 4 * D_A), g_s.reshape(nb, LANE), bias_row, norm_mlstm,
        state_mlstm_m.reshape(nb, H_A), state_mlstm_n.reshape(nb, H_A * DK), state_mlstm_C)

    yb_p, conv_p = _conv_prompt(zc_p, conv_w[0], conv_b, norm_conv, batch, seq)
    yb_s, conv_s = _conv_sample(zc_s.reshape(nb, 3 * D_B), state_conv.reshape(nb, 2 * D_B),
                                conv_w[0], conv_b, norm_conv)

    h2p, h2s = _outproj(h1p, h1s, ha_p, ha_s.reshape(N_TOK_BLOCKS, TM_S, D_A),
                        yb_p, yb_s.reshape(N_TOK_BLOCKS, TM_S, D_B), w_out, 512)

    yp, ys = _ffn(h2p, h2s, norm_ffn2, ffn2_gate, ffn2_up, ffn2_down, gfin, final_norm=True)

    return (
        yp.reshape(batch, seq, D_MODEL),
        ys.reshape(nb, 1, D_MODEL),
        c_p,
        n_p,
        m_p[:, :, 0].reshape(1, batch, H_A),
        conv_p,
        c_s,
        n_s.reshape(1, nb, H_A, DK),
        m_s.reshape(1, nb, H_A),
        conv_s.reshape(1, nb, 2, D_B),
    )
```

```python
import functools

import jax
import jax.numpy as jnp
from jax import lax
from jax.experimental import pallas as pl
from jax.experimental.pallas import tpu as pltpu

F32 = jnp.float32
BF16 = jnp.bfloat16

D_MODEL = 2048
D_A = 1024
D_B = 1024
H_A = 4
DK = 256
DV = 256
G_B = 8
D_FF = 5504
GATE_CAP = 15.0
EPS = 1e-6
GATE_ROW0 = 4 * D_A
CONV_ROW0 = 4 * D_A + 2 * H_A

LANE = 128
N_FFN_BLOCKS = 8
TM_P = 1024
TM_S = 16
TM = TM_P + TM_S
TF = 256
N_MIX_BLOCKS = N_FFN_BLOCKS // 2
MP = 2 * TM_P
MS = 2 * TM_S
PROJ_TN = 512
CONV_TC = 256
OUT_TN = 512
MLSTM_CHUNK = 256
SAMPLE_BS = 8
VMEM_LIMIT = 62 * 1024 * 1024


def _rms(x, g):
    return x * lax.rsqrt(jnp.mean(x * x, axis=-1, keepdims=True) + EPS) * g


def _soft_cap(x):
    return GATE_CAP * jnp.tanh(x / GATE_CAP)


def _log_sigmoid(x):
    return -jax.nn.softplus(-x)


def _bdot(a, b):
    return jnp.dot(a, b, preferred_element_type=F32)


def _bdot_t(a, b):
    return lax.dot_general(a, b, (((1,), (1,)), ((), ())), preferred_element_type=F32)


def _ffn_kernel(xp_ref, xs_ref, g_ref, wg_ref, wu_ref, wd_ref, g2_ref, *rest,
                n_f, last_valid, final_norm):
    if final_norm:
        op_ref, os_ref, xn_ref = rest
    else:
        op_ref, os_ref, nxt_ref, xn_ref = rest
    f = pl.program_id(1)

    @pl.when(f == 0)
    def _():
        xp = xp_ref[...]
        xs = xs_ref[0]
        g = g_ref[...]
        xn_ref[:TM_P, :] = _rms(xp, g).astype(BF16)
        xn_ref[TM_P:, :] = _rms(xs, g).astype(BF16)
        op_ref[...] = xp
        os_ref[0] = xs

    def step(valid):
        xn = xn_ref[...]
        a = _bdot(xn, wg_ref[:, :valid].astype(BF16))
        b = _bdot(xn, wu_ref[:, :valid].astype(BF16))
        hid = (a * jax.nn.sigmoid(a) * b * 0.5).astype(BF16)
        r = _bdot(hid, wd_ref[:valid, :].astype(BF16))
        op_ref[...] += r[:TM_P]
        os_ref[0] += r[TM_P:]

    if last_valid == TF:
        step(TF)
    else:
        @pl.when(f < n_f - 1)
        def _():
            step(TF)

        @pl.when(f == n_f - 1)
        def _():
            step(last_valid)

    @pl.when(f == n_f - 1)
    def _():
        g2 = g2_ref[...]
        if final_norm:
            op_ref[...] = _rms(op_ref[...], g2)
            os_ref[0] = _rms(os_ref[0], g2)
        else:
            nxt_ref[0, :TM_P, :] = _rms(op_ref[...], g2).astype(BF16)
            nxt_ref[0, TM_P:, :] = _rms(os_ref[0], g2).astype(BF16)


def _ffn(xp, xs, g, wg, wu, wd, g2, final_norm):
    n_f = pl.cdiv(D_FF, TF)
    last_valid = D_FF - (n_f - 1) * TF
    kern = functools.partial(_ffn_kernel, n_f=n_f, last_valid=last_valid, final_norm=final_norm)
    out_specs = [
        pl.BlockSpec((TM_P, D_MODEL), lambda i, f: (i, 0)),
        pl.BlockSpec((1, TM_S, D_MODEL), lambda i, f: (i, 0, 0)),
    ]
    out_shape = [
        jax.ShapeDtypeStruct((N_FFN_BLOCKS * TM_P, D_MODEL), F32),
        jax.ShapeDtypeStruct((N_FFN_BLOCKS, TM_S, D_MODEL), F32),
    ]
    if not final_norm:
        out_specs.append(pl.BlockSpec((1, TM, D_MODEL), lambda i, f: (i, 0, 0)))
        out_shape.append(jax.ShapeDtypeStruct((N_FFN_BLOCKS, TM, D_MODEL), BF16))
    return pl.pallas_call(
        kern,
        grid=(N_FFN_BLOCKS, n_f),
        in_specs=[
            pl.BlockSpec((TM_P, D_MODEL), lambda i, f: (i, 0)),
            pl.BlockSpec((1, TM_S, D_MODEL), lambda i, f: (i, 0, 0)),
            pl.BlockSpec((1, D_MODEL), lambda i, f: (0, 0)),
            pl.BlockSpec((None, D_MODEL, TF), lambda i, f: (0, 0, f)),
            pl.BlockSpec((None, D_MODEL, TF), lambda i, f: (0, 0, f)),
            pl.BlockSpec((None, TF, D_MODEL), lambda i, f: (0, f, 0)),
            pl.BlockSpec((1, D_MODEL), lambda i, f: (0, 0)),
        ],
        out_specs=out_specs,
        out_shape=out_shape,
        scratch_shapes=[pltpu.VMEM((TM, D_MODEL), BF16)],
        compiler_params=pltpu.CompilerParams(
            dimension_semantics=("parallel", "arbitrary"), vmem_limit_bytes=VMEM_LIMIT),
        name="ffn_final" if final_norm else "ffn",
    )(xp, xs, g, wg, wu, wd, g2)


def _split_rows(z0, z1, p_ref, s_ref):
    p_ref[:TM_P, :] = z0[:TM_P]
    p_ref[TM_P:, :] = z1[:TM_P]
    s_ref[0, :TM_S, :] = z0[TM_P:]
    s_ref[0, TM_S:, :] = z1[TM_P:]


def _proj_kernel(xn_ref, w_ref, wg_ref, zp_ref, zs_ref, gp_ref, gs_ref):
    w = w_ref[...].astype(BF16)
    _split_rows(_bdot_t(xn_ref[0], w), _bdot_t(xn_ref[1], w), zp_ref, zs_ref)

    @pl.when(pl.program_id(1) == 0)
    def _():
        wg = wg_ref[...].astype(BF16)
        _split_rows(_bdot_t(xn_ref[0], wg), _bdot_t(xn_ref[1], wg), gp_ref, gs_ref)


def _proj(xn_c, w_t):
    n_n = (4 * D_A) // PROJ_TN
    return pl.pallas_call(
        _proj_kernel,
        grid=(N_MIX_BLOCKS, n_n),
        in_specs=[
            pl.BlockSpec((2, TM, D_MODEL), lambda i, j: (i, 0, 0)),
            pl.BlockSpec((PROJ_TN, D_MODEL), lambda i, j: (j, 0)),
            pl.BlockSpec((LANE, D_MODEL), lambda i, j: (GATE_ROW0 // LANE, 0)),
        ],
        out_specs=[
            pl.BlockSpec((MP, PROJ_TN), lambda i, j: (i, j)),
            pl.BlockSpec((1, MS, PROJ_TN), lambda i, j: (i, 0, j)),
            pl.BlockSpec((MP, LANE), lambda i, j: (i, 0)),
            pl.BlockSpec((1, MS, LANE), lambda i, j: (i, 0, 0)),
        ],
        out_shape=[
            jax.ShapeDtypeStruct((N_MIX_BLOCKS * MP, 4 * D_A), F32),
            jax.ShapeDtypeStruct((N_MIX_BLOCKS, MS, 4 * D_A), F32),
            jax.ShapeDtypeStruct((N_MIX_BLOCKS * MP, LANE), F32),
            jax.ShapeDtypeStruct((N_MIX_BLOCKS, MS, LANE), F32),
        ],
        compiler_params=pltpu.CompilerParams(
            dimension_semantics=("parallel", "arbitrary"), vmem_limit_bytes=VMEM_LIMIT),
        name="proj_qkvo",
    )(xn_c, w_t, w_t)


def _group_norm(yb, nrm):
    gw = D_B // G_B
    parts = []
    for g in range(yb.shape[1] // gw):
        seg = yb[:, g * gw:(g + 1) * gw]
        parts.append(seg * lax.rsqrt(jnp.mean(seg * seg, axis=-1, keepdims=True) + EPS))
    return jnp.concatenate(parts, axis=1) * nrm


def _conv_kernel(xn_ref, wgb_ref, wgc_ref, wxc_ref, cw_ref, cb_ref, nrm_ref, buf_ref,
                 yb_ref, cp_ref, cs_ref):
    def proj(w_ref):
        w = w_ref[...].astype(BF16)
        z0 = _bdot_t(xn_ref[0], w)
        z1 = _bdot_t(xn_ref[1], w)
        zp = jnp.concatenate([z0[:TM_P], z1[:TM_P]], axis=0)
        zs = jnp.concatenate([z0[TM_P:], z1[TM_P:]], axis=0)
        return zp, zs

    gb_p, gb_s = proj(wgb_ref)
    gc_p, gc_s = proj(wgc_ref)
    xc_p, xc_s = proj(wxc_ref)
    cw0 = cw_ref[0:1, :]
    cw1 = cw_ref[1:2, :]
    cw2 = cw_ref[2:3, :]
    cb = cb_ref[...]
    nrm = nrm_ref[...]

    u = gc_p * xc_p
    row = lax.broadcasted_iota(jnp.int32, u.shape, 0)
    u1 = jnp.where(row < 1, 0.0, pltpu.roll(u, 1, 0))
    u2 = jnp.where(row < 2, 0.0, pltpu.roll(u, 2, 0))
    yc = cw0 * u2 + cw1 * u1 + cw2 * u + cb
    yb_ref[0, :MP, :] = _group_norm(gb_p * yc, nrm).astype(yb_ref.dtype)
    cp_ref[0, 0] = u[MP - 2:MP, :]

    us = gc_s * xc_s
    b0 = buf_ref[0, :, 0, :]
    b1 = buf_ref[0, :, 1, :]
    ycs = cw0 * b0 + cw1 * b1 + cw2 * us + cb
    yb_ref[0, MP:, :] = _group_norm(gb_s * ycs, nrm).astype(yb_ref.dtype)
    cs_ref[0, :, 0, :] = b1
    cs_ref[0, :, 1, :] = us


def _conv(xn_c, w_t, cw, cb_row, nrm_row, buf):
    tc = CONV_TC
    n_c = D_B // tc
    wspec = lambda base: pl.BlockSpec(
        (pl.Element(tc), pl.Element(D_MODEL)), lambda i, c: (pl.multiple_of(base + c * tc, 8), 0))
    return pl.pallas_call(
        _conv_kernel,
        grid=(N_MIX_BLOCKS, n_c),
        in_specs=[
            pl.BlockSpec((2, TM, D_MODEL), lambda i, c: (i, 0, 0)),
            wspec(CONV_ROW0),
            wspec(CONV_ROW0 + D_B),
            wspec(CONV_ROW0 + 2 * D_B),
            pl.BlockSpec((3, tc), lambda i, c: (0, c)),
            pl.BlockSpec((1, tc), lambda i, c: (0, c)),
            pl.BlockSpec((1, tc), lambda i, c: (0, c)),
            pl.BlockSpec((1, MS, 2, tc), lambda i, c: (i, 0, 0, c)),
        ],
        out_specs=[
            pl.BlockSpec((1, MP + MS, tc), lambda i, c: (i, 0, c)),
            pl.BlockSpec((1, 1, 2, tc), lambda i, c: (0, i, 0, c)),
            pl.BlockSpec((1, MS, 2, tc), lambda i, c: (i, 0, 0, c)),
        ],
        out_shape=[
            jax.ShapeDtypeStruct((N_MIX_BLOCKS, MP + MS, D_B), BF16),
            jax.ShapeDtypeStruct((1, N_MIX_BLOCKS, 2, D_B), F32),
            jax.ShapeDtypeStruct((N_MIX_BLOCKS, MS, 2, D_B), F32),
        ],
        compiler_params=pltpu.CompilerParams(
            dimension_semantics=("parallel", "arbitrary"), vmem_limit_bytes=VMEM_LIMIT),
        name="proj_conv",
    )(xn_c, w_t, w_t, w_t, cw, cb_row, nrm_row, buf)


def _mlstm_prompt_kernel(q_ref, k_ref, v_ref, o_ref, g_ref, bias_ref, nrm_ref,
                         ha_ref, c_ref, n_ref, m_ref, m_s, *, n_chunks):
    L = MLSTM_CHUNK
    c = pl.program_id(1)

    @pl.when(c == 0)
    def _():
        c_ref[...] = jnp.zeros_like(c_ref)
        n_ref[...] = jnp.zeros_like(n_ref)
        m_s[...] = jnp.zeros_like(m_s)

    gz = g_ref[...] + bias_ref[...]
    capped = _soft_cap(gz)
    li = capped
    lf = _log_sigmoid(capped)
    row = lax.broadcasted_iota(jnp.int32, (L, L), 0)
    col = lax.broadcasted_iota(jnp.int32, (L, L), 1)
    causal = row >= col
    tri = causal.astype(F32)
    bc = jnp.dot(tri, lf, precision=lax.Precision.HIGHEST, preferred_element_type=F32)
    bc_t = bc.T
    li_t = li.T

    for h in range(H_A):
        hs = slice(h * DK, (h + 1) * DK)
        b_col = bc[:, H_A + h:H_A + h + 1]
        b_row = bc_t[H_A + h:H_A + h + 1, :]
        i_col = li[:, h:h + 1]
        i_row = li_t[h:h + 1, :]
        m_prev = m_s[h][0:1, 0:1]
        c_prev = c_ref[0, 0, h]
        n_prev = n_ref[0, 0, h:h + 1, :]

        qf = q_ref[:, hs]
        kf = k_ref[:, hs] * (DK ** -0.5)
        qb = qf.astype(BF16)
        kb = kf.astype(BF16)
        vb = v_ref[:, hs].astype(BF16)

        dmat = jnp.where(causal, b_col - b_row + i_row, -jnp.inf)
        m_inter = b_col + m_prev
        m_t = jnp.maximum(m_inter, jnp.max(dmat, axis=-1, keepdims=True))
        s = _bdot_t(qb, kb) * jnp.exp(dmat - m_t)
        sc = jnp.exp(m_inter - m_t)
        num = sc * _bdot(qb, c_prev.astype(BF16)) + _bdot(s.astype(BF16), vb)
        den = sc * jnp.sum(qf * n_prev, axis=-1, keepdims=True) + jnp.sum(s, axis=-1, keepdims=True)
        hh = num / jnp.maximum(jnp.abs(den), jnp.exp(-m_t))
        hn = hh * lax.rsqrt(jnp.mean(hh * hh, axis=-1, keepdims=True) + EPS)
        ha_ref[:, hs] = (hn * nrm_ref[:, hs] * jax.nn.sigmoid(o_ref[:, hs])).astype(ha_ref.dtype)

        b_last = b_row[:, L - 1:L]
        g_row = b_last - b_row + i_row
        m_new = jnp.maximum(b_last + m_prev, jnp.max(g_row, axis=-1, keepdims=True))
        decay = jnp.exp(b_last + m_prev - m_new)
        wk = jnp.exp(b_last - b_col + i_col - m_new) * kf
        c_ref[0, 0, h] = decay * c_prev + lax.dot_general(
            wk.astype(BF16), vb, (((0,), (0,)), ((), ())), preferred_element_type=F32)
        n_ref[0, 0, h:h + 1, :] = decay * n_prev + jnp.sum(wk, axis=0, keepdims=True)
        m_s[h] = jnp.broadcast_to(m_new, (8, LANE))

    @pl.when(c == n_chunks - 1)
    def _():
        m_ref[0] = m_s[:, 0, :]


def _mlstm_prompt(za_p, g_p, bias_row, nrm_row, batch, seq):
    L = MLSTM_CHUNK
    n_chunks = seq // L
    kern = functools.partial(_mlstm_prompt_kernel, n_chunks=n_chunks)
    rows = lambda b, c: b * n_chunks + c
    return pl.pallas_call(
        kern,
        grid=(batch, n_chunks),
        in_specs=[
            pl.BlockSpec((L, D_A), lambda b, c: (rows(b, c), 0)),
            pl.BlockSpec((L, D_A), lambda b, c: (rows(b, c), 1)),
            pl.BlockSpec((L, D_A), lambda b, c: (rows(b, c), 2)),
            pl.BlockSpec((L, D_A), lambda b, c: (rows(b, c), 3)),
            pl.BlockSpec((L, LANE), lambda b, c: (rows(b, c), 0)),
            pl.BlockSpec((1, LANE), lambda b, c: (0, 0)),
            pl.BlockSpec((1, D_A), lambda b, c: (0, 0)),
        ],
        out_specs=[
            pl.BlockSpec((L, D_A), lambda b, c: (rows(b, c), 0)),
            pl.BlockSpec((1, 1, H_A, DK, DV), lambda b, c: (0, b, 0, 0, 0)),
            pl.BlockSpec((1, 1, H_A, DK), lambda b, c: (0, b, 0, 0)),
            pl.BlockSpec((1, H_A, LANE), lambda b, c: (b, 0, 0)),
        ],
        out_shape=[
            jax.ShapeDtypeStruct((batch * seq, D_A), BF16),
            jax.ShapeDtypeStruct((1, batch, H_A, DK, DV), F32),
            jax.ShapeDtypeStruct((1, batch, H_A, DK), F32),
            jax.ShapeDtypeStruct((batch, H_A, LANE), F32),
        ],
        scratch_shapes=[pltpu.VMEM((H_A, 8, LANE), F32)],
        compiler_params=pltpu.CompilerParams(
            dimension_semantics=("parallel", "arbitrary"), vmem_limit_bytes=VMEM_LIMIT),
        name="mlstm_prompt",
    )(za_p, za_p, za_p, za_p, g_p, bias_row, nrm_row)


def _expand_heads(x, width):
    rows = x.shape[0]
    return jnp.concatenate([jnp.broadcast_to(x[:, h:h + 1], (rows, width)) for h in range(H_A)], axis=1)


def _head_sums(x):
    return jnp.concatenate(
        [jnp.sum(x[:, h * DK:(h + 1) * DK], axis=-1, keepdims=True) for h in range(H_A)], axis=1)


def _mlstm_sample_kernel(qt_ref, kt_ref, za_ref, g_ref, bias_ref, nrm_ref, m_ref, n_ref, c_ref,
                         ha_ref, m_out, n_out, c_out, qc_ref):
    bs = SAMPLE_BS
    gz = g_ref[...] + bias_ref[...]
    capped = _soft_cap(gz)
    logi = capped[:, 0:H_A]
    logf = _log_sigmoid(capped)[:, H_A:2 * H_A]
    m_prev = m_ref[...]
    m_inter = logf + m_prev
    m_t = jnp.maximum(m_inter, logi)
    sc = jnp.exp(m_inter - m_t)
    ei = jnp.exp(logi - m_t)
    emt = jnp.exp(-m_t)

    q = za_ref[:, 0:D_A]
    k = za_ref[:, D_A:2 * D_A] * (DK ** -0.5)
    v = za_ref[:, 2 * D_A:3 * D_A]
    o = za_ref[:, 3 * D_A:4 * D_A]
    n_prev = n_ref[...]

    for j in range(bs):
        for h in range(H_A):
            hs = slice(h * DK, (h + 1) * DK)
            c_prev = c_ref[0, j, h]
            q_col = qt_ref[0, hs, j:j + 1]
            k_col = kt_ref[0, hs, j:j + 1] * (DK ** -0.5)
            qc_ref[j:j + 1, hs] = jnp.sum(q_col * c_prev, axis=0, keepdims=True)
            dec = sc[j:j + 1, h:h + 1]
            e = ei[j:j + 1, h:h + 1]
            c_out[0, j, h] = dec * c_prev + (e * k_col) * v[j:j + 1, hs]

    s = _head_sums(q * k) * ei
    den = sc * _head_sums(q * n_prev) + s
    denom = jnp.maximum(jnp.abs(den), emt)
    num = _expand_heads(sc, DV) * qc_ref[...] + _expand_heads(s, DV) * v
    hh = num / _expand_heads(denom, DV)
    ms = _head_sums(hh * hh) * (1.0 / DV)
    hn = hh * lax.rsqrt(_expand_heads(ms, DV) + EPS)
    ha_ref[...] = (hn * nrm_ref[...] * jax.nn.sigmoid(o)).astype(ha_ref.dtype)
    n_out[...] = _expand_heads(sc, DK) * n_prev + _expand_heads(ei, DK) * k
    m_out[...] = m_t


def _mlstm_sample(za_s, g_s, bias_row, nrm_row, m0, n0, c0):
    nb = za_s.shape[0]
    bs = SAMPLE_BS
    steps = nb // bs
    qt = za_s[:, 0:D_A].reshape(steps, bs, D_A).transpose(0, 2, 1)
    kt = za_s[:, D_A:2 * D_A].reshape(steps, bs, D_A).transpose(0, 2, 1)
    return pl.pallas_call(
        _mlstm_sample_kernel,
        grid=(steps,),
        in_specs=[
            pl.BlockSpec((1, D_A, bs), lambda i: (i, 0, 0)),
            pl.BlockSpec((1, D_A, bs), lambda i: (i, 0, 0)),
            pl.BlockSpec((bs, 4 * D_A), lambda i: (i, 0)),
            pl.BlockSpec((bs, LANE), lambda i: (i, 0)),
            pl.BlockSpec((1, LANE), lambda i: (0, 0)),
            pl.BlockSpec((1, D_A), lambda i: (0, 0)),
            pl.BlockSpec((bs, H_A), lambda i: (i, 0)),
            pl.BlockSpec((bs, H_A * DK), lambda i: (i, 0)),
            pl.BlockSpec((1, bs, H_A, DK, DV), lambda i: (0, i, 0, 0, 0)),
        ],
        out_specs=[
            pl.BlockSpec((bs, D_A), lambda i: (i, 0)),
            pl.BlockSpec((bs, H_A), lambda i: (i, 0)),
            pl.BlockSpec((bs, H_A * DK), lambda i: (i, 0)),
            pl.BlockSpec((1, bs, H_A, DK, DV), lambda i: (0, i, 0, 0, 0)),
        ],
        out_shape=[
            jax.ShapeDtypeStruct((nb, D_A), BF16),
            jax.ShapeDtypeStruct((nb, H_A), F32),
            jax.ShapeDtypeStruct((nb, H_A * DK), F32),
            jax.ShapeDtypeStruct((1, nb, H_A, DK, DV), F32),
        ],
        scratch_shapes=[pltpu.VMEM((bs, D_A), F32)],
        compiler_params=pltpu.CompilerParams(
            dimension_semantics=("parallel",), vmem_limit_bytes=VMEM_LIMIT),
        name="mlstm_sample",
    )(qt, kt, za_s, g_s, bias_row, nrm_row, m0, n0, c0)


def _outproj_kernel(hp_ref, hs_ref, ap_ref, as_ref, yb_ref, w_ref, op_ref, os_ref):
    wa = w_ref[:D_A, :].astype(BF16)
    wb = w_ref[D_A:, :].astype(BF16)
    op_ref[...] = hp_ref[...] + _bdot(ap_ref[...], wa) + _bdot(yb_ref[0, :MP, :], wb)
    os_ref[0] = hs_ref[0] + _bdot(as_ref[0], wa) + _bdot(yb_ref[0, MP:, :], wb)


def _outproj(hp, hs, ap, a_s, yb_c, w3):
    tn = OUT_TN
    n_n = D_MODEL // tn
    return pl.pallas_call(
        _outproj_kernel,
        grid=(N_MIX_BLOCKS, n_n),
        in_specs=[
            pl.BlockSpec((MP, tn), lambda i, j: (i, j)),
            pl.BlockSpec((1, MS, tn), lambda i, j: (i, 0, j)),
            pl.BlockSpec((MP, D_A), lambda i, j: (i, 0)),
            pl.BlockSpec((1, MS, D_A), lambda i, j: (i, 0, 0)),
            pl.BlockSpec((1, MP + MS, D_B), lambda i, j: (i, 0, 0)),
            pl.BlockSpec((None, D_A + D_B, tn), lambda i, j: (0, 0, j)),
        ],
        out_specs=[
            pl.BlockSpec((MP, tn), lambda i, j: (i, j)),
            pl.BlockSpec((1, MS, tn), lambda i, j: (i, 0, j)),
        ],
        out_shape=[
            jax.ShapeDtypeStruct((N_MIX_BLOCKS * MP, D_MODEL), F32),
            jax.ShapeDtypeStruct((N_MIX_BLOCKS, MS, D_MODEL), F32),
        ],
        compiler_params=pltpu.CompilerParams(
            dimension_semantics=("parallel", "arbitrary"), vmem_limit_bytes=VMEM_LIMIT),
        name="outproj",
    )(hp, hs, ap, a_s, yb_c, w3)


def kernel(x_prompt, x_sample, state_mlstm_C, state_mlstm_n, state_mlstm_m, state_conv, norm_ffn1, ffn1_gate, ffn1_up, ffn1_down, norm_mix, w_in, b_gates, conv_w, conv_b, norm_mlstm, norm_conv, w_out, norm_ffn2, ffn2_gate, ffn2_up, ffn2_down, norm_final):
    batch, seq, _ = x_prompt.shape
    nb = x_sample.shape[0]
    assert batch == N_MIX_BLOCKS and seq == MP and nb == N_MIX_BLOCKS * MS
    assert norm_ffn1.shape[0] == 1, "single-layer trunk"

    xp = x_prompt.reshape(batch * seq, D_MODEL)
    xs = x_sample.reshape(N_FFN_BLOCKS, TM_S, D_MODEL)

    h1p, h1s, xn_c = _ffn(xp, xs, norm_ffn1, ffn1_gate, ffn1_up, ffn1_down, norm_mix, final_norm=False)

    w_t = jnp.swapaxes(w_in[0], 0, 1)
    za_p, za_s, g_p, g_s = _proj(xn_c, w_t)
    yb_c, conv_p, conv_s = _conv(xn_c, w_t, conv_w[0], conv_b, norm_conv,
                                 state_conv.reshape(N_MIX_BLOCKS, MS, 2, D_B))

    bias_row = jnp.zeros((1, LANE), F32).at[0, :2 * H_A].set(b_gates[0].astype(F32))
    ha_p, c_p, n_p, m_p = _mlstm_prompt(za_p, g_p, bias_row, norm_mlstm, batch, seq)
    ha_s, m_s, n_s, c_s = _mlstm_sample(
        za_s.reshape(nb, 4 * D_A), g_s.reshape(nb, LANE), bias_row, norm_mlstm,
        state_mlstm_m.reshape(nb, H_A), state_mlstm_n.reshape(nb, H_A * DK), state_mlstm_C)

    h2p, h2s = _outproj(h1p, h1s.reshape(N_MIX_BLOCKS, MS, D_MODEL), ha_p,
                        ha_s.reshape(N_MIX_BLOCKS, MS, D_A), yb_c, w_out)

    yp, ys = _ffn(h2p, h2s.reshape(N_FFN_BLOCKS, TM_S, D_MODEL), norm_ffn2, ffn2_gate, ffn2_up,
                  ffn2_down, norm_final.reshape(1, D_MODEL), final_norm=True)

    return (
        yp.reshape(batch, seq, D_MODEL),
        ys.reshape(nb, 1, D_MODEL),
        c_p,
        n_p,
        m_p[:, :, 0].reshape(1, batch, H_A),
        conv_p,
        c_s,
        n_s.reshape(1, nb, H_A, DK),
        m_s.reshape(1, nb, H_A),
        conv_s.reshape(1, nb, 2, D_B),
    )
```

```python
import functools

import jax
import jax.numpy as jnp
from jax import lax
from jax.experimental import pallas as pl
from jax.experimental.pallas import tpu as pltpu

F32 = jnp.float32
BF16 = jnp.bfloat16

D_MODEL = 2048
D_A = 1024
D_B = 1024
H_A = 4
DK = 256
DV = 256
G_B = 8
D_FF = 5504
GATE_CAP = 15.0
EPS = 1e-6
GATE_ROW0 = 4 * D_A
CONV_ROW0 = 4 * D_A + 2 * H_A

LANE = 128
N_FFN_BLOCKS = 8
TM_P = 1024
TM_S = 16
TM = TM_P + TM_S
TF = 256
N_MIX_BLOCKS = N_FFN_BLOCKS // 2
MP = 2 * TM_P
MS = 2 * TM_S
PROJ_TN = 512
CONV_TC = 256
OUT_TN = 512
MLSTM_CHUNK = 256
MLSTM_SEQS = 4
SAMPLE_BS = 8
VMEM_LIMIT = 62 * 1024 * 1024


def _rms(x, g):
    return x * lax.rsqrt(jnp.mean(x * x, axis=-1, keepdims=True) + EPS) * g


def _soft_cap(x):
    return GATE_CAP * jnp.tanh(x / GATE_CAP)


def _log_sigmoid(x):
    return -jax.nn.softplus(-x)


def _bdot(a, b):
    return jnp.dot(a, b, preferred_element_type=F32)


def _bdot_t(a, b):
    return lax.dot_general(a, b, (((1,), (1,)), ((), ())), preferred_element_type=F32)


def _ffn_kernel(xp_ref, xs_ref, g_ref, wg_ref, wu_ref, wd_ref, g2_ref, *rest,
                n_f, last_valid, final_norm):
    if final_norm:
        op_ref, os_ref, xn_ref = rest
    else:
        op_ref, os_ref, nxt_ref, xn_ref = rest
    f = pl.program_id(1)

    @pl.when(f == 0)
    def _():
        xp = xp_ref[...]
        xs = xs_ref[0]
        g = g_ref[...]
        xn_ref[:TM_P, :] = _rms(xp, g).astype(BF16)
        xn_ref[TM_P:, :] = _rms(xs, g).astype(BF16)
        op_ref[...] = xp
        os_ref[0] = xs

    def step(valid):
        xn = xn_ref[...]
        a = _bdot(xn, wg_ref[:, :valid].astype(BF16))
        b = _bdot(xn, wu_ref[:, :valid].astype(BF16))
        hid = (a * jax.nn.sigmoid(a) * b * 0.5).astype(BF16)
        r = _bdot(hid, wd_ref[:valid, :].astype(BF16))
        op_ref[...] += r[:TM_P]
        os_ref[0] += r[TM_P:]

    if last_valid == TF:
        step(TF)
    else:
        @pl.when(f < n_f - 1)
        def _():
            step(TF)

        @pl.when(f == n_f - 1)
        def _():
            step(last_valid)

    @pl.when(f == n_f - 1)
    def _():
        g2 = g2_ref[...]
        if final_norm:
            op_ref[...] = _rms(op_ref[...], g2)
            os_ref[0] = _rms(os_ref[0], g2)
        else:
            nxt_ref[0, :TM_P, :] = _rms(op_ref[...], g2).astype(BF16)
            nxt_ref[0, TM_P:, :] = _rms(os_ref[0], g2).astype(BF16)


def _ffn(xp, xs, g, wg, wu, wd, g2, final_norm):
    n_f = pl.cdiv(D_FF, TF)
    last_valid = D_FF - (n_f - 1) * TF
    kern = functools.partial(_ffn_kernel, n_f=n_f, last_valid=last_valid, final_norm=final_norm)
    out_specs = [
        pl.BlockSpec((TM_P, D_MODEL), lambda i, f: (i, 0)),
        pl.BlockSpec((1, TM_S, D_MODEL), lambda i, f: (i, 0, 0)),
    ]
    out_shape = [
        jax.ShapeDtypeStruct((N_FFN_BLOCKS * TM_P, D_MODEL), F32),
        jax.ShapeDtypeStruct((N_FFN_BLOCKS, TM_S, D_MODEL), F32),
    ]
    if not final_norm:
        out_specs.append(pl.BlockSpec((1, TM, D_MODEL), lambda i, f: (i, 0, 0)))
        out_shape.append(jax.ShapeDtypeStruct((N_FFN_BLOCKS, TM, D_MODEL), BF16))
    return pl.pallas_call(
        kern,
        grid=(N_FFN_BLOCKS, n_f),
        in_specs=[
            pl.BlockSpec((TM_P, D_MODEL), lambda i, f: (i, 0)),
            pl.BlockSpec((1, TM_S, D_MODEL), lambda i, f: (i, 0, 0)),
            pl.BlockSpec((1, D_MODEL), lambda i, f: (0, 0)),
            pl.BlockSpec((None, D_MODEL, TF), lambda i, f: (0, 0, f)),
            pl.BlockSpec((None, D_MODEL, TF), lambda i, f: (0, 0, f)),
            pl.BlockSpec((None, TF, D_MODEL), lambda i, f: (0, f, 0)),
            pl.BlockSpec((1, D_MODEL), lambda i, f: (0, 0)),
        ],
        out_specs=out_specs,
        out_shape=out_shape,
        scratch_shapes=[pltpu.VMEM((TM, D_MODEL), BF16)],
        compiler_params=pltpu.CompilerParams(
            dimension_semantics=("parallel", "arbitrary"), vmem_limit_bytes=VMEM_LIMIT),
        name="ffn_final" if final_norm else "ffn",
    )(xp, xs, g, wg, wu, wd, g2)


def _split_rows(z0, z1, p_ref, s_ref):
    p_ref[:TM_P, :] = z0[:TM_P]
    p_ref[TM_P:, :] = z1[:TM_P]
    s_ref[0, :TM_S, :] = z0[TM_P:]
    s_ref[0, TM_S:, :] = z1[TM_P:]


def _proj_kernel(xn_ref, w_ref, wg_ref, zp_ref, zs_ref, gp_ref, gs_ref):
    w = w_ref[...].astype(BF16)
    _split_rows(_bdot_t(xn_ref[0], w), _bdot_t(xn_ref[1], w), zp_ref, zs_ref)

    @pl.when(pl.program_id(1) == 0)
    def _():
        wg = wg_ref[...].astype(BF16)
        _split_rows(_bdot_t(xn_ref[0], wg), _bdot_t(xn_ref[1], wg), gp_ref, gs_ref)


def _proj(xn_c, w_t):
    n_n = (4 * D_A) // PROJ_TN
    return pl.pallas_call(
        _proj_kernel,
        grid=(N_MIX_BLOCKS, n_n),
        in_specs=[
            pl.BlockSpec((2, TM, D_MODEL), lambda i, j: (i, 0, 0)),
            pl.BlockSpec((PROJ_TN, D_MODEL), lambda i, j: (j, 0)),
            pl.BlockSpec((LANE, D_MODEL), lambda i, j: (GATE_ROW0 // LANE, 0)),
        ],
        out_specs=[
            pl.BlockSpec((MP, PROJ_TN), lambda i, j: (i, j)),
            pl.BlockSpec((1, MS, PROJ_TN), lambda i, j: (i, 0, j)),
            pl.BlockSpec((MP, LANE), lambda i, j: (i, 0)),
            pl.BlockSpec((1, MS, LANE), lambda i, j: (i, 0, 0)),
        ],
        out_shape=[
            jax.ShapeDtypeStruct((N_MIX_BLOCKS * MP, 4 * D_A), F32),
            jax.ShapeDtypeStruct((N_MIX_BLOCKS, MS, 4 * D_A), F32),
            jax.ShapeDtypeStruct((N_MIX_BLOCKS * MP, LANE), F32),
            jax.ShapeDtypeStruct((N_MIX_BLOCKS, MS, LANE), F32),
        ],
        compiler_params=pltpu.CompilerParams(
            dimension_semantics=("parallel", "arbitrary"), vmem_limit_bytes=VMEM_LIMIT),
        name="proj_qkvo",
    )(xn_c, w_t, w_t)


def _group_norm(yb, nrm):
    gw = D_B // G_B
    parts = []
    for g in range(yb.shape[1] // gw):
        seg = yb[:, g * gw:(g + 1) * gw]
        parts.append(seg * lax.rsqrt(jnp.mean(seg * seg, axis=-1, keepdims=True) + EPS))
    return jnp.concatenate(parts, axis=1) * nrm


def _conv_kernel(xn_ref, wgb_ref, wgc_ref, wxc_ref, cw_ref, cb_ref, nrm_ref, buf_ref,
                 yb_ref, cp_ref, cs_ref):
    wgb = wgb_ref[...].astype(BF16)
    wgc = wgc_ref[...].astype(BF16)
    wxc = wxc_ref[...].astype(BF16)
    cw0 = cw_ref[0:1, :]
    cw1 = cw_ref[1:2, :]
    cw2 = cw_ref[2:3, :]
    cb = cb_ref[...]
    nrm = nrm_ref[...]
    row = lax.broadcasted_iota(jnp.int32, (TM_P, wgb.shape[0]), 0)
    zero_row = jnp.zeros((1, wgb.shape[0]), F32)
    um2, um1 = zero_row, zero_row

    for t in range(2):
        xn = xn_ref[t]
        gb = _bdot_t(xn, wgb)
        u = _bdot_t(xn, wgc) * _bdot_t(xn, wxc)

        up = u[:TM_P]
        u1 = jnp.where(row < 1, um1, pltpu.roll(up, 1, 0))
        u2 = jnp.where(row < 1, um2, jnp.where(row < 2, um1, pltpu.roll(up, 2, 0)))
        yc = cw0 * u2 + cw1 * u1 + cw2 * up + cb
        yb_ref[0, t * TM_P:(t + 1) * TM_P, :] = _group_norm(gb[:TM_P] * yc, nrm).astype(yb_ref.dtype)
        um2, um1 = up[TM_P - 2:TM_P - 1, :], up[TM_P - 1:TM_P, :]

        ss = slice(t * TM_S, (t + 1) * TM_S)
        us = u[TM_P:]
        b0 = buf_ref[0, ss, 0, :]
        b1 = buf_ref[0, ss, 1, :]
        ycs = cw0 * b0 + cw1 * b1 + cw2 * us + cb
        yb_ref[0, MP + t * TM_S:MP + (t + 1) * TM_S, :] = _group_norm(gb[TM_P:] * ycs, nrm).astype(yb_ref.dtype)
        cs_ref[0, ss, 0, :] = b1
        cs_ref[0, ss, 1, :] = us

    cp_ref[0, 0, 0:1, :] = um2
    cp_ref[0, 0, 1:2, :] = um1


def _conv(xn_c, w_t, cw, cb_row, nrm_row, buf):
    tc = CONV_TC
    n_c = D_B // tc
    wspec = lambda base: pl.BlockSpec(
        (pl.Element(tc), pl.Element(D_MODEL)), lambda i, c: (pl.multiple_of(base + c * tc, 8), 0))
    return pl.pallas_call(
        _conv_kernel,
        grid=(N_MIX_BLOCKS, n_c),
        in_specs=[
            pl.BlockSpec((2, TM, D_MODEL), lambda i, c: (i, 0, 0)),
            wspec(CONV_ROW0),
            wspec(CONV_ROW0 + D_B),
            wspec(CONV_ROW0 + 2 * D_B),
            pl.BlockSpec((3, tc), lambda i, c: (0, c)),
            pl.BlockSpec((1, tc), lambda i, c: (0, c)),
            pl.BlockSpec((1, tc), lambda i, c: (0, c)),
            pl.BlockSpec((1, MS, 2, tc), lambda i, c: (i, 0, 0, c)),
        ],
        out_specs=[
            pl.BlockSpec((1, MP + MS, tc), lambda i, c: (i, 0, c)),
            pl.BlockSpec((1, 1, 2, tc), lambda i, c: (0, i, 0, c)),
            pl.BlockSpec((1, MS, 2, tc), lambda i, c: (i, 0, 0, c)),
        ],
        out_shape=[
            jax.ShapeDtypeStruct((N_MIX_BLOCKS, MP + MS, D_B), BF16),
            jax.ShapeDtypeStruct((1, N_MIX_BLOCKS, 2, D_B), F32),
            jax.ShapeDtypeStruct((N_MIX_BLOCKS, MS, 2, D_B), F32),
        ],
        compiler_params=pltpu.CompilerParams(
            dimension_semantics=("parallel", "arbitrary"), vmem_limit_bytes=VMEM_LIMIT),
        name="proj_conv",
    )(xn_c, w_t, w_t, w_t, cw, cb_row, nrm_row, buf)


def _mlstm_prompt_kernel(q_ref, k_ref, v_ref, o_ref, g_ref, bias_ref, nrm_ref,
                         ha_ref, c_ref, n_ref, m_ref, m_s, *, n_chunks):
    L = MLSTM_CHUNK
    c = pl.program_id(1)

    @pl.when(c == 0)
    def _():
        c_ref[...] = jnp.zeros_like(c_ref)
        n_ref[...] = jnp.zeros_like(n_ref)
        m_s[...] = jnp.zeros_like(m_s)

    row = lax.broadcasted_iota(jnp.int32, (L, L), 0)
    col = lax.broadcasted_iota(jnp.int32, (L, L), 1)
    causal = row >= col
    tri = causal.astype(BF16)

    for b in range(MLSTM_SEQS):
        _mlstm_prompt_seq(b, causal, tri, q_ref, k_ref, v_ref, o_ref, g_ref, bias_ref, nrm_ref,
                          ha_ref, c_ref, n_ref, m_s)

    @pl.when(c == n_chunks - 1)
    def _():
        for b in range(MLSTM_SEQS):
            m_ref[b] = m_s[b * H_A:(b + 1) * H_A, 0, :]


def _mlstm_prompt_seq(b, causal, tri, q_ref, k_ref, v_ref, o_ref, g_ref, bias_ref, nrm_ref,
                      ha_ref, c_ref, n_ref, m_s):
    L = MLSTM_CHUNK
    capped = _soft_cap(g_ref[b] + bias_ref[...])
    li = capped
    lf = _log_sigmoid(capped)
    hi = lf.astype(BF16)
    r1 = lf - hi.astype(F32)
    mid = r1.astype(BF16)
    lo = (r1 - mid.astype(F32)).astype(BF16)
    parts = _bdot(tri, jnp.concatenate([hi, mid, lo], axis=1))
    bc = parts[:, :LANE] + parts[:, LANE:2 * LANE] + parts[:, 2 * LANE:]
    bc_t = bc.T
    li_t = li.T

    for h in range(H_A):
        hs = slice(h * DK, (h + 1) * DK)
        b_col = bc[:, H_A + h:H_A + h + 1]
        b_row = bc_t[H_A + h:H_A + h + 1, :]
        i_col = li[:, h:h + 1]
        i_row = li_t[h:h + 1, :]
        m_prev = m_s[b * H_A + h][0:1, 0:1]
        c_prev = c_ref[0, b, h]
        n_prev = n_ref[0, b, h:h + 1, :]

        qf = q_ref[b, :, hs]
        kf = k_ref[b, :, hs] * (DK ** -0.5)
        qb = qf.astype(BF16)
        kb = kf.astype(BF16)
        vb = v_ref[b, :, hs].astype(BF16)

        dmat = jnp.where(causal, b_col - b_row + i_row, -jnp.inf)
        m_inter = b_col + m_prev
        m_t = jnp.maximum(m_inter, jnp.max(dmat, axis=-1, keepdims=True))
        s = _bdot_t(qb, kb) * jnp.exp(dmat - m_t)
        sc = jnp.exp(m_inter - m_t)
        num = sc * _bdot(qb, c_prev.astype(BF16)) + _bdot(s.astype(BF16), vb)
        den = sc * jnp.sum(qf * n_prev, axis=-1, keepdims=True) + jnp.sum(s, axis=-1, keepdims=True)
        hh = num * (1.0 / jnp.maximum(jnp.abs(den), jnp.exp(-m_t)))
        hn = hh * lax.rsqrt(jnp.mean(hh * hh, axis=-1, keepdims=True) + EPS)
        ha_ref[b, :, hs] = (hn * nrm_ref[:, hs] * jax.nn.sigmoid(o_ref[b, :, hs])).astype(ha_ref.dtype)

        b_last = b_row[:, L - 1:L]
        g_row = b_last - b_row + i_row
        m_new = jnp.maximum(b_last + m_prev, jnp.max(g_row, axis=-1, keepdims=True))
        decay = jnp.exp(b_last + m_prev - m_new)
        wk = jnp.exp(b_last - b_col + i_col - m_new) * kf
        c_ref[0, b, h] = decay * c_prev + lax.dot_general(
            wk.astype(BF16), vb, (((0,), (0,)), ((), ())), preferred_element_type=F32)
        n_ref[0, b, h:h + 1, :] = decay * n_prev + jnp.sum(wk, axis=0, keepdims=True)
        m_s[b * H_A + h] = jnp.broadcast_to(m_new, (8, LANE))


def _mlstm_prompt(za_p, g_p, bias_row, nrm_row, batch, seq):
    L = MLSTM_CHUNK
    nsq = MLSTM_SEQS
    n_chunks = seq // L
    kern = functools.partial(_mlstm_prompt_kernel, n_chunks=n_chunks)
    za3 = za_p.reshape(batch, seq, 4 * D_A)
    g3 = g_p.reshape(batch, seq, LANE)
    zspec = lambda part: pl.BlockSpec((nsq, L, D_A), lambda b, c: (b, c, part))
    return pl.pallas_call(
        kern,
        grid=(batch // nsq, n_chunks),
        in_specs=[
            zspec(0), zspec(1), zspec(2), zspec(3),
            pl.BlockSpec((nsq, L, LANE), lambda b, c: (b, c, 0)),
            pl.BlockSpec((1, LANE), lambda b, c: (0, 0)),
            pl.BlockSpec((1, D_A), lambda b, c: (0, 0)),
        ],
        out_specs=[
            pl.BlockSpec((nsq, L, D_A), lambda b, c: (b, c, 0)),
            pl.BlockSpec((1, nsq, H_A, DK, DV), lambda b, c: (0, b, 0, 0, 0)),
            pl.BlockSpec((1, nsq, H_A, DK), lambda b, c: (0, b, 0, 0)),
            pl.BlockSpec((nsq, H_A, LANE), lambda b, c: (b, 0, 0)),
        ],
        out_shape=[
            jax.ShapeDtypeStruct((batch, seq, D_A), BF16),
            jax.ShapeDtypeStruct((1, batch, H_A, DK, DV), F32),
            jax.ShapeDtypeStruct((1, batch, H_A, DK), F32),
            jax.ShapeDtypeStruct((batch, H_A, LANE), F32),
        ],
        scratch_shapes=[pltpu.VMEM((nsq * H_A, 8, LANE), F32)],
        compiler_params=pltpu.CompilerParams(
            dimension_semantics=("parallel", "arbitrary"), vmem_limit_bytes=VMEM_LIMIT),
        name="mlstm_prompt",
    )(za3, za3, za3, za3, g3, bias_row, nrm_row)


def _expand_heads(x, width):
    rows = x.shape[0]
    return jnp.concatenate([jnp.broadcast_to(x[:, h:h + 1], (rows, width)) for h in range(H_A)], axis=1)


def _head_sums(x):
    return jnp.concatenate(
        [jnp.sum(x[:, h * DK:(h + 1) * DK], axis=-1, keepdims=True) for h in range(H_A)], axis=1)


def _mlstm_sample_kernel(qt_ref, kt_ref, za_ref, g_ref, bias_ref, nrm_ref, m_ref, n_ref, c_ref,
                         ha_ref, m_out, n_out, c_out, qc_ref):
    bs = SAMPLE_BS
    gz = g_ref[...] + bias_ref[...]
    capped = _soft_cap(gz)
    logi = capped[:, 0:H_A]
    logf = _log_sigmoid(capped)[:, H_A:2 * H_A]
    m_prev = m_ref[...]
    m_inter = logf + m_prev
    m_t = jnp.maximum(m_inter, logi)
    sc = jnp.exp(m_inter - m_t)
    ei = jnp.exp(logi - m_t)
    emt = jnp.exp(-m_t)

    q = za_ref[:, 0:D_A]
    k = za_ref[:, D_A:2 * D_A] * (DK ** -0.5)
    v = za_ref[:, 2 * D_A:3 * D_A]
    o = za_ref[:, 3 * D_A:4 * D_A]
    n_prev = n_ref[...]

    for j in range(bs):
        for h in range(H_A):
            hs = slice(h * DK, (h + 1) * DK)
            c_prev = c_ref[0, j, h]
            q_col = qt_ref[0, hs, j:j + 1]
            k_col = kt_ref[0, hs, j:j + 1] * (DK ** -0.5)
            qc_ref[j:j + 1, hs] = jnp.sum(q_col * c_prev, axis=0, keepdims=True)
            dec = sc[j:j + 1, h:h + 1]
            e = ei[j:j + 1, h:h + 1]
            c_out[0, j, h] = dec * c_prev + (e * k_col) * v[j:j + 1, hs]

    s = _head_sums(q * k) * ei
    den = sc * _head_sums(q * n_prev) + s
    denom = jnp.maximum(jnp.abs(den), emt)
    num = _expand_heads(sc, DV) * qc_ref[...] + _expand_heads(s, DV) * v
    hh = num / _expand_heads(denom, DV)
    ms = _head_sums(hh * hh) * (1.0 / DV)
    hn = hh * lax.rsqrt(_expand_heads(ms, DV) + EPS)
    ha_ref[...] = (hn * nrm_ref[...] * jax.nn.sigmoid(o)).astype(ha_ref.dtype)
    n_out[...] = _expand_heads(sc, DK) * n_prev + _expand_heads(ei, DK) * k
    m_out[...] = m_t


def _mlstm_sample(za_s, g_s, bias_row, nrm_row, m0, n0, c0):
    nb = za_s.shape[0]
    bs = SAMPLE_BS
    steps = nb // bs
    qt = za_s[:, 0:D_A].reshape(steps, bs, D_A).transpose(0, 2, 1)
    kt = za_s[:, D_A:2 * D_A].reshape(steps, bs, D_A).transpose(0, 2, 1)
    return pl.pallas_call(
        _mlstm_sample_kernel,
        grid=(steps,),
        in_specs=[
            pl.BlockSpec((1, D_A, bs), lambda i: (i, 0, 0)),
            pl.BlockSpec((1, D_A, bs), lambda i: (i, 0, 0)),
            pl.BlockSpec((bs, 4 * D_A), lambda i: (i, 0)),
            pl.BlockSpec((bs, LANE), lambda i: (i, 0)),
            pl.BlockSpec((1, LANE), lambda i: (0, 0)),
            pl.BlockSpec((1, D_A), lambda i: (0, 0)),
            pl.BlockSpec((bs, H_A), lambda i: (i, 0)),
            pl.BlockSpec((bs, H_A * DK), lambda i: (i, 0)),
            pl.BlockSpec((1, bs, H_A, DK, DV), lambda i: (0, i, 0, 0, 0)),
        ],
        out_specs=[
            pl.BlockSpec((bs, D_A), lambda i: (i, 0)),
            pl.BlockSpec((bs, H_A), lambda i: (i, 0)),
            pl.BlockSpec((bs, H_A * DK), lambda i: (i, 0)),
            pl.BlockSpec((1, bs, H_A, DK, DV), lambda i: (0, i, 0, 0, 0)),
        ],
        out_shape=[
            jax.ShapeDtypeStruct((nb, D_A), BF16),
            jax.ShapeDtypeStruct((nb, H_A), F32),
            jax.ShapeDtypeStruct((nb, H_A * DK), F32),
            jax.ShapeDtypeStruct((1, nb, H_A, DK, DV), F32),
        ],
        scratch_shapes=[pltpu.VMEM((bs, D_A), F32)],
        compiler_params=pltpu.CompilerParams(
            dimension_semantics=("parallel",), vmem_limit_bytes=VMEM_LIMIT),
        name="mlstm_sample",
    )(qt, kt, za_s, g_s, bias_row, nrm_row, m0, n0, c0)


def _outproj_kernel(hp_ref, hs_ref, ap_ref, as_ref, yb_ref, w_ref, op_ref, os_ref):
    wa = w_ref[:D_A, :].astype(BF16)
    wb = w_ref[D_A:, :].astype(BF16)
    op_ref[...] = hp_ref[...] + _bdot(ap_ref[...], wa) + _bdot(yb_ref[0, :MP, :], wb)
    os_ref[0] = hs_ref[0] + _bdot(as_ref[0], wa) + _bdot(yb_ref[0, MP:, :], wb)


def _outproj(hp, hs, ap, a_s, yb_c, w3):
    tn = OUT_TN
    n_n = D_MODEL // tn
    return pl.pallas_call(
        _outproj_kernel,
        grid=(N_MIX_BLOCKS, n_n),
        in_specs=[
            pl.BlockSpec((MP, tn), lambda i, j: (i, j)),
            pl.BlockSpec((1, MS, tn), lambda i, j: (i, 0, j)),
            pl.BlockSpec((MP, D_A), lambda i, j: (i, 0)),
            pl.BlockSpec((1, MS, D_A), lambda i, j: (i, 0, 0)),
            pl.BlockSpec((1, MP + MS, D_B), lambda i, j: (i, 0, 0)),
            pl.BlockSpec((None, D_A + D_B, tn), lambda i, j: (0, 0, j)),
        ],
        out_specs=[
            pl.BlockSpec((MP, tn), lambda i, j: (i, j)),
            pl.BlockSpec((1, MS, tn), lambda i, j: (i, 0, j)),
        ],
        out_shape=[
            jax.ShapeDtypeStruct((N_MIX_BLOCKS * MP, D_MODEL), F32),
            jax.ShapeDtypeStruct((N_MIX_BLOCKS, MS, D_MODEL), F32),
        ],
        compiler_params=pltpu.CompilerParams(
            dimension_semantics=("parallel", "arbitrary"), vmem_limit_bytes=VMEM_LIMIT),
        name="outproj",
    )(hp, hs, ap, a_s, yb_c, w3)


def kernel(x_prompt, x_sample, state_mlstm_C, state_mlstm_n, state_mlstm_m, state_conv, norm_ffn1, ffn1_gate, ffn1_up, ffn1_down, norm_mix, w_in, b_gates, conv_w, conv_b, norm_mlstm, norm_conv, w_out, norm_ffn2, ffn2_gate, ffn2_up, ffn2_down, norm_final):
    batch, seq, _ = x_prompt.shape
    nb = x_sample.shape[0]
    assert batch == N_MIX_BLOCKS and seq == MP and nb == N_MIX_BLOCKS * MS
    assert norm_ffn1.shape[0] == 1, "single-layer trunk"

    xp = x_prompt.reshape(batch * seq, D_MODEL)
    xs = x_sample.reshape(N_FFN_BLOCKS, TM_S, D_MODEL)

    h1p, h1s, xn_c = _ffn(xp, xs, norm_ffn1, ffn1_gate, ffn1_up, ffn1_down, norm_mix, final_norm=False)

    w_t = jnp.swapaxes(w_in[0], 0, 1)
    za_p, za_s, g_p, g_s = _proj(xn_c, w_t)
    yb_c, conv_p, conv_s = _conv(xn_c, w_t, conv_w[0], conv_b, norm_conv,
                                 state_conv.reshape(N_MIX_BLOCKS, MS, 2, D_B))

    bias_row = jnp.zeros((1, LANE), F32).at[0, :2 * H_A].set(b_gates[0].astype(F32))
    ha_p, c_p, n_p, m_p = _mlstm_prompt(za_p, g_p, bias_row, norm_mlstm, batch, seq)
    ha_s, m_s, n_s, c_s = _mlstm_sample(
        za_s.reshape(nb, 4 * D_A), g_s.reshape(nb, LANE), bias_row, norm_mlstm,
        state_mlstm_m.reshape(nb, H_A), state_mlstm_n.reshape(nb, H_A * DK), state_mlstm_C)

    h2p, h2s = _outproj(h1p, h1s.reshape(N_MIX_BLOCKS, MS, D_MODEL), ha_p.reshape(batch * seq, D_A),
                        ha_s.reshape(N_MIX_BLOCKS, MS, D_A), yb_c, w_out)

    yp, ys = _ffn(h2p, h2s.reshape(N_FFN_BLOCKS, TM_S, D_MODEL), norm_ffn2, ffn2_gate, ffn2_up,
                  ffn2_down, norm_final.reshape(1, D_MODEL), final_norm=True)

    return (
        yp.reshape(batch, seq, D_MODEL),
        ys.reshape(nb, 1, D_MODEL),
        c_p,
        n_p,
        m_p[:, :, 0].reshape(1, batch, H_A),
        conv_p,
        c_s,
        n_s.reshape(1, nb, H_A, DK),
        m_s.reshape(1, nb, H_A),
        conv_s.reshape(1, nb, 2, D_B),
    )
```

```python
import functools

import jax
import jax.numpy as jnp
from jax import lax
from jax.experimental import pallas as pl
from jax.experimental.pallas import tpu as pltpu

F32 = jnp.float32
BF16 = jnp.bfloat16

D_MODEL = 2048
D_A = 1024
D_B = 1024
H_A = 4
DK = 256
DV = 256
G_B = 8
D_FF = 5504
GATE_CAP = 15.0
EPS = 1e-6
GATE_ROW0 = 4 * D_A
CONV_ROW0 = 4 * D_A + 2 * H_A

LANE = 128
N_FFN_BLOCKS = 8
TM_P = 1024
TM_S = 16
TM = TM_P + TM_S
TF = 256
N_MIX_BLOCKS = N_FFN_BLOCKS // 2
MP = 2 * TM_P
MS = 2 * TM_S
PROJ_TN = 512
CONV_TC = 256
OUT_TN = 512
MLSTM_CHUNK = 256
MLSTM_SEQS = 4
SAMPLE_BS = 8
VMEM_LIMIT = 62 * 1024 * 1024


def _rms(x, g):
    return x * lax.rsqrt(jnp.mean(x * x, axis=-1, keepdims=True) + EPS) * g


def _soft_cap(x):
    return GATE_CAP * jnp.tanh(x / GATE_CAP)


def _log_sigmoid(x):
    return -jax.nn.softplus(-x)


def _bdot(a, b):
    return jnp.dot(a, b, preferred_element_type=F32)


def _bdot_t(a, b):
    return lax.dot_general(a, b, (((1,), (1,)), ((), ())), preferred_element_type=F32)


def _ffn_kernel(xp_ref, xs_ref, g_ref, wg_ref, wu_ref, wd_ref, g2_ref, *rest,
                n_f, last_valid, final_norm):
    if final_norm:
        op_ref, os_ref, xn_ref = rest
    else:
        op_ref, os_ref, nxt_ref, xn_ref = rest
    f = pl.program_id(1)

    @pl.when(f == 0)
    def _():
        xp = xp_ref[...]
        xs = xs_ref[0]
        g = g_ref[...]
        xn_ref[:TM_P, :] = _rms(xp, g).astype(BF16)
        xn_ref[TM_P:, :] = _rms(xs, g).astype(BF16)
        op_ref[...] = xp
        os_ref[0] = xs

    def step(valid):
        xn = xn_ref[...]
        a = _bdot(xn, wg_ref[:, :valid].astype(BF16))
        b = _bdot(xn, wu_ref[:, :valid].astype(BF16))
        hid = (a * jax.nn.sigmoid(a) * b * 0.5).astype(BF16)
        r = _bdot(hid, wd_ref[:valid, :].astype(BF16))
        op_ref[...] += r[:TM_P]
        os_ref[0] += r[TM_P:]

    if last_valid == TF:
        step(TF)
    else:
        @pl.when(f < n_f - 1)
        def _():
            step(TF)

        @pl.when(f == n_f - 1)
        def _():
            step(last_valid)

    @pl.when(f == n_f - 1)
    def _():
        g2 = g2_ref[...]
        if final_norm:
            op_ref[...] = _rms(op_ref[...], g2)
            os_ref[0] = _rms(os_ref[0], g2)
        else:
            nxt_ref[0, :TM_P, :] = _rms(op_ref[...], g2).astype(BF16)
            nxt_ref[0, TM_P:, :] = _rms(os_ref[0], g2).astype(BF16)


def _ffn(xp, xs, g, wg, wu, wd, g2, final_norm):
    n_f = pl.cdiv(D_FF, TF)
    last_valid = D_FF - (n_f - 1) * TF
    kern = functools.partial(_ffn_kernel, n_f=n_f, last_valid=last_valid, final_norm=final_norm)
    out_specs = [
        pl.BlockSpec((TM_P, D_MODEL), lambda i, f: (i, 0)),
        pl.BlockSpec((1, TM_S, D_MODEL), lambda i, f: (i, 0, 0)),
    ]
    out_shape = [
        jax.ShapeDtypeStruct((N_FFN_BLOCKS * TM_P, D_MODEL), F32),
        jax.ShapeDtypeStruct((N_FFN_BLOCKS, TM_S, D_MODEL), F32),
    ]
    if not final_norm:
        out_specs.append(pl.BlockSpec((1, TM, D_MODEL), lambda i, f: (i, 0, 0)))
        out_shape.append(jax.ShapeDtypeStruct((N_FFN_BLOCKS, TM, D_MODEL), BF16))
    return pl.pallas_call(
        kern,
        grid=(N_FFN_BLOCKS, n_f),
        in_specs=[
            pl.BlockSpec((TM_P, D_MODEL), lambda i, f: (i, 0)),
            pl.BlockSpec((1, TM_S, D_MODEL), lambda i, f: (i, 0, 0)),
            pl.BlockSpec((1, D_MODEL), lambda i, f: (0, 0)),
            pl.BlockSpec((None, D_MODEL, TF), lambda i, f: (0, 0, f)),
            pl.BlockSpec((None, D_MODEL, TF), lambda i, f: (0, 0, f)),
            pl.BlockSpec((None, TF, D_MODEL), lambda i, f: (0, f, 0)),
            pl.BlockSpec((1, D_MODEL), lambda i, f: (0, 0)),
        ],
        out_specs=out_specs,
        out_shape=out_shape,
        scratch_shapes=[pltpu.VMEM((TM, D_MODEL), BF16)],
        compiler_params=pltpu.CompilerParams(
            dimension_semantics=("parallel", "arbitrary"), vmem_limit_bytes=VMEM_LIMIT),
        name="ffn_final" if final_norm else "ffn",
    )(xp, xs, g, wg, wu, wd, g2)


def _split_rows(z0, z1, p_ref, s_ref):
    p_ref[:TM_P, :] = z0[:TM_P]
    p_ref[TM_P:, :] = z1[:TM_P]
    s_ref[0, :TM_S, :] = z0[TM_P:]
    s_ref[0, TM_S:, :] = z1[TM_P:]


def _proj_kernel(xn_ref, w_ref, wg_ref, zp_ref, zs_ref, gp_ref, gs_ref):
    w = w_ref[...].astype(BF16)
    _split_rows(_bdot_t(xn_ref[0], w), _bdot_t(xn_ref[1], w), zp_ref, zs_ref)

    @pl.when(pl.program_id(1) == 0)
    def _():
        wg = wg_ref[...].astype(BF16)
        _split_rows(_bdot_t(xn_ref[0], wg), _bdot_t(xn_ref[1], wg), gp_ref, gs_ref)


def _proj(xn_c, w_t):
    n_n = (4 * D_A) // PROJ_TN
    return pl.pallas_call(
        _proj_kernel,
        grid=(N_MIX_BLOCKS, n_n),
        in_specs=[
            pl.BlockSpec((2, TM, D_MODEL), lambda i, j: (i, 0, 0)),
            pl.BlockSpec((PROJ_TN, D_MODEL), lambda i, j: (j, 0)),
            pl.BlockSpec((LANE, D_MODEL), lambda i, j: (GATE_ROW0 // LANE, 0)),
        ],
        out_specs=[
            pl.BlockSpec((MP, PROJ_TN), lambda i, j: (i, j)),
            pl.BlockSpec((1, MS, PROJ_TN), lambda i, j: (i, 0, j)),
            pl.BlockSpec((MP, LANE), lambda i, j: (i, 0)),
            pl.BlockSpec((1, MS, LANE), lambda i, j: (i, 0, 0)),
        ],
        out_shape=[
            jax.ShapeDtypeStruct((N_MIX_BLOCKS * MP, 4 * D_A), F32),
            jax.ShapeDtypeStruct((N_MIX_BLOCKS, MS, 4 * D_A), F32),
            jax.ShapeDtypeStruct((N_MIX_BLOCKS * MP, LANE), F32),
            jax.ShapeDtypeStruct((N_MIX_BLOCKS, MS, LANE), F32),
        ],
        compiler_params=pltpu.CompilerParams(
            dimension_semantics=("parallel", "arbitrary"), vmem_limit_bytes=VMEM_LIMIT),
        name="proj_qkvo",
    )(xn_c, w_t, w_t)


def _group_norm(yb, nrm):
    gw = D_B // G_B
    parts = []
    for g in range(yb.shape[1] // gw):
        seg = yb[:, g * gw:(g + 1) * gw]
        parts.append(seg * lax.rsqrt(jnp.mean(seg * seg, axis=-1, keepdims=True) + EPS))
    return jnp.concatenate(parts, axis=1) * nrm


def _conv_kernel(xn_ref, wgb_ref, wgc_ref, wxc_ref, cw_ref, cb_ref, nrm_ref, buf_ref,
                 yb_ref, cp_ref, cs_ref):
    wgb = wgb_ref[...].astype(BF16)
    wgc = wgc_ref[...].astype(BF16)
    wxc = wxc_ref[...].astype(BF16)
    cw0 = cw_ref[0:1, :]
    cw1 = cw_ref[1:2, :]
    cw2 = cw_ref[2:3, :]
    cb = cb_ref[...]
    nrm = nrm_ref[...]
    row = lax.broadcasted_iota(jnp.int32, (TM_P, wgb.shape[0]), 0)
    zero_row = jnp.zeros((1, wgb.shape[0]), F32)
    um2, um1 = zero_row, zero_row

    for t in range(2):
        xn = xn_ref[t]
        gb = _bdot_t(xn, wgb)
        u = _bdot_t(xn, wgc) * _bdot_t(xn, wxc)

        up = u[:TM_P]
        u1 = jnp.where(row < 1, um1, pltpu.roll(up, 1, 0))
        u2 = jnp.where(row < 1, um2, jnp.where(row < 2, um1, pltpu.roll(up, 2, 0)))
        yc = cw0 * u2 + cw1 * u1 + cw2 * up + cb
        yb_ref[0, t * TM_P:(t + 1) * TM_P, :] = _group_norm(gb[:TM_P] * yc, nrm).astype(yb_ref.dtype)
        um2, um1 = up[TM_P - 2:TM_P - 1, :], up[TM_P - 1:TM_P, :]

        ss = slice(t * TM_S, (t + 1) * TM_S)
        us = u[TM_P:]
        b0 = buf_ref[0, ss, 0, :]
        b1 = buf_ref[0, ss, 1, :]
        ycs = cw0 * b0 + cw1 * b1 + cw2 * us + cb
        yb_ref[0, MP + t * TM_S:MP + (t + 1) * TM_S, :] = _group_norm(gb[TM_P:] * ycs, nrm).astype(yb_ref.dtype)
        cs_ref[0, ss, 0, :] = b1
        cs_ref[0, ss, 1, :] = us

    cp_ref[0, 0, 0:1, :] = um2
    cp_ref[0, 0, 1:2, :] = um1


def _conv(xn_c, w_t, cw, cb_row, nrm_row, buf):
    tc = CONV_TC
    n_c = D_B // tc
    wspec = lambda base: pl.BlockSpec(
        (pl.Element(tc), pl.Element(D_MODEL)), lambda i, c: (pl.multiple_of(base + c * tc, 8), 0))
    return pl.pallas_call(
        _conv_kernel,
        grid=(N_MIX_BLOCKS, n_c),
        in_specs=[
            pl.BlockSpec((2, TM, D_MODEL), lambda i, c: (i, 0, 0)),
            wspec(CONV_ROW0),
            wspec(CONV_ROW0 + D_B),
            wspec(CONV_ROW0 + 2 * D_B),
            pl.BlockSpec((3, tc), lambda i, c: (0, c)),
            pl.BlockSpec((1, tc), lambda i, c: (0, c)),
            pl.BlockSpec((1, tc), lambda i, c: (0, c)),
            pl.BlockSpec((1, MS, 2, tc), lambda i, c: (i, 0, 0, c)),
        ],
        out_specs=[
            pl.BlockSpec((1, MP + MS, tc), lambda i, c: (i, 0, c)),
            pl.BlockSpec((1, 1, 2, tc), lambda i, c: (0, i, 0, c)),
            pl.BlockSpec((1, MS, 2, tc), lambda i, c: (i, 0, 0, c)),
        ],
        out_shape=[
            jax.ShapeDtypeStruct((N_MIX_BLOCKS, MP + MS, D_B), BF16),
            jax.ShapeDtypeStruct((1, N_MIX_BLOCKS, 2, D_B), F32),
            jax.ShapeDtypeStruct((N_MIX_BLOCKS, MS, 2, D_B), F32),
        ],
        compiler_params=pltpu.CompilerParams(
            dimension_semantics=("parallel", "arbitrary"), vmem_limit_bytes=VMEM_LIMIT),
        name="proj_conv",
    )(xn_c, w_t, w_t, w_t, cw, cb_row, nrm_row, buf)


def _mlstm_prompt_kernel(q_ref, k_ref, v_ref, o_ref, g_ref, bias_ref, nrm_ref,
                         ha_ref, c_ref, n_ref, m_ref, cx_s, m_s, *, n_chunks):
    L = MLSTM_CHUNK
    c = pl.program_id(1)

    @pl.when(c == 0)
    def _():
        cx_s[...] = jnp.zeros_like(cx_s)
        m_s[...] = jnp.zeros_like(m_s)

    row = lax.broadcasted_iota(jnp.int32, (L, L), 0)
    col = lax.broadcasted_iota(jnp.int32, (L, L), 1)
    causal = row >= col
    tri = causal.astype(BF16)

    for b in range(MLSTM_SEQS):
        _mlstm_prompt_seq(b, causal, tri, q_ref, k_ref, v_ref, o_ref, g_ref, bias_ref, nrm_ref,
                          ha_ref, cx_s, m_s)

    @pl.when(c == n_chunks - 1)
    def _():
        for b in range(MLSTM_SEQS):
            m_ref[b] = m_s[b]
            for h in range(H_A):
                cx = cx_s[b * H_A + h]
                c_ref[0, b, h] = cx[:, :DV]
                n_ref[0, b, h:h + 1, :] = cx[:, DV:].T[h:h + 1, :]


def _prefix_max_rows(x):
    rows = x.shape[0]
    row = lax.broadcasted_iota(jnp.int32, x.shape, 0)
    k = 1
    while k < rows:
        x = jnp.where(row >= k, jnp.maximum(x, pltpu.roll(x, k, 0)), x)
        k *= 2
    return x


def _mlstm_prompt_seq(b, causal, tri, q_ref, k_ref, v_ref, o_ref, g_ref, bias_ref, nrm_ref,
                      ha_ref, cx_s, m_s):
    L = MLSTM_CHUNK
    lane = lax.broadcasted_iota(jnp.int32, (L, LANE), 1)
    lane_dv = lax.broadcasted_iota(jnp.int32, (DV, LANE), 1)
    capped = _soft_cap(g_ref[b] + bias_ref[...])
    lf = _log_sigmoid(capped)
    hi = lf.astype(BF16)
    r1 = lf - hi.astype(F32)
    mid = r1.astype(BF16)
    lo = (r1 - mid.astype(F32)).astype(BF16)
    parts = _bdot(tri, jnp.concatenate([hi, mid, lo], axis=1))
    bc = parts[:, :LANE] + parts[:, LANE:2 * LANE] + parts[:, 2 * LANE:]
    bh = pltpu.roll(bc, LANE - H_A, 1)
    a = capped - bh
    a_t = a.T
    m_prev = m_s[b]
    big_m = jnp.maximum(_prefix_max_rows(a), m_prev[0:1, :])
    sc_all = jnp.exp(m_prev[0:1, :] - big_m)
    emt_all = jnp.exp(-(bh + big_m))
    m_last = big_m[L - 1:L, :]
    w_all = jnp.exp(a - m_last)
    decay_all = sc_all[L - 1:L, :]
    m_s[b] = jnp.broadcast_to(bh[L - 1:L, :] + m_last, (8, LANE))

    nums = []
    nd_tail = jnp.zeros((L, LANE), F32)
    sq_tail = jnp.zeros((L, LANE), F32)
    for h in range(H_A):
        hs = slice(h * DK, (h + 1) * DK)
        onehot = (lane == h).astype(BF16)
        qb = q_ref[b, :, hs].astype(BF16)
        kf = k_ref[b, :, hs] * (DK ** -0.5)
        kb = kf.astype(BF16)
        vx = jnp.concatenate([v_ref[b, :, hs].astype(BF16), onehot], axis=1)
        cx_prev = cx_s[b * H_A + h]

        p = jnp.exp(jnp.where(causal, a_t[h:h + 1, :] - big_m[:, h:h + 1], -jnp.inf))
        s = _bdot_t(qb, kb) * p
        nd = sc_all[:, h:h + 1] * _bdot(qb, cx_prev.astype(BF16)) + _bdot(s.astype(BF16), vx)
        num = nd[:, :DV]
        nums.append(num)
        nd_tail = nd_tail + nd[:, DV:]
        sq_tail = sq_tail + _bdot((num * num).astype(BF16), (lane_dv == h).astype(BF16))

        wk = (w_all[:, h:h + 1] * kf).astype(BF16)
        cx_s[b * H_A + h] = decay_all[:, h:h + 1] * cx_prev + lax.dot_general(
            wk, vx, (((0,), (0,)), ((), ())), preferred_element_type=F32)

    inv = 1.0 / jnp.maximum(jnp.abs(nd_tail), emt_all)
    ms = inv * inv * sq_tail * (1.0 / DV)
    scale_all = inv * lax.rsqrt(ms + EPS)
    for h in range(H_A):
        hs = slice(h * DK, (h + 1) * DK)
        hn = nums[h] * scale_all[:, h:h + 1]
        ha_ref[b, :, hs] = (hn * nrm_ref[:, hs] * jax.nn.sigmoid(o_ref[b, :, hs])).astype(ha_ref.dtype)


def _mlstm_prompt(za_p, g_p, bias_row, nrm_row, batch, seq):
    L = MLSTM_CHUNK
    nsq = MLSTM_SEQS
    n_chunks = seq // L
    kern = functools.partial(_mlstm_prompt_kernel, n_chunks=n_chunks)
    za3 = za_p.reshape(batch, seq, 4 * D_A)
    g3 = g_p.reshape(batch, seq, LANE)
    zspec = lambda part: pl.BlockSpec((nsq, L, D_A), lambda b, c: (b, c, part))
    return pl.pallas_call(
        kern,
        grid=(batch // nsq, n_chunks),
        in_specs=[
            zspec(0), zspec(1), zspec(2), zspec(3),
            pl.BlockSpec((nsq, L, LANE), lambda b, c: (b, c, 0)),
            pl.BlockSpec((1, LANE), lambda b, c: (0, 0)),
            pl.BlockSpec((1, D_A), lambda b, c: (0, 0)),
        ],
        out_specs=[
            pl.BlockSpec((nsq, L, D_A), lambda b, c: (b, c, 0)),
            pl.BlockSpec((1, nsq, H_A, DK, DV), lambda b, c: (0, b, 0, 0, 0)),
            pl.BlockSpec((1, nsq, H_A, DK), lambda b, c: (0, b, 0, 0)),
            pl.BlockSpec((nsq, 8, LANE), lambda b, c: (b, 0, 0)),
        ],
        out_shape=[
            jax.ShapeDtypeStruct((batch, seq, D_A), BF16),
            jax.ShapeDtypeStruct((1, batch, H_A, DK, DV), F32),
            jax.ShapeDtypeStruct((1, batch, H_A, DK), F32),
            jax.ShapeDtypeStruct((batch, 8, LANE), F32),
        ],
        scratch_shapes=[pltpu.VMEM((nsq * H_A, DK, DV + LANE), F32),
                        pltpu.VMEM((nsq, 8, LANE), F32)],
        compiler_params=pltpu.CompilerParams(
            dimension_semantics=("parallel", "arbitrary"), vmem_limit_bytes=VMEM_LIMIT),
        name="mlstm_prompt",
    )(za3, za3, za3, za3, g3, bias_row, nrm_row)


def _expand_heads(x, width):
    rows = x.shape[0]
    return jnp.concatenate([jnp.broadcast_to(x[:, h:h + 1], (rows, width)) for h in range(H_A)], axis=1)


def _head_sums(x):
    return jnp.concatenate(
        [jnp.sum(x[:, h * DK:(h + 1) * DK], axis=-1, keepdims=True) for h in range(H_A)], axis=1)


def _mlstm_sample_kernel(qt_ref, kt_ref, za_ref, g_ref, bias_ref, nrm_ref, m_ref, n_ref, c_ref,
                         ha_ref, m_out, n_out, c_out, qc_ref):
    bs = SAMPLE_BS
    gz = g_ref[...] + bias_ref[...]
    capped = _soft_cap(gz)
    logi = capped[:, 0:H_A]
    logf = _log_sigmoid(capped)[:, H_A:2 * H_A]
    m_prev = m_ref[...]
    m_inter = logf + m_prev
    m_t = jnp.maximum(m_inter, logi)
    sc = jnp.exp(m_inter - m_t)
    ei = jnp.exp(logi - m_t)
    emt = jnp.exp(-m_t)

    q = za_ref[:, 0:D_A]
    k = za_ref[:, D_A:2 * D_A] * (DK ** -0.5)
    v = za_ref[:, 2 * D_A:3 * D_A]
    o = za_ref[:, 3 * D_A:4 * D_A]
    n_prev = n_ref[...]

    for j in range(bs):
        for h in range(H_A):
            hs = slice(h * DK, (h + 1) * DK)
            c_prev = c_ref[0, j, h]
            q_col = qt_ref[0, hs, j:j + 1]
            k_col = kt_ref[0, hs, j:j + 1] * (DK ** -0.5)
            qc_ref[j:j + 1, hs] = jnp.sum(q_col * c_prev, axis=0, keepdims=True)
            dec = sc[j:j + 1, h:h + 1]
            e = ei[j:j + 1, h:h + 1]
            c_out[0, j, h] = dec * c_prev + (e * k_col) * v[j:j + 1, hs]

    s = _head_sums(q * k) * ei
    den = sc * _head_sums(q * n_prev) + s
    denom = jnp.maximum(jnp.abs(den), emt)
    num = _expand_heads(sc, DV) * qc_ref[...] + _expand_heads(s, DV) * v
    hh = num / _expand_heads(denom, DV)
    ms = _head_sums(hh * hh) * (1.0 / DV)
    hn = hh * lax.rsqrt(_expand_heads(ms, DV) + EPS)
    ha_ref[...] = (hn * nrm_ref[...] * jax.nn.sigmoid(o)).astype(ha_ref.dtype)
    n_out[...] = _expand_heads(sc, DK) * n_prev + _expand_heads(ei, DK) * k
    m_out[...] = m_t


def _mlstm_sample(za_s, g_s, bias_row, nrm_row, m0, n0, c0):
    nb = za_s.shape[0]
    bs = SAMPLE_BS
    steps = nb // bs
    qt = za_s[:, 0:D_A].reshape(steps, bs, D_A).transpose(0, 2, 1)
    kt = za_s[:, D_A:2 * D_A].reshape(steps, bs, D_A).transpose(0, 2, 1)
    return pl.pallas_call(
        _mlstm_sample_kernel,
        grid=(steps,),
        in_specs=[
            pl.BlockSpec((1, D_A, bs), lambda i: (i, 0, 0)),
            pl.BlockSpec((1, D_A, bs), lambda i: (i, 0, 0)),
            pl.BlockSpec((bs, 4 * D_A), lambda i: (i, 0)),
            pl.BlockSpec((bs, LANE), lambda i: (i, 0)),
            pl.BlockSpec((1, LANE), lambda i: (0, 0)),
            pl.BlockSpec((1, D_A), lambda i: (0, 0)),
            pl.BlockSpec((bs, H_A), lambda i: (i, 0)),
            pl.BlockSpec((bs, H_A * DK), lambda i: (i, 0)),
            pl.BlockSpec((1, bs, H_A, DK, DV), lambda i: (0, i, 0, 0, 0)),
        ],
        out_specs=[
            pl.BlockSpec((bs, D_A), lambda i: (i, 0)),
            pl.BlockSpec((bs, H_A), lambda i: (i, 0)),
            pl.BlockSpec((bs, H_A * DK), lambda i: (i, 0)),
            pl.BlockSpec((1, bs, H_A, DK, DV), lambda i: (0, i, 0, 0, 0)),
        ],
        out_shape=[
            jax.ShapeDtypeStruct((nb, D_A), BF16),
            jax.ShapeDtypeStruct((nb, H_A), F32),
            jax.ShapeDtypeStruct((nb, H_A * DK), F32),
            jax.ShapeDtypeStruct((1, nb, H_A, DK, DV), F32),
        ],
        scratch_shapes=[pltpu.VMEM((bs, D_A), F32)],
        compiler_params=pltpu.CompilerParams(
            dimension_semantics=("parallel",), vmem_limit_bytes=VMEM_LIMIT),
        name="mlstm_sample",
    )(qt, kt, za_s, g_s, bias_row, nrm_row, m0, n0, c0)


def _outproj_kernel(hp_ref, hs_ref, ap_ref, as_ref, yb_ref, w_ref, op_ref, os_ref):
    wa = w_ref[:D_A, :].astype(BF16)
    wb = w_ref[D_A:, :].astype(BF16)
    op_ref[...] = hp_ref[...] + _bdot(ap_ref[...], wa) + _bdot(yb_ref[0, :MP, :], wb)
    os_ref[0] = hs_ref[0] + _bdot(as_ref[0], wa) + _bdot(yb_ref[0, MP:, :], wb)


def _outproj(hp, hs, ap, a_s, yb_c, w3):
    tn = OUT_TN
    n_n = D_MODEL // tn
    return pl.pallas_call(
        _outproj_kernel,
        grid=(N_MIX_BLOCKS, n_n),
        in_specs=[
            pl.BlockSpec((MP, tn), lambda i, j: (i, j)),
            pl.BlockSpec((1, MS, tn), lambda i, j: (i, 0, j)),
            pl.BlockSpec((MP, D_A), lambda i, j: (i, 0)),
            pl.BlockSpec((1, MS, D_A), lambda i, j: (i, 0, 0)),
            pl.BlockSpec((1, MP + MS, D_B), lambda i, j: (i, 0, 0)),
            pl.BlockSpec((None, D_A + D_B, tn), lambda i, j: (0, 0, j)),
        ],
        out_specs=[
            pl.BlockSpec((MP, tn), lambda i, j: (i, j)),
            pl.BlockSpec((1, MS, tn), lambda i, j: (i, 0, j)),
        ],
        out_shape=[
            jax.ShapeDtypeStruct((N_MIX_BLOCKS * MP, D_MODEL), F32),
            jax.ShapeDtypeStruct((N_MIX_BLOCKS, MS, D_MODEL), F32),
        ],
        compiler_params=pltpu.CompilerParams(
            dimension_semantics=("parallel", "arbitrary"), vmem_limit_bytes=VMEM_LIMIT),
        name="outproj",
    )(hp, hs, ap, a_s, yb_c, w3)


def kernel(x_prompt, x_sample, state_mlstm_C, state_mlstm_n, state_mlstm_m, state_conv, norm_ffn1, ffn1_gate, ffn1_up, ffn1_down, norm_mix, w_in, b_gates, conv_w, conv_b, norm_mlstm, norm_conv, w_out, norm_ffn2, ffn2_gate, ffn2_up, ffn2_down, norm_final):
    batch, seq, _ = x_prompt.shape
    nb = x_sample.shape[0]
    assert batch == N_MIX_BLOCKS and seq == MP and nb == N_MIX_BLOCKS * MS
    assert norm_ffn1.shape[0] == 1, "single-layer trunk"

    xp = x_prompt.reshape(batch * seq, D_MODEL)
    xs = x_sample.reshape(N_FFN_BLOCKS, TM_S, D_MODEL)

    h1p, h1s, xn_c = _ffn(xp, xs, norm_ffn1, ffn1_gate, ffn1_up, ffn1_down, norm_mix, final_norm=False)

    w_t = jnp.swapaxes(w_in[0], 0, 1)
    za_p, za_s, g_p, g_s = _proj(xn_c, w_t)
    yb_c, conv_p, conv_s = _conv(xn_c, w_t, conv_w[0], conv_b, norm_conv,
                                 state_conv.reshape(N_MIX_BLOCKS, MS, 2, D_B))

    bias_row = jnp.zeros((1, LANE), F32).at[0, :2 * H_A].set(b_gates[0].astype(F32))
    ha_p, c_p, n_p, m_p = _mlstm_prompt(za_p, g_p, bias_row, norm_mlstm, batch, seq)
    ha_s, m_s, n_s, c_s = _mlstm_sample(
        za_s.reshape(nb, 4 * D_A), g_s.reshape(nb, LANE), bias_row, norm_mlstm,
        state_mlstm_m.reshape(nb, H_A), state_mlstm_n.reshape(nb, H_A * DK), state_mlstm_C)

    h2p, h2s = _outproj(h1p, h1s.reshape(N_MIX_BLOCKS, MS, D_MODEL), ha_p.reshape(batch * seq, D_A),
                        ha_s.reshape(N_MIX_BLOCKS, MS, D_A), yb_c, w_out)

    yp, ys = _ffn(h2p, h2s.reshape(N_FFN_BLOCKS, TM_S, D_MODEL), norm_ffn2, ffn2_gate, ffn2_up,
                  ffn2_down, norm_final.reshape(1, D_MODEL), final_norm=True)

    return (
        yp.reshape(batch, seq, D_MODEL),
        ys.reshape(nb, 1, D_MODEL),
        c_p,
        n_p,
        m_p[:, 0, :H_A].reshape(1, batch, H_A),
        conv_p,
        c_s,
        n_s.reshape(1, nb, H_A, DK),
        m_s.reshape(1, nb, H_A),
        conv_s.reshape(1, nb, 2, D_B),
    )
```

```python
import functools

import jax
import jax.numpy as jnp
from jax import lax
from jax.experimental import pallas as pl
from jax.experimental.pallas import tpu as pltpu

F32 = jnp.float32
BF16 = jnp.bfloat16

D_MODEL = 2048
D_A = 1024
D_B = 1024
H_A = 4
DK = 256
DV = 256
G_B = 8
D_FF = 5504
GATE_CAP = 15.0
EPS = 1e-6
GATE_ROW0 = 4 * D_A
CONV_ROW0 = 4 * D_A + 2 * H_A

LANE = 128
N_FFN_BLOCKS = 8
TM_P = 1024
TM_S = 16
TM = TM_P + TM_S
TF = 256
N_MIX_BLOCKS = N_FFN_BLOCKS // 2
MP = 2 * TM_P
MS = 2 * TM_S
PROJ_TN = 512
CONV_TC = 256
OUT_TN = 512
MLSTM_CHUNK = 256
MLSTM_SEQS = 4
SAMPLE_BS = 8
VMEM_LIMIT = 62 * 1024 * 1024


def _rms(x, g):
    return x * lax.rsqrt(jnp.mean(x * x, axis=-1, keepdims=True) + EPS) * g


def _soft_cap(x):
    return GATE_CAP * jnp.tanh(x / GATE_CAP)


def _log_sigmoid(x):
    return -jax.nn.softplus(-x)


def _bdot(a, b):
    return jnp.dot(a, b, preferred_element_type=F32)


def _bdot_t(a, b):
    return lax.dot_general(a, b, (((1,), (1,)), ((), ())), preferred_element_type=F32)


def _sample_state_update(f, c_ref, wk_ref, v_ref, dec_ref, c_out):
    r = jnp.minimum(f, TM_S - 1)
    lane = lax.broadcasted_iota(jnp.int32, (D_A, SAMPLE_BS), 1)
    wk_col = jnp.sum(jnp.where(lane == r % SAMPLE_BS, wk_ref[0], 0.0), axis=1, keepdims=True)
    v_row = v_ref[pl.ds(r, 1), :]
    dec = dec_ref[pl.ds(r, 1), :]
    for h in range(H_A):
        hs = slice(h * DK, (h + 1) * DK)
        c_out[0, 0, h] = dec[:, h:h + 1] * c_ref[0, 0, h] + wk_col[hs, :] * v_row[:, hs]


def _ffn_kernel(xp_ref, xs_ref, g_ref, wg_ref, wu_ref, wd_ref, g2_ref, *rest,
                n_f, last_valid, final_norm):
    if final_norm:
        c_ref, wk_ref, v_ref, dec_ref, op_ref, os_ref, c_out, xn_ref = rest
    else:
        op_ref, os_ref, nxt_ref, xn_ref = rest
    f = pl.program_id(1)

    @pl.when(f == 0)
    def _():
        xp = xp_ref[...]
        xs = xs_ref[0]
        g = g_ref[...]
        xn_ref[:TM_P, :] = _rms(xp, g).astype(BF16)
        xn_ref[TM_P:, :] = _rms(xs, g).astype(BF16)
        op_ref[...] = xp
        os_ref[0] = xs

    def step(valid, with_state):
        xn = xn_ref[...]
        a = _bdot(xn, wg_ref[:, :valid].astype(BF16))
        b = _bdot(xn, wu_ref[:, :valid].astype(BF16))
        hid = (a * jax.nn.sigmoid(a) * b * 0.5).astype(BF16)
        r = _bdot(hid, wd_ref[:valid, :].astype(BF16))
        op_ref[...] += r[:TM_P]
        os_ref[0] += r[TM_P:]
        if with_state:
            _sample_state_update(f, c_ref, wk_ref, v_ref, dec_ref, c_out)

    assert last_valid < TF and n_f - 1 >= TM_S, "state update rides the full-width steps"

    @pl.when(f < n_f - 1)
    def _():
        step(TF, final_norm)

    @pl.when(f == n_f - 1)
    def _():
        step(last_valid, False)

    @pl.when(f == n_f - 1)
    def _():
        g2 = g2_ref[...]
        if final_norm:
            op_ref[...] = _rms(op_ref[...], g2)
            os_ref[0] = _rms(os_ref[0], g2)
        else:
            nxt_ref[0, :TM_P, :] = _rms(op_ref[...], g2).astype(BF16)
            nxt_ref[0, TM_P:, :] = _rms(os_ref[0], g2).astype(BF16)


def _ffn(xp, xs, g, wg, wu, wd, g2, state=None):
    final_norm = state is not None
    n_f = pl.cdiv(D_FF, TF)
    last_valid = D_FF - (n_f - 1) * TF
    kern = functools.partial(_ffn_kernel, n_f=n_f, last_valid=last_valid, final_norm=final_norm)
    in_specs = [
        pl.BlockSpec((TM_P, D_MODEL), lambda i, f: (i, 0)),
        pl.BlockSpec((1, TM_S, D_MODEL), lambda i, f: (i, 0, 0)),
        pl.BlockSpec((1, D_MODEL), lambda i, f: (0, 0)),
        pl.BlockSpec((None, D_MODEL, TF), lambda i, f: (0, 0, f)),
        pl.BlockSpec((None, D_MODEL, TF), lambda i, f: (0, 0, f)),
        pl.BlockSpec((None, TF, D_MODEL), lambda i, f: (0, f, 0)),
        pl.BlockSpec((1, D_MODEL), lambda i, f: (0, 0)),
    ]
    out_specs = [
        pl.BlockSpec((TM_P, D_MODEL), lambda i, f: (i, 0)),
        pl.BlockSpec((1, TM_S, D_MODEL), lambda i, f: (i, 0, 0)),
    ]
    out_shape = [
        jax.ShapeDtypeStruct((N_FFN_BLOCKS * TM_P, D_MODEL), F32),
        jax.ShapeDtypeStruct((N_FFN_BLOCKS, TM_S, D_MODEL), F32),
    ]
    args = [xp, xs, g, wg, wu, wd, g2]
    if final_norm:
        c0, wk, za_s, dec = state
        seq = lambda i, f: i * TM_S + jnp.minimum(f, TM_S - 1)
        cspec = pl.BlockSpec((1, 1, H_A, DK, DV), lambda i, f: (0, seq(i, f), 0, 0, 0))
        in_specs += [
            cspec,
            pl.BlockSpec((1, D_A, SAMPLE_BS), lambda i, f: (seq(i, f) // SAMPLE_BS, 0, 0)),
            pl.BlockSpec((TM_S, D_A), lambda i, f: (i, 2)),
            pl.BlockSpec((TM_S, LANE), lambda i, f: (i, 0)),
        ]
        out_specs.append(cspec)
        out_shape.append(jax.ShapeDtypeStruct(c0.shape, F32))
        args += [c0, wk, za_s, dec]
    else:
        out_specs.append(pl.BlockSpec((1, TM, D_MODEL), lambda i, f: (i, 0, 0)))
        out_shape.append(jax.ShapeDtypeStruct((N_FFN_BLOCKS, TM, D_MODEL), BF16))
    return pl.pallas_call(
        kern,
        grid=(N_FFN_BLOCKS, n_f),
        in_specs=in_specs,
        out_specs=out_specs,
        out_shape=out_shape,
        scratch_shapes=[pltpu.VMEM((TM, D_MODEL), BF16)],
        compiler_params=pltpu.CompilerParams(
            dimension_semantics=("parallel", "arbitrary"), vmem_limit_bytes=VMEM_LIMIT),
        name="ffn_final" if final_norm else "ffn",
    )(*args)


def _split_rows(z0, z1, p_ref, s_ref):
    p_ref[:TM_P, :] = z0[:TM_P]
    p_ref[TM_P:, :] = z1[:TM_P]
    s_ref[0, :TM_S, :] = z0[TM_P:]
    s_ref[0, TM_S:, :] = z1[TM_P:]


def _proj_kernel(xn_ref, w_ref, wg_ref, zp_ref, zs_ref, gp_ref, gs_ref):
    w = w_ref[...].astype(BF16)
    _split_rows(_bdot_t(xn_ref[0], w), _bdot_t(xn_ref[1], w), zp_ref, zs_ref)

    @pl.when(pl.program_id(1) == 0)
    def _():
        wg = wg_ref[...].astype(BF16)
        _split_rows(_bdot_t(xn_ref[0], wg), _bdot_t(xn_ref[1], wg), gp_ref, gs_ref)


def _proj(xn_c, w_t):
    n_n = (4 * D_A) // PROJ_TN
    return pl.pallas_call(
        _proj_kernel,
        grid=(N_MIX_BLOCKS, n_n),
        in_specs=[
            pl.BlockSpec((2, TM, D_MODEL), lambda i, j: (i, 0, 0)),
            pl.BlockSpec((PROJ_TN, D_MODEL), lambda i, j: (j, 0)),
            pl.BlockSpec((LANE, D_MODEL), lambda i, j: (GATE_ROW0 // LANE, 0)),
        ],
        out_specs=[
            pl.BlockSpec((MP, PROJ_TN), lambda i, j: (i, j)),
            pl.BlockSpec((1, MS, PROJ_TN), lambda i, j: (i, 0, j)),
            pl.BlockSpec((MP, LANE), lambda i, j: (i, 0)),
            pl.BlockSpec((1, MS, LANE), lambda i, j: (i, 0, 0)),
        ],
        out_shape=[
            jax.ShapeDtypeStruct((N_MIX_BLOCKS * MP, 4 * D_A), F32),
            jax.ShapeDtypeStruct((N_MIX_BLOCKS, MS, 4 * D_A), F32),
            jax.ShapeDtypeStruct((N_MIX_BLOCKS * MP, LANE), F32),
            jax.ShapeDtypeStruct((N_MIX_BLOCKS, MS, LANE), F32),
        ],
        compiler_params=pltpu.CompilerParams(
            dimension_semantics=("parallel", "arbitrary"), vmem_limit_bytes=VMEM_LIMIT),
        name="proj_qkvo",
    )(xn_c, w_t, w_t)


def _group_norm(yb, nrm):
    gw = D_B // G_B
    parts = []
    for g in range(yb.shape[1] // gw):
        seg = yb[:, g * gw:(g + 1) * gw]
        parts.append(seg * lax.rsqrt(jnp.mean(seg * seg, axis=-1, keepdims=True) + EPS))
    return jnp.concatenate(parts, axis=1) * nrm


def _conv_kernel(xn_ref, wgb_ref, wgc_ref, wxc_ref, cw_ref, cb_ref, nrm_ref, buf_ref,
                 yb_ref, cp_ref, cs_ref):
    wgb = wgb_ref[...].astype(BF16)
    wgc = wgc_ref[...].astype(BF16)
    wxc = wxc_ref[...].astype(BF16)
    cw0 = cw_ref[0:1, :]
    cw1 = cw_ref[1:2, :]
    cw2 = cw_ref[2:3, :]
    cb = cb_ref[...]
    nrm = nrm_ref[...]
    row = lax.broadcasted_iota(jnp.int32, (TM_P, wgb.shape[0]), 0)
    zero_row = jnp.zeros((1, wgb.shape[0]), F32)
    um2, um1 = zero_row, zero_row

    for t in range(2):
        xn = xn_ref[t]
        gb = _bdot_t(xn, wgb)
        u = _bdot_t(xn, wgc) * _bdot_t(xn, wxc)

        up = u[:TM_P]
        u1 = jnp.where(row < 1, um1, pltpu.roll(up, 1, 0))
        u2 = jnp.where(row < 1, um2, jnp.where(row < 2, um1, pltpu.roll(up, 2, 0)))
        yc = cw0 * u2 + cw1 * u1 + cw2 * up + cb
        yb_ref[0, t * TM_P:(t + 1) * TM_P, :] = _group_norm(gb[:TM_P] * yc, nrm).astype(yb_ref.dtype)
        um2, um1 = up[TM_P - 2:TM_P - 1, :], up[TM_P - 1:TM_P, :]

        ss = slice(t * TM_S, (t + 1) * TM_S)
        us = u[TM_P:]
        b0 = buf_ref[0, ss, 0, :]
        b1 = buf_ref[0, ss, 1, :]
        ycs = cw0 * b0 + cw1 * b1 + cw2 * us + cb
        yb_ref[0, MP + t * TM_S:MP + (t + 1) * TM_S, :] = _group_norm(gb[TM_P:] * ycs, nrm).astype(yb_ref.dtype)
        cs_ref[0, ss, 0, :] = b1
        cs_ref[0, ss, 1, :] = us

    cp_ref[0, 0, 0:1, :] = um2
    cp_ref[0, 0, 1:2, :] = um1


def _conv(xn_c, w_t, cw, cb_row, nrm_row, buf):
    tc = CONV_TC
    n_c = D_B // tc
    wspec = lambda base: pl.BlockSpec(
        (pl.Element(tc), pl.Element(D_MODEL)), lambda i, c: (pl.multiple_of(base + c * tc, 8), 0))
    return pl.pallas_call(
        _conv_kernel,
        grid=(N_MIX_BLOCKS, n_c),
        in_specs=[
            pl.BlockSpec((2, TM, D_MODEL), lambda i, c: (i, 0, 0)),
            wspec(CONV_ROW0),
            wspec(CONV_ROW0 + D_B),
            wspec(CONV_ROW0 + 2 * D_B),
            pl.BlockSpec((3, tc), lambda i, c: (0, c)),
            pl.BlockSpec((1, tc), lambda i, c: (0, c)),
            pl.BlockSpec((1, tc), lambda i, c: (0, c)),
            pl.BlockSpec((1, MS, 2, tc), lambda i, c: (i, 0, 0, c)),
        ],
        out_specs=[
            pl.BlockSpec((1, MP + MS, tc), lambda i, c: (i, 0, c)),
            pl.BlockSpec((1, 1, 2, tc), lambda i, c: (0, i, 0, c)),
            pl.BlockSpec((1, MS, 2, tc), lambda i, c: (i, 0, 0, c)),
        ],
        out_shape=[
            jax.ShapeDtypeStruct((N_MIX_BLOCKS, MP + MS, D_B), BF16),
            jax.ShapeDtypeStruct((1, N_MIX_BLOCKS, 2, D_B), F32),
            jax.ShapeDtypeStruct((N_MIX_BLOCKS, MS, 2, D_B), F32),
        ],
        compiler_params=pltpu.CompilerParams(
            dimension_semantics=("parallel", "arbitrary"), vmem_limit_bytes=VMEM_LIMIT),
        name="proj_conv",
    )(xn_c, w_t, w_t, w_t, cw, cb_row, nrm_row, buf)


def _mlstm_prompt_kernel(q_ref, k_ref, v_ref, o_ref, g_ref, bias_ref, nrm_ref,
                         ha_ref, c_ref, n_ref, m_ref, cx_s, m_s, *, n_chunks):
    L = MLSTM_CHUNK
    c = pl.program_id(1)

    @pl.when(c == 0)
    def _():
        cx_s[...] = jnp.zeros_like(cx_s)
        m_s[...] = jnp.zeros_like(m_s)

    row = lax.broadcasted_iota(jnp.int32, (L, L), 0)
    col = lax.broadcasted_iota(jnp.int32, (L, L), 1)
    causal = row >= col
    tri = causal.astype(BF16)

    for b in range(MLSTM_SEQS):
        _mlstm_prompt_seq(b, causal, tri, q_ref, k_ref, v_ref, o_ref, g_ref, bias_ref, nrm_ref,
                          ha_ref, cx_s, m_s)

    @pl.when(c == n_chunks - 1)
    def _():
        for b in range(MLSTM_SEQS):
            m_ref[b] = m_s[b]
            for h in range(H_A):
                cx = cx_s[b * H_A + h]
                c_ref[0, b, h] = cx[:, :DV]
                n_ref[0, b, h:h + 1, :] = cx[:, DV:].T[h:h + 1, :]


def _prefix_max_rows(x):
    rows = x.shape[0]
    row = lax.broadcasted_iota(jnp.int32, x.shape, 0)
    k = 1
    while k < rows:
        x = jnp.where(row >= k, jnp.maximum(x, pltpu.roll(x, k, 0)), x)
        k *= 2
    return x


def _mlstm_prompt_seq(b, causal, tri, q_ref, k_ref, v_ref, o_ref, g_ref, bias_ref, nrm_ref,
                      ha_ref, cx_s, m_s):
    L = MLSTM_CHUNK
    lane = lax.broadcasted_iota(jnp.int32, (L, LANE), 1)
    lane_dv = lax.broadcasted_iota(jnp.int32, (DV, LANE), 1)
    capped = _soft_cap(g_ref[b] + bias_ref[...])
    lf = _log_sigmoid(capped)
    hi = lf.astype(BF16)
    r1 = lf - hi.astype(F32)
    mid = r1.astype(BF16)
    lo = (r1 - mid.astype(F32)).astype(BF16)
    parts = _bdot(tri, jnp.concatenate([hi, mid, lo], axis=1))
    bc = parts[:, :LANE] + parts[:, LANE:2 * LANE] + parts[:, 2 * LANE:]
    bh = pltpu.roll(bc, LANE - H_A, 1)
    a = capped - bh
    a_t = a.T
    m_prev = m_s[b]
    big_m = jnp.maximum(_prefix_max_rows(a), m_prev[0:1, :])
    sc_all = jnp.exp(m_prev[0:1, :] - big_m)
    emt_all = jnp.exp(-(bh + big_m))
    m_last = big_m[L - 1:L, :]
    w_all = jnp.exp(a - m_last)
    decay_all = sc_all[L - 1:L, :]
    m_s[b] = jnp.broadcast_to(bh[L - 1:L, :] + m_last, (8, LANE))

    nums = []
    nd_tail = jnp.zeros((L, LANE), F32)
    sq_tail = jnp.zeros((L, LANE), F32)
    for h in range(H_A):
        hs = slice(h * DK, (h + 1) * DK)
        onehot = (lane == h).astype(BF16)
        qb = q_ref[b, :, hs].astype(BF16)
        kf = k_ref[b, :, hs] * (DK ** -0.5)
        kb = kf.astype(BF16)
        vx = jnp.concatenate([v_ref[b, :, hs].astype(BF16), onehot], axis=1)
        cx_prev = cx_s[b * H_A + h]

        p = jnp.exp(jnp.where(causal, a_t[h:h + 1, :] - big_m[:, h:h + 1], -jnp.inf))
        s = _bdot_t(qb, kb) * p
        nd = sc_all[:, h:h + 1] * _bdot(qb, cx_prev.astype(BF16)) + _bdot(s.astype(BF16), vx)
        num = nd[:, :DV]
        nums.append(num)
        nd_tail = nd_tail + nd[:, DV:]
        sq_tail = sq_tail + _bdot((num * num).astype(BF16), (lane_dv == h).astype(BF16))

        wk = (w_all[:, h:h + 1] * kf).astype(BF16)
        cx_s[b * H_A + h] = decay_all[:, h:h + 1] * cx_prev + lax.dot_general(
            wk, vx, (((0,), (0,)), ((), ())), preferred_element_type=F32)

    inv = 1.0 / jnp.maximum(jnp.abs(nd_tail), emt_all)
    ms = inv * inv * sq_tail * (1.0 / DV)
    scale_all = inv * lax.rsqrt(ms + EPS)
    for h in range(H_A):
        hs = slice(h * DK, (h + 1) * DK)
        hn = nums[h] * scale_all[:, h:h + 1]
        ha_ref[b, :, hs] = (hn * nrm_ref[:, hs] * jax.nn.sigmoid(o_ref[b, :, hs])).astype(ha_ref.dtype)


def _mlstm_prompt(za_p, g_p, bias_row, nrm_row, batch, seq):
    L = MLSTM_CHUNK
    nsq = MLSTM_SEQS
    n_chunks = seq // L
    kern = functools.partial(_mlstm_prompt_kernel, n_chunks=n_chunks)
    za3 = za_p.reshape(batch, seq, 4 * D_A)
    g3 = g_p.reshape(batch, seq, LANE)
    zspec = lambda part: pl.BlockSpec((nsq, L, D_A), lambda b, c: (b, c, part))
    return pl.pallas_call(
        kern,
        grid=(batch // nsq, n_chunks),
        in_specs=[
            zspec(0), zspec(1), zspec(2), zspec(3),
            pl.BlockSpec((nsq, L, LANE), lambda b, c: (b, c, 0)),
            pl.BlockSpec((1, LANE), lambda b, c: (0, 0)),
            pl.BlockSpec((1, D_A), lambda b, c: (0, 0)),
        ],
        out_specs=[
            pl.BlockSpec((nsq, L, D_A), lambda b, c: (b, c, 0)),
            pl.BlockSpec((1, nsq, H_A, DK, DV), lambda b, c: (0, b, 0, 0, 0)),
            pl.BlockSpec((1, nsq, H_A, DK), lambda b, c: (0, b, 0, 0)),
            pl.BlockSpec((nsq, 8, LANE), lambda b, c: (b, 0, 0)),
        ],
        out_shape=[
            jax.ShapeDtypeStruct((batch, seq, D_A), BF16),
            jax.ShapeDtypeStruct((1, batch, H_A, DK, DV), F32),
            jax.ShapeDtypeStruct((1, batch, H_A, DK), F32),
            jax.ShapeDtypeStruct((batch, 8, LANE), F32),
        ],
        scratch_shapes=[pltpu.VMEM((nsq * H_A, DK, DV + LANE), F32),
                        pltpu.VMEM((nsq, 8, LANE), F32)],
        compiler_params=pltpu.CompilerParams(
            dimension_semantics=("parallel", "arbitrary"), vmem_limit_bytes=VMEM_LIMIT),
        name="mlstm_prompt",
    )(za3, za3, za3, za3, g3, bias_row, nrm_row)


def _expand_heads(x, width):
    rows = x.shape[0]
    return jnp.concatenate([jnp.broadcast_to(x[:, h:h + 1], (rows, width)) for h in range(H_A)], axis=1)


def _head_sums(x):
    return jnp.concatenate(
        [jnp.sum(x[:, h * DK:(h + 1) * DK], axis=-1, keepdims=True) for h in range(H_A)], axis=1)


def _mlstm_sample_kernel(kt_ref, za_ref, g_ref, bias_ref, nrm_ref, m_ref, n_ref, c_ref,
                         ha_ref, m_out, n_out, wk_out, dec_out, qc_ref):
    bs = SAMPLE_BS
    gz = g_ref[...] + bias_ref[...]
    capped = _soft_cap(gz)
    logi = capped[:, 0:H_A]
    logf = _log_sigmoid(capped)[:, H_A:2 * H_A]
    m_prev = m_ref[...]
    m_inter = logf + m_prev
    m_t = jnp.maximum(m_inter, logi)
    sc = jnp.exp(m_inter - m_t)
    ei = jnp.exp(logi - m_t)
    emt = jnp.exp(-m_t)

    q = za_ref[:, 0:D_A]
    k = za_ref[:, D_A:2 * D_A] * (DK ** -0.5)
    v = za_ref[:, 2 * D_A:3 * D_A]
    o = za_ref[:, 3 * D_A:4 * D_A]
    n_prev = n_ref[...]

    ei_t = jnp.concatenate([ei, jnp.zeros((bs, LANE - H_A), F32)], axis=1).T
    for h in range(H_A):
        hs = slice(h * DK, (h + 1) * DK)
        qb = q[:, hs].astype(BF16)
        for j in range(bs):
            qc = _bdot(qb, c_ref[0, j, h].astype(BF16))
            qc_ref[j:j + 1, hs] = qc[j:j + 1, :]
        wk_out[0, hs, :] = kt_ref[0, hs, :] * (DK ** -0.5) * ei_t[h:h + 1, :]

    s = _head_sums(q * k) * ei
    den = sc * _head_sums(q * n_prev) + s
    denom = jnp.maximum(jnp.abs(den), emt)
    num = _expand_heads(sc, DV) * qc_ref[...] + _expand_heads(s, DV) * v
    hh = num / _expand_heads(denom, DV)
    ms = _head_sums(hh * hh) * (1.0 / DV)
    hn = hh * lax.rsqrt(_expand_heads(ms, DV) + EPS)
    ha_ref[...] = (hn * nrm_ref[...] * jax.nn.sigmoid(o)).astype(ha_ref.dtype)
    n_out[...] = _expand_heads(sc, DK) * n_prev + _expand_heads(ei, DK) * k
    m_out[...] = m_t
    dec_out[...] = jnp.concatenate([sc, jnp.zeros((bs, LANE - H_A), F32)], axis=1)


def _mlstm_sample(za_s, g_s, bias_row, nrm_row, m0, n0, c0):
    nb = za_s.shape[0]
    bs = SAMPLE_BS
    steps = nb // bs
    kt = za_s[:, D_A:2 * D_A].reshape(steps, bs, D_A).transpose(0, 2, 1)
    return pl.pallas_call(
        _mlstm_sample_kernel,
        grid=(steps,),
        in_specs=[
            pl.BlockSpec((1, D_A, bs), lambda i: (i, 0, 0)),
            pl.BlockSpec((bs, 4 * D_A), lambda i: (i, 0)),
            pl.BlockSpec((bs, LANE), lambda i: (i, 0)),
            pl.BlockSpec((1, LANE), lambda i: (0, 0)),
            pl.BlockSpec((1, D_A), lambda i: (0, 0)),
            pl.BlockSpec((bs, H_A), lambda i: (i, 0)),
            pl.BlockSpec((bs, H_A * DK), lambda i: (i, 0)),
            pl.BlockSpec((1, bs, H_A, DK, DV), lambda i: (0, i, 0, 0, 0)),
        ],
        out_specs=[
            pl.BlockSpec((bs, D_A), lambda i: (i, 0)),
            pl.BlockSpec((bs, H_A), lambda i: (i, 0)),
            pl.BlockSpec((bs, H_A * DK), lambda i: (i, 0)),
            pl.BlockSpec((1, D_A, bs), lambda i: (i, 0, 0)),
            pl.BlockSpec((bs, LANE), lambda i: (i, 0)),
        ],
        out_shape=[
            jax.ShapeDtypeStruct((nb, D_A), BF16),
            jax.ShapeDtypeStruct((nb, H_A), F32),
            jax.ShapeDtypeStruct((nb, H_A * DK), F32),
            jax.ShapeDtypeStruct((steps, D_A, bs), F32),
            jax.ShapeDtypeStruct((nb, LANE), F32),
        ],
        scratch_shapes=[pltpu.VMEM((bs, D_A), F32)],
        compiler_params=pltpu.CompilerParams(
            dimension_semantics=("parallel",), vmem_limit_bytes=VMEM_LIMIT),
        name="mlstm_sample",
    )(kt, za_s, g_s, bias_row, nrm_row, m0, n0, c0)


def _outproj_kernel(hp_ref, hs_ref, ap_ref, as_ref, yb_ref, w_ref, op_ref, os_ref):
    wa = w_ref[:D_A, :].astype(BF16)
    wb = w_ref[D_A:, :].astype(BF16)
    op_ref[...] = hp_ref[...] + _bdot(ap_ref[...], wa) + _bdot(yb_ref[0, :MP, :], wb)
    os_ref[0] = hs_ref[0] + _bdot(as_ref[0], wa) + _bdot(yb_ref[0, MP:, :], wb)


def _outproj(hp, hs, ap, a_s, yb_c, w3):
    tn = OUT_TN
    n_n = D_MODEL // tn
    return pl.pallas_call(
        _outproj_kernel,
        grid=(N_MIX_BLOCKS, n_n),
        in_specs=[
            pl.BlockSpec((MP, tn), lambda i, j: (i, j)),
            pl.BlockSpec((1, MS, tn), lambda i, j: (i, 0, j)),
            pl.BlockSpec((MP, D_A), lambda i, j: (i, 0)),
            pl.BlockSpec((1, MS, D_A), lambda i, j: (i, 0, 0)),
            pl.BlockSpec((1, MP + MS, D_B), lambda i, j: (i, 0, 0)),
            pl.BlockSpec((None, D_A + D_B, tn), lambda i, j: (0, 0, j)),
        ],
        out_specs=[
            pl.BlockSpec((MP, tn), lambda i, j: (i, j)),
            pl.BlockSpec((1, MS, tn), lambda i, j: (i, 0, j)),
        ],
        out_shape=[
            jax.ShapeDtypeStruct((N_MIX_BLOCKS * MP, D_MODEL), F32),
            jax.ShapeDtypeStruct((N_MIX_BLOCKS, MS, D_MODEL), F32),
        ],
        compiler_params=pltpu.CompilerParams(
            dimension_semantics=("parallel", "arbitrary"), vmem_limit_bytes=VMEM_LIMIT),
        name="outproj",
    )(hp, hs, ap, a_s, yb_c, w3)


def kernel(x_prompt, x_sample, state_mlstm_C, state_mlstm_n, state_mlstm_m, state_conv, norm_ffn1, ffn1_gate, ffn1_up, ffn1_down, norm_mix, w_in, b_gates, conv_w, conv_b, norm_mlstm, norm_conv, w_out, norm_ffn2, ffn2_gate, ffn2_up, ffn2_down, norm_final):
    batch, seq, _ = x_prompt.shape
    nb = x_sample.shape[0]
    assert batch == N_MIX_BLOCKS and seq == MP and nb == N_MIX_BLOCKS * MS
    assert norm_ffn1.shape[0] == 1, "single-layer trunk"

    xp = x_prompt.reshape(batch * seq, D_MODEL)
    xs = x_sample.reshape(N_FFN_BLOCKS, TM_S, D_MODEL)

    h1p, h1s, xn_c = _ffn(xp, xs, norm_ffn1, ffn1_gate, ffn1_up, ffn1_down, norm_mix)

    w_t = jnp.swapaxes(w_in[0], 0, 1)
    za_p, za_s, g_p, g_s = _proj(xn_c, w_t)
    yb_c, conv_p, conv_s = _conv(xn_c, w_t, conv_w[0], conv_b, norm_conv,
                                 state_conv.reshape(N_MIX_BLOCKS, MS, 2, D_B))

    bias_row = jnp.zeros((1, LANE), F32).at[0, :2 * H_A].set(b_gates[0].astype(F32))
    ha_p, c_p, n_p, m_p = _mlstm_prompt(za_p, g_p, bias_row, norm_mlstm, batch, seq)
    za_s = za_s.reshape(nb, 4 * D_A)
    ha_s, m_s, n_s, wk_s, dec_s = _mlstm_sample(
        za_s, g_s.reshape(nb, LANE), bias_row, norm_mlstm,
        state_mlstm_m.reshape(nb, H_A), state_mlstm_n.reshape(nb, H_A * DK), state_mlstm_C)

    h2p, h2s = _outproj(h1p, h1s.reshape(N_MIX_BLOCKS, MS, D_MODEL), ha_p.reshape(batch * seq, D_A),
                        ha_s.reshape(N_MIX_BLOCKS, MS, D_A), yb_c, w_out)

    yp, ys, c_s = _ffn(h2p, h2s.reshape(N_FFN_BLOCKS, TM_S, D_MODEL), norm_ffn2, ffn2_gate, ffn2_up,
                       ffn2_down, norm_final.reshape(1, D_MODEL),
                       state=(state_mlstm_C, wk_s, za_s, dec_s))

    return (
        yp.reshape(batch, seq, D_MODEL),
        ys.reshape(nb, 1, D_MODEL),
        c_p,
        n_p,
        m_p[:, 0, :H_A].reshape(1, batch, H_A),
        conv_p,
        c_s,
        n_s.reshape(1, nb, H_A, DK),
        m_s.reshape(1, nb, H_A),
        conv_s.reshape(1, nb, 2, D_B),
    )
```

```python
import functools

import jax
import jax.numpy as jnp
from jax import lax
from jax.experimental import pallas as pl
from jax.experimental.pallas import tpu as pltpu

F32 = jnp.float32
BF16 = jnp.bfloat16

D_MODEL = 2048
D_A = 1024
D_B = 1024
H_A = 4
DK = 256
DV = 256
G_B = 8
D_FF = 5504
GATE_CAP = 15.0
EPS = 1e-6
GATE_ROW0 = 4 * D_A
CONV_ROW0 = 4 * D_A + 2 * H_A

LANE = 128
N_FFN_BLOCKS = 8
TM_P = 1024
TM_S = 16
TM = TM_P + TM_S
TF = 256
N_MIX_BLOCKS = N_FFN_BLOCKS // 2
MP = 2 * TM_P
MS = 2 * TM_S
PROJ_TN = 1024
CONV_TC = 256
OUT_TN = 512
MLSTM_CHUNK = 256
MLSTM_SEQS = 4
SAMPLE_BS = 8
VMEM_LIMIT = 62 * 1024 * 1024


def _rms(x, g):
    return x * lax.rsqrt(jnp.mean(x * x, axis=-1, keepdims=True) + EPS) * g


def _soft_cap(x):
    return GATE_CAP * jnp.tanh(x / GATE_CAP)


def _log_sigmoid(x):
    return -jax.nn.softplus(-x)


def _bdot(a, b):
    return jnp.dot(a, b, preferred_element_type=F32)


def _bdot_t(a, b):
    return lax.dot_general(a, b, (((1,), (1,)), ((), ())), preferred_element_type=F32)


def _sample_state_update(f, c_ref, wk_ref, v_ref, dec_ref, c_out):
    r = jnp.minimum(f, TM_S - 1)
    lane = lax.broadcasted_iota(jnp.int32, (D_A, SAMPLE_BS), 1)
    wk_col = jnp.sum(jnp.where(lane == r % SAMPLE_BS, wk_ref[0], 0.0), axis=1, keepdims=True)
    v_row = v_ref[pl.ds(r, 1), :]
    dec = dec_ref[pl.ds(r, 1), :]
    for h in range(H_A):
        hs = slice(h * DK, (h + 1) * DK)
        c_out[0, 0, h] = dec[:, h:h + 1] * c_ref[0, 0, h] + wk_col[hs, :] * v_row[:, hs]


def _ffn_kernel(xp_ref, xs_ref, g_ref, wg_ref, wu_ref, wd_ref, g2_ref, *rest,
                n_f, last_valid, final_norm):
    if final_norm:
        c_ref, wk_ref, v_ref, dec_ref, op_ref, os_ref, c_out, xn_ref = rest
    else:
        op_ref, os_ref, nxt_ref, xn_ref = rest
    f = pl.program_id(1)

    def step(valid, first, with_state):
        if first:
            g = g_ref[...]
            xn_ref[:TM_P, :] = _rms(xp_ref[...], g).astype(BF16)
            xn_ref[TM_P:, :] = _rms(xs_ref[0], g).astype(BF16)
        xn = xn_ref[...]
        a = _bdot(xn, wg_ref[:, :valid].astype(BF16))
        b = _bdot(xn, wu_ref[:, :valid].astype(BF16))
        hid = (a * jax.nn.sigmoid(a) * b * 0.5).astype(BF16)
        r = _bdot(hid, wd_ref[:valid, :].astype(BF16))
        if first:
            op_ref[...] = xp_ref[...] + r[:TM_P]
            os_ref[0] = xs_ref[0] + r[TM_P:]
        else:
            op_ref[...] += r[:TM_P]
            os_ref[0] += r[TM_P:]
        if with_state:
            _sample_state_update(f, c_ref, wk_ref, v_ref, dec_ref, c_out)

    assert last_valid < TF and n_f - 1 >= TM_S, "state update rides the full-width steps"

    @pl.when(f == 0)
    def _():
        step(TF, True, final_norm)

    @pl.when(jnp.logical_and(f > 0, f < n_f - 1))
    def _():
        step(TF, False, final_norm)

    @pl.when(f == n_f - 1)
    def _():
        step(last_valid, False, False)
        g2 = g2_ref[...]
        if final_norm:
            op_ref[...] = _rms(op_ref[...], g2)
            os_ref[0] = _rms(os_ref[0], g2)
        else:
            nxt_ref[0, :TM_P, :] = _rms(op_ref[...], g2).astype(BF16)
            nxt_ref[0, TM_P:, :] = _rms(os_ref[0], g2).astype(BF16)


def _ffn(xp, xs, g, wg, wu, wd, g2, state=None):
    final_norm = state is not None
    n_f = pl.cdiv(D_FF, TF)
    last_valid = D_FF - (n_f - 1) * TF
    kern = functools.partial(_ffn_kernel, n_f=n_f, last_valid=last_valid, final_norm=final_norm)
    in_specs = [
        pl.BlockSpec((TM_P, D_MODEL), lambda i, f: (i, 0)),
        pl.BlockSpec((1, TM_S, D_MODEL), lambda i, f: (i, 0, 0)),
        pl.BlockSpec((1, D_MODEL), lambda i, f: (0, 0)),
        pl.BlockSpec((None, D_MODEL, TF), lambda i, f: (0, 0, f)),
        pl.BlockSpec((None, D_MODEL, TF), lambda i, f: (0, 0, f)),
        pl.BlockSpec((None, TF, D_MODEL), lambda i, f: (0, f, 0)),
        pl.BlockSpec((1, D_MODEL), lambda i, f: (0, 0)),
    ]
    out_specs = [
        pl.BlockSpec((TM_P, D_MODEL), lambda i, f: (i, 0)),
        pl.BlockSpec((1, TM_S, D_MODEL), lambda i, f: (i, 0, 0)),
    ]
    out_shape = [
        jax.ShapeDtypeStruct((N_FFN_BLOCKS * TM_P, D_MODEL), F32),
        jax.ShapeDtypeStruct((N_FFN_BLOCKS, TM_S, D_MODEL), F32),
    ]
    args = [xp, xs, g, wg, wu, wd, g2]
    if final_norm:
        c0, wk, za_s, dec = state
        seq = lambda i, f: i * TM_S + jnp.minimum(f, TM_S - 1)
        cspec = pl.BlockSpec((1, 1, H_A, DK, DV), lambda i, f: (0, seq(i, f), 0, 0, 0))
        in_specs += [
            cspec,
            pl.BlockSpec((1, D_A, SAMPLE_BS), lambda i, f: (seq(i, f) // SAMPLE_BS, 0, 0)),
            pl.BlockSpec((TM_S, D_A), lambda i, f: (i, 2)),
            pl.BlockSpec((TM_S, LANE), lambda i, f: (i, 0)),
        ]
        out_specs.append(cspec)
        out_shape.append(jax.ShapeDtypeStruct(c0.shape, F32))
        args += [c0, wk, za_s, dec]
    else:
        out_specs.append(pl.BlockSpec((1, TM, D_MODEL), lambda i, f: (i, 0, 0)))
        out_shape.append(jax.ShapeDtypeStruct((N_FFN_BLOCKS, TM, D_MODEL), BF16))
    return pl.pallas_call(
        kern,
        grid=(N_FFN_BLOCKS, n_f),
        in_specs=in_specs,
        out_specs=out_specs,
        out_shape=out_shape,
        scratch_shapes=[pltpu.VMEM((TM, D_MODEL), BF16)],
        compiler_params=pltpu.CompilerParams(
            dimension_semantics=("parallel", "arbitrary"), vmem_limit_bytes=VMEM_LIMIT),
        name="ffn_final" if final_norm else "ffn",
    )(*args)


def _split_rows(z0, z1, p_ref, s_ref):
    p_ref[:TM_P, :] = z0[:TM_P]
    p_ref[TM_P:, :] = z1[:TM_P]
    s_ref[0, :TM_S, :] = z0[TM_P:]
    s_ref[0, TM_S:, :] = z1[TM_P:]


def _proj_kernel(xn_ref, w_ref, wg_ref, zp_ref, zs_ref, gp_ref, gs_ref):
    w = w_ref[...].astype(BF16)
    _split_rows(_bdot_t(xn_ref[0], w), _bdot_t(xn_ref[1], w), zp_ref, zs_ref)

    @pl.when(pl.program_id(1) == 0)
    def _():
        wg = wg_ref[...].astype(BF16)
        _split_rows(_bdot_t(xn_ref[0], wg), _bdot_t(xn_ref[1], wg), gp_ref, gs_ref)


def _proj(xn_c, w_t):
    n_n = (4 * D_A) // PROJ_TN
    return pl.pallas_call(
        _proj_kernel,
        grid=(N_MIX_BLOCKS, n_n),
        in_specs=[
            pl.BlockSpec((2, TM, D_MODEL), lambda i, j: (i, 0, 0)),
            pl.BlockSpec((PROJ_TN, D_MODEL), lambda i, j: (j, 0)),
            pl.BlockSpec((LANE, D_MODEL), lambda i, j: (GATE_ROW0 // LANE, 0)),
        ],
        out_specs=[
            pl.BlockSpec((MP, PROJ_TN), lambda i, j: (i, j)),
            pl.BlockSpec((1, MS, PROJ_TN), lambda i, j: (i, 0, j)),
            pl.BlockSpec((MP, LANE), lambda i, j: (i, 0)),
            pl.BlockSpec((1, MS, LANE), lambda i, j: (i, 0, 0)),
        ],
        out_shape=[
            jax.ShapeDtypeStruct((N_MIX_BLOCKS * MP, 4 * D_A), F32),
            jax.ShapeDtypeStruct((N_MIX_BLOCKS, MS, 4 * D_A), F32),
            jax.ShapeDtypeStruct((N_MIX_BLOCKS * MP, LANE), F32),
            jax.ShapeDtypeStruct((N_MIX_BLOCKS, MS, LANE), F32),
        ],
        compiler_params=pltpu.CompilerParams(
            dimension_semantics=("parallel", "arbitrary"), vmem_limit_bytes=VMEM_LIMIT),
        name="proj_qkvo",
    )(xn_c, w_t, w_t)


def _group_norm(yb, nrm):
    gw = D_B // G_B
    parts = []
    for g in range(yb.shape[1] // gw):
        seg = yb[:, g * gw:(g + 1) * gw]
        parts.append(seg * lax.rsqrt(jnp.mean(seg * seg, axis=-1, keepdims=True) + EPS))
    return jnp.concatenate(parts, axis=1) * nrm


def _conv_kernel(xn_ref, wgb_ref, wgc_ref, wxc_ref, cw_ref, cb_ref, nrm_ref, buf_ref,
                 yb_ref, cp_ref, cs_ref):
    wgb = wgb_ref[...].astype(BF16)
    wgc = wgc_ref[...].astype(BF16)
    wxc = wxc_ref[...].astype(BF16)
    cw0 = cw_ref[0:1, :]
    cw1 = cw_ref[1:2, :]
    cw2 = cw_ref[2:3, :]
    cb = cb_ref[...]
    nrm = nrm_ref[...]
    row = lax.broadcasted_iota(jnp.int32, (TM_P, wgb.shape[0]), 0)
    zero_row = jnp.zeros((1, wgb.shape[0]), F32)
    um2, um1 = zero_row, zero_row

    for t in range(2):
        xn = xn_ref[t]
        gb = _bdot_t(xn, wgb)
        u = _bdot_t(xn, wgc) * _bdot_t(xn, wxc)

        up = u[:TM_P]
        u1 = jnp.where(row < 1, um1, pltpu.roll(up, 1, 0))
        u2 = jnp.where(row < 1, um2, jnp.where(row < 2, um1, pltpu.roll(up, 2, 0)))
        yc = cw0 * u2 + cw1 * u1 + cw2 * up + cb
        yb_ref[0, t * TM_P:(t + 1) * TM_P, :] = _group_norm(gb[:TM_P] * yc, nrm).astype(yb_ref.dtype)
        um2, um1 = up[TM_P - 2:TM_P - 1, :], up[TM_P - 1:TM_P, :]

        ss = slice(t * TM_S, (t + 1) * TM_S)
        us = u[TM_P:]
        b0 = buf_ref[0, ss, 0, :]
        b1 = buf_ref[0, ss, 1, :]
        ycs = cw0 * b0 + cw1 * b1 + cw2 * us + cb
        yb_ref[0, MP + t * TM_S:MP + (t + 1) * TM_S, :] = _group_norm(gb[TM_P:] * ycs, nrm).astype(yb_ref.dtype)
        cs_ref[0, ss, 0, :] = b1
        cs_ref[0, ss, 1, :] = us

    cp_ref[0, 0, 0:1, :] = um2
    cp_ref[0, 0, 1:2, :] = um1


def _conv(xn_c, w_t, cw, cb_row, nrm_row, buf):
    tc = CONV_TC
    n_c = D_B // tc
    wspec = lambda base: pl.BlockSpec(
        (pl.Element(tc), pl.Element(D_MODEL)), lambda i, c: (pl.multiple_of(base + c * tc, 8), 0))
    return pl.pallas_call(
        _conv_kernel,
        grid=(N_MIX_BLOCKS, n_c),
        in_specs=[
            pl.BlockSpec((2, TM, D_MODEL), lambda i, c: (i, 0, 0)),
            wspec(CONV_ROW0),
            wspec(CONV_ROW0 + D_B),
            wspec(CONV_ROW0 + 2 * D_B),
            pl.BlockSpec((3, tc), lambda i, c: (0, c)),
            pl.BlockSpec((1, tc), lambda i, c: (0, c)),
            pl.BlockSpec((1, tc), lambda i, c: (0, c)),
            pl.BlockSpec((1, MS, 2, tc), lambda i, c: (i, 0, 0, c)),
        ],
        out_specs=[
            pl.BlockSpec((1, MP + MS, tc), lambda i, c: (i, 0, c)),
            pl.BlockSpec((1, 1, 2, tc), lambda i, c: (0, i, 0, c)),
            pl.BlockSpec((1, MS, 2, tc), lambda i, c: (i, 0, 0, c)),
        ],
        out_shape=[
            jax.ShapeDtypeStruct((N_MIX_BLOCKS, MP + MS, D_B), BF16),
            jax.ShapeDtypeStruct((1, N_MIX_BLOCKS, 2, D_B), F32),
            jax.ShapeDtypeStruct((N_MIX_BLOCKS, MS, 2, D_B), F32),
        ],
        compiler_params=pltpu.CompilerParams(
            dimension_semantics=("parallel", "arbitrary"), vmem_limit_bytes=VMEM_LIMIT),
        name="proj_conv",
    )(xn_c, w_t, w_t, w_t, cw, cb_row, nrm_row, buf)


def _mlstm_prompt_kernel(q_ref, k_ref, v_ref, o_ref, g_ref, bias_ref, nrm_ref,
                         ha_ref, c_ref, n_ref, m_ref, cx_s, m_s, *, n_chunks):
    L = MLSTM_CHUNK
    c = pl.program_id(1)

    @pl.when(c == 0)
    def _():
        cx_s[...] = jnp.zeros_like(cx_s)
        m_s[...] = jnp.zeros_like(m_s)

    row = lax.broadcasted_iota(jnp.int32, (L, L), 0)
    col = lax.broadcasted_iota(jnp.int32, (L, L), 1)
    causal = row >= col
    tri = causal.astype(BF16)

    for b in range(MLSTM_SEQS):
        _mlstm_prompt_seq(b, causal, tri, q_ref, k_ref, v_ref, o_ref, g_ref, bias_ref, nrm_ref,
                          ha_ref, cx_s, m_s)

    @pl.when(c == n_chunks - 1)
    def _():
        for b in range(MLSTM_SEQS):
            m_ref[b] = m_s[b]
            for h in range(H_A):
                cx = cx_s[b * H_A + h]
                c_ref[0, b, h] = cx[:, :DV]
                n_ref[0, b, h:h + 1, :] = cx[:, DV:].T[h:h + 1, :]


def _prefix_max_rows(x):
    rows = x.shape[0]
    row = lax.broadcasted_iota(jnp.int32, x.shape, 0)
    k = 1
    while k < rows:
        x = jnp.where(row >= k, jnp.maximum(x, pltpu.roll(x, k, 0)), x)
        k *= 2
    return x


def _mlstm_prompt_seq(b, causal, tri, q_ref, k_ref, v_ref, o_ref, g_ref, bias_ref, nrm_ref,
                      ha_ref, cx_s, m_s):
    L = MLSTM_CHUNK
    lane = lax.broadcasted_iota(jnp.int32, (L, LANE), 1)
    lane_dv = lax.broadcasted_iota(jnp.int32, (DV, LANE), 1)
    capped = _soft_cap(g_ref[b] + bias_ref[...])
    lf = _log_sigmoid(capped)
    hi = lf.astype(BF16)
    r1 = lf - hi.astype(F32)
    mid = r1.astype(BF16)
    lo = (r1 - mid.astype(F32)).astype(BF16)
    parts = _bdot(tri, jnp.concatenate([hi, mid, lo], axis=1))
    bc = parts[:, :LANE] + parts[:, LANE:2 * LANE] + parts[:, 2 * LANE:]
    bh = pltpu.roll(bc, LANE - H_A, 1)
    a = capped - bh
    a_t = a.T
    m_prev = m_s[b]
    big_m = jnp.maximum(_prefix_max_rows(a), m_prev[0:1, :])
    sc_all = jnp.exp(m_prev[0:1, :] - big_m)
    emt_all = jnp.exp(-(bh + big_m))
    m_last = big_m[L - 1:L, :]
    w_all = jnp.exp(a - m_last)
    decay_all = sc_all[L - 1:L, :]
    m_s[b] = jnp.broadcast_to(bh[L - 1:L, :] + m_last, (8, LANE))

    nums = []
    nd_tail = jnp.zeros((L, LANE), F32)
    sq_tail = jnp.zeros((L, LANE), F32)
    for h in range(H_A):
        hs = slice(h * DK, (h + 1) * DK)
        onehot = (lane == h).astype(BF16)
        qb = q_ref[b, :, hs].astype(BF16)
        kf = k_ref[b, :, hs] * (DK ** -0.5)
        kb = kf.astype(BF16)
        vx = jnp.concatenate([v_ref[b, :, hs].astype(BF16), onehot], axis=1)
        cx_prev = cx_s[b * H_A + h]

        p = jnp.exp(jnp.where(causal, a_t[h:h + 1, :] - big_m[:, h:h + 1], -jnp.inf))
        s = _bdot_t(qb, kb) * p
        nd = sc_all[:, h:h + 1] * _bdot(qb, cx_prev.astype(BF16)) + _bdot(s.astype(BF16), vx)
        num = nd[:, :DV]
        nums.append(num)
        nd_tail = nd_tail + nd[:, DV:]
        sq_tail = sq_tail + _bdot((num * num).astype(BF16), (lane_dv == h).astype(BF16))

        wk = (w_all[:, h:h + 1] * kf).astype(BF16)
        cx_s[b * H_A + h] = decay_all[:, h:h + 1] * cx_prev + lax.dot_general(
            wk, vx, (((0,), (0,)), ((), ())), preferred_element_type=F32)

    inv = 1.0 / jnp.maximum(jnp.abs(nd_tail), emt_all)
    ms = inv * inv * sq_tail * (1.0 / DV)
    scale_all = inv * lax.rsqrt(ms + EPS)
    for h in range(H_A):
        hs = slice(h * DK, (h + 1) * DK)
        hn = nums[h] * scale_all[:, h:h + 1]
        ha_ref[b, :, hs] = (hn * nrm_ref[:, hs] * jax.nn.sigmoid(o_ref[b, :, hs])).astype(ha_ref.dtype)


def _mlstm_prompt(za_p, g_p, bias_row, nrm_row, batch, seq):
    L = MLSTM_CHUNK
    nsq = MLSTM_SEQS
    n_chunks = seq // L
    kern = functools.partial(_mlstm_prompt_kernel, n_chunks=n_chunks)
    za3 = za_p.reshape(batch, seq, 4 * D_A)
    g3 = g_p.reshape(batch, seq, LANE)
    zspec = lambda part: pl.BlockSpec((nsq, L, D_A), lambda b, c: (b, c, part))
    return pl.pallas_call(
        kern,
        grid=(batch // nsq, n_chunks),
        in_specs=[
            zspec(0), zspec(1), zspec(2), zspec(3),
            pl.BlockSpec((nsq, L, LANE), lambda b, c: (b, c, 0)),
            pl.BlockSpec((1, LANE), lambda b, c: (0, 0)),
            pl.BlockSpec((1, D_A), lambda b, c: (0, 0)),
        ],
        out_specs=[
            pl.BlockSpec((nsq, L, D_A), lambda b, c: (b, c, 0)),
            pl.BlockSpec((1, nsq, H_A, DK, DV), lambda b, c: (0, b, 0, 0, 0)),
            pl.BlockSpec((1, nsq, H_A, DK), lambda b, c: (0, b, 0, 0)),
            pl.BlockSpec((nsq, 8, LANE), lambda b, c: (b, 0, 0)),
        ],
        out_shape=[
            jax.ShapeDtypeStruct((batch, seq, D_A), BF16),
            jax.ShapeDtypeStruct((1, batch, H_A, DK, DV), F32),
            jax.ShapeDtypeStruct((1, batch, H_A, DK), F32),
            jax.ShapeDtypeStruct((batch, 8, LANE), F32),
        ],
        scratch_shapes=[pltpu.VMEM((nsq * H_A, DK, DV + LANE), F32),
                        pltpu.VMEM((nsq, 8, LANE), F32)],
        compiler_params=pltpu.CompilerParams(
            dimension_semantics=("parallel", "arbitrary"), vmem_limit_bytes=VMEM_LIMIT),
        name="mlstm_prompt",
    )(za3, za3, za3, za3, g3, bias_row, nrm_row)


def _expand_heads(x, width):
    rows = x.shape[0]
    return jnp.concatenate([jnp.broadcast_to(x[:, h:h + 1], (rows, width)) for h in range(H_A)], axis=1)


def _head_sums(x):
    return jnp.concatenate(
        [jnp.sum(x[:, h * DK:(h + 1) * DK], axis=-1, keepdims=True) for h in range(H_A)], axis=1)


def _mlstm_sample_kernel(kt_ref, za_ref, g_ref, bias_ref, nrm_ref, m_ref, n_ref, c_ref,
                         ha_ref, m_out, n_out, wk_out, dec_out, qc_ref):
    bs = SAMPLE_BS
    gz = g_ref[...] + bias_ref[...]
    capped = _soft_cap(gz)
    logi = capped[:, 0:H_A]
    logf = _log_sigmoid(capped)[:, H_A:2 * H_A]
    m_prev = m_ref[...]
    m_inter = logf + m_prev
    m_t = jnp.maximum(m_inter, logi)
    sc = jnp.exp(m_inter - m_t)
    ei = jnp.exp(logi - m_t)
    emt = jnp.exp(-m_t)

    q = za_ref[:, 0:D_A]
    k = za_ref[:, D_A:2 * D_A] * (DK ** -0.5)
    v = za_ref[:, 2 * D_A:3 * D_A]
    o = za_ref[:, 3 * D_A:4 * D_A]
    n_prev = n_ref[...]

    ei_t = jnp.concatenate([ei, jnp.zeros((bs, LANE - H_A), F32)], axis=1).T
    for h in range(H_A):
        hs = slice(h * DK, (h + 1) * DK)
        qb = q[:, hs].astype(BF16)
        for j in range(bs):
            qc = _bdot(qb, c_ref[0, j, h].astype(BF16))
            qc_ref[j:j + 1, hs] = qc[j:j + 1, :]
        wk_out[0, hs, :] = kt_ref[0, hs, :] * (DK ** -0.5) * ei_t[h:h + 1, :]

    s = _head_sums(q * k) * ei
    den = sc * _head_sums(q * n_prev) + s
    denom = jnp.maximum(jnp.abs(den), emt)
    num = _expand_heads(sc, DV) * qc_ref[...] + _expand_heads(s, DV) * v
    hh = num / _expand_heads(denom, DV)
    ms = _head_sums(hh * hh) * (1.0 / DV)
    hn = hh * lax.rsqrt(_expand_heads(ms, DV) + EPS)
    ha_ref[...] = (hn * nrm_ref[...] * jax.nn.sigmoid(o)).astype(ha_ref.dtype)
    n_out[...] = _expand_heads(sc, DK) * n_prev + _expand_heads(ei, DK) * k
    m_out[...] = m_t
    dec_out[...] = jnp.concatenate([sc, jnp.zeros((bs, LANE - H_A), F32)], axis=1)


def _mlstm_sample(za_s, g_s, bias_row, nrm_row, m0, n0, c0):
    nb = za_s.shape[0]
    bs = SAMPLE_BS
    steps = nb // bs
    kt = za_s[:, D_A:2 * D_A].reshape(steps, bs, D_A).transpose(0, 2, 1)
    return pl.pallas_call(
        _mlstm_sample_kernel,
        grid=(steps,),
        in_specs=[
            pl.BlockSpec((1, D_A, bs), lambda i: (i, 0, 0)),
            pl.BlockSpec((bs, 4 * D_A), lambda i: (i, 0)),
            pl.BlockSpec((bs, LANE), lambda i: (i, 0)),
            pl.BlockSpec((1, LANE), lambda i: (0, 0)),
            pl.BlockSpec((1, D_A), lambda i: (0, 0)),
            pl.BlockSpec((bs, H_A), lambda i: (i, 0)),
            pl.BlockSpec((bs, H_A * DK), lambda i: (i, 0)),
            pl.BlockSpec((1, bs, H_A, DK, DV), lambda i: (0, i, 0, 0, 0)),
        ],
        out_specs=[
            pl.BlockSpec((bs, D_A), lambda i: (i, 0)),
            pl.BlockSpec((bs, H_A), lambda i: (i, 0)),
            pl.BlockSpec((bs, H_A * DK), lambda i: (i, 0)),
            pl.BlockSpec((1, D_A, bs), lambda i: (i, 0, 0)),
            pl.BlockSpec((bs, LANE), lambda i: (i, 0)),
        ],
        out_shape=[
            jax.ShapeDtypeStruct((nb, D_A), BF16),
            jax.ShapeDtypeStruct((nb, H_A), F32),
            jax.ShapeDtypeStruct((nb, H_A * DK), F32),
            jax.ShapeDtypeStruct((steps, D_A, bs), F32),
            jax.ShapeDtypeStruct((nb, LANE), F32),
        ],
        scratch_shapes=[pltpu.VMEM((bs, D_A), F32)],
        compiler_params=pltpu.CompilerParams(
            dimension_semantics=("parallel",), vmem_limit_bytes=VMEM_LIMIT),
        name="mlstm_sample",
    )(kt, za_s, g_s, bias_row, nrm_row, m0, n0, c0)


def _outproj_kernel(hp_ref, hs_ref, ap_ref, as_ref, yb_ref, w_ref, op_ref, os_ref):
    wa = w_ref[:D_A, :].astype(BF16)
    wb = w_ref[D_A:, :].astype(BF16)
    op_ref[...] = hp_ref[...] + _bdot(ap_ref[...], wa) + _bdot(yb_ref[0, :MP, :], wb)
    os_ref[0] = hs_ref[0] + _bdot(as_ref[0], wa) + _bdot(yb_ref[0, MP:, :], wb)


def _outproj(hp, hs, ap, a_s, yb_c, w3):
    tn = OUT_TN
    n_n = D_MODEL // tn
    return pl.pallas_call(
        _outproj_kernel,
        grid=(N_MIX_BLOCKS, n_n),
        in_specs=[
            pl.BlockSpec((MP, tn), lambda i, j: (i, j)),
            pl.BlockSpec((1, MS, tn), lambda i, j: (i, 0, j)),
            pl.BlockSpec((MP, D_A), lambda i, j: (i, 0)),
            pl.BlockSpec((1, MS, D_A), lambda i, j: (i, 0, 0)),
            pl.BlockSpec((1, MP + MS, D_B), lambda i, j: (i, 0, 0)),
            pl.BlockSpec((None, D_A + D_B, tn), lambda i, j: (0, 0, j)),
        ],
        out_specs=[
            pl.BlockSpec((MP, tn), lambda i, j: (i, j)),
            pl.BlockSpec((1, MS, tn), lambda i, j: (i, 0, j)),
        ],
        out_shape=[
            jax.ShapeDtypeStruct((N_MIX_BLOCKS * MP, D_MODEL), F32),
            jax.ShapeDtypeStruct((N_MIX_BLOCKS, MS, D_MODEL), F32),
        ],
        compiler_params=pltpu.CompilerParams(
            dimension_semantics=("parallel", "arbitrary"), vmem_limit_bytes=VMEM_LIMIT),
        name="outproj",
    )(hp, hs, ap, a_s, yb_c, w3)


def kernel(x_prompt, x_sample, state_mlstm_C, state_mlstm_n, state_mlstm_m, state_conv, norm_ffn1, ffn1_gate, ffn1_up, ffn1_down, norm_mix, w_in, b_gates, conv_w, conv_b, norm_mlstm, norm_conv, w_out, norm_ffn2, ffn2_gate, ffn2_up, ffn2_down, norm_final):
    batch, seq, _ = x_prompt.shape
    nb = x_sample.shape[0]
    assert batch == N_MIX_BLOCKS and seq == MP and nb == N_MIX_BLOCKS * MS
    assert norm_ffn1.shape[0] == 1, "single-layer trunk"

    xp = x_prompt.reshape(batch * seq, D_MODEL)
    xs = x_sample.reshape(N_FFN_BLOCKS, TM_S, D_MODEL)

    h1p, h1s, xn_c = _ffn(xp, xs, norm_ffn1, ffn1_gate, ffn1_up, ffn1_down, norm_mix)

    w_t = jnp.swapaxes(w_in[0], 0, 1)
    za_p, za_s, g_p, g_s = _proj(xn_c, w_t)
    yb_c, conv_p, conv_s = _conv(xn_c, w_t, conv_w[0], conv_b, norm_conv,
                                 state_conv.reshape(N_MIX_BLOCKS, MS, 2, D_B))

    bias_row = jnp.zeros((1, LANE), F32).at[0, :2 * H_A].set(b_gates[0].astype(F32))
    ha_p, c_p, n_p, m_p = _mlstm_prompt(za_p, g_p, bias_row, norm_mlstm, batch, seq)
    za_s = za_s.reshape(nb, 4 * D_A)
    ha_s, m_s, n_s, wk_s, dec_s = _mlstm_sample(
        za_s, g_s.reshape(nb, LANE), bias_row, norm_mlstm,
        state_mlstm_m.reshape(nb, H_A), state_mlstm_n.reshape(nb, H_A * DK), state_mlstm_C)

    h2p, h2s = _outproj(h1p, h1s.reshape(N_MIX_BLOCKS, MS, D_MODEL), ha_p.reshape(batch * seq, D_A),
                        ha_s.reshape(N_MIX_BLOCKS, MS, D_A), yb_c, w_out)

    yp, ys, c_s = _ffn(h2p, h2s.reshape(N_FFN_BLOCKS, TM_S, D_MODEL), norm_ffn2, ffn2_gate, ffn2_up,
                       ffn2_down, norm_final.reshape(1, D_MODEL),
                       state=(state_mlstm_C, wk_s, za_s, dec_s))

    return (
        yp.reshape(batch, seq, D_MODEL),
        ys.reshape(nb, 1, D_MODEL),
        c_p,
        n_p,
        m_p[:, 0, :H_A].reshape(1, batch, H_A),
        conv_p,
        c_s,
        n_s.reshape(1, nb, H_A, DK),
        m_s.reshape(1, nb, H_A),
        conv_s.reshape(1, nb, 2, D_B),
    )
```

```python
import functools

import jax
import jax.numpy as jnp
from jax import lax
from jax.experimental import pallas as pl
from jax.experimental.pallas import tpu as pltpu

F32 = jnp.float32
BF16 = jnp.bfloat16

D_MODEL = 2048
D_A = 1024
D_B = 1024
H_A = 4
DK = 256
DV = 256
G_B = 8
D_FF = 5504
GATE_CAP = 15.0
EPS = 1e-6
GATE_ROW0 = 4 * D_A
CONV_ROW0 = 4 * D_A + 2 * H_A

LANE = 128
N_FFN_BLOCKS = 8
TM_P = 1024
TM_S = 16
TM = TM_P + TM_S
TF = 256
N_MIX_BLOCKS = N_FFN_BLOCKS // 2
MP = 2 * TM_P
MS = 2 * TM_S
PROJ_TN = 1024
CONV_TC = 256
OUT_TN = 512
MLSTM_CHUNK = 256
MLSTM_SEQS = 4
SAMPLE_BS = 8
VMEM_LIMIT = 62 * 1024 * 1024


def _rms(x, g):
    return x * lax.rsqrt(jnp.mean(x * x, axis=-1, keepdims=True) + EPS) * g


def _soft_cap(x):
    return GATE_CAP * jnp.tanh(x / GATE_CAP)


def _log_sigmoid(x):
    return -jax.nn.softplus(-x)


def _bdot(a, b):
    return jnp.dot(a, b, preferred_element_type=F32)


def _bdot_t(a, b):
    return lax.dot_general(a, b, (((1,), (1,)), ((), ())), preferred_element_type=F32)


def _sample_state_update(f, c_ref, wk_ref, v_ref, dec_ref, c_out):
    r = jnp.minimum(f, TM_S - 1)
    seq = pl.program_id(0) * TM_S + r
    lane = lax.broadcasted_iota(jnp.int32, (D_A, LANE), 1)
    wk_col = jnp.sum(jnp.where(lane == seq, wk_ref[...], 0.0), axis=1, keepdims=True)
    v_row = v_ref[pl.ds(r, 1), :]
    dec = dec_ref[pl.ds(r, 1), :]
    for h in range(H_A):
        hs = slice(h * DK, (h + 1) * DK)
        c_out[0, 0, h] = dec[:, h:h + 1] * c_ref[0, 0, h] + wk_col[hs, :] * v_row[:, hs]


def _ffn_kernel(xp_ref, xs_ref, g_ref, wg_ref, wu_ref, wd_ref, g2_ref, *rest,
                n_f, last_valid, final_norm):
    if final_norm:
        c_ref, wk_ref, v_ref, dec_ref, op_ref, os_ref, c_out, xn_ref = rest
    else:
        op_ref, os_ref, nxt_ref, xn_ref = rest
    f = pl.program_id(1)

    def step(valid, first, with_state):
        if first:
            g = g_ref[...]
            xn_ref[:TM_P, :] = _rms(xp_ref[...], g).astype(BF16)
            xn_ref[TM_P:, :] = _rms(xs_ref[0], g).astype(BF16)
        xn = xn_ref[...]
        a = _bdot(xn, wg_ref[:, :valid].astype(BF16))
        b = _bdot(xn, wu_ref[:, :valid].astype(BF16))
        hid = (a * jax.nn.sigmoid(a) * b * 0.5).astype(BF16)
        r = _bdot(hid, wd_ref[:valid, :].astype(BF16))
        if first:
            op_ref[...] = xp_ref[...] + r[:TM_P]
            os_ref[0] = xs_ref[0] + r[TM_P:]
        else:
            op_ref[...] += r[:TM_P]
            os_ref[0] += r[TM_P:]
        if with_state:
            _sample_state_update(f, c_ref, wk_ref, v_ref, dec_ref, c_out)

    assert last_valid < TF and n_f - 1 >= TM_S, "state update rides the full-width steps"

    @pl.when(f == 0)
    def _():
        step(TF, True, final_norm)

    @pl.when(jnp.logical_and(f > 0, f < n_f - 1))
    def _():
        step(TF, False, final_norm)

    @pl.when(f == n_f - 1)
    def _():
        step(last_valid, False, False)
        g2 = g2_ref[...]
        if final_norm:
            op_ref[...] = _rms(op_ref[...], g2)
            os_ref[0] = _rms(os_ref[0], g2)
        else:
            nxt_ref[0, :TM_P, :] = _rms(op_ref[...], g2).astype(BF16)
            nxt_ref[0, TM_P:, :] = _rms(os_ref[0], g2).astype(BF16)


def _ffn(xp, xs, g, wg, wu, wd, g2, state=None):
    final_norm = state is not None
    n_f = pl.cdiv(D_FF, TF)
    last_valid = D_FF - (n_f - 1) * TF
    kern = functools.partial(_ffn_kernel, n_f=n_f, last_valid=last_valid, final_norm=final_norm)
    in_specs = [
        pl.BlockSpec((TM_P, D_MODEL), lambda i, f: (i, 0)),
        pl.BlockSpec((1, TM_S, D_MODEL), lambda i, f: (i, 0, 0)),
        pl.BlockSpec((1, D_MODEL), lambda i, f: (0, 0)),
        pl.BlockSpec((None, D_MODEL, TF), lambda i, f: (0, 0, f)),
        pl.BlockSpec((None, D_MODEL, TF), lambda i, f: (0, 0, f)),
        pl.BlockSpec((None, TF, D_MODEL), lambda i, f: (0, f, 0)),
        pl.BlockSpec((1, D_MODEL), lambda i, f: (0, 0)),
    ]
    out_specs = [
        pl.BlockSpec((TM_P, D_MODEL), lambda i, f: (i, 0)),
        pl.BlockSpec((1, TM_S, D_MODEL), lambda i, f: (i, 0, 0)),
    ]
    out_shape = [
        jax.ShapeDtypeStruct((N_FFN_BLOCKS * TM_P, D_MODEL), F32),
        jax.ShapeDtypeStruct((N_FFN_BLOCKS, TM_S, D_MODEL), F32),
    ]
    args = [xp, xs, g, wg, wu, wd, g2]
    if final_norm:
        c0, wk, za_s, dec = state
        seq = lambda i, f: i * TM_S + jnp.minimum(f, TM_S - 1)
        cspec = pl.BlockSpec((1, 1, H_A, DK, DV), lambda i, f: (0, seq(i, f), 0, 0, 0))
        in_specs += [
            cspec,
            pl.BlockSpec((D_A, LANE), lambda i, f: (0, 0)),
            pl.BlockSpec((TM_S, D_A), lambda i, f: (i, 2)),
            pl.BlockSpec((TM_S, LANE), lambda i, f: (i, 0)),
        ]
        out_specs.append(cspec)
        out_shape.append(jax.ShapeDtypeStruct(c0.shape, F32))
        args += [c0, wk, za_s, dec]
    else:
        out_specs.append(pl.BlockSpec((1, TM, D_MODEL), lambda i, f: (i, 0, 0)))
        out_shape.append(jax.ShapeDtypeStruct((N_FFN_BLOCKS, TM, D_MODEL), BF16))
    return pl.pallas_call(
        kern,
        grid=(N_FFN_BLOCKS, n_f),
        in_specs=in_specs,
        out_specs=out_specs,
        out_shape=out_shape,
        scratch_shapes=[pltpu.VMEM((TM, D_MODEL), BF16)],
        compiler_params=pltpu.CompilerParams(
            dimension_semantics=("parallel", "arbitrary"), vmem_limit_bytes=VMEM_LIMIT),
        name="ffn_final" if final_norm else "ffn",
    )(*args)


def _split_rows(z0, z1, p_ref, s_ref):
    p_ref[:TM_P, :] = z0[:TM_P]
    p_ref[TM_P:, :] = z1[:TM_P]
    s_ref[0, :TM_S, :] = z0[TM_P:]
    s_ref[0, TM_S:, :] = z1[TM_P:]


def _proj_kernel(xn_ref, w_ref, wg_ref, zp_ref, zs_ref, gp_ref, gs_ref):
    w = w_ref[...].astype(BF16)
    _split_rows(_bdot_t(xn_ref[0], w), _bdot_t(xn_ref[1], w), zp_ref, zs_ref)

    @pl.when(pl.program_id(1) == 0)
    def _():
        wg = wg_ref[...].astype(BF16)
        _split_rows(_bdot_t(xn_ref[0], wg), _bdot_t(xn_ref[1], wg), gp_ref, gs_ref)


def _proj(xn_c, w_t):
    n_n = (4 * D_A) // PROJ_TN
    return pl.pallas_call(
        _proj_kernel,
        grid=(N_MIX_BLOCKS, n_n),
        in_specs=[
            pl.BlockSpec((2, TM, D_MODEL), lambda i, j: (i, 0, 0)),
            pl.BlockSpec((PROJ_TN, D_MODEL), lambda i, j: (j, 0)),
            pl.BlockSpec((LANE, D_MODEL), lambda i, j: (GATE_ROW0 // LANE, 0)),
        ],
        out_specs=[
            pl.BlockSpec((MP, PROJ_TN), lambda i, j: (i, j)),
            pl.BlockSpec((1, MS, PROJ_TN), lambda i, j: (i, 0, j)),
            pl.BlockSpec((MP, LANE), lambda i, j: (i, 0)),
            pl.BlockSpec((1, MS, LANE), lambda i, j: (i, 0, 0)),
        ],
        out_shape=[
            jax.ShapeDtypeStruct((N_MIX_BLOCKS * MP, 4 * D_A), F32),
            jax.ShapeDtypeStruct((N_MIX_BLOCKS, MS, 4 * D_A), F32),
            jax.ShapeDtypeStruct((N_MIX_BLOCKS * MP, LANE), F32),
            jax.ShapeDtypeStruct((N_MIX_BLOCKS, MS, LANE), F32),
        ],
        compiler_params=pltpu.CompilerParams(
            dimension_semantics=("parallel", "arbitrary"), vmem_limit_bytes=VMEM_LIMIT),
        name="proj_qkvo",
    )(xn_c, w_t, w_t)


def _group_norm(yb, nrm):
    gw = D_B // G_B
    parts = []
    for g in range(yb.shape[1] // gw):
        seg = yb[:, g * gw:(g + 1) * gw]
        parts.append(seg * lax.rsqrt(jnp.mean(seg * seg, axis=-1, keepdims=True) + EPS))
    return jnp.concatenate(parts, axis=1) * nrm


def _conv_kernel(xn_ref, wgb_ref, wgc_ref, wxc_ref, cw_ref, cb_ref, nrm_ref, buf_ref, q_ref, c_ref,
                 yb_ref, cp_ref, cs_ref, qc_ref):
    for h in range(H_A):
        hs = slice(h * DK, (h + 1) * DK)
        qb = q_ref[:, hs].astype(BF16)
        for j in range(SAMPLE_BS):
            qc = _bdot(qb, c_ref[0, j, h].astype(BF16))
            qc_ref[j:j + 1, hs] = qc[j:j + 1, :]

    wgb = wgb_ref[...].astype(BF16)
    wgc = wgc_ref[...].astype(BF16)
    wxc = wxc_ref[...].astype(BF16)
    cw0 = cw_ref[0:1, :]
    cw1 = cw_ref[1:2, :]
    cw2 = cw_ref[2:3, :]
    cb = cb_ref[...]
    nrm = nrm_ref[...]
    row = lax.broadcasted_iota(jnp.int32, (TM_P, wgb.shape[0]), 0)
    zero_row = jnp.zeros((1, wgb.shape[0]), F32)
    um2, um1 = zero_row, zero_row

    for t in range(2):
        xn = xn_ref[t]
        gb = _bdot_t(xn, wgb)
        u = _bdot_t(xn, wgc) * _bdot_t(xn, wxc)

        up = u[:TM_P]
        u1 = jnp.where(row < 1, um1, pltpu.roll(up, 1, 0))
        u2 = jnp.where(row < 1, um2, jnp.where(row < 2, um1, pltpu.roll(up, 2, 0)))
        yc = cw0 * u2 + cw1 * u1 + cw2 * up + cb
        yb_ref[0, t * TM_P:(t + 1) * TM_P, :] = _group_norm(gb[:TM_P] * yc, nrm).astype(yb_ref.dtype)
        um2, um1 = up[TM_P - 2:TM_P - 1, :], up[TM_P - 1:TM_P, :]

        ss = slice(t * TM_S, (t + 1) * TM_S)
        us = u[TM_P:]
        b0 = buf_ref[0, ss, 0, :]
        b1 = buf_ref[0, ss, 1, :]
        ycs = cw0 * b0 + cw1 * b1 + cw2 * us + cb
        yb_ref[0, MP + t * TM_S:MP + (t + 1) * TM_S, :] = _group_norm(gb[TM_P:] * ycs, nrm).astype(yb_ref.dtype)
        cs_ref[0, ss, 0, :] = b1
        cs_ref[0, ss, 1, :] = us

    cp_ref[0, 0, 0:1, :] = um2
    cp_ref[0, 0, 1:2, :] = um1


def _conv(xn_c, w_t, cw, cb_row, nrm_row, buf, za_s, c0):
    tc = CONV_TC
    n_c = D_B // tc
    assert N_MIX_BLOCKS * n_c * SAMPLE_BS == za_s.shape[0], "one sample group per grid step"
    wspec = lambda base: pl.BlockSpec(
        (pl.Element(tc), pl.Element(D_MODEL)), lambda i, c: (pl.multiple_of(base + c * tc, 8), 0))
    grp = lambda i, c: i * n_c + c
    return pl.pallas_call(
        _conv_kernel,
        grid=(N_MIX_BLOCKS, n_c),
        in_specs=[
            pl.BlockSpec((2, TM, D_MODEL), lambda i, c: (i, 0, 0)),
            wspec(CONV_ROW0),
            wspec(CONV_ROW0 + D_B),
            wspec(CONV_ROW0 + 2 * D_B),
            pl.BlockSpec((3, tc), lambda i, c: (0, c)),
            pl.BlockSpec((1, tc), lambda i, c: (0, c)),
            pl.BlockSpec((1, tc), lambda i, c: (0, c)),
            pl.BlockSpec((1, MS, 2, tc), lambda i, c: (i, 0, 0, c)),
            pl.BlockSpec((SAMPLE_BS, D_A), lambda i, c: (grp(i, c), 0)),
            pl.BlockSpec((1, SAMPLE_BS, H_A, DK, DV), lambda i, c: (0, grp(i, c), 0, 0, 0)),
        ],
        out_specs=[
            pl.BlockSpec((1, MP + MS, tc), lambda i, c: (i, 0, c)),
            pl.BlockSpec((1, 1, 2, tc), lambda i, c: (0, i, 0, c)),
            pl.BlockSpec((1, MS, 2, tc), lambda i, c: (i, 0, 0, c)),
            pl.BlockSpec((SAMPLE_BS, D_A), lambda i, c: (grp(i, c), 0)),
        ],
        out_shape=[
            jax.ShapeDtypeStruct((N_MIX_BLOCKS, MP + MS, D_B), BF16),
            jax.ShapeDtypeStruct((1, N_MIX_BLOCKS, 2, D_B), F32),
            jax.ShapeDtypeStruct((N_MIX_BLOCKS, MS, 2, D_B), F32),
            jax.ShapeDtypeStruct((za_s.shape[0], D_A), F32),
        ],
        compiler_params=pltpu.CompilerParams(
            dimension_semantics=("parallel", "arbitrary"), vmem_limit_bytes=VMEM_LIMIT),
        name="proj_conv",
    )(xn_c, w_t, w_t, w_t, cw, cb_row, nrm_row, buf, za_s, c0)


def _mlstm_prompt_kernel(q_ref, k_ref, v_ref, o_ref, g_ref, bias_ref, nrm_ref,
                         ha_ref, c_ref, n_ref, m_ref, cx_s, m_s, *, n_chunks):
    L = MLSTM_CHUNK
    c = pl.program_id(1)

    @pl.when(c == 0)
    def _():
        cx_s[...] = jnp.zeros_like(cx_s)
        m_s[...] = jnp.zeros_like(m_s)

    row = lax.broadcasted_iota(jnp.int32, (L, L), 0)
    col = lax.broadcasted_iota(jnp.int32, (L, L), 1)
    causal = row >= col
    tri = causal.astype(BF16)

    for b in range(MLSTM_SEQS):
        _mlstm_prompt_seq(b, causal, tri, q_ref, k_ref, v_ref, o_ref, g_ref, bias_ref, nrm_ref,
                          ha_ref, cx_s, m_s)

    @pl.when(c == n_chunks - 1)
    def _():
        for b in range(MLSTM_SEQS):
            m_ref[b] = m_s[b]
            for h in range(H_A):
                cx = cx_s[b * H_A + h]
                c_ref[0, b, h] = cx[:, :DV]
                n_ref[0, b, h:h + 1, :] = cx[:, DV:].T[h:h + 1, :]


def _prefix_max_rows(x):
    rows = x.shape[0]
    row = lax.broadcasted_iota(jnp.int32, x.shape, 0)
    k = 1
    while k < rows:
        x = jnp.where(row >= k, jnp.maximum(x, pltpu.roll(x, k, 0)), x)
        k *= 2
    return x


def _mlstm_prompt_seq(b, causal, tri, q_ref, k_ref, v_ref, o_ref, g_ref, bias_ref, nrm_ref,
                      ha_ref, cx_s, m_s):
    L = MLSTM_CHUNK
    lane = lax.broadcasted_iota(jnp.int32, (L, LANE), 1)
    lane_dv = lax.broadcasted_iota(jnp.int32, (DV, LANE), 1)
    capped = _soft_cap(g_ref[b] + bias_ref[...])
    lf = _log_sigmoid(capped)
    hi = lf.astype(BF16)
    r1 = lf - hi.astype(F32)
    mid = r1.astype(BF16)
    lo = (r1 - mid.astype(F32)).astype(BF16)
    parts = _bdot(tri, jnp.concatenate([hi, mid, lo], axis=1))
    bc = parts[:, :LANE] + parts[:, LANE:2 * LANE] + parts[:, 2 * LANE:]
    bh = pltpu.roll(bc, LANE - H_A, 1)
    a = capped - bh
    a_t = a.T
    m_prev = m_s[b]
    big_m = jnp.maximum(_prefix_max_rows(a), m_prev[0:1, :])
    sc_all = jnp.exp(m_prev[0:1, :] - big_m)
    emt_all = jnp.exp(-(bh + big_m))
    m_last = big_m[L - 1:L, :]
    w_all = jnp.exp(a - m_last)
    decay_all = sc_all[L - 1:L, :]
    m_s[b] = jnp.broadcast_to(bh[L - 1:L, :] + m_last, (8, LANE))

    nums = []
    nd_tail = jnp.zeros((L, LANE), F32)
    sq_tail = jnp.zeros((L, LANE), F32)
    for h in range(H_A):
        hs = slice(h * DK, (h + 1) * DK)
        onehot = (lane == h).astype(BF16)
        qb = q_ref[b, :, hs].astype(BF16)
        kf = k_ref[b, :, hs] * (DK ** -0.5)
        kb = kf.astype(BF16)
        vx = jnp.concatenate([v_ref[b, :, hs].astype(BF16), onehot], axis=1)
        cx_prev = cx_s[b * H_A + h]

        p = jnp.exp(jnp.where(causal, a_t[h:h + 1, :] - big_m[:, h:h + 1], -jnp.inf))
        s = _bdot_t(qb, kb) * p
        nd = sc_all[:, h:h + 1] * _bdot(qb, cx_prev.astype(BF16)) + _bdot(s.astype(BF16), vx)
        num = nd[:, :DV]
        nums.append(num)
        nd_tail = nd_tail + nd[:, DV:]
        sq_tail = sq_tail + _bdot((num * num).astype(BF16), (lane_dv == h).astype(BF16))

        wk = (w_all[:, h:h + 1] * kf).astype(BF16)
        cx_s[b * H_A + h] = decay_all[:, h:h + 1] * cx_prev + lax.dot_general(
            wk, vx, (((0,), (0,)), ((), ())), preferred_element_type=F32)

    inv = 1.0 / jnp.maximum(jnp.abs(nd_tail), emt_all)
    ms = inv * inv * sq_tail * (1.0 / DV)
    scale_all = inv * lax.rsqrt(ms + EPS)
    for h in range(H_A):
        hs = slice(h * DK, (h + 1) * DK)
        hn = nums[h] * scale_all[:, h:h + 1]
        ha_ref[b, :, hs] = (hn * nrm_ref[:, hs] * jax.nn.sigmoid(o_ref[b, :, hs])).astype(ha_ref.dtype)


def _mlstm_prompt(za_p, g_p, bias_row, nrm_row, batch, seq):
    L = MLSTM_CHUNK
    nsq = MLSTM_SEQS
    n_chunks = seq // L
    kern = functools.partial(_mlstm_prompt_kernel, n_chunks=n_chunks)
    za3 = za_p.reshape(batch, seq, 4 * D_A)
    g3 = g_p.reshape(batch, seq, LANE)
    zspec = lambda part: pl.BlockSpec((nsq, L, D_A), lambda b, c: (b, c, part))
    return pl.pallas_call(
        kern,
        grid=(batch // nsq, n_chunks),
        in_specs=[
            zspec(0), zspec(1), zspec(2), zspec(3),
            pl.BlockSpec((nsq, L, LANE), lambda b, c: (b, c, 0)),
            pl.BlockSpec((1, LANE), lambda b, c: (0, 0)),
            pl.BlockSpec((1, D_A), lambda b, c: (0, 0)),
        ],
        out_specs=[
            pl.BlockSpec((nsq, L, D_A), lambda b, c: (b, c, 0)),
            pl.BlockSpec((1, nsq, H_A, DK, DV), lambda b, c: (0, b, 0, 0, 0)),
            pl.BlockSpec((1, nsq, H_A, DK), lambda b, c: (0, b, 0, 0)),
            pl.BlockSpec((nsq, 8, LANE), lambda b, c: (b, 0, 0)),
        ],
        out_shape=[
            jax.ShapeDtypeStruct((batch, seq, D_A), BF16),
            jax.ShapeDtypeStruct((1, batch, H_A, DK, DV), F32),
            jax.ShapeDtypeStruct((1, batch, H_A, DK), F32),
            jax.ShapeDtypeStruct((batch, 8, LANE), F32),
        ],
        scratch_shapes=[pltpu.VMEM((nsq * H_A, DK, DV + LANE), F32),
                        pltpu.VMEM((nsq, 8, LANE), F32)],
        compiler_params=pltpu.CompilerParams(
            dimension_semantics=("parallel", "arbitrary"), vmem_limit_bytes=VMEM_LIMIT),
        name="mlstm_prompt",
    )(za3, za3, za3, za3, g3, bias_row, nrm_row)


def _expand_heads(x, width):
    rows = x.shape[0]
    return jnp.concatenate([jnp.broadcast_to(x[:, h:h + 1], (rows, width)) for h in range(H_A)], axis=1)


def _head_sums(x):
    return jnp.concatenate(
        [jnp.sum(x[:, h * DK:(h + 1) * DK], axis=-1, keepdims=True) for h in range(H_A)], axis=1)


def _mlstm_sample_kernel(za_ref, qc_ref, g_ref, bias_ref, nrm_ref, m_ref, n_ref,
                         ha_ref, m_out, n_out, wk_out, dec_out):
    nb = za_ref.shape[0]
    capped = _soft_cap(g_ref[...] + bias_ref[...])
    logi = capped[:, 0:H_A]
    logf = _log_sigmoid(capped)[:, H_A:2 * H_A]
    m_prev = m_ref[...]
    m_inter = logf + m_prev
    m_t = jnp.maximum(m_inter, logi)
    sc = jnp.exp(m_inter - m_t)
    ei = jnp.exp(logi - m_t)
    emt = jnp.exp(-m_t)

    q = za_ref[:, 0:D_A]
    k = za_ref[:, D_A:2 * D_A] * (DK ** -0.5)
    v = za_ref[:, 2 * D_A:3 * D_A]
    o = za_ref[:, 3 * D_A:4 * D_A]
    n_prev = n_ref[...]

    s = _head_sums(q * k) * ei
    den = sc * _head_sums(q * n_prev) + s
    denom = jnp.maximum(jnp.abs(den), emt)
    num = _expand_heads(sc, DV) * qc_ref[...] + _expand_heads(s, DV) * v
    hh = num / _expand_heads(denom, DV)
    ms = _head_sums(hh * hh) * (1.0 / DV)
    hn = hh * lax.rsqrt(_expand_heads(ms, DV) + EPS)
    ha_ref[...] = (hn * nrm_ref[...] * jax.nn.sigmoid(o)).astype(ha_ref.dtype)
    n_out[...] = _expand_heads(sc, DK) * n_prev + _expand_heads(ei, DK) * k
    m_out[...] = m_t
    pad = jnp.zeros((nb, LANE - H_A), F32)
    dec_out[...] = jnp.concatenate([sc, pad], axis=1)

    k_t = k.T
    ei_t = jnp.concatenate([ei, pad], axis=1).T
    for h in range(H_A):
        hs = slice(h * DK, (h + 1) * DK)
        wk_out[hs, :] = k_t[hs, :] * ei_t[h:h + 1, :]


def _mlstm_sample(za_s, qc, g_s, bias_row, nrm_row, m0, n0):
    nb = za_s.shape[0]
    assert nb == LANE, "the sequence axis is transposed onto the lanes"
    return pl.pallas_call(
        _mlstm_sample_kernel,
        out_shape=[
            jax.ShapeDtypeStruct((nb, D_A), BF16),
            jax.ShapeDtypeStruct((nb, H_A), F32),
            jax.ShapeDtypeStruct((nb, H_A * DK), F32),
            jax.ShapeDtypeStruct((D_A, nb), F32),
            jax.ShapeDtypeStruct((nb, LANE), F32),
        ],
        compiler_params=pltpu.CompilerParams(vmem_limit_bytes=VMEM_LIMIT),
        name="mlstm_sample",
    )(za_s, qc, g_s, bias_row, nrm_row, m0, n0)


def _outproj_kernel(hp_ref, hs_ref, ap_ref, as_ref, yb_ref, w_ref, op_ref, os_ref):
    wa = w_ref[:D_A, :].astype(BF16)
    wb = w_ref[D_A:, :].astype(BF16)
    op_ref[...] = hp_ref[...] + _bdot(ap_ref[...], wa) + _bdot(yb_ref[0, :MP, :], wb)
    os_ref[0] = hs_ref[0] + _bdot(as_ref[0], wa) + _bdot(yb_ref[0, MP:, :], wb)


def _outproj(hp, hs, ap, a_s, yb_c, w3):
    tn = OUT_TN
    n_n = D_MODEL // tn
    return pl.pallas_call(
        _outproj_kernel,
        grid=(N_MIX_BLOCKS, n_n),
        in_specs=[
            pl.BlockSpec((MP, tn), lambda i, j: (i, j)),
            pl.BlockSpec((1, MS, tn), lambda i, j: (i, 0, j)),
            pl.BlockSpec((MP, D_A), lambda i, j: (i, 0)),
            pl.BlockSpec((1, MS, D_A), lambda i, j: (i, 0, 0)),
            pl.BlockSpec((1, MP + MS, D_B), lambda i, j: (i, 0, 0)),
            pl.BlockSpec((None, D_A + D_B, tn), lambda i, j: (0, 0, j)),
        ],
        out_specs=[
            pl.BlockSpec((MP, tn), lambda i, j: (i, j)),
            pl.BlockSpec((1, MS, tn), lambda i, j: (i, 0, j)),
        ],
        out_shape=[
            jax.ShapeDtypeStruct((N_MIX_BLOCKS * MP, D_MODEL), F32),
            jax.ShapeDtypeStruct((N_MIX_BLOCKS, MS, D_MODEL), F32),
        ],
        compiler_params=pltpu.CompilerParams(
            dimension_semantics=("parallel", "arbitrary"), vmem_limit_bytes=VMEM_LIMIT),
        name="outproj",
    )(hp, hs, ap, a_s, yb_c, w3)


def kernel(x_prompt, x_sample, state_mlstm_C, state_mlstm_n, state_mlstm_m, state_conv, norm_ffn1, ffn1_gate, ffn1_up, ffn1_down, norm_mix, w_in, b_gates, conv_w, conv_b, norm_mlstm, norm_conv, w_out, norm_ffn2, ffn2_gate, ffn2_up, ffn2_down, norm_final):
    batch, seq, _ = x_prompt.shape
    nb = x_sample.shape[0]
    assert batch == N_MIX_BLOCKS and seq == MP and nb == N_MIX_BLOCKS * MS
    assert norm_ffn1.shape[0] == 1, "single-layer trunk"

    xp = x_prompt.reshape(batch * seq, D_MODEL)
    xs = x_sample.reshape(N_FFN_BLOCKS, TM_S, D_MODEL)

    h1p, h1s, xn_c = _ffn(xp, xs, norm_ffn1, ffn1_gate, ffn1_up, ffn1_down, norm_mix)

    w_t = jnp.swapaxes(w_in[0], 0, 1)
    za_p, za_s, g_p, g_s = _proj(xn_c, w_t)
    za_s = za_s.reshape(nb, 4 * D_A)
    yb_c, conv_p, conv_s, qc_s = _conv(xn_c, w_t, conv_w[0], conv_b, norm_conv,
                                       state_conv.reshape(N_MIX_BLOCKS, MS, 2, D_B), za_s, state_mlstm_C)

    bias_row = jnp.zeros((1, LANE), F32).at[0, :2 * H_A].set(b_gates[0].astype(F32))
    ha_p, c_p, n_p, m_p = _mlstm_prompt(za_p, g_p, bias_row, norm_mlstm, batch, seq)
    ha_s, m_s, n_s, wk_s, dec_s = _mlstm_sample(
        za_s, qc_s, g_s.reshape(nb, LANE), bias_row, norm_mlstm,
        state_mlstm_m.reshape(nb, H_A), state_mlstm_n.reshape(nb, H_A * DK))

    h2p, h2s = _outproj(h1p, h1s.reshape(N_MIX_BLOCKS, MS, D_MODEL), ha_p.reshape(batch * seq, D_A),
                        ha_s.reshape(N_MIX_BLOCKS, MS, D_A), yb_c, w_out)

    yp, ys, c_s = _ffn(h2p, h2s.reshape(N_FFN_BLOCKS, TM_S, D_MODEL), norm_ffn2, ffn2_gate, ffn2_up,
                       ffn2_down, norm_final.reshape(1, D_MODEL),
                       state=(state_mlstm_C, wk_s, za_s, dec_s))

    return (
        yp.reshape(batch, seq, D_MODEL),
        ys.reshape(nb, 1, D_MODEL),
        c_p,
        n_p,
        m_p[:, 0, :H_A].reshape(1, batch, H_A),
        conv_p,
        c_s,
        n_s.reshape(1, nb, H_A, DK),
        m_s.reshape(1, nb, H_A),
        conv_s.reshape(1, nb, 2, D_B),
    )
```

```python
import functools

import jax
import jax.numpy as jnp
from jax import lax
from jax.experimental import pallas as pl
from jax.experimental.pallas import tpu as pltpu

F32 = jnp.float32
BF16 = jnp.bfloat16

D_MODEL = 2048
D_A = 1024
D_B = 1024
H_A = 4
DK = 256
DV = 256
G_B = 8
D_FF = 5504
GATE_CAP = 15.0
EPS = 1e-6
GATE_ROW0 = 4 * D_A
CONV_ROW0 = 4 * D_A + 2 * H_A

LANE = 128
N_FFN_BLOCKS = 8
TM_P = 1024
TM_S = 16
TM = TM_P + TM_S
TF = 256
N_MIX_BLOCKS = N_FFN_BLOCKS // 2
MP = 2 * TM_P
MS = 2 * TM_S
PROJ_TN = 1024
CONV_TC = 256
OUT_TN = 512
MLSTM_CHUNK = 256
MLSTM_SEQS = 4
SAMPLE_BS = 8
VMEM_LIMIT = 62 * 1024 * 1024


def _rms(x, g):
    return x * lax.rsqrt(jnp.mean(x * x, axis=-1, keepdims=True) + EPS) * g


def _soft_cap(x):
    return GATE_CAP * jnp.tanh(x / GATE_CAP)


def _log_sigmoid(x):
    return -jax.nn.softplus(-x)


def _bdot(a, b):
    return jnp.dot(a, b, preferred_element_type=F32)


def _bdot_t(a, b):
    return lax.dot_general(a, b, (((1,), (1,)), ((), ())), preferred_element_type=F32)


def _sample_state_update(f, c_ref, wk_ref, v_ref, dec_ref, c_out):
    r = jnp.minimum(f, TM_S - 1)
    seq = pl.program_id(0) * TM_S + r
    lane = lax.broadcasted_iota(jnp.int32, (D_A, LANE), 1)
    wk_col = jnp.sum(jnp.where(lane == seq, wk_ref[...], 0.0), axis=1, keepdims=True)
    v_row = v_ref[pl.ds(r, 1), :]
    dec = dec_ref[pl.ds(r, 1), :]
    for h in range(H_A):
        hs = slice(h * DK, (h + 1) * DK)
        c_out[0, 0, h] = dec[:, h:h + 1] * c_ref[0, 0, h] + wk_col[hs, :] * v_row[:, hs]


def _ffn_kernel(xp_ref, xs_ref, g_ref, wg_ref, wu_ref, wd_ref, g2_ref, *rest,
                n_f, last_valid, final_norm):
    if final_norm:
        c_ref, wk_ref, v_ref, dec_ref, op_ref, os_ref, c_out, xn_ref = rest
    else:
        op_ref, os_ref, nxt_ref, xn_ref = rest
    f = pl.program_id(1)

    def step(valid, first, with_state):
        if first:
            g = g_ref[...]
            xn_ref[:TM_P, :] = _rms(xp_ref[...], g).astype(BF16)
            xn_ref[TM_P:, :] = _rms(xs_ref[0], g).astype(BF16)
        xn = xn_ref[...]
        a = _bdot(xn, wg_ref[:, :valid].astype(BF16))
        b = _bdot(xn, wu_ref[:, :valid].astype(BF16))
        hid = (a * jax.nn.sigmoid(a) * b * 0.5).astype(BF16)
        r = _bdot(hid, wd_ref[:valid, :].astype(BF16))
        if first:
            op_ref[...] = xp_ref[...] + r[:TM_P]
            os_ref[0] = xs_ref[0] + r[TM_P:]
        else:
            op_ref[...] += r[:TM_P]
            os_ref[0] += r[TM_P:]
        if with_state:
            _sample_state_update(f, c_ref, wk_ref, v_ref, dec_ref, c_out)

    assert last_valid < TF and n_f - 1 >= TM_S, "state update rides the full-width steps"

    @pl.when(f == 0)
    def _():
        step(TF, True, final_norm)

    @pl.when(jnp.logical_and(f > 0, f < n_f - 1))
    def _():
        step(TF, False, final_norm)

    @pl.when(f == n_f - 1)
    def _():
        step(last_valid, False, False)
        g2 = g2_ref[...]
        if final_norm:
            op_ref[...] = _rms(op_ref[...], g2)
            os_ref[0] = _rms(os_ref[0], g2)
        else:
            nxt_ref[0, :TM_P, :] = _rms(op_ref[...], g2).astype(BF16)
            nxt_ref[0, TM_P:, :] = _rms(os_ref[0], g2).astype(BF16)


def _ffn(xp, xs, g, wg, wu, wd, g2, state=None):
    final_norm = state is not None
    n_f = pl.cdiv(D_FF, TF)
    last_valid = D_FF - (n_f - 1) * TF
    kern = functools.partial(_ffn_kernel, n_f=n_f, last_valid=last_valid, final_norm=final_norm)
    in_specs = [
        pl.BlockSpec((TM_P, D_MODEL), lambda i, f: (i, 0)),
        pl.BlockSpec((1, TM_S, D_MODEL), lambda i, f: (i, 0, 0)),
        pl.BlockSpec((1, D_MODEL), lambda i, f: (0, 0)),
        pl.BlockSpec((None, D_MODEL, TF), lambda i, f: (0, 0, f)),
        pl.BlockSpec((None, D_MODEL, TF), lambda i, f: (0, 0, f)),
        pl.BlockSpec((None, TF, D_MODEL), lambda i, f: (0, f, 0)),
        pl.BlockSpec((1, D_MODEL), lambda i, f: (0, 0)),
    ]
    out_specs = [
        pl.BlockSpec((TM_P, D_MODEL), lambda i, f: (i, 0)),
        pl.BlockSpec((1, TM_S, D_MODEL), lambda i, f: (i, 0, 0)),
    ]
    out_shape = [
        jax.ShapeDtypeStruct((N_FFN_BLOCKS * TM_P, D_MODEL), F32),
        jax.ShapeDtypeStruct((N_FFN_BLOCKS, TM_S, D_MODEL), F32),
    ]
    args = [xp, xs, g, wg, wu, wd, g2]
    if final_norm:
        c0, wk, za_s, dec = state
        seq = lambda i, f: i * TM_S + jnp.minimum(f, TM_S - 1)
        cspec = pl.BlockSpec((1, 1, H_A, DK, DV), lambda i, f: (0, seq(i, f), 0, 0, 0))
        in_specs += [
            cspec,
            pl.BlockSpec((D_A, LANE), lambda i, f: (0, 0)),
            pl.BlockSpec((TM_S, D_A), lambda i, f: (i, 2)),
            pl.BlockSpec((TM_S, LANE), lambda i, f: (i, 0)),
        ]
        out_specs.append(cspec)
        out_shape.append(jax.ShapeDtypeStruct(c0.shape, F32))
        args += [c0, wk, za_s, dec]
    else:
        out_specs.append(pl.BlockSpec((1, TM, D_MODEL), lambda i, f: (i, 0, 0)))
        out_shape.append(jax.ShapeDtypeStruct((N_FFN_BLOCKS, TM, D_MODEL), BF16))
    return pl.pallas_call(
        kern,
        grid=(N_FFN_BLOCKS, n_f),
        in_specs=in_specs,
        out_specs=out_specs,
        out_shape=out_shape,
        scratch_shapes=[pltpu.VMEM((TM, D_MODEL), BF16)],
        compiler_params=pltpu.CompilerParams(
            dimension_semantics=("parallel", "arbitrary"), vmem_limit_bytes=VMEM_LIMIT),
        name="ffn_final" if final_norm else "ffn",
    )(*args)


def _split_rows(z0, z1, p_ref, s_ref):
    p_ref[:TM_P, :] = z0[:TM_P]
    p_ref[TM_P:, :] = z1[:TM_P]
    s_ref[0, :TM_S, :] = z0[TM_P:]
    s_ref[0, TM_S:, :] = z1[TM_P:]


def _proj_kernel(xn_ref, w_ref, wg_ref, zp_ref, zs_ref, gp_ref, gs_ref):
    w = w_ref[...].astype(BF16)
    _split_rows(_bdot_t(xn_ref[0], w), _bdot_t(xn_ref[1], w), zp_ref, zs_ref)

    @pl.when(pl.program_id(1) == 0)
    def _():
        wg = wg_ref[...].astype(BF16)
        _split_rows(_bdot_t(xn_ref[0], wg), _bdot_t(xn_ref[1], wg), gp_ref, gs_ref)


def _proj(xn_c, w_t):
    n_n = (4 * D_A) // PROJ_TN
    return pl.pallas_call(
        _proj_kernel,
        grid=(N_MIX_BLOCKS, n_n),
        in_specs=[
            pl.BlockSpec((2, TM, D_MODEL), lambda i, j: (i, 0, 0)),
            pl.BlockSpec((PROJ_TN, D_MODEL), lambda i, j: (j, 0)),
            pl.BlockSpec((LANE, D_MODEL), lambda i, j: (GATE_ROW0 // LANE, 0)),
        ],
        out_specs=[
            pl.BlockSpec((MP, PROJ_TN), lambda i, j: (i, j)),
            pl.BlockSpec((1, MS, PROJ_TN), lambda i, j: (i, 0, j)),
            pl.BlockSpec((MP, LANE), lambda i, j: (i, 0)),
            pl.BlockSpec((1, MS, LANE), lambda i, j: (i, 0, 0)),
        ],
        out_shape=[
            jax.ShapeDtypeStruct((N_MIX_BLOCKS * MP, 4 * D_A), F32),
            jax.ShapeDtypeStruct((N_MIX_BLOCKS, MS, 4 * D_A), F32),
            jax.ShapeDtypeStruct((N_MIX_BLOCKS * MP, LANE), F32),
            jax.ShapeDtypeStruct((N_MIX_BLOCKS, MS, LANE), F32),
        ],
        compiler_params=pltpu.CompilerParams(
            dimension_semantics=("parallel", "arbitrary"), vmem_limit_bytes=VMEM_LIMIT),
        name="proj_qkvo",
    )(xn_c, w_t, w_t)


def _group_norm(yb, nrm):
    gw = D_B // G_B
    parts = []
    for g in range(yb.shape[1] // gw):
        seg = yb[:, g * gw:(g + 1) * gw]
        parts.append(seg * lax.rsqrt(jnp.mean(seg * seg, axis=-1, keepdims=True) + EPS))
    return jnp.concatenate(parts, axis=1) * nrm


def _conv_kernel(xn_ref, wgb_ref, wgc_ref, wxc_ref, cw_ref, cb_ref, nrm_ref, buf_ref, q_ref, c_ref,
                 yb_ref, cp_ref, cs_ref, qc_ref):
    for h in range(H_A):
        hs = slice(h * DK, (h + 1) * DK)
        qb = q_ref[:, hs].astype(BF16)
        for j in range(SAMPLE_BS):
            qc = _bdot(qb, c_ref[0, j, h].astype(BF16))
            qc_ref[j:j + 1, hs] = qc[j:j + 1, :]

    wgb = wgb_ref[...].astype(BF16)
    wgc = wgc_ref[...].astype(BF16)
    wxc = wxc_ref[...].astype(BF16)
    cw0 = cw_ref[0:1, :]
    cw1 = cw_ref[1:2, :]
    cw2 = cw_ref[2:3, :]
    cb = cb_ref[...]
    nrm = nrm_ref[...]
    row = lax.broadcasted_iota(jnp.int32, (TM_P, wgb.shape[0]), 0)
    zero_row = jnp.zeros((1, wgb.shape[0]), F32)
    um2, um1 = zero_row, zero_row

    for t in range(2):
        xn = xn_ref[t]
        gb = _bdot_t(xn, wgb)
        u = _bdot_t(xn, wgc) * _bdot_t(xn, wxc)

        up = u[:TM_P]
        u1 = jnp.where(row < 1, um1, pltpu.roll(up, 1, 0))
        u2 = jnp.where(row < 1, um2, jnp.where(row < 2, um1, pltpu.roll(up, 2, 0)))
        yc = cw0 * u2 + cw1 * u1 + cw2 * up + cb
        yb_ref[0, t * TM_P:(t + 1) * TM_P, :] = _group_norm(gb[:TM_P] * yc, nrm).astype(yb_ref.dtype)
        um2, um1 = up[TM_P - 2:TM_P - 1, :], up[TM_P - 1:TM_P, :]

        ss = slice(t * TM_S, (t + 1) * TM_S)
        us = u[TM_P:]
        b0 = buf_ref[0, ss, 0, :]
        b1 = buf_ref[0, ss, 1, :]
        ycs = cw0 * b0 + cw1 * b1 + cw2 * us + cb
        yb_ref[0, MP + t * TM_S:MP + (t + 1) * TM_S, :] = _group_norm(gb[TM_P:] * ycs, nrm).astype(yb_ref.dtype)
        cs_ref[0, ss, 0, :] = b1
        cs_ref[0, ss, 1, :] = us

    cp_ref[0, 0, 0:1, :] = um2
    cp_ref[0, 0, 1:2, :] = um1


def _conv(xn_c, w_t, cw, cb_row, nrm_row, buf, za_s, c0):
    tc = CONV_TC
    n_c = D_B // tc
    assert N_MIX_BLOCKS * n_c * SAMPLE_BS == za_s.shape[0], "one sample group per grid step"
    wspec = lambda base: pl.BlockSpec(
        (pl.Element(tc), pl.Element(D_MODEL)), lambda i, c: (pl.multiple_of(base + c * tc, 8), 0))
    grp = lambda i, c: i * n_c + c
    return pl.pallas_call(
        _conv_kernel,
        grid=(N_MIX_BLOCKS, n_c),
        in_specs=[
            pl.BlockSpec((2, TM, D_MODEL), lambda i, c: (i, 0, 0)),
            wspec(CONV_ROW0),
            wspec(CONV_ROW0 + D_B),
            wspec(CONV_ROW0 + 2 * D_B),
            pl.BlockSpec((3, tc), lambda i, c: (0, c)),
            pl.BlockSpec((1, tc), lambda i, c: (0, c)),
            pl.BlockSpec((1, tc), lambda i, c: (0, c)),
            pl.BlockSpec((1, MS, 2, tc), lambda i, c: (i, 0, 0, c)),
            pl.BlockSpec((SAMPLE_BS, D_A), lambda i, c: (grp(i, c), 0)),
            pl.BlockSpec((1, SAMPLE_BS, H_A, DK, DV), lambda i, c: (0, grp(i, c), 0, 0, 0)),
        ],
        out_specs=[
            pl.BlockSpec((1, MP + MS, tc), lambda i, c: (i, 0, c)),
            pl.BlockSpec((1, 1, 2, tc), lambda i, c: (0, i, 0, c)),
            pl.BlockSpec((1, MS, 2, tc), lambda i, c: (i, 0, 0, c)),
            pl.BlockSpec((SAMPLE_BS, D_A), lambda i, c: (grp(i, c), 0)),
        ],
        out_shape=[
            jax.ShapeDtypeStruct((N_MIX_BLOCKS, MP + MS, D_B), BF16),
            jax.ShapeDtypeStruct((1, N_MIX_BLOCKS, 2, D_B), F32),
            jax.ShapeDtypeStruct((N_MIX_BLOCKS, MS, 2, D_B), F32),
            jax.ShapeDtypeStruct((za_s.shape[0], D_A), F32),
        ],
        compiler_params=pltpu.CompilerParams(
            dimension_semantics=("parallel", "arbitrary"), vmem_limit_bytes=VMEM_LIMIT),
        name="proj_conv",
    )(xn_c, w_t, w_t, w_t, cw, cb_row, nrm_row, buf, za_s, c0)


def _mlstm_prompt_kernel(q_ref, k_ref, v_ref, o_ref, g_ref, bias_ref, nrm_ref,
                         ha_ref, c_ref, n_ref, m_ref, cx_s, m_s, *, n_chunks):
    L = MLSTM_CHUNK
    c = pl.program_id(1)

    @pl.when(c == 0)
    def _():
        cx_s[...] = jnp.zeros_like(cx_s)
        m_s[...] = jnp.zeros_like(m_s)

    row = lax.broadcasted_iota(jnp.int32, (L, L), 0)
    col = lax.broadcasted_iota(jnp.int32, (L, L), 1)
    causal = row >= col
    tri = causal.astype(BF16)

    for b in range(MLSTM_SEQS):
        _mlstm_prompt_seq(b, causal, tri, q_ref, k_ref, v_ref, o_ref, g_ref, bias_ref, nrm_ref,
                          ha_ref, cx_s, m_s)

    @pl.when(c == n_chunks - 1)
    def _():
        for b in range(MLSTM_SEQS):
            m_ref[b] = m_s[b]
            for h in range(H_A):
                cx = cx_s[b * H_A + h]
                c_ref[0, b, h] = cx[:, :DV]
                n_ref[0, b, h:h + 1, :] = cx[:, DV:].T[h:h + 1, :]


def _prefix_max_rows(x):
    rows = x.shape[0]
    row = lax.broadcasted_iota(jnp.int32, x.shape, 0)
    k = 1
    while k < rows:
        x = jnp.where(row >= k, jnp.maximum(x, pltpu.roll(x, k, 0)), x)
        k *= 2
    return x


def _mlstm_prompt_seq(b, causal, tri, q_ref, k_ref, v_ref, o_ref, g_ref, bias_ref, nrm_ref,
                      ha_ref, cx_s, m_s):
    L = MLSTM_CHUNK
    lane = lax.broadcasted_iota(jnp.int32, (L, LANE), 1)
    lane_dv = lax.broadcasted_iota(jnp.int32, (DV, LANE), 1)
    capped = _soft_cap(g_ref[b] + bias_ref[...])
    lf = _log_sigmoid(capped)
    hi = lf.astype(BF16)
    r1 = lf - hi.astype(F32)
    mid = r1.astype(BF16)
    lo = (r1 - mid.astype(F32)).astype(BF16)
    parts = _bdot(tri, jnp.concatenate([hi, mid, lo], axis=1))
    bc = parts[:, :LANE] + parts[:, LANE:2 * LANE] + parts[:, 2 * LANE:]
    bh = pltpu.roll(bc, LANE - H_A, 1)
    a = capped - bh
    a_t = a.T
    m_prev = m_s[b]
    big_m = jnp.maximum(_prefix_max_rows(a), m_prev[0:1, :])
    sc_all = jnp.exp(m_prev[0:1, :] - big_m)
    emt_all = jnp.exp(-(bh + big_m))
    m_last = big_m[L - 1:L, :]
    w_all = jnp.exp(a - m_last)
    decay_all = sc_all[L - 1:L, :]
    m_s[b] = jnp.broadcast_to(bh[L - 1:L, :] + m_last, (8, LANE))

    nums = []
    nd_tail = jnp.zeros((L, LANE), F32)
    sq_tail = jnp.zeros((L, LANE), F32)
    for h in range(H_A):
        hs = slice(h * DK, (h + 1) * DK)
        onehot = (lane == h).astype(BF16)
        qb = q_ref[b, :, hs].astype(BF16)
        kf = k_ref[b, :, hs] * (DK ** -0.5)
        kb = kf.astype(BF16)
        vx = jnp.concatenate([v_ref[b, :, hs].astype(BF16), onehot], axis=1)
        cx_prev = cx_s[b * H_A + h]

        p = jnp.exp(jnp.where(causal, a_t[h:h + 1, :] - big_m[:, h:h + 1], -jnp.inf))
        s = _bdot_t(qb, kb) * p
        nd = sc_all[:, h:h + 1] * _bdot(qb, cx_prev.astype(BF16)) + _bdot(s.astype(BF16), vx)
        num = nd[:, :DV]
        nums.append(num)
        nd_tail = nd_tail + nd[:, DV:]
        sq_tail = sq_tail + _bdot((num * num).astype(BF16), (lane_dv == h).astype(BF16))

        wk = (w_all[:, h:h + 1] * kf).astype(BF16)
        cx_s[b * H_A + h] = decay_all[:, h:h + 1] * cx_prev + lax.dot_general(
            wk, vx, (((0,), (0,)), ((), ())), preferred_element_type=F32)

    inv = 1.0 / jnp.maximum(jnp.abs(nd_tail), emt_all)
    ms = inv * inv * sq_tail * (1.0 / DV)
    scale_all = inv * lax.rsqrt(ms + EPS)
    for h in range(H_A):
        hs = slice(h * DK, (h + 1) * DK)
        hn = nums[h] * scale_all[:, h:h + 1]
        ha_ref[b, :, hs] = (hn * nrm_ref[:, hs] * jax.nn.sigmoid(o_ref[b, :, hs])).astype(ha_ref.dtype)


def _mlstm_prompt(za_p, g_p, bias_row, nrm_row, batch, seq):
    L = MLSTM_CHUNK
    nsq = MLSTM_SEQS
    n_chunks = seq // L
    kern = functools.partial(_mlstm_prompt_kernel, n_chunks=n_chunks)
    za3 = za_p.reshape(batch, seq, 4 * D_A)
    g3 = g_p.reshape(batch, seq, LANE)
    zspec = lambda part: pl.BlockSpec((nsq, L, D_A), lambda b, c: (b, c, part))
    return pl.pallas_call(
        kern,
        grid=(batch // nsq, n_chunks),
        in_specs=[
            zspec(0), zspec(1), zspec(2), zspec(3),
            pl.BlockSpec((nsq, L, LANE), lambda b, c: (b, c, 0)),
            pl.BlockSpec((1, LANE), lambda b, c: (0, 0)),
            pl.BlockSpec((1, D_A), lambda b, c: (0, 0)),
        ],
        out_specs=[
            pl.BlockSpec((nsq, L, D_A), lambda b, c: (b, c, 0)),
            pl.BlockSpec((1, nsq, H_A, DK, DV), lambda b, c: (0, b, 0, 0, 0)),
            pl.BlockSpec((1, nsq, H_A, DK), lambda b, c: (0, b, 0, 0)),
            pl.BlockSpec((nsq, 8, LANE), lambda b, c: (b, 0, 0)),
        ],
        out_shape=[
            jax.ShapeDtypeStruct((batch, seq, D_A), BF16),
            jax.ShapeDtypeStruct((1, batch, H_A, DK, DV), F32),
            jax.ShapeDtypeStruct((1, batch, H_A, DK), F32),
            jax.ShapeDtypeStruct((batch, 8, LANE), F32),
        ],
        scratch_shapes=[pltpu.VMEM((nsq * H_A, DK, DV + LANE), F32),
                        pltpu.VMEM((nsq, 8, LANE), F32)],
        compiler_params=pltpu.CompilerParams(
            dimension_semantics=("parallel", "arbitrary"), vmem_limit_bytes=VMEM_LIMIT),
        name="mlstm_prompt",
    )(za3, za3, za3, za3, g3, bias_row, nrm_row)


def _expand_heads(x, width):
    rows = x.shape[0]
    return jnp.concatenate([jnp.broadcast_to(x[:, h:h + 1], (rows, width)) for h in range(H_A)], axis=1)


def _head_sums(x):
    return jnp.concatenate(
        [jnp.sum(x[:, h * DK:(h + 1) * DK], axis=-1, keepdims=True) for h in range(H_A)], axis=1)


def _mlstm_sample_kernel(za_ref, qc_ref, g_ref, bias_ref, nrm_ref, m_ref, n_ref,
                         ha_ref, m_out, n_out, wk_out, dec_out):
    nb = za_ref.shape[0]
    capped = _soft_cap(g_ref[...] + bias_ref[...])
    logi = capped[:, 0:H_A]
    logf = _log_sigmoid(capped)[:, H_A:2 * H_A]
    m_prev = m_ref[...]
    m_inter = logf + m_prev
    m_t = jnp.maximum(m_inter, logi)
    sc = jnp.exp(m_inter - m_t)
    ei = jnp.exp(logi - m_t)
    emt = jnp.exp(-m_t)

    q = za_ref[:, 0:D_A]
    k = za_ref[:, D_A:2 * D_A] * (DK ** -0.5)
    v = za_ref[:, 2 * D_A:3 * D_A]
    o = za_ref[:, 3 * D_A:4 * D_A]
    n_prev = n_ref[...]

    s = _head_sums(q * k) * ei
    den = sc * _head_sums(q * n_prev) + s
    denom = jnp.maximum(jnp.abs(den), emt)
    num = _expand_heads(sc, DV) * qc_ref[...] + _expand_heads(s, DV) * v
    hh = num / _expand_heads(denom, DV)
    ms = _head_sums(hh * hh) * (1.0 / DV)
    hn = hh * lax.rsqrt(_expand_heads(ms, DV) + EPS)
    ha_ref[...] = (hn * nrm_ref[...] * jax.nn.sigmoid(o)).astype(ha_ref.dtype)
    n_out[...] = _expand_heads(sc, DK) * n_prev + _expand_heads(ei, DK) * k
    m_out[...] = m_t
    pad = jnp.zeros((nb, LANE - H_A), F32)
    dec_out[...] = jnp.concatenate([sc, pad], axis=1)

    k_t = k.T
    ei_t = jnp.concatenate([ei, pad], axis=1).T
    for h in range(H_A):
        hs = slice(h * DK, (h + 1) * DK)
        wk_out[hs, :] = k_t[hs, :] * ei_t[h:h + 1, :]


def _mlstm_sample(za_s, qc, g_s, bias_row, nrm_row, m0, n0):
    nb = za_s.shape[0]
    assert nb == LANE, "the sequence axis is transposed onto the lanes"
    return pl.pallas_call(
        _mlstm_sample_kernel,
        out_shape=[
            jax.ShapeDtypeStruct((nb, D_A), BF16),
            jax.ShapeDtypeStruct((nb, H_A), F32),
            jax.ShapeDtypeStruct((nb, H_A * DK), F32),
            jax.ShapeDtypeStruct((D_A, nb), F32),
            jax.ShapeDtypeStruct((nb, LANE), F32),
        ],
        compiler_params=pltpu.CompilerParams(vmem_limit_bytes=VMEM_LIMIT),
        name="mlstm_sample",
    )(za_s, qc, g_s, bias_row, nrm_row, m0, n0)


def _outproj_kernel(hp_ref, hs_ref, ap_ref, as_ref, yb_ref, w_ref, op_ref, os_ref, wc_ref):
    j = pl.program_id(1)

    @pl.when(pl.program_id(0) == 0)
    def _():
        wc_ref[j] = w_ref[...].astype(BF16)

    wa = wc_ref[j, :D_A, :]
    wb = wc_ref[j, D_A:, :]
    op_ref[...] = hp_ref[...] + _bdot(ap_ref[...], wa) + _bdot(yb_ref[0, :MP, :], wb)
    os_ref[0] = hs_ref[0] + _bdot(as_ref[0], wa) + _bdot(yb_ref[0, MP:, :], wb)


def _outproj(hp, hs, ap, a_s, yb_c, w3):
    tn = OUT_TN
    n_n = D_MODEL // tn
    return pl.pallas_call(
        _outproj_kernel,
        grid=(N_MIX_BLOCKS, n_n),
        in_specs=[
            pl.BlockSpec((MP, tn), lambda i, j: (i, j)),
            pl.BlockSpec((1, MS, tn), lambda i, j: (i, 0, j)),
            pl.BlockSpec((MP, D_A), lambda i, j: (i, 0)),
            pl.BlockSpec((1, MS, D_A), lambda i, j: (i, 0, 0)),
            pl.BlockSpec((1, MP + MS, D_B), lambda i, j: (i, 0, 0)),
            pl.BlockSpec((None, D_A + D_B, tn), lambda i, j: (0, 0, jnp.where(i == 0, j, n_n - 1))),
        ],
        out_specs=[
            pl.BlockSpec((MP, tn), lambda i, j: (i, j)),
            pl.BlockSpec((1, MS, tn), lambda i, j: (i, 0, j)),
        ],
        out_shape=[
            jax.ShapeDtypeStruct((N_MIX_BLOCKS * MP, D_MODEL), F32),
            jax.ShapeDtypeStruct((N_MIX_BLOCKS, MS, D_MODEL), F32),
        ],
        scratch_shapes=[pltpu.VMEM((n_n, D_A + D_B, tn), BF16)],
        compiler_params=pltpu.CompilerParams(
            dimension_semantics=("arbitrary", "arbitrary"), vmem_limit_bytes=VMEM_LIMIT),
        name="outproj",
    )(hp, hs, ap, a_s, yb_c, w3)


def kernel(x_prompt, x_sample, state_mlstm_C, state_mlstm_n, state_mlstm_m, state_conv, norm_ffn1, ffn1_gate, ffn1_up, ffn1_down, norm_mix, w_in, b_gates, conv_w, conv_b, norm_mlstm, norm_conv, w_out, norm_ffn2, ffn2_gate, ffn2_up, ffn2_down, norm_final):
    batch, seq, _ = x_prompt.shape
    nb = x_sample.shape[0]
    assert batch == N_MIX_BLOCKS and seq == MP and nb == N_MIX_BLOCKS * MS
    assert norm_ffn1.shape[0] == 1, "single-layer trunk"

    xp = x_prompt.reshape(batch * seq, D_MODEL)
    xs = x_sample.reshape(N_FFN_BLOCKS, TM_S, D_MODEL)

    h1p, h1s, xn_c = _ffn(xp, xs, norm_ffn1, ffn1_gate, ffn1_up, ffn1_down, norm_mix)

    w_t = jnp.swapaxes(w_in[0], 0, 1)
    za_p, za_s, g_p, g_s = _proj(xn_c, w_t)
    za_s = za_s.reshape(nb, 4 * D_A)
    yb_c, conv_p, conv_s, qc_s = _conv(xn_c, w_t, conv_w[0], conv_b, norm_conv,
                                       state_conv.reshape(N_MIX_BLOCKS, MS, 2, D_B), za_s, state_mlstm_C)

    bias_row = jnp.zeros((1, LANE), F32).at[0, :2 * H_A].set(b_gates[0].astype(F32))
    ha_p, c_p, n_p, m_p = _mlstm_prompt(za_p, g_p, bias_row, norm_mlstm, batch, seq)
    ha_s, m_s, n_s, wk_s, dec_s = _mlstm_sample(
        za_s, qc_s, g_s.reshape(nb, LANE), bias_row, norm_mlstm,
        state_mlstm_m.reshape(nb, H_A), state_mlstm_n.reshape(nb, H_A * DK))

    h2p, h2s = _outproj(h1p, h1s.reshape(N_MIX_BLOCKS, MS, D_MODEL), ha_p.reshape(batch * seq, D_A),
                        ha_s.reshape(N_MIX_BLOCKS, MS, D_A), yb_c, w_out)

    yp, ys, c_s = _ffn(h2p, h2s.reshape(N_FFN_BLOCKS, TM_S, D_MODEL), norm_ffn2, ffn2_gate, ffn2_up,
                       ffn2_down, norm_final.reshape(1, D_MODEL),
                       state=(state_mlstm_C, wk_s, za_s, dec_s))

    return (
        yp.reshape(batch, seq, D_MODEL),
        ys.reshape(nb, 1, D_MODEL),
        c_p,
        n_p,
        m_p[:, 0, :H_A].reshape(1, batch, H_A),
        conv_p,
        c_s,
        n_s.reshape(1, nb, H_A, DK),
        m_s.reshape(1, nb, H_A),
        conv_s.reshape(1, nb, 2, D_B),
    )
```

```python
import functools

import jax
import jax.numpy as jnp
from jax import lax
from jax.experimental import pallas as pl
from jax.experimental.pallas import tpu as pltpu

F32 = jnp.float32
BF16 = jnp.bfloat16

D_MODEL = 2048
D_A = 1024
D_B = 1024
H_A = 4
DK = 256
DV = 256
G_B = 8
D_FF = 5504
GATE_CAP = 15.0
EPS = 1e-6
GATE_ROW0 = 4 * D_A
CONV_ROW0 = 4 * D_A + 2 * H_A

LANE = 128
N_FFN_BLOCKS = 8
TM_P = 1024
TM_S = 16
TM = TM_P + TM_S
TF = 256
TAIL_SPLIT = 528
N_MIX_BLOCKS = N_FFN_BLOCKS // 2
MP = 2 * TM_P
MS = 2 * TM_S
PROJ_TN = 1024
CONV_TC = 256
OUT_TN = 512
MLSTM_CHUNK = 256
MLSTM_SEQS = 4
SAMPLE_BS = 8
VMEM_LIMIT = 62 * 1024 * 1024


def _rms(x, g):
    return x * lax.rsqrt(jnp.mean(x * x, axis=-1, keepdims=True) + EPS) * g


def _soft_cap(x):
    return GATE_CAP * jnp.tanh(x / GATE_CAP)


def _log_sigmoid(x):
    return -jax.nn.softplus(-x)


def _bdot(a, b):
    return jnp.dot(a, b, preferred_element_type=F32)


def _bdot_t(a, b):
    return lax.dot_general(a, b, (((1,), (1,)), ((), ())), preferred_element_type=F32)


def _sample_state_update(f, c_ref, wk_ref, v_ref, dec_ref, c_out):
    r = jnp.minimum(f, TM_S - 1)
    seq = pl.program_id(0) * TM_S + r
    lane = lax.broadcasted_iota(jnp.int32, (D_A, LANE), 1)
    wk_col = jnp.sum(jnp.where(lane == seq, wk_ref[...], 0.0), axis=1, keepdims=True)
    v_row = v_ref[pl.ds(r, 1), :]
    dec = dec_ref[pl.ds(r, 1), :]
    for h in range(H_A):
        hs = slice(h * DK, (h + 1) * DK)
        c_out[0, 0, h] = dec[:, h:h + 1] * c_ref[0, 0, h] + wk_col[hs, :] * v_row[:, hs]


def _ffn_kernel(xp_ref, xs_ref, g_ref, wg_ref, wu_ref, wd_ref, wt_ref, g2_ref, *rest,
                n_f, last_valid, final_norm):
    if final_norm:
        c_ref, wk_ref, v_ref, dec_ref, op_ref, os_ref, c_out, xn_ref = rest
    else:
        op_ref, os_ref, nxt_ref, xn_ref = rest
    f = pl.program_id(1)

    def step(valid, first, with_state):
        if first:
            g = g_ref[...]
            xn_ref[:TM_P, :] = _rms(xp_ref[...], g).astype(BF16)
            xn_ref[TM_P:, :] = _rms(xs_ref[0], g).astype(BF16)
        xn = xn_ref[...]
        if valid == TF:
            a = _bdot(xn, wg_ref[...].astype(BF16))
            b = _bdot(xn, wu_ref[...].astype(BF16))
        else:
            wt = wt_ref[...].astype(BF16)
            ab = jnp.concatenate([_bdot(xn[:TAIL_SPLIT], wt), _bdot(xn[TAIL_SPLIT:], wt)], axis=0)
            a, b = ab[:, :valid], ab[:, valid:]
        hid = (a * jax.nn.sigmoid(a) * b * 0.5).astype(BF16)
        r = _bdot(hid, wd_ref[:valid, :].astype(BF16))
        if first:
            op_ref[...] = xp_ref[...] + r[:TM_P]
            os_ref[0] = xs_ref[0] + r[TM_P:]
        else:
            op_ref[...] += r[:TM_P]
            os_ref[0] += r[TM_P:]
        if with_state:
            _sample_state_update(f, c_ref, wk_ref, v_ref, dec_ref, c_out)

    assert 2 * last_valid == TF and n_f - 1 >= TM_S, "half-width tail; state update rides the full-width steps"

    @pl.when(f == 0)
    def _():
        step(TF, True, final_norm)

    @pl.when(jnp.logical_and(f > 0, f < n_f - 1))
    def _():
        step(TF, False, final_norm)

    @pl.when(f == n_f - 1)
    def _():
        step(last_valid, False, False)
        g2 = g2_ref[...]
        if final_norm:
            op_ref[...] = _rms(op_ref[...], g2)
            os_ref[0] = _rms(os_ref[0], g2)
        else:
            nxt_ref[0, :TM_P, :] = _rms(op_ref[...], g2).astype(BF16)
            nxt_ref[0, TM_P:, :] = _rms(os_ref[0], g2).astype(BF16)


def _ffn(xp, xs, g, wg, wu, wd, g2, state=None):
    final_norm = state is not None
    n_f = pl.cdiv(D_FF, TF)
    last_valid = D_FF - (n_f - 1) * TF
    kern = functools.partial(_ffn_kernel, n_f=n_f, last_valid=last_valid, final_norm=final_norm)
    tail0 = (n_f - 1) * TF
    w_tail = jnp.concatenate([wg[0, :, tail0:], wu[0, :, tail0:]], axis=1)
    full = lambda i, f: (0, 0, jnp.minimum(f, n_f - 2))
    in_specs = [
        pl.BlockSpec((TM_P, D_MODEL), lambda i, f: (i, 0)),
        pl.BlockSpec((1, TM_S, D_MODEL), lambda i, f: (i, 0, 0)),
        pl.BlockSpec((1, D_MODEL), lambda i, f: (0, 0)),
        pl.BlockSpec((None, D_MODEL, TF), full),
        pl.BlockSpec((None, D_MODEL, TF), full),
        pl.BlockSpec((None, TF, D_MODEL), lambda i, f: (0, f, 0)),
        pl.BlockSpec((D_MODEL, TF), lambda i, f: (0, 0), pipeline_mode=pl.Buffered(1)),
        pl.BlockSpec((1, D_MODEL), lambda i, f: (0, 0)),
    ]
    out_specs = [
        pl.BlockSpec((TM_P, D_MODEL), lambda i, f: (i, 0)),
        pl.BlockSpec((1, TM_S, D_MODEL), lambda i, f: (i, 0, 0)),
    ]
    out_shape = [
        jax.ShapeDtypeStruct((N_FFN_BLOCKS * TM_P, D_MODEL), F32),
        jax.ShapeDtypeStruct((N_FFN_BLOCKS, TM_S, D_MODEL), F32),
    ]
    args = [xp, xs, g, wg, wu, wd, w_tail, g2]
    if final_norm:
        c0, wk, za_s, dec = state
        seq = lambda i, f: i * TM_S + jnp.minimum(f, TM_S - 1)
        cspec = pl.BlockSpec((1, 1, H_A, DK, DV), lambda i, f: (0, seq(i, f), 0, 0, 0))
        in_specs += [
            cspec,
            pl.BlockSpec((D_A, LANE), lambda i, f: (0, 0)),
            pl.BlockSpec((TM_S, D_A), lambda i, f: (i, 2)),
            pl.BlockSpec((TM_S, LANE), lambda i, f: (i, 0)),
        ]
        out_specs.append(cspec)
        out_shape.append(jax.ShapeDtypeStruct(c0.shape, F32))
        args += [c0, wk, za_s, dec]
    else:
        out_specs.append(pl.BlockSpec((1, TM, D_MODEL), lambda i, f: (i, 0, 0)))
        out_shape.append(jax.ShapeDtypeStruct((N_FFN_BLOCKS, TM, D_MODEL), BF16))
    return pl.pallas_call(
        kern,
        grid=(N_FFN_BLOCKS, n_f),
        in_specs=in_specs,
        out_specs=out_specs,
        out_shape=out_shape,
        scratch_shapes=[pltpu.VMEM((TM, D_MODEL), BF16)],
        compiler_params=pltpu.CompilerParams(
            dimension_semantics=("parallel", "arbitrary"), vmem_limit_bytes=VMEM_LIMIT),
        name="ffn_final" if final_norm else "ffn",
    )(*args)


def _split_rows(z0, z1, p_ref, s_ref):
    p_ref[:TM_P, :] = z0[:TM_P]
    p_ref[TM_P:, :] = z1[:TM_P]
    s_ref[0, :TM_S, :] = z0[TM_P:]
    s_ref[0, TM_S:, :] = z1[TM_P:]


def _proj_kernel(xn_ref, w_ref, wg_ref, zp_ref, zs_ref, gp_ref, gs_ref):
    w = w_ref[...].astype(BF16)
    _split_rows(_bdot_t(xn_ref[0], w), _bdot_t(xn_ref[1], w), zp_ref, zs_ref)

    @pl.when(pl.program_id(1) == 0)
    def _():
        wg = wg_ref[...].astype(BF16)
        _split_rows(_bdot_t(xn_ref[0], wg), _bdot_t(xn_ref[1], wg), gp_ref, gs_ref)


def _proj(xn_c, w_t):
    n_n = (4 * D_A) // PROJ_TN
    return pl.pallas_call(
        _proj_kernel,
        grid=(N_MIX_BLOCKS, n_n),
        in_specs=[
            pl.BlockSpec((2, TM, D_MODEL), lambda i, j: (i, 0, 0)),
            pl.BlockSpec((PROJ_TN, D_MODEL), lambda i, j: (j, 0)),
            pl.BlockSpec((LANE, D_MODEL), lambda i, j: (GATE_ROW0 // LANE, 0)),
        ],
        out_specs=[
            pl.BlockSpec((MP, PROJ_TN), lambda i, j: (i, j)),
            pl.BlockSpec((1, MS, PROJ_TN), lambda i, j: (i, 0, j)),
            pl.BlockSpec((MP, LANE), lambda i, j: (i, 0)),
            pl.BlockSpec((1, MS, LANE), lambda i, j: (i, 0, 0)),
        ],
        out_shape=[
            jax.ShapeDtypeStruct((N_MIX_BLOCKS * MP, 4 * D_A), F32),
            jax.ShapeDtypeStruct((N_MIX_BLOCKS, MS, 4 * D_A), F32),
            jax.ShapeDtypeStruct((N_MIX_BLOCKS * MP, LANE), F32),
            jax.ShapeDtypeStruct((N_MIX_BLOCKS, MS, LANE), F32),
        ],
        compiler_params=pltpu.CompilerParams(
            dimension_semantics=("parallel", "arbitrary"), vmem_limit_bytes=VMEM_LIMIT),
        name="proj_qkvo",
    )(xn_c, w_t, w_t)


def _group_norm(yb, nrm):
    gw = D_B // G_B
    parts = []
    for g in range(yb.shape[1] // gw):
        seg = yb[:, g * gw:(g + 1) * gw]
        parts.append(seg * lax.rsqrt(jnp.mean(seg * seg, axis=-1, keepdims=True) + EPS))
    return jnp.concatenate(parts, axis=1) * nrm


def _conv_kernel(xn_ref, wgb_ref, wgc_ref, wxc_ref, cw_ref, cb_ref, nrm_ref, buf_ref, q_ref, c_ref,
                 yb_ref, cp_ref, cs_ref, qc_ref):
    for h in range(H_A):
        hs = slice(h * DK, (h + 1) * DK)
        qb = q_ref[:, hs].astype(BF16)
        for j in range(SAMPLE_BS):
            qc = _bdot(qb, c_ref[0, j, h].astype(BF16))
            qc_ref[j:j + 1, hs] = qc[j:j + 1, :]

    wgb = wgb_ref[...].astype(BF16)
    wgc = wgc_ref[...].astype(BF16)
    wxc = wxc_ref[...].astype(BF16)
    cw0 = cw_ref[0:1, :]
    cw1 = cw_ref[1:2, :]
    cw2 = cw_ref[2:3, :]
    cb = cb_ref[...]
    nrm = nrm_ref[...]
    row = lax.broadcasted_iota(jnp.int32, (TM_P, wgb.shape[0]), 0)
    zero_row = jnp.zeros((1, wgb.shape[0]), F32)
    um2, um1 = zero_row, zero_row

    for t in range(2):
        xn = xn_ref[t]
        gb = _bdot_t(xn, wgb)
        u = _bdot_t(xn, wgc) * _bdot_t(xn, wxc)

        up = u[:TM_P]
        u1 = jnp.where(row < 1, um1, pltpu.roll(up, 1, 0))
        u2 = jnp.where(row < 1, um2, jnp.where(row < 2, um1, pltpu.roll(up, 2, 0)))
        yc = cw0 * u2 + cw1 * u1 + cw2 * up + cb
        yb_ref[0, t * TM_P:(t + 1) * TM_P, :] = _group_norm(gb[:TM_P] * yc, nrm).astype(yb_ref.dtype)
        um2, um1 = up[TM_P - 2:TM_P - 1, :], up[TM_P - 1:TM_P, :]

        ss = slice(t * TM_S, (t + 1) * TM_S)
        us = u[TM_P:]
        b0 = buf_ref[0, ss, 0, :]
        b1 = buf_ref[0, ss, 1, :]
        ycs = cw0 * b0 + cw1 * b1 + cw2 * us + cb
        yb_ref[0, MP + t * TM_S:MP + (t + 1) * TM_S, :] = _group_norm(gb[TM_P:] * ycs, nrm).astype(yb_ref.dtype)
        cs_ref[0, ss, 0, :] = b1
        cs_ref[0, ss, 1, :] = us

    cp_ref[0, 0, 0:1, :] = um2
    cp_ref[0, 0, 1:2, :] = um1


def _conv(xn_c, w_t, cw, cb_row, nrm_row, buf, za_s, c0):
    tc = CONV_TC
    n_c = D_B // tc
    assert N_MIX_BLOCKS * n_c * SAMPLE_BS == za_s.shape[0], "one sample group per grid step"
    wspec = lambda base: pl.BlockSpec(
        (pl.Element(tc), pl.Element(D_MODEL)), lambda i, c: (pl.multiple_of(base + c * tc, 8), 0))
    grp = lambda i, c: i * n_c + c
    return pl.pallas_call(
        _conv_kernel,
        grid=(N_MIX_BLOCKS, n_c),
        in_specs=[
            pl.BlockSpec((2, TM, D_MODEL), lambda i, c: (i, 0, 0)),
            wspec(CONV_ROW0),
            wspec(CONV_ROW0 + D_B),
            wspec(CONV_ROW0 + 2 * D_B),
            pl.BlockSpec((3, tc), lambda i, c: (0, c)),
            pl.BlockSpec((1, tc), lambda i, c: (0, c)),
            pl.BlockSpec((1, tc), lambda i, c: (0, c)),
            pl.BlockSpec((1, MS, 2, tc), lambda i, c: (i, 0, 0, c)),
            pl.BlockSpec((SAMPLE_BS, D_A), lambda i, c: (grp(i, c), 0)),
            pl.BlockSpec((1, SAMPLE_BS, H_A, DK, DV), lambda i, c: (0, grp(i, c), 0, 0, 0)),
        ],
        out_specs=[
            pl.BlockSpec((1, MP + MS, tc), lambda i, c: (i, 0, c)),
            pl.BlockSpec((1, 1, 2, tc), lambda i, c: (0, i, 0, c)),
            pl.BlockSpec((1, MS, 2, tc), lambda i, c: (i, 0, 0, c)),
            pl.BlockSpec((SAMPLE_BS, D_A), lambda i, c: (grp(i, c), 0)),
        ],
        out_shape=[
            jax.ShapeDtypeStruct((N_MIX_BLOCKS, MP + MS, D_B), BF16),
            jax.ShapeDtypeStruct((1, N_MIX_BLOCKS, 2, D_B), F32),
            jax.ShapeDtypeStruct((N_MIX_BLOCKS, MS, 2, D_B), F32),
            jax.ShapeDtypeStruct((za_s.shape[0], D_A), F32),
        ],
        compiler_params=pltpu.CompilerParams(
            dimension_semantics=("parallel", "arbitrary"), vmem_limit_bytes=VMEM_LIMIT),
        name="proj_conv",
    )(xn_c, w_t, w_t, w_t, cw, cb_row, nrm_row, buf, za_s, c0)


def _mlstm_prompt_kernel(q_ref, k_ref, v_ref, o_ref, g_ref, bias_ref, nrm_ref,
                         ha_ref, c_ref, n_ref, m_ref, cx_s, m_s, *, n_chunks):
    L = MLSTM_CHUNK
    c = pl.program_id(1)

    @pl.when(c == 0)
    def _():
        cx_s[...] = jnp.zeros_like(cx_s)
        m_s[...] = jnp.zeros_like(m_s)

    row = lax.broadcasted_iota(jnp.int32, (L, L), 0)
    col = lax.broadcasted_iota(jnp.int32, (L, L), 1)
    causal = row >= col
    tri = causal.astype(BF16)

    for b in range(MLSTM_SEQS):
        _mlstm_prompt_seq(b, causal, tri, q_ref, k_ref, v_ref, o_ref, g_ref, bias_ref, nrm_ref,
                          ha_ref, cx_s, m_s)

    @pl.when(c == n_chunks - 1)
    def _():
        for b in range(MLSTM_SEQS):
            m_ref[b] = m_s[b]
            for h in range(H_A):
                cx = cx_s[b * H_A + h]
                c_ref[0, b, h] = cx[:, :DV]
                n_ref[0, b, h:h + 1, :] = cx[:, DV:].T[h:h + 1, :]


def _prefix_max_rows(x):
    rows = x.shape[0]
    row = lax.broadcasted_iota(jnp.int32, x.shape, 0)
    k = 1
    while k < rows:
        x = jnp.where(row >= k, jnp.maximum(x, pltpu.roll(x, k, 0)), x)
        k *= 2
    return x


def _mlstm_prompt_seq(b, causal, tri, q_ref, k_ref, v_ref, o_ref, g_ref, bias_ref, nrm_ref,
                      ha_ref, cx_s, m_s):
    L = MLSTM_CHUNK
    lane = lax.broadcasted_iota(jnp.int32, (L, LANE), 1)
    lane_dv = lax.broadcasted_iota(jnp.int32, (DV, LANE), 1)
    capped = _soft_cap(g_ref[b] + bias_ref[...])
    lf = _log_sigmoid(capped)
    hi = lf.astype(BF16)
    r1 = lf - hi.astype(F32)
    mid = r1.astype(BF16)
    lo = (r1 - mid.astype(F32)).astype(BF16)
    parts = _bdot(tri, jnp.concatenate([hi, mid, lo], axis=1))
    bc = parts[:, :LANE] + parts[:, LANE:2 * LANE] + parts[:, 2 * LANE:]
    bh = pltpu.roll(bc, LANE - H_A, 1)
    a = capped - bh
    a_t = a.T
    m_prev = m_s[b]
    big_m = jnp.maximum(_prefix_max_rows(a), m_prev[0:1, :])
    sc_all = jnp.exp(m_prev[0:1, :] - big_m)
    emt_all = jnp.exp(-(bh + big_m))
    m_last = big_m[L - 1:L, :]
    w_all = jnp.exp(a - m_last)
    decay_all = sc_all[L - 1:L, :]
    m_s[b] = jnp.broadcast_to(bh[L - 1:L, :] + m_last, (8, LANE))

    nums = []
    nd_tail = jnp.zeros((L, LANE), F32)
    sq_tail = jnp.zeros((L, LANE), F32)
    for h in range(H_A):
        hs = slice(h * DK, (h + 1) * DK)
        onehot = (lane == h).astype(BF16)
        qb = q_ref[b, :, hs].astype(BF16)
        kf = k_ref[b, :, hs] * (DK ** -0.5)
        kb = kf.astype(BF16)
        vx = jnp.concatenate([v_ref[b, :, hs].astype(BF16), onehot], axis=1)
        cx_prev = cx_s[b * H_A + h]

        p = jnp.exp(jnp.where(causal, a_t[h:h + 1, :] - big_m[:, h:h + 1], -jnp.inf))
        s = _bdot_t(qb, kb) * p
        nd = sc_all[:, h:h + 1] * _bdot(qb, cx_prev.astype(BF16)) + _bdot(s.astype(BF16), vx)
        num = nd[:, :DV]
        nums.append(num)
        nd_tail = nd_tail + nd[:, DV:]
        sq_tail = sq_tail + _bdot((num * num).astype(BF16), (lane_dv == h).astype(BF16))

        wk = (w_all[:, h:h + 1] * kf).astype(BF16)
        cx_s[b * H_A + h] = decay_all[:, h:h + 1] * cx_prev + lax.dot_general(
            wk, vx, (((0,), (0,)), ((), ())), preferred_element_type=F32)

    inv = 1.0 / jnp.maximum(jnp.abs(nd_tail), emt_all)
    ms = inv * inv * sq_tail * (1.0 / DV)
    scale_all = inv * lax.rsqrt(ms + EPS)
    for h in range(H_A):
        hs = slice(h * DK, (h + 1) * DK)
        hn = nums[h] * scale_all[:, h:h + 1]
        ha_ref[b, :, hs] = (hn * nrm_ref[:, hs] * jax.nn.sigmoid(o_ref[b, :, hs])).astype(ha_ref.dtype)


def _mlstm_prompt(za_p, g_p, bias_row, nrm_row, batch, seq):
    L = MLSTM_CHUNK
    nsq = MLSTM_SEQS
    n_chunks = seq // L
    kern = functools.partial(_mlstm_prompt_kernel, n_chunks=n_chunks)
    za3 = za_p.reshape(batch, seq, 4 * D_A)
    g3 = g_p.reshape(batch, seq, LANE)
    zspec = lambda part: pl.BlockSpec((nsq, L, D_A), lambda b, c: (b, c, part))
    return pl.pallas_call(
        kern,
        grid=(batch // nsq, n_chunks),
        in_specs=[
            zspec(0), zspec(1), zspec(2), zspec(3),
            pl.BlockSpec((nsq, L, LANE), lambda b, c: (b, c, 0)),
            pl.BlockSpec((1, LANE), lambda b, c: (0, 0)),
            pl.BlockSpec((1, D_A), lambda b, c: (0, 0)),
        ],
        out_specs=[
            pl.BlockSpec((nsq, L, D_A), lambda b, c: (b, c, 0)),
            pl.BlockSpec((1, nsq, H_A, DK, DV), lambda b, c: (0, b, 0, 0, 0)),
            pl.BlockSpec((1, nsq, H_A, DK), lambda b, c: (0, b, 0, 0)),
            pl.BlockSpec((nsq, 8, LANE), lambda b, c: (b, 0, 0)),
        ],
        out_shape=[
            jax.ShapeDtypeStruct((batch, seq, D_A), BF16),
            jax.ShapeDtypeStruct((1, batch, H_A, DK, DV), F32),
            jax.ShapeDtypeStruct((1, batch, H_A, DK), F32),
            jax.ShapeDtypeStruct((batch, 8, LANE), F32),
        ],
        scratch_shapes=[pltpu.VMEM((nsq * H_A, DK, DV + LANE), F32),
                        pltpu.VMEM((nsq, 8, LANE), F32)],
        compiler_params=pltpu.CompilerParams(
            dimension_semantics=("parallel", "arbitrary"), vmem_limit_bytes=VMEM_LIMIT),
        name="mlstm_prompt",
    )(za3, za3, za3, za3, g3, bias_row, nrm_row)


def _expand_heads(x, width):
    rows = x.shape[0]
    return jnp.concatenate([jnp.broadcast_to(x[:, h:h + 1], (rows, width)) for h in range(H_A)], axis=1)


def _head_sums(x):
    return jnp.concatenate(
        [jnp.sum(x[:, h * DK:(h + 1) * DK], axis=-1, keepdims=True) for h in range(H_A)], axis=1)


def _mlstm_sample_kernel(za_ref, qc_ref, g_ref, bias_ref, nrm_ref, m_ref, n_ref,
                         ha_ref, m_out, n_out, wk_out, dec_out):
    nb = za_ref.shape[0]
    capped = _soft_cap(g_ref[...] + bias_ref[...])
    logi = capped[:, 0:H_A]
    logf = _log_sigmoid(capped)[:, H_A:2 * H_A]
    m_prev = m_ref[...]
    m_inter = logf + m_prev
    m_t = jnp.maximum(m_inter, logi)
    sc = jnp.exp(m_inter - m_t)
    ei = jnp.exp(logi - m_t)
    emt = jnp.exp(-m_t)

    q = za_ref[:, 0:D_A]
    k = za_ref[:, D_A:2 * D_A] * (DK ** -0.5)
    v = za_ref[:, 2 * D_A:3 * D_A]
    o = za_ref[:, 3 * D_A:4 * D_A]
    n_prev = n_ref[...]

    s = _head_sums(q * k) * ei
    den = sc * _head_sums(q * n_prev) + s
    denom = jnp.maximum(jnp.abs(den), emt)
    num = _expand_heads(sc, DV) * qc_ref[...] + _expand_heads(s, DV) * v
    hh = num / _expand_heads(denom, DV)
    ms = _head_sums(hh * hh) * (1.0 / DV)
    hn = hh * lax.rsqrt(_expand_heads(ms, DV) + EPS)
    ha_ref[...] = (hn * nrm_ref[...] * jax.nn.sigmoid(o)).astype(ha_ref.dtype)
    n_out[...] = _expand_heads(sc, DK) * n_prev + _expand_heads(ei, DK) * k
    m_out[...] = m_t
    pad = jnp.zeros((nb, LANE - H_A), F32)
    dec_out[...] = jnp.concatenate([sc, pad], axis=1)

    k_t = k.T
    ei_t = jnp.concatenate([ei, pad], axis=1).T
    for h in range(H_A):
        hs = slice(h * DK, (h + 1) * DK)
        wk_out[hs, :] = k_t[hs, :] * ei_t[h:h + 1, :]


def _mlstm_sample(za_s, qc, g_s, bias_row, nrm_row, m0, n0):
    nb = za_s.shape[0]
    assert nb == LANE, "the sequence axis is transposed onto the lanes"
    return pl.pallas_call(
        _mlstm_sample_kernel,
        out_shape=[
            jax.ShapeDtypeStruct((nb, D_A), BF16),
            jax.ShapeDtypeStruct((nb, H_A), F32),
            jax.ShapeDtypeStruct((nb, H_A * DK), F32),
            jax.ShapeDtypeStruct((D_A, nb), F32),
            jax.ShapeDtypeStruct((nb, LANE), F32),
        ],
        compiler_params=pltpu.CompilerParams(vmem_limit_bytes=VMEM_LIMIT),
        name="mlstm_sample",
    )(za_s, qc, g_s, bias_row, nrm_row, m0, n0)


def _outproj_kernel(hp_ref, hs_ref, ap_ref, as_ref, yb_ref, w_ref, op_ref, os_ref, wc_ref):
    j = pl.program_id(1)

    @pl.when(pl.program_id(0) == 0)
    def _():
        wc_ref[j] = w_ref[...].astype(BF16)

    wa = wc_ref[j, :D_A, :]
    wb = wc_ref[j, D_A:, :]
    op_ref[...] = hp_ref[...] + _bdot(ap_ref[...], wa) + _bdot(yb_ref[0, :MP, :], wb)
    os_ref[0] = hs_ref[0] + _bdot(as_ref[0], wa) + _bdot(yb_ref[0, MP:, :], wb)


def _outproj(hp, hs, ap, a_s, yb_c, w3):
    tn = OUT_TN
    n_n = D_MODEL // tn
    return pl.pallas_call(
        _outproj_kernel,
        grid=(N_MIX_BLOCKS, n_n),
        in_specs=[
            pl.BlockSpec((MP, tn), lambda i, j: (i, j)),
            pl.BlockSpec((1, MS, tn), lambda i, j: (i, 0, j)),
            pl.BlockSpec((MP, D_A), lambda i, j: (i, 0)),
            pl.BlockSpec((1, MS, D_A), lambda i, j: (i, 0, 0)),
            pl.BlockSpec((1, MP + MS, D_B), lambda i, j: (i, 0, 0)),
            pl.BlockSpec((None, D_A + D_B, tn), lambda i, j: (0, 0, jnp.where(i == 0, j, n_n - 1))),
        ],
        out_specs=[
            pl.BlockSpec((MP, tn), lambda i, j: (i, j)),
            pl.BlockSpec((1, MS, tn), lambda i, j: (i, 0, j)),
        ],
        out_shape=[
            jax.ShapeDtypeStruct((N_MIX_BLOCKS * MP, D_MODEL), F32),
            jax.ShapeDtypeStruct((N_MIX_BLOCKS, MS, D_MODEL), F32),
        ],
        scratch_shapes=[pltpu.VMEM((n_n, D_A + D_B, tn), BF16)],
        compiler_params=pltpu.CompilerParams(
            dimension_semantics=("arbitrary", "arbitrary"), vmem_limit_bytes=VMEM_LIMIT),
        name="outproj",
    )(hp, hs, ap, a_s, yb_c, w3)


def kernel(x_prompt, x_sample, state_mlstm_C, state_mlstm_n, state_mlstm_m, state_conv, norm_ffn1, ffn1_gate, ffn1_up, ffn1_down, norm_mix, w_in, b_gates, conv_w, conv_b, norm_mlstm, norm_conv, w_out, norm_ffn2, ffn2_gate, ffn2_up, ffn2_down, norm_final):
    batch, seq, _ = x_prompt.shape
    nb = x_sample.shape[0]
    assert batch == N_MIX_BLOCKS and seq == MP and nb == N_MIX_BLOCKS * MS
    assert norm_ffn1.shape[0] == 1, "single-layer trunk"

    xp = x_prompt.reshape(batch * seq, D_MODEL)
    xs = x_sample.reshape(N_FFN_BLOCKS, TM_S, D_MODEL)

    h1p, h1s, xn_c = _ffn(xp, xs, norm_ffn1, ffn1_gate, ffn1_up, ffn1_down, norm_mix)

    w_t = jnp.swapaxes(w_in[0], 0, 1)
    za_p, za_s, g_p, g_s = _proj(xn_c, w_t)
    za_s = za_s.reshape(nb, 4 * D_A)
    yb_c, conv_p, conv_s, qc_s = _conv(xn_c, w_t, conv_w[0], conv_b, norm_conv,
                                       state_conv.reshape(N_MIX_BLOCKS, MS, 2, D_B), za_s, state_mlstm_C)

    bias_row = jnp.zeros((1, LANE), F32).at[0, :2 * H_A].set(b_gates[0].astype(F32))
    ha_p, c_p, n_p, m_p = _mlstm_prompt(za_p, g_p, bias_row, norm_mlstm, batch, seq)
    ha_s, m_s, n_s, wk_s, dec_s = _mlstm_sample(
        za_s, qc_s, g_s.reshape(nb, LANE), bias_row, norm_mlstm,
        state_mlstm_m.reshape(nb, H_A), state_mlstm_n.reshape(nb, H_A * DK))

    h2p, h2s = _outproj(h1p, h1s.reshape(N_MIX_BLOCKS, MS, D_MODEL), ha_p.reshape(batch * seq, D_A),
                        ha_s.reshape(N_MIX_BLOCKS, MS, D_A), yb_c, w_out)

    yp, ys, c_s = _ffn(h2p, h2s.reshape(N_FFN_BLOCKS, TM_S, D_MODEL), norm_ffn2, ffn2_gate, ffn2_up,
                       ffn2_down, norm_final.reshape(1, D_MODEL),
                       state=(state_mlstm_C, wk_s, za_s, dec_s))

    return (
        yp.reshape(batch, seq, D_MODEL),
        ys.reshape(nb, 1, D_MODEL),
        c_p,
        n_p,
        m_p[:, 0, :H_A].reshape(1, batch, H_A),
        conv_p,
        c_s,
        n_s.reshape(1, nb, H_A, DK),
        m_s.reshape(1, nb, H_A),
        conv_s.reshape(1, nb, 2, D_B),
    )
```

```python
import functools

import jax
import jax.numpy as jnp
from jax import lax
from jax.experimental import pallas as pl
from jax.experimental.pallas import tpu as pltpu

F32 = jnp.float32
BF16 = jnp.bfloat16

D_MODEL = 2048
D_A = 1024
D_B = 1024
H_A = 4
DK = 256
DV = 256
G_B = 8
D_FF = 5504
GATE_CAP = 15.0
EPS = 1e-6
GATE_ROW0 = 4 * D_A
CONV_ROW0 = 4 * D_A + 2 * H_A

LANE = 128
N_FFN_BLOCKS = 8
TM_P = 1024
TM_S = 16
TM = TM_P + TM_S
TF = 256
TAIL_SPLIT = 528
N_MIX_BLOCKS = N_FFN_BLOCKS // 2
MP = 2 * TM_P
MS = 2 * TM_S
PROJ_TN = 1024
CONV_TC = 256
OUT_TN = 512
MLSTM_CHUNK = 256
MLSTM_SEQS = 4
SAMPLE_BS = 8
VMEM_LIMIT = 62 * 1024 * 1024


def _rms(x, g):
    return x * lax.rsqrt(jnp.mean(x * x, axis=-1, keepdims=True) + EPS) * g


def _soft_cap(x):
    return GATE_CAP * jnp.tanh(x / GATE_CAP)


def _log_sigmoid(x):
    return -jax.nn.softplus(-x)


def _bdot(a, b):
    return jnp.dot(a, b, preferred_element_type=F32)


def _bdot_t(a, b):
    return lax.dot_general(a, b, (((1,), (1,)), ((), ())), preferred_element_type=F32)


def _sample_state_update(f, c_ref, wk_ref, v_ref, dec_ref, c_out):
    r = jnp.minimum(f, TM_S - 1)
    seq = pl.program_id(0) * TM_S + r
    lane = lax.broadcasted_iota(jnp.int32, (D_A, LANE), 1)
    wk_col = jnp.sum(jnp.where(lane == seq, wk_ref[...], 0.0), axis=1, keepdims=True)
    v_row = v_ref[pl.ds(r, 1), :]
    dec = dec_ref[pl.ds(r, 1), :]
    for h in range(H_A):
        hs = slice(h * DK, (h + 1) * DK)
        c_out[0, 0, h] = dec[:, h:h + 1] * c_ref[0, 0, h] + wk_col[hs, :] * v_row[:, hs]


def _ffn_kernel(xp_ref, xs_ref, g_ref, wg_ref, wu_ref, wd_ref, wt_ref, g2_ref, *rest,
                n_f, last_valid, final_norm):
    if final_norm:
        c_ref, wk_ref, v_ref, dec_ref, op_ref, os_ref, c_out, xn_ref = rest
    else:
        op_ref, os_ref, nxt_ref, xn_ref = rest
    f = pl.program_id(1)

    def step(valid, first, with_state):
        if first:
            g = g_ref[...]
            xn_ref[:TM_P, :] = _rms(xp_ref[...], g).astype(BF16)
            xn_ref[TM_P:, :] = _rms(xs_ref[0], g).astype(BF16)
        xn = xn_ref[...]
        if valid == TF:
            a = _bdot(xn, wg_ref[...].astype(BF16))
            b = _bdot(xn, wu_ref[...].astype(BF16))
        else:
            wt = wt_ref[...].astype(BF16)
            ab = jnp.concatenate([_bdot(xn[:TAIL_SPLIT], wt), _bdot(xn[TAIL_SPLIT:], wt)], axis=0)
            a, b = ab[:, :valid], ab[:, valid:]
        hid = (a * jax.nn.sigmoid(a) * b * 0.5).astype(BF16)
        r = _bdot(hid, wd_ref[:valid, :].astype(BF16))
        if first:
            op_ref[...] = xp_ref[...] + r[:TM_P]
            os_ref[0] = xs_ref[0] + r[TM_P:]
        else:
            op_ref[...] += r[:TM_P]
            os_ref[0] += r[TM_P:]
        if with_state:
            _sample_state_update(f, c_ref, wk_ref, v_ref, dec_ref, c_out)

    assert 2 * last_valid == TF and n_f - 1 >= TM_S, "half-width tail; state update rides the full-width steps"

    @pl.when(f == 0)
    def _():
        step(TF, True, final_norm)

    @pl.when(jnp.logical_and(f > 0, f < n_f - 1))
    def _():
        step(TF, False, final_norm)

    @pl.when(f == n_f - 1)
    def _():
        step(last_valid, False, False)
        g2 = g2_ref[...]
        if final_norm:
            op_ref[...] = _rms(op_ref[...], g2)
            os_ref[0] = _rms(os_ref[0], g2)
        else:
            nxt_ref[0, :TM_P, :] = _rms(op_ref[...], g2).astype(BF16)
            nxt_ref[0, TM_P:, :] = _rms(os_ref[0], g2).astype(BF16)


def _ffn(xp, xs, g, wg, wu, wd, g2, state=None):
    final_norm = state is not None
    n_f = pl.cdiv(D_FF, TF)
    last_valid = D_FF - (n_f - 1) * TF
    kern = functools.partial(_ffn_kernel, n_f=n_f, last_valid=last_valid, final_norm=final_norm)
    tail0 = (n_f - 1) * TF
    w_tail = jnp.concatenate([wg[0, :, tail0:], wu[0, :, tail0:]], axis=1)
    full = lambda i, f: (0, 0, jnp.minimum(f, n_f - 2))
    ahead = lambda i, f: jnp.minimum(i + (f >= 2).astype(jnp.int32), N_FFN_BLOCKS - 1)
    in_specs = [
        pl.BlockSpec((TM_P, D_MODEL), lambda i, f: (ahead(i, f), 0)),
        pl.BlockSpec((1, TM_S, D_MODEL), lambda i, f: (ahead(i, f), 0, 0)),
        pl.BlockSpec((1, D_MODEL), lambda i, f: (0, 0)),
        pl.BlockSpec((None, D_MODEL, TF), full),
        pl.BlockSpec((None, D_MODEL, TF), full),
        pl.BlockSpec((None, TF, D_MODEL), lambda i, f: (0, f, 0)),
        pl.BlockSpec((D_MODEL, TF), lambda i, f: (0, 0), pipeline_mode=pl.Buffered(1)),
        pl.BlockSpec((1, D_MODEL), lambda i, f: (0, 0)),
    ]
    out_specs = [
        pl.BlockSpec((TM_P, D_MODEL), lambda i, f: (i, 0)),
        pl.BlockSpec((1, TM_S, D_MODEL), lambda i, f: (i, 0, 0)),
    ]
    out_shape = [
        jax.ShapeDtypeStruct((N_FFN_BLOCKS * TM_P, D_MODEL), F32),
        jax.ShapeDtypeStruct((N_FFN_BLOCKS, TM_S, D_MODEL), F32),
    ]
    args = [xp, xs, g, wg, wu, wd, w_tail, g2]
    if final_norm:
        c0, wk, za_s, dec = state
        seq = lambda i, f: i * TM_S + jnp.minimum(f, TM_S - 1)
        cspec = pl.BlockSpec((1, 1, H_A, DK, DV), lambda i, f: (0, seq(i, f), 0, 0, 0))
        in_specs += [
            cspec,
            pl.BlockSpec((D_A, LANE), lambda i, f: (0, 0)),
            pl.BlockSpec((TM_S, D_A), lambda i, f: (i, 2)),
            pl.BlockSpec((TM_S, LANE), lambda i, f: (i, 0)),
        ]
        out_specs.append(cspec)
        out_shape.append(jax.ShapeDtypeStruct(c0.shape, F32))
        args += [c0, wk, za_s, dec]
    else:
        out_specs.append(pl.BlockSpec((1, TM, D_MODEL), lambda i, f: (i, 0, 0)))
        out_shape.append(jax.ShapeDtypeStruct((N_FFN_BLOCKS, TM, D_MODEL), BF16))
    return pl.pallas_call(
        kern,
        grid=(N_FFN_BLOCKS, n_f),
        in_specs=in_specs,
        out_specs=out_specs,
        out_shape=out_shape,
        scratch_shapes=[pltpu.VMEM((TM, D_MODEL), BF16)],
        compiler_params=pltpu.CompilerParams(
            dimension_semantics=("parallel", "arbitrary"), vmem_limit_bytes=VMEM_LIMIT),
        name="ffn_final" if final_norm else "ffn",
    )(*args)


def _split_rows(z0, z1, p_ref, s_ref):
    p_ref[:TM_P, :] = z0[:TM_P]
    p_ref[TM_P:, :] = z1[:TM_P]
    s_ref[0, :TM_S, :] = z0[TM_P:]
    s_ref[0, TM_S:, :] = z1[TM_P:]


def _proj_kernel(xn_ref, w_ref, wg_ref, zp_ref, zs_ref, gp_ref, gs_ref):
    w = w_ref[...].astype(BF16)
    _split_rows(_bdot_t(xn_ref[0], w), _bdot_t(xn_ref[1], w), zp_ref, zs_ref)

    @pl.when(pl.program_id(1) == 0)
    def _():
        wg = wg_ref[...].astype(BF16)
        _split_rows(_bdot_t(xn_ref[0], wg), _bdot_t(xn_ref[1], wg), gp_ref, gs_ref)


def _proj(xn_c, w_t):
    n_n = (4 * D_A) // PROJ_TN
    return pl.pallas_call(
        _proj_kernel,
        grid=(N_MIX_BLOCKS, n_n),
        in_specs=[
            pl.BlockSpec((2, TM, D_MODEL), lambda i, j: (i, 0, 0)),
            pl.BlockSpec((PROJ_TN, D_MODEL), lambda i, j: (j, 0)),
            pl.BlockSpec((LANE, D_MODEL), lambda i, j: (GATE_ROW0 // LANE, 0)),
        ],
        out_specs=[
            pl.BlockSpec((MP, PROJ_TN), lambda i, j: (i, j)),
            pl.BlockSpec((1, MS, PROJ_TN), lambda i, j: (i, 0, j)),
            pl.BlockSpec((MP, LANE), lambda i, j: (i, 0)),
            pl.BlockSpec((1, MS, LANE), lambda i, j: (i, 0, 0)),
        ],
        out_shape=[
            jax.ShapeDtypeStruct((N_MIX_BLOCKS * MP, 4 * D_A), F32),
            jax.ShapeDtypeStruct((N_MIX_BLOCKS, MS, 4 * D_A), F32),
            jax.ShapeDtypeStruct((N_MIX_BLOCKS * MP, LANE), F32),
            jax.ShapeDtypeStruct((N_MIX_BLOCKS, MS, LANE), F32),
        ],
        compiler_params=pltpu.CompilerParams(
            dimension_semantics=("parallel", "arbitrary"), vmem_limit_bytes=VMEM_LIMIT),
        name="proj_qkvo",
    )(xn_c, w_t, w_t)


def _group_norm(yb, nrm):
    gw = D_B // G_B
    parts = []
    for g in range(yb.shape[1] // gw):
        seg = yb[:, g * gw:(g + 1) * gw]
        parts.append(seg * lax.rsqrt(jnp.mean(seg * seg, axis=-1, keepdims=True) + EPS))
    return jnp.concatenate(parts, axis=1) * nrm


def _conv_kernel(xn_ref, wgb_ref, wgc_ref, wxc_ref, cw_ref, cb_ref, nrm_ref, buf_ref, q_ref, c_ref,
                 yb_ref, cp_ref, cs_ref, qc_ref):
    for h in range(H_A):
        hs = slice(h * DK, (h + 1) * DK)
        qb = q_ref[:, hs].astype(BF16)
        for j in range(SAMPLE_BS):
            qc = _bdot(qb, c_ref[0, j, h].astype(BF16))
            qc_ref[j:j + 1, hs] = qc[j:j + 1, :]

    wgb = wgb_ref[...].astype(BF16)
    wgc = wgc_ref[...].astype(BF16)
    wxc = wxc_ref[...].astype(BF16)
    cw0 = cw_ref[0:1, :]
    cw1 = cw_ref[1:2, :]
    cw2 = cw_ref[2:3, :]
    cb = cb_ref[...]
    nrm = nrm_ref[...]
    row = lax.broadcasted_iota(jnp.int32, (TM_P, wgb.shape[0]), 0)
    zero_row = jnp.zeros((1, wgb.shape[0]), F32)
    um2, um1 = zero_row, zero_row

    for t in range(2):
        xn = xn_ref[t]
        gb = _bdot_t(xn, wgb)
        u = _bdot_t(xn, wgc) * _bdot_t(xn, wxc)

        up = u[:TM_P]
        u1 = jnp.where(row < 1, um1, pltpu.roll(up, 1, 0))
        u2 = jnp.where(row < 1, um2, jnp.where(row < 2, um1, pltpu.roll(up, 2, 0)))
        yc = cw0 * u2 + cw1 * u1 + cw2 * up + cb
        yb_ref[0, t * TM_P:(t + 1) * TM_P, :] = _group_norm(gb[:TM_P] * yc, nrm).astype(yb_ref.dtype)
        um2, um1 = up[TM_P - 2:TM_P - 1, :], up[TM_P - 1:TM_P, :]

        ss = slice(t * TM_S, (t + 1) * TM_S)
        us = u[TM_P:]
        b0 = buf_ref[0, ss, 0, :]
        b1 = buf_ref[0, ss, 1, :]
        ycs = cw0 * b0 + cw1 * b1 + cw2 * us + cb
        yb_ref[0, MP + t * TM_S:MP + (t + 1) * TM_S, :] = _group_norm(gb[TM_P:] * ycs, nrm).astype(yb_ref.dtype)
        cs_ref[0, ss, 0, :] = b1
        cs_ref[0, ss, 1, :] = us

    cp_ref[0, 0, 0:1, :] = um2
    cp_ref[0, 0, 1:2, :] = um1


def _conv(xn_c, w_t, cw, cb_row, nrm_row, buf, za_s, c0):
    tc = CONV_TC
    n_c = D_B // tc
    assert N_MIX_BLOCKS * n_c * SAMPLE_BS == za_s.shape[0], "one sample group per grid step"
    wspec = lambda base: pl.BlockSpec(
        (pl.Element(tc), pl.Element(D_MODEL)), lambda i, c: (pl.multiple_of(base + c * tc, 8), 0))
    grp = lambda i, c: i * n_c + c
    return pl.pallas_call(
        _conv_kernel,
        grid=(N_MIX_BLOCKS, n_c),
        in_specs=[
            pl.BlockSpec((2, TM, D_MODEL), lambda i, c: (i, 0, 0)),
            wspec(CONV_ROW0),
            wspec(CONV_ROW0 + D_B),
            wspec(CONV_ROW0 + 2 * D_B),
            pl.BlockSpec((3, tc), lambda i, c: (0, c)),
            pl.BlockSpec((1, tc), lambda i, c: (0, c)),
            pl.BlockSpec((1, tc), lambda i, c: (0, c)),
            pl.BlockSpec((1, MS, 2, tc), lambda i, c: (i, 0, 0, c)),
            pl.BlockSpec((SAMPLE_BS, D_A), lambda i, c: (grp(i, c), 0)),
            pl.BlockSpec((1, SAMPLE_BS, H_A, DK, DV), lambda i, c: (0, grp(i, c), 0, 0, 0)),
        ],
        out_specs=[
            pl.BlockSpec((1, MP + MS, tc), lambda i, c: (i, 0, c)),
            pl.BlockSpec((1, 1, 2, tc), lambda i, c: (0, i, 0, c)),
            pl.BlockSpec((1, MS, 2, tc), lambda i, c: (i, 0, 0, c)),
            pl.BlockSpec((SAMPLE_BS, D_A), lambda i, c: (grp(i, c), 0)),
        ],
        out_shape=[
            jax.ShapeDtypeStruct((N_MIX_BLOCKS, MP + MS, D_B), BF16),
            jax.ShapeDtypeStruct((1, N_MIX_BLOCKS, 2, D_B), F32),
            jax.ShapeDtypeStruct((N_MIX_BLOCKS, MS, 2, D_B), F32),
            jax.ShapeDtypeStruct((za_s.shape[0], D_A), F32),
        ],
        compiler_params=pltpu.CompilerParams(
            dimension_semantics=("parallel", "arbitrary"), vmem_limit_bytes=VMEM_LIMIT),
        name="proj_conv",
    )(xn_c, w_t, w_t, w_t, cw, cb_row, nrm_row, buf, za_s, c0)


def _mlstm_prompt_kernel(q_ref, k_ref, v_ref, o_ref, g_ref, bias_ref, nrm_ref,
                         ha_ref, c_ref, n_ref, m_ref, cx_s, m_s, *, n_chunks):
    L = MLSTM_CHUNK
    c = pl.program_id(1)

    @pl.when(c == 0)
    def _():
        cx_s[...] = jnp.zeros_like(cx_s)
        m_s[...] = jnp.zeros_like(m_s)

    row = lax.broadcasted_iota(jnp.int32, (L, L), 0)
    col = lax.broadcasted_iota(jnp.int32, (L, L), 1)
    causal = row >= col
    tri = causal.astype(BF16)

    for b in range(MLSTM_SEQS):
        _mlstm_prompt_seq(b, causal, tri, q_ref, k_ref, v_ref, o_ref, g_ref, bias_ref, nrm_ref,
                          ha_ref, cx_s, m_s)

    @pl.when(c == n_chunks - 1)
    def _():
        for b in range(MLSTM_SEQS):
            m_ref[b] = m_s[b]
            for h in range(H_A):
                cx = cx_s[b * H_A + h]
                c_ref[0, b, h] = cx[:, :DV]
                n_ref[0, b, h:h + 1, :] = cx[:, DV:].T[h:h + 1, :]


def _prefix_max_rows(x):
    rows = x.shape[0]
    row = lax.broadcasted_iota(jnp.int32, x.shape, 0)
    k = 1
    while k < rows:
        x = jnp.where(row >= k, jnp.maximum(x, pltpu.roll(x, k, 0)), x)
        k *= 2
    return x


def _mlstm_prompt_seq(b, causal, tri, q_ref, k_ref, v_ref, o_ref, g_ref, bias_ref, nrm_ref,
                      ha_ref, cx_s, m_s):
    L = MLSTM_CHUNK
    lane = lax.broadcasted_iota(jnp.int32, (L, LANE), 1)
    lane_dv = lax.broadcasted_iota(jnp.int32, (DV, LANE), 1)
    capped = _soft_cap(g_ref[b] + bias_ref[...])
    lf = _log_sigmoid(capped)
    hi = lf.astype(BF16)
    r1 = lf - hi.astype(F32)
    mid = r1.astype(BF16)
    lo = (r1 - mid.astype(F32)).astype(BF16)
    parts = _bdot(tri, jnp.concatenate([hi, mid, lo], axis=1))
    bc = parts[:, :LANE] + parts[:, LANE:2 * LANE] + parts[:, 2 * LANE:]
    bh = pltpu.roll(bc, LANE - H_A, 1)
    a = capped - bh
    a_t = a.T
    m_prev = m_s[b]
    big_m = jnp.maximum(_prefix_max_rows(a), m_prev[0:1, :])
    sc_all = jnp.exp(m_prev[0:1, :] - big_m)
    emt_all = jnp.exp(-(bh + big_m))
    m_last = big_m[L - 1:L, :]
    w_all = jnp.exp(a - m_last)
    decay_all = sc_all[L - 1:L, :]
    m_s[b] = jnp.broadcast_to(bh[L - 1:L, :] + m_last, (8, LANE))

    nums = []
    nd_tail = jnp.zeros((L, LANE), F32)
    sq_tail = jnp.zeros((L, LANE), F32)
    for h in range(H_A):
        hs = slice(h * DK, (h + 1) * DK)
        onehot = (lane == h).astype(BF16)
        qb = q_ref[b, :, hs].astype(BF16)
        kf = k_ref[b, :, hs] * (DK ** -0.5)
        kb = kf.astype(BF16)
        vx = jnp.concatenate([v_ref[b, :, hs].astype(BF16), onehot], axis=1)
        cx_prev = cx_s[b * H_A + h]

        p = jnp.exp(jnp.where(causal, a_t[h:h + 1, :] - big_m[:, h:h + 1], -jnp.inf))
        s = _bdot_t(qb, kb) * p
        nd = sc_all[:, h:h + 1] * _bdot(qb, cx_prev.astype(BF16)) + _bdot(s.astype(BF16), vx)
        num = nd[:, :DV]
        nums.append(num)
        nd_tail = nd_tail + nd[:, DV:]
        sq_tail = sq_tail + _bdot((num * num).astype(BF16), (lane_dv == h).astype(BF16))

        wk = (w_all[:, h:h + 1] * kf).astype(BF16)
        cx_s[b * H_A + h] = decay_all[:, h:h + 1] * cx_prev + lax.dot_general(
            wk, vx, (((0,), (0,)), ((), ())), preferred_element_type=F32)

    inv = 1.0 / jnp.maximum(jnp.abs(nd_tail), emt_all)
    ms = inv * inv * sq_tail * (1.0 / DV)
    scale_all = inv * lax.rsqrt(ms + EPS)
    for h in range(H_A):
        hs = slice(h * DK, (h + 1) * DK)
        hn = nums[h] * scale_all[:, h:h + 1]
        ha_ref[b, :, hs] = (hn * nrm_ref[:, hs] * jax.nn.sigmoid(o_ref[b, :, hs])).astype(ha_ref.dtype)


def _mlstm_prompt(za_p, g_p, bias_row, nrm_row, batch, seq):
    L = MLSTM_CHUNK
    nsq = MLSTM_SEQS
    n_chunks = seq // L
    kern = functools.partial(_mlstm_prompt_kernel, n_chunks=n_chunks)
    za3 = za_p.reshape(batch, seq, 4 * D_A)
    g3 = g_p.reshape(batch, seq, LANE)
    zspec = lambda part: pl.BlockSpec((nsq, L, D_A), lambda b, c: (b, c, part))
    return pl.pallas_call(
        kern,
        grid=(batch // nsq, n_chunks),
        in_specs=[
            zspec(0), zspec(1), zspec(2), zspec(3),
            pl.BlockSpec((nsq, L, LANE), lambda b, c: (b, c, 0)),
            pl.BlockSpec((1, LANE), lambda b, c: (0, 0)),
            pl.BlockSpec((1, D_A), lambda b, c: (0, 0)),
        ],
        out_specs=[
            pl.BlockSpec((nsq, L, D_A), lambda b, c: (b, c, 0)),
            pl.BlockSpec((1, nsq, H_A, DK, DV), lambda b, c: (0, b, 0, 0, 0)),
            pl.BlockSpec((1, nsq, H_A, DK), lambda b, c: (0, b, 0, 0)),
            pl.BlockSpec((nsq, 8, LANE), lambda b, c: (b, 0, 0)),
        ],
        out_shape=[
            jax.ShapeDtypeStruct((batch, seq, D_A), BF16),
            jax.ShapeDtypeStruct((1, batch, H_A, DK, DV), F32),
            jax.ShapeDtypeStruct((1, batch, H_A, DK), F32),
            jax.ShapeDtypeStruct((batch, 8, LANE), F32),
        ],
        scratch_shapes=[pltpu.VMEM((nsq * H_A, DK, DV + LANE), F32),
                        pltpu.VMEM((nsq, 8, LANE), F32)],
        compiler_params=pltpu.CompilerParams(
            dimension_semantics=("parallel", "arbitrary"), vmem_limit_bytes=VMEM_LIMIT),
        name="mlstm_prompt",
    )(za3, za3, za3, za3, g3, bias_row, nrm_row)


def _expand_heads(x, width):
    rows = x.shape[0]
    return jnp.concatenate([jnp.broadcast_to(x[:, h:h + 1], (rows, width)) for h in range(H_A)], axis=1)


def _head_sums(x):
    return jnp.concatenate(
        [jnp.sum(x[:, h * DK:(h + 1) * DK], axis=-1, keepdims=True) for h in range(H_A)], axis=1)


def _mlstm_sample_kernel(za_ref, qc_ref, g_ref, bias_ref, nrm_ref, m_ref, n_ref,
                         ha_ref, m_out, n_out, wk_out, dec_out):
    nb = za_ref.shape[0]
    capped = _soft_cap(g_ref[...] + bias_ref[...])
    logi = capped[:, 0:H_A]
    logf = _log_sigmoid(capped)[:, H_A:2 * H_A]
    m_prev = m_ref[...]
    m_inter = logf + m_prev
    m_t = jnp.maximum(m_inter, logi)
    sc = jnp.exp(m_inter - m_t)
    ei = jnp.exp(logi - m_t)
    emt = jnp.exp(-m_t)

    q = za_ref[:, 0:D_A]
    k = za_ref[:, D_A:2 * D_A] * (DK ** -0.5)
    v = za_ref[:, 2 * D_A:3 * D_A]
    o = za_ref[:, 3 * D_A:4 * D_A]
    n_prev = n_ref[...]

    s = _head_sums(q * k) * ei
    den = sc * _head_sums(q * n_prev) + s
    denom = jnp.maximum(jnp.abs(den), emt)
    num = _expand_heads(sc, DV) * qc_ref[...] + _expand_heads(s, DV) * v
    hh = num / _expand_heads(denom, DV)
    ms = _head_sums(hh * hh) * (1.0 / DV)
    hn = hh * lax.rsqrt(_expand_heads(ms, DV) + EPS)
    ha_ref[...] = (hn * nrm_ref[...] * jax.nn.sigmoid(o)).astype(ha_ref.dtype)
    n_out[...] = _expand_heads(sc, DK) * n_prev + _expand_heads(ei, DK) * k
    m_out[...] = m_t
    pad = jnp.zeros((nb, LANE - H_A), F32)
    dec_out[...] = jnp.concatenate([sc, pad], axis=1)

    k_t = k.T
    ei_t = jnp.concatenate([ei, pad], axis=1).T
    for h in range(H_A):
        hs = slice(h * DK, (h + 1) * DK)
        wk_out[hs, :] = k_t[hs, :] * ei_t[h:h + 1, :]


def _mlstm_sample(za_s, qc, g_s, bias_row, nrm_row, m0, n0):
    nb = za_s.shape[0]
    assert nb == LANE, "the sequence axis is transposed onto the lanes"
    return pl.pallas_call(
        _mlstm_sample_kernel,
        out_shape=[
            jax.ShapeDtypeStruct((nb, D_A), BF16),
            jax.ShapeDtypeStruct((nb, H_A), F32),
            jax.ShapeDtypeStruct((nb, H_A * DK), F32),
            jax.ShapeDtypeStruct((D_A, nb), F32),
            jax.ShapeDtypeStruct((nb, LANE), F32),
        ],
        compiler_params=pltpu.CompilerParams(vmem_limit_bytes=VMEM_LIMIT),
        name="mlstm_sample",
    )(za_s, qc, g_s, bias_row, nrm_row, m0, n0)


def _outproj_kernel(hp_ref, hs_ref, ap_ref, as_ref, yb_ref, w_ref, op_ref, os_ref, wc_ref):
    j = pl.program_id(1)

    @pl.when(pl.program_id(0) == 0)
    def _():
        wc_ref[j] = w_ref[...].astype(BF16)

    wa = wc_ref[j, :D_A, :]
    wb = wc_ref[j, D_A:, :]
    op_ref[...] = hp_ref[...] + _bdot(ap_ref[...], wa) + _bdot(yb_ref[0, :MP, :], wb)
    os_ref[0] = hs_ref[0] + _bdot(as_ref[0], wa) + _bdot(yb_ref[0, MP:, :], wb)


def _outproj(hp, hs, ap, a_s, yb_c, w3):
    tn = OUT_TN
    n_n = D_MODEL // tn
    return pl.pallas_call(
        _outproj_kernel,
        grid=(N_MIX_BLOCKS, n_n),
        in_specs=[
            pl.BlockSpec((MP, tn), lambda i, j: (i, j)),
            pl.BlockSpec((1, MS, tn), lambda i, j: (i, 0, j)),
            pl.BlockSpec((MP, D_A), lambda i, j: (i, 0)),
            pl.BlockSpec((1, MS, D_A), lambda i, j: (i, 0, 0)),
            pl.BlockSpec((1, MP + MS, D_B), lambda i, j: (i, 0, 0)),
            pl.BlockSpec((None, D_A + D_B, tn), lambda i, j: (0, 0, jnp.where(i == 0, j, n_n - 1))),
        ],
        out_specs=[
            pl.BlockSpec((MP, tn), lambda i, j: (i, j)),
            pl.BlockSpec((1, MS, tn), lambda i, j: (i, 0, j)),
        ],
        out_shape=[
            jax.ShapeDtypeStruct((N_MIX_BLOCKS * MP, D_MODEL), F32),
            jax.ShapeDtypeStruct((N_MIX_BLOCKS, MS, D_MODEL), F32),
        ],
        scratch_shapes=[pltpu.VMEM((n_n, D_A + D_B, tn), BF16)],
        compiler_params=pltpu.CompilerParams(
            dimension_semantics=("arbitrary", "arbitrary"), vmem_limit_bytes=VMEM_LIMIT),
        name="outproj",
    )(hp, hs, ap, a_s, yb_c, w3)


def kernel(x_prompt, x_sample, state_mlstm_C, state_mlstm_n, state_mlstm_m, state_conv, norm_ffn1, ffn1_gate, ffn1_up, ffn1_down, norm_mix, w_in, b_gates, conv_w, conv_b, norm_mlstm, norm_conv, w_out, norm_ffn2, ffn2_gate, ffn2_up, ffn2_down, norm_final):
    batch, seq, _ = x_prompt.shape
    nb = x_sample.shape[0]
    assert batch == N_MIX_BLOCKS and seq == MP and nb == N_MIX_BLOCKS * MS
    assert norm_ffn1.shape[0] == 1, "single-layer trunk"

    xp = x_prompt.reshape(batch * seq, D_MODEL)
    xs = x_sample.reshape(N_FFN_BLOCKS, TM_S, D_MODEL)

    h1p, h1s, xn_c = _ffn(xp, xs, norm_ffn1, ffn1_gate, ffn1_up, ffn1_down, norm_mix)

    w_t = jnp.swapaxes(w_in[0], 0, 1)
    za_p, za_s, g_p, g_s = _proj(xn_c, w_t)
    za_s = za_s.reshape(nb, 4 * D_A)
    yb_c, conv_p, conv_s, qc_s = _conv(xn_c, w_t, conv_w[0], conv_b, norm_conv,
                                       state_conv.reshape(N_MIX_BLOCKS, MS, 2, D_B), za_s, state_mlstm_C)

    bias_row = jnp.zeros((1, LANE), F32).at[0, :2 * H_A].set(b_gates[0].astype(F32))
    ha_p, c_p, n_p, m_p = _mlstm_prompt(za_p, g_p, bias_row, norm_mlstm, batch, seq)
    ha_s, m_s, n_s, wk_s, dec_s = _mlstm_sample(
        za_s, qc_s, g_s.reshape(nb, LANE), bias_row, norm_mlstm,
        state_mlstm_m.reshape(nb, H_A), state_mlstm_n.reshape(nb, H_A * DK))

    h2p, h2s = _outproj(h1p, h1s.reshape(N_MIX_BLOCKS, MS, D_MODEL), ha_p.reshape(batch * seq, D_A),
                        ha_s.reshape(N_MIX_BLOCKS, MS, D_A), yb_c, w_out)

    yp, ys, c_s = _ffn(h2p, h2s.reshape(N_FFN_BLOCKS, TM_S, D_MODEL), norm_ffn2, ffn2_gate, ffn2_up,
                       ffn2_down, norm_final.reshape(1, D_MODEL),
                       state=(state_mlstm_C, wk_s, za_s, dec_s))

    return (
        yp.reshape(batch, seq, D_MODEL),
        ys.reshape(nb, 1, D_MODEL),
        c_p,
        n_p,
        m_p[:, 0, :H_A].reshape(1, batch, H_A),
        conv_p,
        c_s,
        n_s.reshape(1, nb, H_A, DK),
        m_s.reshape(1, nb, H_A),
        conv_s.reshape(1, nb, 2, D_B),
    )
```

```python
import functools

import jax
import jax.numpy as jnp
from jax import lax
from jax.experimental import pallas as pl
from jax.experimental.pallas import tpu as pltpu

F32 = jnp.float32
BF16 = jnp.bfloat16

D_MODEL = 2048
D_A = 1024
D_B = 1024
H_A = 4
DK = 256
DV = 256
G_B = 8
D_FF = 5504
GATE_CAP = 15.0
EPS = 1e-6
GATE_ROW0 = 4 * D_A
CONV_ROW0 = 4 * D_A + 2 * H_A

LANE = 128
N_FFN_BLOCKS = 8
TM_P = 1024
TM_H = TM_P // 2
TM_S = 16
TM = TM_P + TM_S
TF = 256
N_MIX_BLOCKS = N_FFN_BLOCKS // 2
MP = 2 * TM_P
MS = 2 * TM_S
PROJ_TN = 1024
CONV_TC = 256
OUT_TN = 512
MLSTM_CHUNK = 256
MLSTM_SEQS = 4
SAMPLE_BS = 8
VMEM_LIMIT = 62 * 1024 * 1024


def _rms(x, g):
    return x * lax.rsqrt(jnp.mean(x * x, axis=-1, keepdims=True) + EPS) * g


def _soft_cap(x):
    return GATE_CAP * jnp.tanh(x / GATE_CAP)


def _log_sigmoid(x):
    return -jax.nn.softplus(-x)


def _bdot(a, b):
    return jnp.dot(a, b, preferred_element_type=F32)


def _bdot_t(a, b):
    return lax.dot_general(a, b, (((1,), (1,)), ((), ())), preferred_element_type=F32)


def _sample_state_update(f, c_ref, wk_ref, v_ref, dec_ref, c_out):
    r = jnp.minimum(f, TM_S - 1)
    seq = pl.program_id(0) * TM_S + r
    lane = lax.broadcasted_iota(jnp.int32, (D_A, LANE), 1)
    wk_col = jnp.sum(jnp.where(lane == seq, wk_ref[...], 0.0), axis=1, keepdims=True)
    v_row = v_ref[pl.ds(r, 1), :]
    dec = dec_ref[pl.ds(r, 1), :]
    for h in range(H_A):
        hs = slice(h * DK, (h + 1) * DK)
        c_out[0, 0, h] = dec[:, h:h + 1] * c_ref[0, 0, h] + wk_col[hs, :] * v_row[:, hs]


def _ffn_kernel(xa_ref, xb_ref, xs_ref, g_ref, wg_ref, wu_ref, wd_ref, g2_ref, *rest,
                n_f, last_valid, final_norm):
    if final_norm:
        c_ref, wk_ref, v_ref, dec_ref, op_ref, os_ref, c_out, xn_ref = rest
    else:
        op_ref, os_ref, nxt_ref, xn_ref = rest
    f = pl.program_id(1)

    def step(valid, first, with_state):
        if first:
            g = g_ref[...]
            xn_ref[:TM_H, :] = _rms(xa_ref[...], g).astype(BF16)
            xn_ref[TM_H:TM_P, :] = _rms(xb_ref[...], g).astype(BF16)
            xn_ref[TM_P:, :] = _rms(xs_ref[0], g).astype(BF16)
        xn = xn_ref[...]
        a = _bdot(xn, wg_ref[:, :valid].astype(BF16))
        b = _bdot(xn, wu_ref[:, :valid].astype(BF16))
        hid = (a * jax.nn.sigmoid(a) * b * 0.5).astype(BF16)
        r = _bdot(hid, wd_ref[:valid, :].astype(BF16))
        if first:
            op_ref[:TM_H, :] = xa_ref[...] + r[:TM_H]
            op_ref[TM_H:, :] = xb_ref[...] + r[TM_H:TM_P]
            os_ref[0] = xs_ref[0] + r[TM_P:]
        else:
            op_ref[...] += r[:TM_P]
            os_ref[0] += r[TM_P:]
        if with_state:
            _sample_state_update(f, c_ref, wk_ref, v_ref, dec_ref, c_out)

    assert last_valid < TF and n_f - 1 >= TM_S, "state update rides the full-width steps"

    @pl.when(f == 0)
    def _():
        step(TF, True, final_norm)

    @pl.when(jnp.logical_and(f > 0, f < n_f - 1))
    def _():
        step(TF, False, final_norm)

    @pl.when(f == n_f - 1)
    def _():
        step(last_valid, False, False)
        g2 = g2_ref[...]
        if final_norm:
            op_ref[...] = _rms(op_ref[...], g2)
            os_ref[0] = _rms(os_ref[0], g2)
        else:
            nxt_ref[0, :TM_P, :] = _rms(op_ref[...], g2).astype(BF16)
            nxt_ref[0, TM_P:, :] = _rms(os_ref[0], g2).astype(BF16)


def _ffn(xp, xs, g, wg, wu, wd, g2, state=None):
    final_norm = state is not None
    n_f = pl.cdiv(D_FF, TF)
    last_valid = D_FF - (n_f - 1) * TF
    kern = functools.partial(_ffn_kernel, n_f=n_f, last_valid=last_valid, final_norm=final_norm)
    def ahead(half, start):
        return lambda i, f: (2 * jnp.minimum(i + (f >= start).astype(jnp.int32), N_FFN_BLOCKS - 1) + half, 0)
    in_specs = [
        pl.BlockSpec((TM_H, D_MODEL), ahead(0, 2)),
        pl.BlockSpec((TM_H, D_MODEL), ahead(1, 4)),
        pl.BlockSpec((1, TM_S, D_MODEL), lambda i, f: (i, 0, 0)),
        pl.BlockSpec((1, D_MODEL), lambda i, f: (0, 0)),
        pl.BlockSpec((None, D_MODEL, TF), lambda i, f: (0, 0, f)),
        pl.BlockSpec((None, D_MODEL, TF), lambda i, f: (0, 0, f)),
        pl.BlockSpec((None, TF, D_MODEL), lambda i, f: (0, f, 0)),
        pl.BlockSpec((1, D_MODEL), lambda i, f: (0, 0)),
    ]
    out_specs = [
        pl.BlockSpec((TM_P, D_MODEL), lambda i, f: (i, 0)),
        pl.BlockSpec((1, TM_S, D_MODEL), lambda i, f: (i, 0, 0)),
    ]
    out_shape = [
        jax.ShapeDtypeStruct((N_FFN_BLOCKS * TM_P, D_MODEL), F32),
        jax.ShapeDtypeStruct((N_FFN_BLOCKS, TM_S, D_MODEL), F32),
    ]
    args = [xp, xp, xs, g, wg, wu, wd, g2]
    if final_norm:
        c0, wk, za_s, dec = state
        seq = lambda i, f: i * TM_S + jnp.minimum(f, TM_S - 1)
        cspec = pl.BlockSpec((1, 1, H_A, DK, DV), lambda i, f: (0, seq(i, f), 0, 0, 0))
        in_specs += [
            cspec,
            pl.BlockSpec((D_A, LANE), lambda i, f: (0, 0)),
            pl.BlockSpec((TM_S, D_A), lambda i, f: (i, 2)),
            pl.BlockSpec((TM_S, LANE), lambda i, f: (i, 0)),
        ]
        out_specs.append(cspec)
        out_shape.append(jax.ShapeDtypeStruct(c0.shape, F32))
        args += [c0, wk, za_s, dec]
    else:
        out_specs.append(pl.BlockSpec((1, TM, D_MODEL), lambda i, f: (i, 0, 0)))
        out_shape.append(jax.ShapeDtypeStruct((N_FFN_BLOCKS, TM, D_MODEL), BF16))
    return pl.pallas_call(
        kern,
        grid=(N_FFN_BLOCKS, n_f),
        in_specs=in_specs,
        out_specs=out_specs,
        out_shape=out_shape,
        scratch_shapes=[pltpu.VMEM((TM, D_MODEL), BF16)],
        compiler_params=pltpu.CompilerParams(
            dimension_semantics=("parallel", "arbitrary"), vmem_limit_bytes=VMEM_LIMIT),
        name="ffn_final" if final_norm else "ffn",
    )(*args)


def _split_rows(z0, z1, p_ref, s_ref):
    p_ref[:TM_P, :] = z0[:TM_P]
    p_ref[TM_P:, :] = z1[:TM_P]
    s_ref[0, :TM_S, :] = z0[TM_P:]
    s_ref[0, TM_S:, :] = z1[TM_P:]


def _proj_kernel(xn_ref, w_ref, wg_ref, zp_ref, zs_ref, gp_ref, gs_ref):
    w = w_ref[...].astype(BF16)
    _split_rows(_bdot_t(xn_ref[0], w), _bdot_t(xn_ref[1], w), zp_ref, zs_ref)

    @pl.when(pl.program_id(1) == 0)
    def _():
        wg = wg_ref[...].astype(BF16)
        _split_rows(_bdot_t(xn_ref[0], wg), _bdot_t(xn_ref[1], wg), gp_ref, gs_ref)


def _proj(xn_c, w_t):
    n_n = (4 * D_A) // PROJ_TN
    return pl.pallas_call(
        _proj_kernel,
        grid=(N_MIX_BLOCKS, n_n),
        in_specs=[
            pl.BlockSpec((2, TM, D_MODEL), lambda i, j: (i, 0, 0)),
            pl.BlockSpec((PROJ_TN, D_MODEL), lambda i, j: (j, 0)),
            pl.BlockSpec((LANE, D_MODEL), lambda i, j: (GATE_ROW0 // LANE, 0)),
        ],
        out_specs=[
            pl.BlockSpec((MP, PROJ_TN), lambda i, j: (i, j)),
            pl.BlockSpec((1, MS, PROJ_TN), lambda i, j: (i, 0, j)),
            pl.BlockSpec((MP, LANE), lambda i, j: (i, 0)),
            pl.BlockSpec((1, MS, LANE), lambda i, j: (i, 0, 0)),
        ],
        out_shape=[
            jax.ShapeDtypeStruct((N_MIX_BLOCKS * MP, 4 * D_A), F32),
            jax.ShapeDtypeStruct((N_MIX_BLOCKS, MS, 4 * D_A), F32),
            jax.ShapeDtypeStruct((N_MIX_BLOCKS * MP, LANE), F32),
            jax.ShapeDtypeStruct((N_MIX_BLOCKS, MS, LANE), F32),
        ],
        compiler_params=pltpu.CompilerParams(
            dimension_semantics=("parallel", "arbitrary"), vmem_limit_bytes=VMEM_LIMIT),
        name="proj_qkvo",
    )(xn_c, w_t, w_t)


def _group_norm(yb, nrm):
    gw = D_B // G_B
    parts = []
    for g in range(yb.shape[1] // gw):
        seg = yb[:, g * gw:(g + 1) * gw]
        parts.append(seg * lax.rsqrt(jnp.mean(seg * seg, axis=-1, keepdims=True) + EPS))
    return jnp.concatenate(parts, axis=1) * nrm


def _conv_kernel(xn_ref, wgb_ref, wgc_ref, wxc_ref, cw_ref, cb_ref, nrm_ref, buf_ref, q_ref, c_ref,
                 yb_ref, cp_ref, cs_ref, qc_ref):
    for h in range(H_A):
        hs = slice(h * DK, (h + 1) * DK)
        qb = q_ref[:, hs].astype(BF16)
        for j in range(SAMPLE_BS):
            qc = _bdot(qb, c_ref[0, j, h].astype(BF16))
            qc_ref[j:j + 1, hs] = qc[j:j + 1, :]

    wgb = wgb_ref[...].astype(BF16)
    wgc = wgc_ref[...].astype(BF16)
    wxc = wxc_ref[...].astype(BF16)
    cw0 = cw_ref[0:1, :]
    cw1 = cw_ref[1:2, :]
    cw2 = cw_ref[2:3, :]
    cb = cb_ref[...]
    nrm = nrm_ref[...]
    row = lax.broadcasted_iota(jnp.int32, (TM_P, wgb.shape[0]), 0)
    zero_row = jnp.zeros((1, wgb.shape[0]), F32)
    um2, um1 = zero_row, zero_row

    for t in range(2):
        xn = xn_ref[t]
        gb = _bdot_t(xn, wgb)
        u = _bdot_t(xn, wgc) * _bdot_t(xn, wxc)

        up = u[:TM_P]
        u1 = jnp.where(row < 1, um1, pltpu.roll(up, 1, 0))
        u2 = jnp.where(row < 1, um2, jnp.where(row < 2, um1, pltpu.roll(up, 2, 0)))
        yc = cw0 * u2 + cw1 * u1 + cw2 * up + cb
        yb_ref[0, t * TM_P:(t + 1) * TM_P, :] = _group_norm(gb[:TM_P] * yc, nrm).astype(yb_ref.dtype)
        um2, um1 = up[TM_P - 2:TM_P - 1, :], up[TM_P - 1:TM_P, :]

        ss = slice(t * TM_S, (t + 1) * TM_S)
        us = u[TM_P:]
        b0 = buf_ref[0, ss, 0, :]
        b1 = buf_ref[0, ss, 1, :]
        ycs = cw0 * b0 + cw1 * b1 + cw2 * us + cb
        yb_ref[0, MP + t * TM_S:MP + (t + 1) * TM_S, :] = _group_norm(gb[TM_P:] * ycs, nrm).astype(yb_ref.dtype)
        cs_ref[0, ss, 0, :] = b1
        cs_ref[0, ss, 1, :] = us

    cp_ref[0, 0, 0:1, :] = um2
    cp_ref[0, 0, 1:2, :] = um1


def _conv(xn_c, w_t, cw, cb_row, nrm_row, buf, za_s, c0):
    tc = CONV_TC
    n_c = D_B // tc
    assert N_MIX_BLOCKS * n_c * SAMPLE_BS == za_s.shape[0], "one sample group per grid step"
    wspec = lambda base: pl.BlockSpec(
        (pl.Element(tc), pl.Element(D_MODEL)), lambda i, c: (pl.multiple_of(base + c * tc, 8), 0))
    grp = lambda i, c: i * n_c + c
    return pl.pallas_call(
        _conv_kernel,
        grid=(N_MIX_BLOCKS, n_c),
        in_specs=[
            pl.BlockSpec((2, TM, D_MODEL), lambda i, c: (i, 0, 0)),
            wspec(CONV_ROW0),
            wspec(CONV_ROW0 + D_B),
            wspec(CONV_ROW0 + 2 * D_B),
            pl.BlockSpec((3, tc), lambda i, c: (0, c)),
            pl.BlockSpec((1, tc), lambda i, c: (0, c)),
            pl.BlockSpec((1, tc), lambda i, c: (0, c)),
            pl.BlockSpec((1, MS, 2, tc), lambda i, c: (i, 0, 0, c)),
            pl.BlockSpec((SAMPLE_BS, D_A), lambda i, c: (grp(i, c), 0)),
            pl.BlockSpec((1, SAMPLE_BS, H_A, DK, DV), lambda i, c: (0, grp(i, c), 0, 0, 0)),
        ],
        out_specs=[
            pl.BlockSpec((1, MP + MS, tc), lambda i, c: (i, 0, c)),
            pl.BlockSpec((1, 1, 2, tc), lambda i, c: (0, i, 0, c)),
            pl.BlockSpec((1, MS, 2, tc), lambda i, c: (i, 0, 0, c)),
            pl.BlockSpec((SAMPLE_BS, D_A), lambda i, c: (grp(i, c), 0)),
        ],
        out_shape=[
            jax.ShapeDtypeStruct((N_MIX_BLOCKS, MP + MS, D_B), BF16),
            jax.ShapeDtypeStruct((1, N_MIX_BLOCKS, 2, D_B), F32),
            jax.ShapeDtypeStruct((N_MIX_BLOCKS, MS, 2, D_B), F32),
            jax.ShapeDtypeStruct((za_s.shape[0], D_A), F32),
        ],
        compiler_params=pltpu.CompilerParams(
            dimension_semantics=("parallel", "arbitrary"), vmem_limit_bytes=VMEM_LIMIT),
        name="proj_conv",
    )(xn_c, w_t, w_t, w_t, cw, cb_row, nrm_row, buf, za_s, c0)


def _mlstm_prompt_kernel(q_ref, k_ref, v_ref, o_ref, g_ref, bias_ref, nrm_ref,
                         ha_ref, c_ref, n_ref, m_ref, cx_s, m_s, *, n_chunks):
    L = MLSTM_CHUNK
    c = pl.program_id(1)

    @pl.when(c == 0)
    def _():
        cx_s[...] = jnp.zeros_like(cx_s)
        m_s[...] = jnp.zeros_like(m_s)

    row = lax.broadcasted_iota(jnp.int32, (L, L), 0)
    col = lax.broadcasted_iota(jnp.int32, (L, L), 1)
    causal = row >= col
    tri = causal.astype(BF16)

    for b in range(MLSTM_SEQS):
        _mlstm_prompt_seq(b, causal, tri, q_ref, k_ref, v_ref, o_ref, g_ref, bias_ref, nrm_ref,
                          ha_ref, cx_s, m_s)

    @pl.when(c == n_chunks - 1)
    def _():
        for b in range(MLSTM_SEQS):
            m_ref[b] = m_s[b]
            for h in range(H_A):
                cx = cx_s[b * H_A + h]
                c_ref[0, b, h] = cx[:, :DV]
                n_ref[0, b, h:h + 1, :] = cx[:, DV:].T[h:h + 1, :]


def _prefix_max_rows(x):
    rows = x.shape[0]
    row = lax.broadcasted_iota(jnp.int32, x.shape, 0)
    k = 1
    while k < rows:
        x = jnp.where(row >= k, jnp.maximum(x, pltpu.roll(x, k, 0)), x)
        k *= 2
    return x


def _mlstm_prompt_seq(b, causal, tri, q_ref, k_ref, v_ref, o_ref, g_ref, bias_ref, nrm_ref,
                      ha_ref, cx_s, m_s):
    L = MLSTM_CHUNK
    lane = lax.broadcasted_iota(jnp.int32, (L, LANE), 1)
    lane_dv = lax.broadcasted_iota(jnp.int32, (DV, LANE), 1)
    capped = _soft_cap(g_ref[b] + bias_ref[...])
    lf = _log_sigmoid(capped)
    hi = lf.astype(BF16)
    r1 = lf - hi.astype(F32)
    mid = r1.astype(BF16)
    lo = (r1 - mid.astype(F32)).astype(BF16)
    parts = _bdot(tri, jnp.concatenate([hi, mid, lo], axis=1))
    bc = parts[:, :LANE] + parts[:, LANE:2 * LANE] + parts[:, 2 * LANE:]
    bh = pltpu.roll(bc, LANE - H_A, 1)
    a = capped - bh
    a_t = a.T
    m_prev = m_s[b]
    big_m = jnp.maximum(_prefix_max_rows(a), m_prev[0:1, :])
    sc_all = jnp.exp(m_prev[0:1, :] - big_m)
    emt_all = jnp.exp(-(bh + big_m))
    m_last = big_m[L - 1:L, :]
    w_all = jnp.exp(a - m_last)
    decay_all = sc_all[L - 1:L, :]
    m_s[b] = jnp.broadcast_to(bh[L - 1:L, :] + m_last, (8, LANE))

    nums = []
    nd_tail = jnp.zeros((L, LANE), F32)
    sq_tail = jnp.zeros((L, LANE), F32)
    for h in range(H_A):
        hs = slice(h * DK, (h + 1) * DK)
        onehot = (lane == h).astype(BF16)
        qb = q_ref[b, :, hs].astype(BF16)
        kf = k_ref[b, :, hs] * (DK ** -0.5)
        kb = kf.astype(BF16)
        vx = jnp.concatenate([v_ref[b, :, hs].astype(BF16), onehot], axis=1)
        cx_prev = cx_s[b * H_A + h]

        p = jnp.exp(jnp.where(causal, a_t[h:h + 1, :] - big_m[:, h:h + 1], -jnp.inf))
        s = _bdot_t(qb, kb) * p
        nd = sc_all[:, h:h + 1] * _bdot(qb, cx_prev.astype(BF16)) + _bdot(s.astype(BF16), vx)
        num = nd[:, :DV]
        nums.append(num)
        nd_tail = nd_tail + nd[:, DV:]
        sq_tail = sq_tail + _bdot((num * num).astype(BF16), (lane_dv == h).astype(BF16))

        wk = (w_all[:, h:h + 1] * kf).astype(BF16)
        cx_s[b * H_A + h] = decay_all[:, h:h + 1] * cx_prev + lax.dot_general(
            wk, vx, (((0,), (0,)), ((), ())), preferred_element_type=F32)

    inv = 1.0 / jnp.maximum(jnp.abs(nd_tail), emt_all)
    ms = inv * inv * sq_tail * (1.0 / DV)
    scale_all = inv * lax.rsqrt(ms + EPS)
    for h in range(H_A):
        hs = slice(h * DK, (h + 1) * DK)
        hn = nums[h] * scale_all[:, h:h + 1]
        ha_ref[b, :, hs] = (hn * nrm_ref[:, hs] * jax.nn.sigmoid(o_ref[b, :, hs])).astype(ha_ref.dtype)


def _mlstm_prompt(za_p, g_p, bias_row, nrm_row, batch, seq):
    L = MLSTM_CHUNK
    nsq = MLSTM_SEQS
    n_chunks = seq // L
    kern = functools.partial(_mlstm_prompt_kernel, n_chunks=n_chunks)
    za3 = za_p.reshape(batch, seq, 4 * D_A)
    g3 = g_p.reshape(batch, seq, LANE)
    zspec = lambda part: pl.BlockSpec((nsq, L, D_A), lambda b, c: (b, c, part))
    return pl.pallas_call(
        kern,
        grid=(batch // nsq, n_chunks),
        in_specs=[
            zspec(0), zspec(1), zspec(2), zspec(3),
            pl.BlockSpec((nsq, L, LANE), lambda b, c: (b, c, 0)),
            pl.BlockSpec((1, LANE), lambda b, c: (0, 0)),
            pl.BlockSpec((1, D_A), lambda b, c: (0, 0)),
        ],
        out_specs=[
            pl.BlockSpec((nsq, L, D_A), lambda b, c: (b, c, 0)),
            pl.BlockSpec((1, nsq, H_A, DK, DV), lambda b, c: (0, b, 0, 0, 0)),
            pl.BlockSpec((1, nsq, H_A, DK), lambda b, c: (0, b, 0, 0)),
            pl.BlockSpec((nsq, 8, LANE), lambda b, c: (b, 0, 0)),
        ],
        out_shape=[
            jax.ShapeDtypeStruct((batch, seq, D_A), BF16),
            jax.ShapeDtypeStruct((1, batch, H_A, DK, DV), F32),
            jax.ShapeDtypeStruct((1, batch, H_A, DK), F32),
            jax.ShapeDtypeStruct((batch, 8, LANE), F32),
        ],
        scratch_shapes=[pltpu.VMEM((nsq * H_A, DK, DV + LANE), F32),
                        pltpu.VMEM((nsq, 8, LANE), F32)],
        compiler_params=pltpu.CompilerParams(
            dimension_semantics=("parallel", "arbitrary"), vmem_limit_bytes=VMEM_LIMIT),
        name="mlstm_prompt",
    )(za3, za3, za3, za3, g3, bias_row, nrm_row)


def _expand_heads(x, width):
    rows = x.shape[0]
    return jnp.concatenate([jnp.broadcast_to(x[:, h:h + 1], (rows, width)) for h in range(H_A)], axis=1)


def _head_sums(x):
    return jnp.concatenate(
        [jnp.sum(x[:, h * DK:(h + 1) * DK], axis=-1, keepdims=True) for h in range(H_A)], axis=1)


def _mlstm_sample_kernel(za_ref, qc_ref, g_ref, bias_ref, nrm_ref, m_ref, n_ref,
                         ha_ref, m_out, n_out, wk_out, dec_out):
    nb = za_ref.shape[0]
    capped = _soft_cap(g_ref[...] + bias_ref[...])
    logi = capped[:, 0:H_A]
    logf = _log_sigmoid(capped)[:, H_A:2 * H_A]
    m_prev = m_ref[...]
    m_inter = logf + m_prev
    m_t = jnp.maximum(m_inter, logi)
    sc = jnp.exp(m_inter - m_t)
    ei = jnp.exp(logi - m_t)
    emt = jnp.exp(-m_t)

    q = za_ref[:, 0:D_A]
    k = za_ref[:, D_A:2 * D_A] * (DK ** -0.5)
    v = za_ref[:, 2 * D_A:3 * D_A]
    o = za_ref[:, 3 * D_A:4 * D_A]
    n_prev = n_ref[...]

    s = _head_sums(q * k) * ei
    den = sc * _head_sums(q * n_prev) + s
    denom = jnp.maximum(jnp.abs(den), emt)
    num = _expand_heads(sc, DV) * qc_ref[...] + _expand_heads(s, DV) * v
    hh = num / _expand_heads(denom, DV)
    ms = _head_sums(hh * hh) * (1.0 / DV)
    hn = hh * lax.rsqrt(_expand_heads(ms, DV) + EPS)
    ha_ref[...] = (hn * nrm_ref[...] * jax.nn.sigmoid(o)).astype(ha_ref.dtype)
    n_out[...] = _expand_heads(sc, DK) * n_prev + _expand_heads(ei, DK) * k
    m_out[...] = m_t
    pad = jnp.zeros((nb, LANE - H_A), F32)
    dec_out[...] = jnp.concatenate([sc, pad], axis=1)

    k_t = k.T
    ei_t = jnp.concatenate([ei, pad], axis=1).T
    for h in range(H_A):
        hs = slice(h * DK, (h + 1) * DK)
        wk_out[hs, :] = k_t[hs, :] * ei_t[h:h + 1, :]


def _mlstm_sample(za_s, qc, g_s, bias_row, nrm_row, m0, n0):
    nb = za_s.shape[0]
    assert nb == LANE, "the sequence axis is transposed onto the lanes"
    return pl.pallas_call(
        _mlstm_sample_kernel,
        out_shape=[
            jax.ShapeDtypeStruct((nb, D_A), BF16),
            jax.ShapeDtypeStruct((nb, H_A), F32),
            jax.ShapeDtypeStruct((nb, H_A * DK), F32),
            jax.ShapeDtypeStruct((D_A, nb), F32),
            jax.ShapeDtypeStruct((nb, LANE), F32),
        ],
        compiler_params=pltpu.CompilerParams(vmem_limit_bytes=VMEM_LIMIT),
        name="mlstm_sample",
    )(za_s, qc, g_s, bias_row, nrm_row, m0, n0)


def _outproj_kernel(hp_ref, hs_ref, ap_ref, as_ref, yb_ref, w_ref, op_ref, os_ref, wc_ref):
    j = pl.program_id(1)

    @pl.when(pl.program_id(0) == 0)
    def _():
        wc_ref[j] = w_ref[...].astype(BF16)

    wa = wc_ref[j, :D_A, :]
    wb = wc_ref[j, D_A:, :]
    op_ref[...] = hp_ref[...] + _bdot(ap_ref[...], wa) + _bdot(yb_ref[0, :MP, :], wb)
    os_ref[0] = hs_ref[0] + _bdot(as_ref[0], wa) + _bdot(yb_ref[0, MP:, :], wb)


def _outproj(hp, hs, ap, a_s, yb_c, w3):
    tn = OUT_TN
    n_n = D_MODEL // tn
    return pl.pallas_call(
        _outproj_kernel,
        grid=(N_MIX_BLOCKS, n_n),
        in_specs=[
            pl.BlockSpec((MP, tn), lambda i, j: (i, j)),
            pl.BlockSpec((1, MS, tn), lambda i, j: (i, 0, j)),
            pl.BlockSpec((MP, D_A), lambda i, j: (i, 0)),
            pl.BlockSpec((1, MS, D_A), lambda i, j: (i, 0, 0)),
            pl.BlockSpec((1, MP + MS, D_B), lambda i, j: (i, 0, 0)),
            pl.BlockSpec((None, D_A + D_B, tn), lambda i, j: (0, 0, jnp.where(i == 0, j, n_n - 1))),
        ],
        out_specs=[
            pl.BlockSpec((MP, tn), lambda i, j: (i, j)),
            pl.BlockSpec((1, MS, tn), lambda i, j: (i, 0, j)),
        ],
        out_shape=[
            jax.ShapeDtypeStruct((N_MIX_BLOCKS * MP, D_MODEL), F32),
            jax.ShapeDtypeStruct((N_MIX_BLOCKS, MS, D_MODEL), F32),
        ],
        scratch_shapes=[pltpu.VMEM((n_n, D_A + D_B, tn), BF16)],
        compiler_params=pltpu.CompilerParams(
            dimension_semantics=("arbitrary", "arbitrary"), vmem_limit_bytes=VMEM_LIMIT),
        name="outproj",
    )(hp, hs, ap, a_s, yb_c, w3)


def kernel(x_prompt, x_sample, state_mlstm_C, state_mlstm_n, state_mlstm_m, state_conv, norm_ffn1, ffn1_gate, ffn1_up, ffn1_down, norm_mix, w_in, b_gates, conv_w, conv_b, norm_mlstm, norm_conv, w_out, norm_ffn2, ffn2_gate, ffn2_up, ffn2_down, norm_final):
    batch, seq, _ = x_prompt.shape
    nb = x_sample.shape[0]
    assert batch == N_MIX_BLOCKS and seq == MP and nb == N_MIX_BLOCKS * MS
    assert norm_ffn1.shape[0] == 1, "single-layer trunk"

    xp = x_prompt.reshape(batch * seq, D_MODEL)
    xs = x_sample.reshape(N_FFN_BLOCKS, TM_S, D_MODEL)

    h1p, h1s, xn_c = _ffn(xp, xs, norm_ffn1, ffn1_gate, ffn1_up, ffn1_down, norm_mix)

    w_t = jnp.swapaxes(w_in[0], 0, 1)
    za_p, za_s, g_p, g_s = _proj(xn_c, w_t)
    za_s = za_s.reshape(nb, 4 * D_A)
    yb_c, conv_p, conv_s, qc_s = _conv(xn_c, w_t, conv_w[0], conv_b, norm_conv,
                                       state_conv.reshape(N_MIX_BLOCKS, MS, 2, D_B), za_s, state_mlstm_C)

    bias_row = jnp.zeros((1, LANE), F32).at[0, :2 * H_A].set(b_gates[0].astype(F32))
    ha_p, c_p, n_p, m_p = _mlstm_prompt(za_p, g_p, bias_row, norm_mlstm, batch, seq)
    ha_s, m_s, n_s, wk_s, dec_s = _mlstm_sample(
        za_s, qc_s, g_s.reshape(nb, LANE), bias_row, norm_mlstm,
        state_mlstm_m.reshape(nb, H_A), state_mlstm_n.reshape(nb, H_A * DK))

    h2p, h2s = _outproj(h1p, h1s.reshape(N_MIX_BLOCKS, MS, D_MODEL), ha_p.reshape(batch * seq, D_A),
                        ha_s.reshape(N_MIX_BLOCKS, MS, D_A), yb_c, w_out)

    yp, ys, c_s = _ffn(h2p, h2s.reshape(N_FFN_BLOCKS, TM_S, D_MODEL), norm_ffn2, ffn2_gate, ffn2_up,
                       ffn2_down, norm_final.reshape(1, D_MODEL),
                       state=(state_mlstm_C, wk_s, za_s, dec_s))

    return (
        yp.reshape(batch, seq, D_MODEL),
        ys.reshape(nb, 1, D_MODEL),
        c_p,
        n_p,
        m_p[:, 0, :H_A].reshape(1, batch, H_A),
        conv_p,
        c_s,
        n_s.reshape(1, nb, H_A, DK),
        m_s.reshape(1, nb, H_A),
        conv_s.reshape(1, nb, 2, D_B),
    )
```

```python
import functools

import jax
import jax.numpy as jnp
from jax import lax
from jax.experimental import pallas as pl
from jax.experimental.pallas import tpu as pltpu

F32 = jnp.float32
BF16 = jnp.bfloat16

D_MODEL = 2048
D_A = 1024
D_B = 1024
H_A = 4
DK = 256
DV = 256
G_B = 8
D_FF = 5504
GATE_CAP = 15.0
EPS = 1e-6
GATE_ROW0 = 4 * D_A
CONV_ROW0 = 4 * D_A + 2 * H_A

LANE = 128
N_FFN_BLOCKS = 8
TM_P = 1024
TM_S = 16
TM = TM_P + TM_S
TF = 256
N_MIX_BLOCKS = N_FFN_BLOCKS // 2
MP = 2 * TM_P
MS = 2 * TM_S
PROJ_TN = 1024
CONV_TC = 256
OUT_TN = 512
MLSTM_CHUNK = 256
MLSTM_SEQS = 4
SAMPLE_BS = 8
VMEM_LIMIT = 62 * 1024 * 1024


def _rms(x, g):
    return x * lax.rsqrt(jnp.mean(x * x, axis=-1, keepdims=True) + EPS) * g


def _soft_cap(x):
    return GATE_CAP * jnp.tanh(x / GATE_CAP)


def _log_sigmoid(x):
    return -jax.nn.softplus(-x)


def _bdot(a, b):
    return jnp.dot(a, b, preferred_element_type=F32)


def _bdot_t(a, b):
    return lax.dot_general(a, b, (((1,), (1,)), ((), ())), preferred_element_type=F32)


def _sample_state_update(f, c_ref, wk_ref, v_ref, dec_ref, c_out):
    r = jnp.minimum(f, TM_S - 1)
    seq = pl.program_id(0) * TM_S + r
    lane = lax.broadcasted_iota(jnp.int32, (D_A, LANE), 1)
    wk_col = jnp.sum(jnp.where(lane == seq, wk_ref[...], 0.0), axis=1, keepdims=True)
    v_row = v_ref[pl.ds(r, 1), :]
    dec = dec_ref[pl.ds(r, 1), :]
    for h in range(H_A):
        hs = slice(h * DK, (h + 1) * DK)
        c_out[0, 0, h] = dec[:, h:h + 1] * c_ref[0, 0, h] + wk_col[hs, :] * v_row[:, hs]


def _ffn_kernel(xp_ref, xs_ref, g_ref, wg_ref, wu_ref, wd_ref, g2_ref, *rest,
                n_f, last_valid, final_norm):
    if final_norm:
        c_ref, wk_ref, v_ref, dec_ref, op_ref, os_ref, c_out, xn_ref = rest
    else:
        op_ref, os_ref, nxt_ref, xn_ref = rest
    f = pl.program_id(1)

    def step(valid, first, with_state):
        if first:
            g = g_ref[...]
            xn_ref[:TM_P, :] = _rms(xp_ref[...], g).astype(BF16)
            xn_ref[TM_P:, :] = _rms(xs_ref[0], g).astype(BF16)
        xn = xn_ref[...]
        a = _bdot(xn, wg_ref[:, :valid].astype(BF16))
        b = _bdot(xn, wu_ref[:, :valid].astype(BF16))
        hid = (a * jax.nn.sigmoid(a) * b * 0.5).astype(BF16)
        r = _bdot(hid, wd_ref[:valid, :].astype(BF16))
        if first:
            op_ref[...] = xp_ref[...] + r[:TM_P]
            os_ref[0] = xs_ref[0] + r[TM_P:]
        else:
            op_ref[...] += r[:TM_P]
            os_ref[0] += r[TM_P:]
        if with_state:
            _sample_state_update(f, c_ref, wk_ref, v_ref, dec_ref, c_out)

    assert last_valid < TF and n_f - 1 >= TM_S, "state update rides the full-width steps"

    @pl.when(f == 0)
    def _():
        step(TF, True, final_norm)

    @pl.when(jnp.logical_and(f > 0, f < n_f - 1))
    def _():
        step(TF, False, final_norm)

    @pl.when(f == n_f - 1)
    def _():
        step(last_valid, False, False)
        g2 = g2_ref[...]
        if final_norm:
            op_ref[...] = _rms(op_ref[...], g2)
            os_ref[0] = _rms(os_ref[0], g2)
        else:
            nxt_ref[0, :TM_P, :] = _rms(op_ref[...], g2).astype(BF16)
            nxt_ref[0, TM_P:, :] = _rms(os_ref[0], g2).astype(BF16)


def _ffn(xp, xs, g, wg, wu, wd, g2, state=None):
    final_norm = state is not None
    n_f = pl.cdiv(D_FF, TF)
    last_valid = D_FF - (n_f - 1) * TF
    kern = functools.partial(_ffn_kernel, n_f=n_f, last_valid=last_valid, final_norm=final_norm)
    in_specs = [
        pl.BlockSpec((TM_P, D_MODEL), lambda i, f: (i, 0)),
        pl.BlockSpec((1, TM_S, D_MODEL), lambda i, f: (i, 0, 0)),
        pl.BlockSpec((1, D_MODEL), lambda i, f: (0, 0)),
        pl.BlockSpec((None, D_MODEL, TF), lambda i, f: (0, 0, f)),
        pl.BlockSpec((None, D_MODEL, TF), lambda i, f: (0, 0, f)),
        pl.BlockSpec((None, TF, D_MODEL), lambda i, f: (0, f, 0)),
        pl.BlockSpec((1, D_MODEL), lambda i, f: (0, 0)),
    ]
    out_specs = [
        pl.BlockSpec((TM_P, D_MODEL), lambda i, f: (i, 0)),
        pl.BlockSpec((1, TM_S, D_MODEL), lambda i, f: (i, 0, 0)),
    ]
    out_shape = [
        jax.ShapeDtypeStruct((N_FFN_BLOCKS * TM_P, D_MODEL), F32),
        jax.ShapeDtypeStruct((N_FFN_BLOCKS, TM_S, D_MODEL), F32),
    ]
    args = [xp, xs, g, wg, wu, wd, g2]
    if final_norm:
        c0, wk, za_s, dec = state
        seq = lambda i, f: i * TM_S + jnp.minimum(f, TM_S - 1)
        cspec = pl.BlockSpec((1, 1, H_A, DK, DV), lambda i, f: (0, seq(i, f), 0, 0, 0))
        in_specs += [
            cspec,
            pl.BlockSpec((D_A, LANE), lambda i, f: (0, 0)),
            pl.BlockSpec((TM_S, D_A), lambda i, f: (i, 2)),
            pl.BlockSpec((TM_S, LANE), lambda i, f: (i, 0)),
        ]
        out_specs.append(cspec)
        out_shape.append(jax.ShapeDtypeStruct(c0.shape, F32))
        args += [c0, wk, za_s, dec]
    else:
        out_specs.append(pl.BlockSpec((1, TM, D_MODEL), lambda i, f: (i, 0, 0)))
        out_shape.append(jax.ShapeDtypeStruct((N_FFN_BLOCKS, TM, D_MODEL), BF16))
    return pl.pallas_call(
        kern,
        grid=(N_FFN_BLOCKS, n_f),
        in_specs=in_specs,
        out_specs=out_specs,
        out_shape=out_shape,
        scratch_shapes=[pltpu.VMEM((TM, D_MODEL), BF16)],
        compiler_params=pltpu.CompilerParams(
            dimension_semantics=("parallel", "arbitrary"), vmem_limit_bytes=VMEM_LIMIT),
        name="ffn_final" if final_norm else "ffn",
    )(*args)


def _split_rows(z0, z1, p_ref, s_ref):
    p_ref[:TM_P, :] = z0[:TM_P]
    p_ref[TM_P:, :] = z1[:TM_P]
    s_ref[0, :TM_S, :] = z0[TM_P:]
    s_ref[0, TM_S:, :] = z1[TM_P:]


def _proj_kernel(xn_ref, w_ref, wg_ref, zp_ref, zs_ref, gp_ref, gs_ref):
    w = w_ref[...].astype(BF16)
    _split_rows(_bdot_t(xn_ref[0], w), _bdot_t(xn_ref[1], w), zp_ref, zs_ref)

    @pl.when(pl.program_id(1) == 0)
    def _():
        wg = wg_ref[...].astype(BF16)
        _split_rows(_bdot_t(xn_ref[0], wg), _bdot_t(xn_ref[1], wg), gp_ref, gs_ref)


def _proj(xn_c, w_t):
    n_n = (4 * D_A) // PROJ_TN
    return pl.pallas_call(
        _proj_kernel,
        grid=(N_MIX_BLOCKS, n_n),
        in_specs=[
            pl.BlockSpec((2, TM, D_MODEL), lambda i, j: (i, 0, 0)),
            pl.BlockSpec((PROJ_TN, D_MODEL), lambda i, j: (j, 0)),
            pl.BlockSpec((LANE, D_MODEL), lambda i, j: (GATE_ROW0 // LANE, 0)),
        ],
        out_specs=[
            pl.BlockSpec((MP, PROJ_TN), lambda i, j: (i, j)),
            pl.BlockSpec((1, MS, PROJ_TN), lambda i, j: (i, 0, j)),
            pl.BlockSpec((MP, LANE), lambda i, j: (i, 0)),
            pl.BlockSpec((1, MS, LANE), lambda i, j: (i, 0, 0)),
        ],
        out_shape=[
            jax.ShapeDtypeStruct((N_MIX_BLOCKS * MP, 4 * D_A), F32),
            jax.ShapeDtypeStruct((N_MIX_BLOCKS, MS, 4 * D_A), F32),
            jax.ShapeDtypeStruct((N_MIX_BLOCKS * MP, LANE), F32),
            jax.ShapeDtypeStruct((N_MIX_BLOCKS, MS, LANE), F32),
        ],
        compiler_params=pltpu.CompilerParams(
            dimension_semantics=("parallel", "arbitrary"), vmem_limit_bytes=VMEM_LIMIT),
        name="proj_qkvo",
    )(xn_c, w_t, w_t)


def _group_norm(yb, nrm):
    gw = D_B // G_B
    parts = []
    for g in range(yb.shape[1] // gw):
        seg = yb[:, g * gw:(g + 1) * gw]
        parts.append(seg * lax.rsqrt(jnp.mean(seg * seg, axis=-1, keepdims=True) + EPS))
    return jnp.concatenate(parts, axis=1) * nrm


def _conv_kernel(xn_ref, wgb_ref, wgc_ref, wxc_ref, cw_ref, cb_ref, nrm_ref, buf_ref, q_ref, c_ref,
                 yb_ref, cp_ref, cs_ref, qc_ref):
    q_t = jnp.concatenate([q_ref[...], jnp.zeros((LANE - SAMPLE_BS, D_A), F32)], axis=0).T
    for h in range(H_A):
        hs = slice(h * DK, (h + 1) * DK)
        for j in range(SAMPLE_BS):
            qc_ref[j:j + 1, hs] = jnp.sum(q_t[hs, j:j + 1] * c_ref[0, j, h], axis=0, keepdims=True)

    wgb = wgb_ref[...].astype(BF16)
    wgc = wgc_ref[...].astype(BF16)
    wxc = wxc_ref[...].astype(BF16)
    cw0 = cw_ref[0:1, :]
    cw1 = cw_ref[1:2, :]
    cw2 = cw_ref[2:3, :]
    cb = cb_ref[...]
    nrm = nrm_ref[...]
    row = lax.broadcasted_iota(jnp.int32, (TM_P, wgb.shape[0]), 0)
    zero_row = jnp.zeros((1, wgb.shape[0]), F32)
    um2, um1 = zero_row, zero_row

    for t in range(2):
        xn = xn_ref[t]
        u = _bdot_t(xn, wgc) * _bdot_t(xn, wxc)
        gb = _bdot_t(xn, wgb)

        up = u[:TM_P]
        u1 = jnp.where(row < 1, um1, pltpu.roll(up, 1, 0))
        u2 = jnp.where(row < 1, um2, jnp.where(row < 2, um1, pltpu.roll(up, 2, 0)))
        yc = cw0 * u2 + cw1 * u1 + cw2 * up + cb
        yb_ref[0, t * TM_P:(t + 1) * TM_P, :] = _group_norm(gb[:TM_P] * yc, nrm).astype(yb_ref.dtype)
        um2, um1 = up[TM_P - 2:TM_P - 1, :], up[TM_P - 1:TM_P, :]

        ss = slice(t * TM_S, (t + 1) * TM_S)
        us = u[TM_P:]
        b0 = buf_ref[0, ss, 0, :]
        b1 = buf_ref[0, ss, 1, :]
        ycs = cw0 * b0 + cw1 * b1 + cw2 * us + cb
        yb_ref[0, MP + t * TM_S:MP + (t + 1) * TM_S, :] = _group_norm(gb[TM_P:] * ycs, nrm).astype(yb_ref.dtype)
        cs_ref[0, ss, 0, :] = b1
        cs_ref[0, ss, 1, :] = us

    cp_ref[0, 0, 0:1, :] = um2
    cp_ref[0, 0, 1:2, :] = um1


def _conv(xn_c, w_t, cw, cb_row, nrm_row, buf, za_s, c0):
    tc = CONV_TC
    n_c = D_B // tc
    assert N_MIX_BLOCKS * n_c * SAMPLE_BS == za_s.shape[0], "one sample group per grid step"
    wspec = lambda base: pl.BlockSpec(
        (pl.Element(tc), pl.Element(D_MODEL)), lambda i, c: (pl.multiple_of(base + c * tc, 8), 0))
    grp = lambda i, c: i * n_c + c
    return pl.pallas_call(
        _conv_kernel,
        grid=(N_MIX_BLOCKS, n_c),
        in_specs=[
            pl.BlockSpec((2, TM, D_MODEL), lambda i, c: (i, 0, 0)),
            wspec(CONV_ROW0),
            wspec(CONV_ROW0 + D_B),
            wspec(CONV_ROW0 + 2 * D_B),
            pl.BlockSpec((3, tc), lambda i, c: (0, c)),
            pl.BlockSpec((1, tc), lambda i, c: (0, c)),
            pl.BlockSpec((1, tc), lambda i, c: (0, c)),
            pl.BlockSpec((1, MS, 2, tc), lambda i, c: (i, 0, 0, c)),
            pl.BlockSpec((SAMPLE_BS, D_A), lambda i, c: (grp(i, c), 0)),
            pl.BlockSpec((1, SAMPLE_BS, H_A, DK, DV), lambda i, c: (0, grp(i, c), 0, 0, 0)),
        ],
        out_specs=[
            pl.BlockSpec((1, MP + MS, tc), lambda i, c: (i, 0, c)),
            pl.BlockSpec((1, 1, 2, tc), lambda i, c: (0, i, 0, c)),
            pl.BlockSpec((1, MS, 2, tc), lambda i, c: (i, 0, 0, c)),
            pl.BlockSpec((SAMPLE_BS, D_A), lambda i, c: (grp(i, c), 0)),
        ],
        out_shape=[
            jax.ShapeDtypeStruct((N_MIX_BLOCKS, MP + MS, D_B), BF16),
            jax.ShapeDtypeStruct((1, N_MIX_BLOCKS, 2, D_B), F32),
            jax.ShapeDtypeStruct((N_MIX_BLOCKS, MS, 2, D_B), F32),
            jax.ShapeDtypeStruct((za_s.shape[0], D_A), F32),
        ],
        compiler_params=pltpu.CompilerParams(
            dimension_semantics=("parallel", "arbitrary"), vmem_limit_bytes=VMEM_LIMIT),
        name="proj_conv",
    )(xn_c, w_t, w_t, w_t, cw, cb_row, nrm_row, buf, za_s, c0)


def _mlstm_prompt_kernel(q_ref, k_ref, v_ref, o_ref, g_ref, bias_ref, nrm_ref,
                         ha_ref, c_ref, n_ref, m_ref, cx_s, m_s, *, n_chunks):
    L = MLSTM_CHUNK
    c = pl.program_id(1)

    @pl.when(c == 0)
    def _():
        cx_s[...] = jnp.zeros_like(cx_s)
        m_s[...] = jnp.zeros_like(m_s)

    row = lax.broadcasted_iota(jnp.int32, (L, L), 0)
    col = lax.broadcasted_iota(jnp.int32, (L, L), 1)
    causal = row >= col
    tri = causal.astype(BF16)

    for b in range(MLSTM_SEQS):
        _mlstm_prompt_seq(b, causal, tri, q_ref, k_ref, v_ref, o_ref, g_ref, bias_ref, nrm_ref,
                          ha_ref, cx_s, m_s)

    @pl.when(c == n_chunks - 1)
    def _():
        for b in range(MLSTM_SEQS):
            m_ref[b] = m_s[b]
            for h in range(H_A):
                cx = cx_s[b * H_A + h]
                c_ref[0, b, h] = cx[:, :DV]
                n_ref[0, b, h:h + 1, :] = cx[:, DV:].T[h:h + 1, :]


def _prefix_max_rows(x):
    rows = x.shape[0]
    row = lax.broadcasted_iota(jnp.int32, x.shape, 0)
    k = 1
    while k < rows:
        x = jnp.where(row >= k, jnp.maximum(x, pltpu.roll(x, k, 0)), x)
        k *= 2
    return x


def _mlstm_prompt_seq(b, causal, tri, q_ref, k_ref, v_ref, o_ref, g_ref, bias_ref, nrm_ref,
                      ha_ref, cx_s, m_s):
    L = MLSTM_CHUNK
    lane = lax.broadcasted_iota(jnp.int32, (L, LANE), 1)
    lane_dv = lax.broadcasted_iota(jnp.int32, (DV, LANE), 1)
    capped = _soft_cap(g_ref[b] + bias_ref[...])
    lf = _log_sigmoid(capped)
    hi = lf.astype(BF16)
    r1 = lf - hi.astype(F32)
    mid = r1.astype(BF16)
    lo = (r1 - mid.astype(F32)).astype(BF16)
    parts = _bdot(tri, jnp.concatenate([hi, mid, lo], axis=1))
    bc = parts[:, :LANE] + parts[:, LANE:2 * LANE] + parts[:, 2 * LANE:]
    bh = pltpu.roll(bc, LANE - H_A, 1)
    a = capped - bh
    a_t = a.T
    m_prev = m_s[b]
    big_m = jnp.maximum(_prefix_max_rows(a), m_prev[0:1, :])
    sc_all = jnp.exp(m_prev[0:1, :] - big_m)
    emt_all = jnp.exp(-(bh + big_m))
    m_last = big_m[L - 1:L, :]
    w_all = jnp.exp(a - m_last)
    decay_all = sc_all[L - 1:L, :]
    m_s[b] = jnp.broadcast_to(bh[L - 1:L, :] + m_last, (8, LANE))

    nums = []
    nd_tail = jnp.zeros((L, LANE), F32)
    sq_tail = jnp.zeros((L, LANE), F32)
    for h in range(H_A):
        hs = slice(h * DK, (h + 1) * DK)
        onehot = (lane == h).astype(BF16)
        qb = q_ref[b, :, hs].astype(BF16)
        kf = k_ref[b, :, hs] * (DK ** -0.5)
        kb = kf.astype(BF16)
        vx = jnp.concatenate([v_ref[b, :, hs].astype(BF16), onehot], axis=1)
        cx_prev = cx_s[b * H_A + h]

        p = jnp.exp(jnp.where(causal, a_t[h:h + 1, :] - big_m[:, h:h + 1], -jnp.inf))
        s = _bdot_t(qb, kb) * p
        nd = sc_all[:, h:h + 1] * _bdot(qb, cx_prev.astype(BF16)) + _bdot(s.astype(BF16), vx)
        num = nd[:, :DV]
        nums.append(num)
        nd_tail = nd_tail + nd[:, DV:]
        sq_tail = sq_tail + _bdot((num * num).astype(BF16), (lane_dv == h).astype(BF16))

        wk = (w_all[:, h:h + 1] * kf).astype(BF16)
        cx_s[b * H_A + h] = decay_all[:, h:h + 1] * cx_prev + lax.dot_general(
            wk, vx, (((0,), (0,)), ((), ())), preferred_element_type=F32)

    inv = 1.0 / jnp.maximum(jnp.abs(nd_tail), emt_all)
    ms = inv * inv * sq_tail * (1.0 / DV)
    scale_all = inv * lax.rsqrt(ms + EPS)
    for h in range(H_A):
        hs = slice(h * DK, (h + 1) * DK)
        hn = nums[h] * scale_all[:, h:h + 1]
        ha_ref[b, :, hs] = (hn * nrm_ref[:, hs] * jax.nn.sigmoid(o_ref[b, :, hs])).astype(ha_ref.dtype)


def _mlstm_prompt(za_p, g_p, bias_row, nrm_row, batch, seq):
    L = MLSTM_CHUNK
    nsq = MLSTM_SEQS
    n_chunks = seq // L
    kern = functools.partial(_mlstm_prompt_kernel, n_chunks=n_chunks)
    za3 = za_p.reshape(batch, seq, 4 * D_A)
    g3 = g_p.reshape(batch, seq, LANE)
    zspec = lambda part: pl.BlockSpec((nsq, L, D_A), lambda b, c: (b, c, part))
    return pl.pallas_call(
        kern,
        grid=(batch // nsq, n_chunks),
        in_specs=[
            zspec(0), zspec(1), zspec(2), zspec(3),
            pl.BlockSpec((nsq, L, LANE), lambda b, c: (b, c, 0)),
            pl.BlockSpec((1, LANE), lambda b, c: (0, 0)),
            pl.BlockSpec((1, D_A), lambda b, c: (0, 0)),
        ],
        out_specs=[
            pl.BlockSpec((nsq, L, D_A), lambda b, c: (b, c, 0)),
            pl.BlockSpec((1, nsq, H_A, DK, DV), lambda b, c: (0, b, 0, 0, 0)),
            pl.BlockSpec((1, nsq, H_A, DK), lambda b, c: (0, b, 0, 0)),
            pl.BlockSpec((nsq, 8, LANE), lambda b, c: (b, 0, 0)),
        ],
        out_shape=[
            jax.ShapeDtypeStruct((batch, seq, D_A), BF16),
            jax.ShapeDtypeStruct((1, batch, H_A, DK, DV), F32),
            jax.ShapeDtypeStruct((1, batch, H_A, DK), F32),
            jax.ShapeDtypeStruct((batch, 8, LANE), F32),
        ],
        scratch_shapes=[pltpu.VMEM((nsq * H_A, DK, DV + LANE), F32),
                        pltpu.VMEM((nsq, 8, LANE), F32)],
        compiler_params=pltpu.CompilerParams(
            dimension_semantics=("parallel", "arbitrary"), vmem_limit_bytes=VMEM_LIMIT),
        name="mlstm_prompt",
    )(za3, za3, za3, za3, g3, bias_row, nrm_row)


def _expand_heads(x, width):
    rows = x.shape[0]
    return jnp.concatenate([jnp.broadcast_to(x[:, h:h + 1], (rows, width)) for h in range(H_A)], axis=1)


def _head_sums(x):
    return jnp.concatenate(
        [jnp.sum(x[:, h * DK:(h + 1) * DK], axis=-1, keepdims=True) for h in range(H_A)], axis=1)


def _mlstm_sample_kernel(za_ref, qc_ref, g_ref, bias_ref, nrm_ref, m_ref, n_ref,
                         ha_ref, m_out, n_out, wk_out, dec_out):
    nb = za_ref.shape[0]
    capped = _soft_cap(g_ref[...] + bias_ref[...])
    logi = capped[:, 0:H_A]
    logf = _log_sigmoid(capped)[:, H_A:2 * H_A]
    m_prev = m_ref[...]
    m_inter = logf + m_prev
    m_t = jnp.maximum(m_inter, logi)
    sc = jnp.exp(m_inter - m_t)
    ei = jnp.exp(logi - m_t)
    emt = jnp.exp(-m_t)

    q = za_ref[:, 0:D_A]
    k = za_ref[:, D_A:2 * D_A] * (DK ** -0.5)
    v = za_ref[:, 2 * D_A:3 * D_A]
    o = za_ref[:, 3 * D_A:4 * D_A]
    n_prev = n_ref[...]

    s = _head_sums(q * k) * ei
    den = sc * _head_sums(q * n_prev) + s
    denom = jnp.maximum(jnp.abs(den), emt)
    num = _expand_heads(sc, DV) * qc_ref[...] + _expand_heads(s, DV) * v
    hh = num / _expand_heads(denom, DV)
    ms = _head_sums(hh * hh) * (1.0 / DV)
    hn = hh * lax.rsqrt(_expand_heads(ms, DV) + EPS)
    ha_ref[...] = (hn * nrm_ref[...] * jax.nn.sigmoid(o)).astype(ha_ref.dtype)
    n_out[...] = _expand_heads(sc, DK) * n_prev + _expand_heads(ei, DK) * k
    m_out[...] = m_t
    pad = jnp.zeros((nb, LANE - H_A), F32)
    dec_out[...] = jnp.concatenate([sc, pad], axis=1)

    k_t = k.T
    ei_t = jnp.concatenate([ei, pad], axis=1).T
    for h in range(H_A):
        hs = slice(h * DK, (h + 1) * DK)
        wk_out[hs, :] = k_t[hs, :] * ei_t[h:h + 1, :]


def _mlstm_sample(za_s, qc, g_s, bias_row, nrm_row, m0, n0):
    nb = za_s.shape[0]
    assert nb == LANE, "the sequence axis is transposed onto the lanes"
    return pl.pallas_call(
        _mlstm_sample_kernel,
        out_shape=[
            jax.ShapeDtypeStruct((nb, D_A), BF16),
            jax.ShapeDtypeStruct((nb, H_A), F32),
            jax.ShapeDtypeStruct((nb, H_A * DK), F32),
            jax.ShapeDtypeStruct((D_A, nb), F32),
            jax.ShapeDtypeStruct((nb, LANE), F32),
        ],
        compiler_params=pltpu.CompilerParams(vmem_limit_bytes=VMEM_LIMIT),
        name="mlstm_sample",
    )(za_s, qc, g_s, bias_row, nrm_row, m0, n0)


def _outproj_kernel(hp_ref, hs_ref, ap_ref, as_ref, yb_ref, w_ref, op_ref, os_ref, wc_ref):
    j = pl.program_id(1)

    @pl.when(pl.program_id(0) == 0)
    def _():
        wc_ref[j] = w_ref[...].astype(BF16)

    wa = wc_ref[j, :D_A, :]
    wb = wc_ref[j, D_A:, :]
    op_ref[...] = hp_ref[...] + _bdot(ap_ref[...], wa) + _bdot(yb_ref[0, :MP, :], wb)
    os_ref[0] = hs_ref[0] + _bdot(as_ref[0], wa) + _bdot(yb_ref[0, MP:, :], wb)


def _outproj(hp, hs, ap, a_s, yb_c, w3):
    tn = OUT_TN
    n_n = D_MODEL // tn
    return pl.pallas_call(
        _outproj_kernel,
        grid=(N_MIX_BLOCKS, n_n),
        in_specs=[
            pl.BlockSpec((MP, tn), lambda i, j: (i, j)),
            pl.BlockSpec((1, MS, tn), lambda i, j: (i, 0, j)),
            pl.BlockSpec((MP, D_A), lambda i, j: (i, 0)),
            pl.BlockSpec((1, MS, D_A), lambda i, j: (i, 0, 0)),
            pl.BlockSpec((1, MP + MS, D_B), lambda i, j: (i, 0, 0)),
            pl.BlockSpec((None, D_A + D_B, tn), lambda i, j: (0, 0, jnp.where(i == 0, j, n_n - 1))),
        ],
        out_specs=[
            pl.BlockSpec((MP, tn), lambda i, j: (i, j)),
            pl.BlockSpec((1, MS, tn), lambda i, j: (i, 0, j)),
        ],
        out_shape=[
            jax.ShapeDtypeStruct((N_MIX_BLOCKS * MP, D_MODEL), F32),
            jax.ShapeDtypeStruct((N_MIX_BLOCKS, MS, D_MODEL), F32),
        ],
        scratch_shapes=[pltpu.VMEM((n_n, D_A + D_B, tn), BF16)],
        compiler_params=pltpu.CompilerParams(
            dimension_semantics=("arbitrary", "arbitrary"), vmem_limit_bytes=VMEM_LIMIT),
        name="outproj",
    )(hp, hs, ap, a_s, yb_c, w3)


def kernel(x_prompt, x_sample, state_mlstm_C, state_mlstm_n, state_mlstm_m, state_conv, norm_ffn1, ffn1_gate, ffn1_up, ffn1_down, norm_mix, w_in, b_gates, conv_w, conv_b, norm_mlstm, norm_conv, w_out, norm_ffn2, ffn2_gate, ffn2_up, ffn2_down, norm_final):
    batch, seq, _ = x_prompt.shape
    nb = x_sample.shape[0]
    assert batch == N_MIX_BLOCKS and seq == MP and nb == N_MIX_BLOCKS * MS
    assert norm_ffn1.shape[0] == 1, "single-layer trunk"

    xp = x_prompt.reshape(batch * seq, D_MODEL)
    xs = x_sample.reshape(N_FFN_BLOCKS, TM_S, D_MODEL)

    h1p, h1s, xn_c = _ffn(xp, xs, norm_ffn1, ffn1_gate, ffn1_up, ffn1_down, norm_mix)

    w_t = jnp.swapaxes(w_in[0], 0, 1)
    za_p, za_s, g_p, g_s = _proj(xn_c, w_t)
    za_s = za_s.reshape(nb, 4 * D_A)
    yb_c, conv_p, conv_s, qc_s = _conv(xn_c, w_t, conv_w[0], conv_b, norm_conv,
                                       state_conv.reshape(N_MIX_BLOCKS, MS, 2, D_B), za_s, state_mlstm_C)

    bias_row = jnp.zeros((1, LANE), F32).at[0, :2 * H_A].set(b_gates[0].astype(F32))
    ha_p, c_p, n_p, m_p = _mlstm_prompt(za_p, g_p, bias_row, norm_mlstm, batch, seq)
    ha_s, m_s, n_s, wk_s, dec_s = _mlstm_sample(
        za_s, qc_s, g_s.reshape(nb, LANE), bias_row, norm_mlstm,
        state_mlstm_m.reshape(nb, H_A), state_mlstm_n.reshape(nb, H_A * DK))

    h2p, h2s = _outproj(h1p, h1s.reshape(N_MIX_BLOCKS, MS, D_MODEL), ha_p.reshape(batch * seq, D_A),
                        ha_s.reshape(N_MIX_BLOCKS, MS, D_A), yb_c, w_out)

    yp, ys, c_s = _ffn(h2p, h2s.reshape(N_FFN_BLOCKS, TM_S, D_MODEL), norm_ffn2, ffn2_gate, ffn2_up,
                       ffn2_down, norm_final.reshape(1, D_MODEL),
                       state=(state_mlstm_C, wk_s, za_s, dec_s))

    return (
        yp.reshape(batch, seq, D_MODEL),
        ys.reshape(nb, 1, D_MODEL),
        c_p,
        n_p,
        m_p[:, 0, :H_A].reshape(1, batch, H_A),
        conv_p,
        c_s,
        n_s.reshape(1, nb, H_A, DK),
        m_s.reshape(1, nb, H_A),
        conv_s.reshape(1, nb, 2, D_B),
    )
```

```python
import functools

import jax
import jax.numpy as jnp
from jax import lax
from jax.experimental import pallas as pl
from jax.experimental.pallas import tpu as pltpu

F32 = jnp.float32
BF16 = jnp.bfloat16

D_MODEL = 2048
D_A = 1024
D_B = 1024
H_A = 4
DK = 256
DV = 256
G_B = 8
D_FF = 5504
GATE_CAP = 15.0
EPS = 1e-6
GATE_ROW0 = 4 * D_A
CONV_ROW0 = 4 * D_A + 2 * H_A

LANE = 128
N_FFN_BLOCKS = 8
TM_P = 1024
TM_S = 16
TM = TM_P + TM_S
TF = 256
N_MIX_BLOCKS = N_FFN_BLOCKS // 2
MP = 2 * TM_P
MS = 2 * TM_S
PROJ_TN = 1024
CONV_TC = 256
OUT_TN = 512
MLSTM_CHUNK = 256
MLSTM_SEQS = 4
SAMPLE_BS = 8
VMEM_LIMIT = 62 * 1024 * 1024


def _rms(x, g):
    return x * lax.rsqrt(jnp.mean(x * x, axis=-1, keepdims=True) + EPS) * g


def _soft_cap(x):
    return GATE_CAP * jnp.tanh(x / GATE_CAP)


def _log_sigmoid(x):
    return -jax.nn.softplus(-x)


def _bdot(a, b):
    return jnp.dot(a, b, preferred_element_type=F32)


def _bdot_t(a, b):
    return lax.dot_general(a, b, (((1,), (1,)), ((), ())), preferred_element_type=F32)


def _sample_state_update(f, c_ref, wk_ref, v_ref, dec_ref, c_out):
    r = jnp.minimum(f, TM_S - 1)
    seq = pl.program_id(0) * TM_S + r
    lane = lax.broadcasted_iota(jnp.int32, (D_A, LANE), 1)
    wk_col = jnp.sum(jnp.where(lane == seq, wk_ref[...], 0.0), axis=1, keepdims=True)
    v_row = v_ref[pl.ds(r, 1), :]
    dec = dec_ref[pl.ds(r, 1), :]
    for h in range(H_A):
        hs = slice(h * DK, (h + 1) * DK)
        c_out[0, 0, h] = dec[:, h:h + 1] * c_ref[0, 0, h] + wk_col[hs, :] * v_row[:, hs]


def _ffn_kernel(xp_ref, xs_ref, g_ref, wg_ref, wu_ref, wd_ref, g2_ref, *rest,
                n_f, last_valid, final_norm):
    if final_norm:
        c_ref, wk_ref, v_ref, dec_ref, op_ref, os_ref, c_out, xn_ref = rest
    else:
        op_ref, os_ref, nxt_ref, xn_ref = rest
    f = pl.program_id(1)

    def step(valid, first, with_state):
        if first:
            g = g_ref[...]
            xn_ref[:TM_P, :] = _rms(xp_ref[...], g).astype(BF16)
            xn_ref[TM_P:, :] = _rms(xs_ref[0], g).astype(BF16)
        xn = xn_ref[...]
        a = _bdot(xn, wg_ref[:, :valid].astype(BF16))
        b = _bdot(xn, wu_ref[:, :valid].astype(BF16))
        hid = (a * jax.nn.sigmoid(a) * b * 0.5).astype(BF16)
        r = _bdot(hid, wd_ref[:valid, :].astype(BF16))
        if first:
            op_ref[...] = xp_ref[...] + r[:TM_P]
            os_ref[0] = xs_ref[0] + r[TM_P:]
        else:
            op_ref[...] += r[:TM_P]
            os_ref[0] += r[TM_P:]
        if with_state:
            _sample_state_update(f, c_ref, wk_ref, v_ref, dec_ref, c_out)

    assert last_valid < TF and n_f - 1 >= TM_S, "state update rides the full-width steps"

    @pl.when(f == 0)
    def _():
        step(TF, True, final_norm)

    @pl.when(jnp.logical_and(f > 0, f < n_f - 1))
    def _():
        step(TF, False, final_norm)

    @pl.when(f == n_f - 1)
    def _():
        step(last_valid, False, False)
        g2 = g2_ref[...]
        if final_norm:
            op_ref[...] = _rms(op_ref[...], g2)
            os_ref[0] = _rms(os_ref[0], g2)
        else:
            nxt_ref[0, :TM_P, :] = _rms(op_ref[...], g2).astype(BF16)
            nxt_ref[0, TM_P:, :] = _rms(os_ref[0], g2).astype(BF16)


def _ffn(xp, xs, g, wg, wu, wd, g2, state=None):
    final_norm = state is not None
    n_f = pl.cdiv(D_FF, TF)
    last_valid = D_FF - (n_f - 1) * TF
    kern = functools.partial(_ffn_kernel, n_f=n_f, last_valid=last_valid, final_norm=final_norm)
    in_specs = [
        pl.BlockSpec((TM_P, D_MODEL), lambda i, f: (i, 0)),
        pl.BlockSpec((1, TM_S, D_MODEL), lambda i, f: (i, 0, 0)),
        pl.BlockSpec((1, D_MODEL), lambda i, f: (0, 0)),
        pl.BlockSpec((None, D_MODEL, TF), lambda i, f: (0, 0, f)),
        pl.BlockSpec((None, D_MODEL, TF), lambda i, f: (0, 0, f)),
        pl.BlockSpec((None, TF, D_MODEL), lambda i, f: (0, f, 0)),
        pl.BlockSpec((1, D_MODEL), lambda i, f: (0, 0)),
    ]
    out_specs = [
        pl.BlockSpec((TM_P, D_MODEL), lambda i, f: (i, 0)),
        pl.BlockSpec((1, TM_S, D_MODEL), lambda i, f: (i, 0, 0)),
    ]
    out_shape = [
        jax.ShapeDtypeStruct((N_FFN_BLOCKS * TM_P, D_MODEL), F32),
        jax.ShapeDtypeStruct((N_FFN_BLOCKS, TM_S, D_MODEL), F32),
    ]
    args = [xp, xs, g, wg, wu, wd, g2]
    if final_norm:
        c0, wk, za_s, dec = state
        seq = lambda i, f: i * TM_S + jnp.minimum(f, TM_S - 1)
        cspec = pl.BlockSpec((1, 1, H_A, DK, DV), lambda i, f: (0, seq(i, f), 0, 0, 0))
        in_specs += [
            cspec,
            pl.BlockSpec((D_A, LANE), lambda i, f: (0, 0)),
            pl.BlockSpec((TM_S, D_A), lambda i, f: (i, 2)),
            pl.BlockSpec((TM_S, LANE), lambda i, f: (i, 0)),
        ]
        out_specs.append(cspec)
        out_shape.append(jax.ShapeDtypeStruct(c0.shape, F32))
        args += [c0, wk, za_s, dec]
    else:
        out_specs.append(pl.BlockSpec((1, TM, D_MODEL), lambda i, f: (i, 0, 0)))
        out_shape.append(jax.ShapeDtypeStruct((N_FFN_BLOCKS, TM, D_MODEL), BF16))
    return pl.pallas_call(
        kern,
        grid=(N_FFN_BLOCKS, n_f),
        in_specs=in_specs,
        out_specs=out_specs,
        out_shape=out_shape,
        scratch_shapes=[pltpu.VMEM((TM, D_MODEL), BF16)],
        compiler_params=pltpu.CompilerParams(
            dimension_semantics=("parallel", "arbitrary"), vmem_limit_bytes=VMEM_LIMIT),
        name="ffn_final" if final_norm else "ffn",
    )(*args)


def _split_rows(z0, z1, p_ref, s_ref):
    p_ref[:TM_P, :] = z0[:TM_P]
    p_ref[TM_P:, :] = z1[:TM_P]
    s_ref[0, :TM_S, :] = z0[TM_P:]
    s_ref[0, TM_S:, :] = z1[TM_P:]


def _proj_kernel(xn_ref, w_ref, wg_ref, zp_ref, zs_ref, gp_ref, gs_ref):
    w = w_ref[...].astype(BF16)
    _split_rows(_bdot_t(xn_ref[0], w), _bdot_t(xn_ref[1], w), zp_ref, zs_ref)

    @pl.when(pl.program_id(1) == 0)
    def _():
        wg = wg_ref[...].astype(BF16)
        _split_rows(_bdot_t(xn_ref[0], wg), _bdot_t(xn_ref[1], wg), gp_ref, gs_ref)


def _proj(xn_c, w_t):
    n_n = (4 * D_A) // PROJ_TN
    return pl.pallas_call(
        _proj_kernel,
        grid=(N_MIX_BLOCKS, n_n),
        in_specs=[
            pl.BlockSpec((2, TM, D_MODEL), lambda i, j: (i, 0, 0)),
            pl.BlockSpec((PROJ_TN, D_MODEL), lambda i, j: (j, 0)),
            pl.BlockSpec((LANE, D_MODEL), lambda i, j: (GATE_ROW0 // LANE, 0)),
        ],
        out_specs=[
            pl.BlockSpec((MP, PROJ_TN), lambda i, j: (i, j)),
            pl.BlockSpec((1, MS, PROJ_TN), lambda i, j: (i, 0, j)),
            pl.BlockSpec((MP, LANE), lambda i, j: (i, 0)),
            pl.BlockSpec((1, MS, LANE), lambda i, j: (i, 0, 0)),
        ],
        out_shape=[
            jax.ShapeDtypeStruct((N_MIX_BLOCKS * MP, 4 * D_A), F32),
            jax.ShapeDtypeStruct((N_MIX_BLOCKS, MS, 4 * D_A), F32),
            jax.ShapeDtypeStruct((N_MIX_BLOCKS * MP, LANE), F32),
            jax.ShapeDtypeStruct((N_MIX_BLOCKS, MS, LANE), F32),
        ],
        compiler_params=pltpu.CompilerParams(
            dimension_semantics=("parallel", "arbitrary"), vmem_limit_bytes=VMEM_LIMIT),
        name="proj_qkvo",
    )(xn_c, w_t, w_t)


def _group_norm(yb, nrm):
    gw = D_B // G_B
    parts = []
    for g in range(yb.shape[1] // gw):
        seg = yb[:, g * gw:(g + 1) * gw]
        parts.append(seg * lax.rsqrt(jnp.mean(seg * seg, axis=-1, keepdims=True) + EPS))
    return jnp.concatenate(parts, axis=1) * nrm


def _conv_kernel(xn_ref, wgb_ref, wgc_ref, wxc_ref, cw_ref, cb_ref, nrm_ref, buf_ref, q_ref, c_ref,
                 yb_ref, cp_ref, cs_ref, qc_ref):
    q_t = jnp.concatenate([q_ref[...], jnp.zeros((LANE - SAMPLE_BS, D_A), F32)], axis=0).T
    for h in range(H_A):
        hs = slice(h * DK, (h + 1) * DK)
        for j in range(SAMPLE_BS):
            qc_ref[j:j + 1, hs] = jnp.sum(q_t[hs, j:j + 1] * c_ref[0, j, h], axis=0, keepdims=True)

    wgb = wgb_ref[...].astype(BF16)
    wgc = wgc_ref[...].astype(BF16)
    wxc = wxc_ref[...].astype(BF16)
    cw0 = cw_ref[0:1, :]
    cw1 = cw_ref[1:2, :]
    cw2 = cw_ref[2:3, :]
    cb = cb_ref[...]
    nrm = nrm_ref[...]
    row = lax.broadcasted_iota(jnp.int32, (TM_P, wgb.shape[0]), 0)
    zero_row = jnp.zeros((1, wgb.shape[0]), F32)
    um2, um1 = zero_row, zero_row

    for t in range(2):
        xn = xn_ref[t]
        u = _bdot_t(xn, wgc) * _bdot_t(xn, wxc)
        gb = _bdot_t(xn, wgb)

        up = u[:TM_P]
        u1 = jnp.where(row < 1, um1, pltpu.roll(up, 1, 0))
        u2 = jnp.where(row < 1, um2, jnp.where(row < 2, um1, pltpu.roll(up, 2, 0)))
        yc = cw0 * u2 + cw1 * u1 + cw2 * up + cb
        yb_ref[0, t * TM_P:(t + 1) * TM_P, :] = _group_norm(gb[:TM_P] * yc, nrm).astype(yb_ref.dtype)
        um2, um1 = up[TM_P - 2:TM_P - 1, :], up[TM_P - 1:TM_P, :]

        ss = slice(t * TM_S, (t + 1) * TM_S)
        us = u[TM_P:]
        b0 = buf_ref[0, ss, 0, :]
        b1 = buf_ref[0, ss, 1, :]
        ycs = cw0 * b0 + cw1 * b1 + cw2 * us + cb
        yb_ref[0, MP + t * TM_S:MP + (t + 1) * TM_S, :] = _group_norm(gb[TM_P:] * ycs, nrm).astype(yb_ref.dtype)
        cs_ref[0, ss, 0, :] = b1
        cs_ref[0, ss, 1, :] = us

    cp_ref[0, 0, 0:1, :] = um2
    cp_ref[0, 0, 1:2, :] = um1


def _conv(xn_c, w_t, cw, cb_row, nrm_row, buf, za_s, c0):
    tc = CONV_TC
    n_c = D_B // tc
    assert N_MIX_BLOCKS * n_c * SAMPLE_BS == za_s.shape[0], "one sample group per grid step"
    wspec = lambda base: pl.BlockSpec(
        (pl.Element(tc), pl.Element(D_MODEL)), lambda i, c: (pl.multiple_of(base + c * tc, 8), 0))
    grp = lambda i, c: i * n_c + c
    return pl.pallas_call(
        _conv_kernel,
        grid=(N_MIX_BLOCKS, n_c),
        in_specs=[
            pl.BlockSpec((2, TM, D_MODEL), lambda i, c: (i, 0, 0)),
            wspec(CONV_ROW0),
            wspec(CONV_ROW0 + D_B),
            wspec(CONV_ROW0 + 2 * D_B),
            pl.BlockSpec((None, 3, tc), lambda i, c: (0, 0, c)),
            pl.BlockSpec((1, tc), lambda i, c: (0, c)),
            pl.BlockSpec((1, tc), lambda i, c: (0, c)),
            pl.BlockSpec((1, MS, 2, tc), lambda i, c: (i, 0, 0, c)),
            pl.BlockSpec((SAMPLE_BS, D_A), lambda i, c: (grp(i, c), 0)),
            pl.BlockSpec((1, SAMPLE_BS, H_A, DK, DV), lambda i, c: (0, grp(i, c), 0, 0, 0)),
        ],
        out_specs=[
            pl.BlockSpec((1, MP + MS, tc), lambda i, c: (i, 0, c)),
            pl.BlockSpec((1, 1, 2, tc), lambda i, c: (0, i, 0, c)),
            pl.BlockSpec((1, MS, 2, tc), lambda i, c: (i, 0, 0, c)),
            pl.BlockSpec((SAMPLE_BS, D_A), lambda i, c: (grp(i, c), 0)),
        ],
        out_shape=[
            jax.ShapeDtypeStruct((N_MIX_BLOCKS, MP + MS, D_B), BF16),
            jax.ShapeDtypeStruct((1, N_MIX_BLOCKS, 2, D_B), F32),
            jax.ShapeDtypeStruct((N_MIX_BLOCKS, MS, 2, D_B), F32),
            jax.ShapeDtypeStruct((za_s.shape[0], D_A), F32),
        ],
        compiler_params=pltpu.CompilerParams(
            dimension_semantics=("parallel", "arbitrary"), vmem_limit_bytes=VMEM_LIMIT),
        name="proj_conv",
    )(xn_c, w_t, w_t, w_t, cw, cb_row, nrm_row, buf, za_s, c0)


def _mlstm_prompt_kernel(q_ref, k_ref, v_ref, o_ref, g_ref, bias_ref, nrm_ref,
                         ha_ref, c_ref, n_ref, m_ref, cx_s, m_s, *, n_chunks):
    L = MLSTM_CHUNK
    c = pl.program_id(1)

    @pl.when(c == 0)
    def _():
        cx_s[...] = jnp.zeros_like(cx_s)
        m_s[...] = jnp.zeros_like(m_s)

    row = lax.broadcasted_iota(jnp.int32, (L, L), 0)
    col = lax.broadcasted_iota(jnp.int32, (L, L), 1)
    causal = row >= col
    tri = causal.astype(BF16)

    for b in range(MLSTM_SEQS):
        _mlstm_prompt_seq(b, causal, tri, q_ref, k_ref, v_ref, o_ref, g_ref, bias_ref, nrm_ref,
                          ha_ref, cx_s, m_s)

    @pl.when(c == n_chunks - 1)
    def _():
        for b in range(MLSTM_SEQS):
            m_ref[b] = m_s[b]
            for h in range(H_A):
                cx = cx_s[b * H_A + h]
                c_ref[0, b, h] = cx[:, :DV]
                n_ref[0, b, h:h + 1, :] = cx[:, DV:].T[h:h + 1, :]


def _prefix_max_rows(x):
    rows = x.shape[0]
    row = lax.broadcasted_iota(jnp.int32, x.shape, 0)
    k = 1
    while k < rows:
        x = jnp.where(row >= k, jnp.maximum(x, pltpu.roll(x, k, 0)), x)
        k *= 2
    return x


def _mlstm_prompt_seq(b, causal, tri, q_ref, k_ref, v_ref, o_ref, g_ref, bias_ref, nrm_ref,
                      ha_ref, cx_s, m_s):
    L = MLSTM_CHUNK
    lane = lax.broadcasted_iota(jnp.int32, (L, LANE), 1)
    lane_dv = lax.broadcasted_iota(jnp.int32, (DV, LANE), 1)
    capped = _soft_cap(g_ref[b] + bias_ref[...])
    lf = _log_sigmoid(capped)
    hi = lf.astype(BF16)
    r1 = lf - hi.astype(F32)
    mid = r1.astype(BF16)
    lo = (r1 - mid.astype(F32)).astype(BF16)
    parts = _bdot(tri, jnp.concatenate([hi, mid, lo], axis=1))
    bc = parts[:, :LANE] + parts[:, LANE:2 * LANE] + parts[:, 2 * LANE:]
    bh = pltpu.roll(bc, LANE - H_A, 1)
    a = capped - bh
    a_t = a.T
    m_prev = m_s[b]
    big_m = jnp.maximum(_prefix_max_rows(a), m_prev[0:1, :])
    sc_all = jnp.exp(m_prev[0:1, :] - big_m)
    emt_all = jnp.exp(-(bh + big_m))
    m_last = big_m[L - 1:L, :]
    w_all = jnp.exp(a - m_last)
    decay_all = sc_all[L - 1:L, :]
    m_s[b] = jnp.broadcast_to(bh[L - 1:L, :] + m_last, (8, LANE))

    nums = []
    nd_tail = jnp.zeros((L, LANE), F32)
    sq_tail = jnp.zeros((L, LANE), F32)
    for h in range(H_A):
        hs = slice(h * DK, (h + 1) * DK)
        onehot = (lane == h).astype(BF16)
        qb = q_ref[b, :, hs].astype(BF16)
        kf = k_ref[b, :, hs] * (DK ** -0.5)
        kb = kf.astype(BF16)
        vx = jnp.concatenate([v_ref[b, :, hs].astype(BF16), onehot], axis=1)
        cx_prev = cx_s[b * H_A + h]

        p = jnp.exp(jnp.where(causal, a_t[h:h + 1, :] - big_m[:, h:h + 1], -jnp.inf))
        s = _bdot_t(qb, kb) * p
        nd = sc_all[:, h:h + 1] * _bdot(qb, cx_prev.astype(BF16)) + _bdot(s.astype(BF16), vx)
        num = nd[:, :DV]
        nums.append(num)
        nd_tail = nd_tail + nd[:, DV:]
        sq_tail = sq_tail + _bdot((num * num).astype(BF16), (lane_dv == h).astype(BF16))

        wk = (w_all[:, h:h + 1] * kf).astype(BF16)
        cx_s[b * H_A + h] = decay_all[:, h:h + 1] * cx_prev + lax.dot_general(
            wk, vx, (((0,), (0,)), ((), ())), preferred_element_type=F32)

    inv = 1.0 / jnp.maximum(jnp.abs(nd_tail), emt_all)
    ms = inv * inv * sq_tail * (1.0 / DV)
    scale_all = inv * lax.rsqrt(ms + EPS)
    for h in range(H_A):
        hs = slice(h * DK, (h + 1) * DK)
        hn = nums[h] * scale_all[:, h:h + 1]
        ha_ref[b, :, hs] = (hn * nrm_ref[:, hs] * jax.nn.sigmoid(o_ref[b, :, hs])).astype(ha_ref.dtype)


def _mlstm_prompt(za_p, g_p, bias_row, nrm_row, batch, seq):
    L = MLSTM_CHUNK
    nsq = MLSTM_SEQS
    n_chunks = seq // L
    kern = functools.partial(_mlstm_prompt_kernel, n_chunks=n_chunks)
    za3 = za_p.reshape(batch, seq, 4 * D_A)
    g3 = g_p.reshape(batch, seq, LANE)
    zspec = lambda part: pl.BlockSpec((nsq, L, D_A), lambda b, c: (b, c, part))
    return pl.pallas_call(
        kern,
        grid=(batch // nsq, n_chunks),
        in_specs=[
            zspec(0), zspec(1), zspec(2), zspec(3),
            pl.BlockSpec((nsq, L, LANE), lambda b, c: (b, c, 0)),
            pl.BlockSpec((1, LANE), lambda b, c: (0, 0)),
            pl.BlockSpec((1, D_A), lambda b, c: (0, 0)),
        ],
        out_specs=[
            pl.BlockSpec((nsq, L, D_A), lambda b, c: (b, c, 0)),
            pl.BlockSpec((1, nsq, H_A, DK, DV), lambda b, c: (0, b, 0, 0, 0)),
            pl.BlockSpec((1, nsq, H_A, DK), lambda b, c: (0, b, 0, 0)),
            pl.BlockSpec((nsq, 8, LANE), lambda b, c: (b, 0, 0)),
        ],
        out_shape=[
            jax.ShapeDtypeStruct((batch, seq, D_A), BF16),
            jax.ShapeDtypeStruct((1, batch, H_A, DK, DV), F32),
            jax.ShapeDtypeStruct((1, batch, H_A, DK), F32),
            jax.ShapeDtypeStruct((batch, 8, LANE), F32),
        ],
        scratch_shapes=[pltpu.VMEM((nsq * H_A, DK, DV + LANE), F32),
                        pltpu.VMEM((nsq, 8, LANE), F32)],
        compiler_params=pltpu.CompilerParams(
            dimension_semantics=("parallel", "arbitrary"), vmem_limit_bytes=VMEM_LIMIT),
        name="mlstm_prompt",
    )(za3, za3, za3, za3, g3, bias_row, nrm_row)


def _expand_heads(x, width):
    rows = x.shape[0]
    return jnp.concatenate([jnp.broadcast_to(x[:, h:h + 1], (rows, width)) for h in range(H_A)], axis=1)


def _head_sums(x):
    return jnp.concatenate(
        [jnp.sum(x[:, h * DK:(h + 1) * DK], axis=-1, keepdims=True) for h in range(H_A)], axis=1)


def _mlstm_sample_kernel(za_ref, qc_ref, g_ref, bias_ref, nrm_ref, m_ref, n_ref,
                         ha_ref, m_out, n_out, wk_out, dec_out):
    nb = za_ref.shape[0]
    capped = _soft_cap(g_ref[...] + bias_ref[...])
    logi = capped[:, 0:H_A]
    logf = _log_sigmoid(capped)[:, H_A:2 * H_A]
    m_prev = m_ref[0]
    m_inter = logf + m_prev
    m_t = jnp.maximum(m_inter, logi)
    sc = jnp.exp(m_inter - m_t)
    ei = jnp.exp(logi - m_t)
    emt = jnp.exp(-m_t)

    q = za_ref[:, 0:D_A]
    k = za_ref[:, D_A:2 * D_A] * (DK ** -0.5)
    v = za_ref[:, 2 * D_A:3 * D_A]
    o = za_ref[:, 3 * D_A:4 * D_A]
    n_prev = jnp.concatenate([n_ref[0, :, h, :] for h in range(H_A)], axis=1)

    s = _head_sums(q * k) * ei
    den = sc * _head_sums(q * n_prev) + s
    denom = jnp.maximum(jnp.abs(den), emt)
    num = _expand_heads(sc, DV) * qc_ref[...] + _expand_heads(s, DV) * v
    hh = num / _expand_heads(denom, DV)
    ms = _head_sums(hh * hh) * (1.0 / DV)
    hn = hh * lax.rsqrt(_expand_heads(ms, DV) + EPS)
    ha_ref[...] = (hn * nrm_ref[...] * jax.nn.sigmoid(o)).astype(ha_ref.dtype)
    n_new = _expand_heads(sc, DK) * n_prev + _expand_heads(ei, DK) * k
    for h in range(H_A):
        n_out[0, :, h, :] = n_new[:, h * DK:(h + 1) * DK]
    m_out[0] = m_t
    pad = jnp.zeros((nb, LANE - H_A), F32)
    dec_out[...] = jnp.concatenate([sc, pad], axis=1)

    k_t = k.T
    ei_t = jnp.concatenate([ei, pad], axis=1).T
    for h in range(H_A):
        hs = slice(h * DK, (h + 1) * DK)
        wk_out[hs, :] = k_t[hs, :] * ei_t[h:h + 1, :]


def _mlstm_sample(za_s, qc, g_s, bias_row, nrm_row, m0, n0):
    nb = za_s.shape[0]
    assert nb == LANE, "the sequence axis is transposed onto the lanes"
    return pl.pallas_call(
        _mlstm_sample_kernel,
        out_shape=[
            jax.ShapeDtypeStruct((nb, D_A), BF16),
            jax.ShapeDtypeStruct((1, nb, H_A), F32),
            jax.ShapeDtypeStruct((1, nb, H_A, DK), F32),
            jax.ShapeDtypeStruct((D_A, nb), F32),
            jax.ShapeDtypeStruct((nb, LANE), F32),
        ],
        compiler_params=pltpu.CompilerParams(vmem_limit_bytes=VMEM_LIMIT),
        name="mlstm_sample",
    )(za_s, qc, g_s, bias_row, nrm_row, m0, n0)


def _outproj_kernel(hp_ref, hs_ref, ap_ref, as_ref, yb_ref, w_ref, op_ref, os_ref, wc_ref):
    j = pl.program_id(1)

    @pl.when(pl.program_id(0) == 0)
    def _():
        wc_ref[j] = w_ref[...].astype(BF16)

    wa = wc_ref[j, :D_A, :]
    wb = wc_ref[j, D_A:, :]
    op_ref[...] = hp_ref[...] + _bdot(ap_ref[...], wa) + _bdot(yb_ref[0, :MP, :], wb)
    os_ref[0] = hs_ref[0] + _bdot(as_ref[0], wa) + _bdot(yb_ref[0, MP:, :], wb)


def _outproj(hp, hs, ap, a_s, yb_c, w3):
    tn = OUT_TN
    n_n = D_MODEL // tn
    return pl.pallas_call(
        _outproj_kernel,
        grid=(N_MIX_BLOCKS, n_n),
        in_specs=[
            pl.BlockSpec((MP, tn), lambda i, j: (i, j)),
            pl.BlockSpec((1, MS, tn), lambda i, j: (i, 0, j)),
            pl.BlockSpec((MP, D_A), lambda i, j: (i, 0)),
            pl.BlockSpec((1, MS, D_A), lambda i, j: (i, 0, 0)),
            pl.BlockSpec((1, MP + MS, D_B), lambda i, j: (i, 0, 0)),
            pl.BlockSpec((None, D_A + D_B, tn), lambda i, j: (0, 0, jnp.where(i == 0, j, n_n - 1))),
        ],
        out_specs=[
            pl.BlockSpec((MP, tn), lambda i, j: (i, j)),
            pl.BlockSpec((1, MS, tn), lambda i, j: (i, 0, j)),
        ],
        out_shape=[
            jax.ShapeDtypeStruct((N_MIX_BLOCKS * MP, D_MODEL), F32),
            jax.ShapeDtypeStruct((N_MIX_BLOCKS, MS, D_MODEL), F32),
        ],
        scratch_shapes=[pltpu.VMEM((n_n, D_A + D_B, tn), BF16)],
        compiler_params=pltpu.CompilerParams(
            dimension_semantics=("arbitrary", "arbitrary"), vmem_limit_bytes=VMEM_LIMIT),
        name="outproj",
    )(hp, hs, ap, a_s, yb_c, w3)


def kernel(x_prompt, x_sample, state_mlstm_C, state_mlstm_n, state_mlstm_m, state_conv, norm_ffn1, ffn1_gate, ffn1_up, ffn1_down, norm_mix, w_in, b_gates, conv_w, conv_b, norm_mlstm, norm_conv, w_out, norm_ffn2, ffn2_gate, ffn2_up, ffn2_down, norm_final):
    batch, seq, _ = x_prompt.shape
    nb = x_sample.shape[0]
    assert batch == N_MIX_BLOCKS and seq == MP and nb == N_MIX_BLOCKS * MS
    assert norm_ffn1.shape[0] == 1, "single-layer trunk"

    xp = x_prompt.reshape(batch * seq, D_MODEL)
    xs = x_sample.reshape(N_FFN_BLOCKS, TM_S, D_MODEL)

    h1p, h1s, xn_c = _ffn(xp, xs, norm_ffn1, ffn1_gate, ffn1_up, ffn1_down, norm_mix)

    w_t = jnp.swapaxes(w_in[0], 0, 1)
    za_p, za_s, g_p, g_s = _proj(xn_c, w_t)
    za_s = za_s.reshape(nb, 4 * D_A)
    yb_c, conv_p, conv_s, qc_s = _conv(xn_c, w_t, conv_w, conv_b, norm_conv,
                                       state_conv.reshape(N_MIX_BLOCKS, MS, 2, D_B), za_s, state_mlstm_C)

    bias_row = jnp.pad(b_gates.astype(F32), ((0, 0), (0, LANE - 2 * H_A)))
    ha_p, c_p, n_p, m_p = _mlstm_prompt(za_p, g_p, bias_row, norm_mlstm, batch, seq)
    ha_s, m_s, n_s, wk_s, dec_s = _mlstm_sample(
        za_s, qc_s, g_s.reshape(nb, LANE), bias_row, norm_mlstm,
        state_mlstm_m, state_mlstm_n)

    h2p, h2s = _outproj(h1p, h1s.reshape(N_MIX_BLOCKS, MS, D_MODEL), ha_p.reshape(batch * seq, D_A),
                        ha_s.reshape(N_MIX_BLOCKS, MS, D_A), yb_c, w_out)

    yp, ys, c_s = _ffn(h2p, h2s.reshape(N_FFN_BLOCKS, TM_S, D_MODEL), norm_ffn2, ffn2_gate, ffn2_up,
                       ffn2_down, norm_final.reshape(1, D_MODEL),
                       state=(state_mlstm_C, wk_s, za_s, dec_s))

    return (
        yp.reshape(batch, seq, D_MODEL),
        ys.reshape(nb, 1, D_MODEL),
        c_p,
        n_p,
        m_p[:, 0, :H_A].reshape(1, batch, H_A),
        conv_p,
        c_s,
        n_s,
        m_s,
        conv_s.reshape(1, nb, 2, D_B),
    )
```

```python
import functools

import jax
import jax.numpy as jnp
from jax import lax
from jax.experimental import pallas as pl
from jax.experimental.pallas import tpu as pltpu

F32 = jnp.float32
BF16 = jnp.bfloat16

D_MODEL = 2048
D_A = 1024
D_B = 1024
H_A = 4
DK = 256
DV = 256
G_B = 8
D_FF = 5504
GATE_CAP = 15.0
EPS = 1e-6
GATE_ROW0 = 4 * D_A
CONV_ROW0 = 4 * D_A + 2 * H_A

LANE = 128
N_FFN_BLOCKS = 8
TM_P = 1024
TM_S = 16
TM = TM_P + TM_S
TF = 256
N_MIX_BLOCKS = N_FFN_BLOCKS // 2
MP = 2 * TM_P
MS = 2 * TM_S
PROJ_TN = 1024
CONV_TC = 256
OUT_TN = 512
MLSTM_CHUNK = 256
MLSTM_SEQS = 2
SAMPLE_BS = 8
VMEM_LIMIT = 62 * 1024 * 1024


def _rms(x, g):
    return x * lax.rsqrt(jnp.mean(x * x, axis=-1, keepdims=True) + EPS) * g


def _soft_cap(x):
    return GATE_CAP * jnp.tanh(x / GATE_CAP)


def _log_sigmoid(x):
    return -jax.nn.softplus(-x)


def _bdot(a, b):
    return jnp.dot(a, b, preferred_element_type=F32)


def _bdot_t(a, b):
    return lax.dot_general(a, b, (((1,), (1,)), ((), ())), preferred_element_type=F32)


def _sample_state_update(f, c_ref, wk_ref, v_ref, dec_ref, c_out):
    r = jnp.minimum(f, TM_S - 1)
    seq = pl.program_id(0) * TM_S + r
    lane = lax.broadcasted_iota(jnp.int32, (D_A, LANE), 1)
    wk_col = jnp.sum(jnp.where(lane == seq, wk_ref[...], 0.0), axis=1, keepdims=True)
    v_row = v_ref[pl.ds(r, 1), :]
    dec = dec_ref[pl.ds(r, 1), :]
    for h in range(H_A):
        hs = slice(h * DK, (h + 1) * DK)
        c_out[0, 0, h] = dec[:, h:h + 1] * c_ref[0, 0, h] + wk_col[hs, :] * v_row[:, hs]


def _ffn_kernel(xp_ref, xs_ref, g_ref, wg_ref, wu_ref, wd_ref, g2_ref, *rest,
                n_f, last_valid, final_norm):
    if final_norm:
        c_ref, wk_ref, v_ref, dec_ref, op_ref, os_ref, c_out, xn_ref = rest
    else:
        op_ref, os_ref, nxt_ref, xn_ref = rest
    f = pl.program_id(1)

    def step(valid, first, with_state):
        if first:
            g = g_ref[...]
            xn_ref[:TM_P, :] = _rms(xp_ref[...], g).astype(BF16)
            xn_ref[TM_P:, :] = _rms(xs_ref[0], g).astype(BF16)
        xn = xn_ref[...]
        a = _bdot(xn, wg_ref[:, :valid].astype(BF16))
        b = _bdot(xn, wu_ref[:, :valid].astype(BF16))
        hid = (a * jax.nn.sigmoid(a) * b * 0.5).astype(BF16)
        r = _bdot(hid, wd_ref[:valid, :].astype(BF16))
        if first:
            op_ref[...] = xp_ref[...] + r[:TM_P]
            os_ref[0] = xs_ref[0] + r[TM_P:]
        else:
            op_ref[...] += r[:TM_P]
            os_ref[0] += r[TM_P:]
        if with_state:
            _sample_state_update(f, c_ref, wk_ref, v_ref, dec_ref, c_out)

    assert last_valid < TF and n_f - 1 >= TM_S, "state update rides the full-width steps"

    @pl.when(f == 0)
    def _():
        step(TF, True, final_norm)

    @pl.when(jnp.logical_and(f > 0, f < n_f - 1))
    def _():
        step(TF, False, final_norm)

    @pl.when(f == n_f - 1)
    def _():
        step(last_valid, False, False)
        g2 = g2_ref[...]
        if final_norm:
            op_ref[...] = _rms(op_ref[...], g2)
            os_ref[0] = _rms(os_ref[0], g2)
        else:
            nxt_ref[0, :TM_P, :] = _rms(op_ref[...], g2).astype(BF16)
            nxt_ref[0, TM_P:, :] = _rms(os_ref[0], g2).astype(BF16)


def _ffn(xp, xs, g, wg, wu, wd, g2, state=None):
    final_norm = state is not None
    n_f = pl.cdiv(D_FF, TF)
    last_valid = D_FF - (n_f - 1) * TF
    kern = functools.partial(_ffn_kernel, n_f=n_f, last_valid=last_valid, final_norm=final_norm)
    in_specs = [
        pl.BlockSpec((TM_P, D_MODEL), lambda i, f: (i, 0)),
        pl.BlockSpec((1, TM_S, D_MODEL), lambda i, f: (i, 0, 0)),
        pl.BlockSpec((1, D_MODEL), lambda i, f: (0, 0)),
        pl.BlockSpec((None, D_MODEL, TF), lambda i, f: (0, 0, f)),
        pl.BlockSpec((None, D_MODEL, TF), lambda i, f: (0, 0, f)),
        pl.BlockSpec((None, TF, D_MODEL), lambda i, f: (0, f, 0)),
        pl.BlockSpec((1, D_MODEL), lambda i, f: (0, 0)),
    ]
    out_specs = [
        pl.BlockSpec((TM_P, D_MODEL), lambda i, f: (i, 0)),
        pl.BlockSpec((1, TM_S, D_MODEL), lambda i, f: (i, 0, 0)),
    ]
    out_shape = [
        jax.ShapeDtypeStruct((N_FFN_BLOCKS * TM_P, D_MODEL), F32),
        jax.ShapeDtypeStruct((N_FFN_BLOCKS, TM_S, D_MODEL), F32),
    ]
    args = [xp, xs, g, wg, wu, wd, g2]
    if final_norm:
        c0, wk, za_s, dec = state
        seq = lambda i, f: i * TM_S + jnp.minimum(f, TM_S - 1)
        cspec = pl.BlockSpec((1, 1, H_A, DK, DV), lambda i, f: (0, seq(i, f), 0, 0, 0))
        in_specs += [
            cspec,
            pl.BlockSpec((D_A, LANE), lambda i, f: (0, 0)),
            pl.BlockSpec((TM_S, D_A), lambda i, f: (i, 2)),
            pl.BlockSpec((TM_S, LANE), lambda i, f: (i, 0)),
        ]
        out_specs.append(cspec)
        out_shape.append(jax.ShapeDtypeStruct(c0.shape, F32))
        args += [c0, wk, za_s, dec]
    else:
        out_specs.append(pl.BlockSpec((1, TM, D_MODEL), lambda i, f: (i, 0, 0)))
        out_shape.append(jax.ShapeDtypeStruct((N_FFN_BLOCKS, TM, D_MODEL), BF16))
    return pl.pallas_call(
        kern,
        grid=(N_FFN_BLOCKS, n_f),
        in_specs=in_specs,
        out_specs=out_specs,
        out_shape=out_shape,
        scratch_shapes=[pltpu.VMEM((TM, D_MODEL), BF16)],
        compiler_params=pltpu.CompilerParams(
            dimension_semantics=("parallel", "arbitrary"), vmem_limit_bytes=VMEM_LIMIT),
        name="ffn_final" if final_norm else "ffn",
    )(*args)


def _split_rows(z0, z1, p_ref, s_ref):
    p_ref[:TM_P, :] = z0[:TM_P]
    p_ref[TM_P:, :] = z1[:TM_P]
    s_ref[0, :TM_S, :] = z0[TM_P:]
    s_ref[0, TM_S:, :] = z1[TM_P:]


def _proj_kernel(xn_ref, w_ref, wg_ref, zp_ref, zs_ref, gp_ref, gs_ref):
    w = w_ref[...].astype(BF16)
    _split_rows(_bdot_t(xn_ref[0], w), _bdot_t(xn_ref[1], w), zp_ref, zs_ref)

    @pl.when(pl.program_id(1) == 0)
    def _():
        wg = wg_ref[...].astype(BF16)
        _split_rows(_bdot_t(xn_ref[0], wg), _bdot_t(xn_ref[1], wg), gp_ref, gs_ref)


def _proj(xn_c, w_t):
    n_n = (4 * D_A) // PROJ_TN
    return pl.pallas_call(
        _proj_kernel,
        grid=(N_MIX_BLOCKS, n_n),
        in_specs=[
            pl.BlockSpec((2, TM, D_MODEL), lambda i, j: (i, 0, 0)),
            pl.BlockSpec((PROJ_TN, D_MODEL), lambda i, j: (j, 0)),
            pl.BlockSpec((LANE, D_MODEL), lambda i, j: (GATE_ROW0 // LANE, 0)),
        ],
        out_specs=[
            pl.BlockSpec((MP, PROJ_TN), lambda i, j: (i, j)),
            pl.BlockSpec((1, MS, PROJ_TN), lambda i, j: (i, 0, j)),
            pl.BlockSpec((MP, LANE), lambda i, j: (i, 0)),
            pl.BlockSpec((1, MS, LANE), lambda i, j: (i, 0, 0)),
        ],
        out_shape=[
            jax.ShapeDtypeStruct((N_MIX_BLOCKS * MP, 4 * D_A), F32),
            jax.ShapeDtypeStruct((N_MIX_BLOCKS, MS, 4 * D_A), F32),
            jax.ShapeDtypeStruct((N_MIX_BLOCKS * MP, LANE), F32),
            jax.ShapeDtypeStruct((N_MIX_BLOCKS, MS, LANE), F32),
        ],
        compiler_params=pltpu.CompilerParams(
            dimension_semantics=("parallel", "arbitrary"), vmem_limit_bytes=VMEM_LIMIT),
        name="proj_qkvo",
    )(xn_c, w_t, w_t)


def _group_norm(yb, nrm):
    gw = D_B // G_B
    parts = []
    for g in range(yb.shape[1] // gw):
        seg = yb[:, g * gw:(g + 1) * gw]
        parts.append(seg * lax.rsqrt(jnp.mean(seg * seg, axis=-1, keepdims=True) + EPS))
    return jnp.concatenate(parts, axis=1) * nrm


def _conv_kernel(xn_ref, wgb_ref, wgc_ref, wxc_ref, cw_ref, cb_ref, nrm_ref, buf_ref, q_ref, c_ref,
                 yb_ref, cp_ref, cs_ref, qc_ref):
    q_t = jnp.concatenate([q_ref[...], jnp.zeros((LANE - SAMPLE_BS, D_A), F32)], axis=0).T
    for h in range(H_A):
        hs = slice(h * DK, (h + 1) * DK)
        for j in range(SAMPLE_BS):
            qc_ref[j:j + 1, hs] = jnp.sum(q_t[hs, j:j + 1] * c_ref[0, j, h], axis=0, keepdims=True)

    wgb = wgb_ref[...].astype(BF16)
    wgc = wgc_ref[...].astype(BF16)
    wxc = wxc_ref[...].astype(BF16)
    cw0 = cw_ref[0:1, :]
    cw1 = cw_ref[1:2, :]
    cw2 = cw_ref[2:3, :]
    cb = cb_ref[...]
    nrm = nrm_ref[...]
    row = lax.broadcasted_iota(jnp.int32, (TM_P, wgb.shape[0]), 0)
    zero_row = jnp.zeros((1, wgb.shape[0]), F32)
    um2, um1 = zero_row, zero_row

    for t in range(2):
        xn = xn_ref[t]
        u = _bdot_t(xn, wgc) * _bdot_t(xn, wxc)
        gb = _bdot_t(xn, wgb)

        up = u[:TM_P]
        u1 = jnp.where(row < 1, um1, pltpu.roll(up, 1, 0))
        u2 = jnp.where(row < 1, um2, jnp.where(row < 2, um1, pltpu.roll(up, 2, 0)))
        yc = cw0 * u2 + cw1 * u1 + cw2 * up + cb
        yb_ref[0, t * TM_P:(t + 1) * TM_P, :] = _group_norm(gb[:TM_P] * yc, nrm).astype(yb_ref.dtype)
        um2, um1 = up[TM_P - 2:TM_P - 1, :], up[TM_P - 1:TM_P, :]

        ss = slice(t * TM_S, (t + 1) * TM_S)
        us = u[TM_P:]
        b0 = buf_ref[0, ss, 0, :]
        b1 = buf_ref[0, ss, 1, :]
        ycs = cw0 * b0 + cw1 * b1 + cw2 * us + cb
        yb_ref[0, MP + t * TM_S:MP + (t + 1) * TM_S, :] = _group_norm(gb[TM_P:] * ycs, nrm).astype(yb_ref.dtype)
        cs_ref[0, ss, 0, :] = b1
        cs_ref[0, ss, 1, :] = us

    cp_ref[0, 0, 0:1, :] = um2
    cp_ref[0, 0, 1:2, :] = um1


def _conv(xn_c, w_t, cw, cb_row, nrm_row, buf, za_s, c0):
    tc = CONV_TC
    n_c = D_B // tc
    assert N_MIX_BLOCKS * n_c * SAMPLE_BS == za_s.shape[0], "one sample group per grid step"
    wspec = lambda base: pl.BlockSpec(
        (pl.Element(tc), pl.Element(D_MODEL)), lambda i, c: (pl.multiple_of(base + c * tc, 8), 0))
    grp = lambda i, c: i * n_c + c
    return pl.pallas_call(
        _conv_kernel,
        grid=(N_MIX_BLOCKS, n_c),
        in_specs=[
            pl.BlockSpec((2, TM, D_MODEL), lambda i, c: (i, 0, 0)),
            wspec(CONV_ROW0),
            wspec(CONV_ROW0 + D_B),
            wspec(CONV_ROW0 + 2 * D_B),
            pl.BlockSpec((3, tc), lambda i, c: (0, c)),
            pl.BlockSpec((1, tc), lambda i, c: (0, c)),
            pl.BlockSpec((1, tc), lambda i, c: (0, c)),
            pl.BlockSpec((1, MS, 2, tc), lambda i, c: (i, 0, 0, c)),
            pl.BlockSpec((SAMPLE_BS, D_A), lambda i, c: (grp(i, c), 0)),
            pl.BlockSpec((1, SAMPLE_BS, H_A, DK, DV), lambda i, c: (0, grp(i, c), 0, 0, 0)),
        ],
        out_specs=[
            pl.BlockSpec((1, MP + MS, tc), lambda i, c: (i, 0, c)),
            pl.BlockSpec((1, 1, 2, tc), lambda i, c: (0, i, 0, c)),
            pl.BlockSpec((1, MS, 2, tc), lambda i, c: (i, 0, 0, c)),
            pl.BlockSpec((SAMPLE_BS, D_A), lambda i, c: (grp(i, c), 0)),
        ],
        out_shape=[
            jax.ShapeDtypeStruct((N_MIX_BLOCKS, MP + MS, D_B), BF16),
            jax.ShapeDtypeStruct((1, N_MIX_BLOCKS, 2, D_B), F32),
            jax.ShapeDtypeStruct((N_MIX_BLOCKS, MS, 2, D_B), F32),
            jax.ShapeDtypeStruct((za_s.shape[0], D_A), F32),
        ],
        compiler_params=pltpu.CompilerParams(
            dimension_semantics=("parallel", "arbitrary"), vmem_limit_bytes=VMEM_LIMIT),
        name="proj_conv",
    )(xn_c, w_t, w_t, w_t, cw, cb_row, nrm_row, buf, za_s, c0)


def _mlstm_prompt_kernel(q_ref, k_ref, v_ref, o_ref, g_ref, bias_ref, nrm_ref,
                         ha_ref, c_ref, n_ref, m_ref, cx_s, m_s, *, n_chunks):
    L = MLSTM_CHUNK
    c = pl.program_id(1)

    @pl.when(c == 0)
    def _():
        cx_s[...] = jnp.zeros_like(cx_s)
        m_s[...] = jnp.zeros_like(m_s)

    row = lax.broadcasted_iota(jnp.int32, (L, L), 0)
    col = lax.broadcasted_iota(jnp.int32, (L, L), 1)
    causal = row >= col
    tri = causal.astype(BF16)

    for b in range(MLSTM_SEQS):
        _mlstm_prompt_seq(b, causal, tri, q_ref, k_ref, v_ref, o_ref, g_ref, bias_ref, nrm_ref,
                          ha_ref, cx_s, m_s)

    @pl.when(c == n_chunks - 1)
    def _():
        for b in range(MLSTM_SEQS):
            m_ref[b] = m_s[b]
            for h in range(H_A):
                cx = cx_s[b * H_A + h]
                c_ref[0, b, h] = cx[:, :DV]
                n_ref[0, b, h:h + 1, :] = cx[:, DV:].T[h:h + 1, :]


def _prefix_max_rows(x):
    rows = x.shape[0]
    row = lax.broadcasted_iota(jnp.int32, x.shape, 0)
    k = 1
    while k < rows:
        x = jnp.where(row >= k, jnp.maximum(x, pltpu.roll(x, k, 0)), x)
        k *= 2
    return x


def _mlstm_prompt_seq(b, causal, tri, q_ref, k_ref, v_ref, o_ref, g_ref, bias_ref, nrm_ref,
                      ha_ref, cx_s, m_s):
    L = MLSTM_CHUNK
    lane = lax.broadcasted_iota(jnp.int32, (L, LANE), 1)
    lane_dv = lax.broadcasted_iota(jnp.int32, (DV, LANE), 1)
    capped = _soft_cap(g_ref[b] + bias_ref[...])
    lf = _log_sigmoid(capped)
    hi = lf.astype(BF16)
    r1 = lf - hi.astype(F32)
    mid = r1.astype(BF16)
    lo = (r1 - mid.astype(F32)).astype(BF16)
    parts = _bdot(tri, jnp.concatenate([hi, mid, lo], axis=1))
    bc = parts[:, :LANE] + parts[:, LANE:2 * LANE] + parts[:, 2 * LANE:]
    bh = pltpu.roll(bc, LANE - H_A, 1)
    a = capped - bh
    a_t = a.T
    m_prev = m_s[b]
    big_m = jnp.maximum(_prefix_max_rows(a), m_prev[0:1, :])
    sc_all = jnp.exp(m_prev[0:1, :] - big_m)
    emt_all = jnp.exp(-(bh + big_m))
    m_last = big_m[L - 1:L, :]
    w_all = jnp.exp(a - m_last)
    decay_all = sc_all[L - 1:L, :]
    m_s[b] = jnp.broadcast_to(bh[L - 1:L, :] + m_last, (8, LANE))

    nums = []
    nd_tail = jnp.zeros((L, LANE), F32)
    sq_tail = jnp.zeros((L, LANE), F32)
    for h in range(H_A):
        hs = slice(h * DK, (h + 1) * DK)
        onehot = (lane == h).astype(BF16)
        qb = q_ref[b, :, hs].astype(BF16)
        kf = k_ref[b, :, hs] * (DK ** -0.5)
        kb = kf.astype(BF16)
        vx = jnp.concatenate([v_ref[b, :, hs].astype(BF16), onehot], axis=1)
        cx_prev = cx_s[b * H_A + h]

        p = jnp.exp(jnp.where(causal, a_t[h:h + 1, :] - big_m[:, h:h + 1], -jnp.inf))
        s = _bdot_t(qb, kb) * p
        nd = sc_all[:, h:h + 1] * _bdot(qb, cx_prev.astype(BF16)) + _bdot(s.astype(BF16), vx)
        num = nd[:, :DV]
        nums.append(num)
        nd_tail = nd_tail + nd[:, DV:]
        sq_tail = sq_tail + _bdot((num * num).astype(BF16), (lane_dv == h).astype(BF16))

        wk = (w_all[:, h:h + 1] * kf).astype(BF16)
        cx_s[b * H_A + h] = decay_all[:, h:h + 1] * cx_prev + lax.dot_general(
            wk, vx, (((0,), (0,)), ((), ())), preferred_element_type=F32)

    inv = 1.0 / jnp.maximum(jnp.abs(nd_tail), emt_all)
    ms = inv * inv * sq_tail * (1.0 / DV)
    scale_all = inv * lax.rsqrt(ms + EPS)
    for h in range(H_A):
        hs = slice(h * DK, (h + 1) * DK)
        hn = nums[h] * scale_all[:, h:h + 1]
        ha_ref[b, :, hs] = (hn * nrm_ref[:, hs] * jax.nn.sigmoid(o_ref[b, :, hs])).astype(ha_ref.dtype)


def _mlstm_prompt(za_p, g_p, bias_row, nrm_row, batch, seq):
    L = MLSTM_CHUNK
    nsq = MLSTM_SEQS
    n_chunks = seq // L
    kern = functools.partial(_mlstm_prompt_kernel, n_chunks=n_chunks)
    za3 = za_p.reshape(batch, seq, 4 * D_A)
    g3 = g_p.reshape(batch, seq, LANE)
    zspec = lambda part: pl.BlockSpec((nsq, L, D_A), lambda b, c: (b, c, part))
    return pl.pallas_call(
        kern,
        grid=(batch // nsq, n_chunks),
        in_specs=[
            zspec(0), zspec(1), zspec(2), zspec(3),
            pl.BlockSpec((nsq, L, LANE), lambda b, c: (b, c, 0)),
            pl.BlockSpec((1, LANE), lambda b, c: (0, 0)),
            pl.BlockSpec((1, D_A), lambda b, c: (0, 0)),
        ],
        out_specs=[
            pl.BlockSpec((nsq, L, D_A), lambda b, c: (b, c, 0)),
            pl.BlockSpec((1, nsq, H_A, DK, DV), lambda b, c: (0, b, 0, 0, 0)),
            pl.BlockSpec((1, nsq, H_A, DK), lambda b, c: (0, b, 0, 0)),
            pl.BlockSpec((nsq, 8, LANE), lambda b, c: (b, 0, 0)),
        ],
        out_shape=[
            jax.ShapeDtypeStruct((batch, seq, D_A), BF16),
            jax.ShapeDtypeStruct((1, batch, H_A, DK, DV), F32),
            jax.ShapeDtypeStruct((1, batch, H_A, DK), F32),
            jax.ShapeDtypeStruct((batch, 8, LANE), F32),
        ],
        scratch_shapes=[pltpu.VMEM((nsq * H_A, DK, DV + LANE), F32),
                        pltpu.VMEM((nsq, 8, LANE), F32)],
        compiler_params=pltpu.CompilerParams(
            dimension_semantics=("parallel", "arbitrary"), vmem_limit_bytes=VMEM_LIMIT),
        name="mlstm_prompt",
    )(za3, za3, za3, za3, g3, bias_row, nrm_row)


def _expand_heads(x, width):
    rows = x.shape[0]
    return jnp.concatenate([jnp.broadcast_to(x[:, h:h + 1], (rows, width)) for h in range(H_A)], axis=1)


def _head_sums(x):
    return jnp.concatenate(
        [jnp.sum(x[:, h * DK:(h + 1) * DK], axis=-1, keepdims=True) for h in range(H_A)], axis=1)


def _mlstm_sample_kernel(za_ref, qc_ref, g_ref, bias_ref, nrm_ref, m_ref, n_ref,
                         ha_ref, m_out, n_out, wk_out, dec_out):
    nb = za_ref.shape[0]
    capped = _soft_cap(g_ref[...] + bias_ref[...])
    logi = capped[:, 0:H_A]
    logf = _log_sigmoid(capped)[:, H_A:2 * H_A]
    m_prev = m_ref[...]
    m_inter = logf + m_prev
    m_t = jnp.maximum(m_inter, logi)
    sc = jnp.exp(m_inter - m_t)
    ei = jnp.exp(logi - m_t)
    emt = jnp.exp(-m_t)

    q = za_ref[:, 0:D_A]
    k = za_ref[:, D_A:2 * D_A] * (DK ** -0.5)
    v = za_ref[:, 2 * D_A:3 * D_A]
    o = za_ref[:, 3 * D_A:4 * D_A]
    n_prev = n_ref[...]

    s = _head_sums(q * k) * ei
    den = sc * _head_sums(q * n_prev) + s
    denom = jnp.maximum(jnp.abs(den), emt)
    num = _expand_heads(sc, DV) * qc_ref[...] + _expand_heads(s, DV) * v
    hh = num / _expand_heads(denom, DV)
    ms = _head_sums(hh * hh) * (1.0 / DV)
    hn = hh * lax.rsqrt(_expand_heads(ms, DV) + EPS)
    ha_ref[...] = (hn * nrm_ref[...] * jax.nn.sigmoid(o)).astype(ha_ref.dtype)
    n_out[...] = _expand_heads(sc, DK) * n_prev + _expand_heads(ei, DK) * k
    m_out[...] = m_t
    pad = jnp.zeros((nb, LANE - H_A), F32)
    dec_out[...] = jnp.concatenate([sc, pad], axis=1)

    k_t = k.T
    ei_t = jnp.concatenate([ei, pad], axis=1).T
    for h in range(H_A):
        hs = slice(h * DK, (h + 1) * DK)
        wk_out[hs, :] = k_t[hs, :] * ei_t[h:h + 1, :]


def _mlstm_sample(za_s, qc, g_s, bias_row, nrm_row, m0, n0):
    nb = za_s.shape[0]
    assert nb == LANE, "the sequence axis is transposed onto the lanes"
    return pl.pallas_call(
        _mlstm_sample_kernel,
        out_shape=[
            jax.ShapeDtypeStruct((nb, D_A), BF16),
            jax.ShapeDtypeStruct((nb, H_A), F32),
            jax.ShapeDtypeStruct((nb, H_A * DK), F32),
            jax.ShapeDtypeStruct((D_A, nb), F32),
            jax.ShapeDtypeStruct((nb, LANE), F32),
        ],
        compiler_params=pltpu.CompilerParams(vmem_limit_bytes=VMEM_LIMIT),
        name="mlstm_sample",
    )(za_s, qc, g_s, bias_row, nrm_row, m0, n0)


def _outproj_kernel(hp_ref, hs_ref, ap_ref, as_ref, yb_ref, w_ref, op_ref, os_ref, wc_ref):
    j = pl.program_id(1)

    @pl.when(pl.program_id(0) == 0)
    def _():
        wc_ref[j] = w_ref[...].astype(BF16)

    wa = wc_ref[j, :D_A, :]
    wb = wc_ref[j, D_A:, :]
    op_ref[...] = hp_ref[...] + _bdot(ap_ref[...], wa) + _bdot(yb_ref[0, :MP, :], wb)
    os_ref[0] = hs_ref[0] + _bdot(as_ref[0], wa) + _bdot(yb_ref[0, MP:, :], wb)


def _outproj(hp, hs, ap, a_s, yb_c, w3):
    tn = OUT_TN
    n_n = D_MODEL // tn
    return pl.pallas_call(
        _outproj_kernel,
        grid=(N_MIX_BLOCKS, n_n),
        in_specs=[
            pl.BlockSpec((MP, tn), lambda i, j: (i, j)),
            pl.BlockSpec((1, MS, tn), lambda i, j: (i, 0, j)),
            pl.BlockSpec((MP, D_A), lambda i, j: (i, 0)),
            pl.BlockSpec((1, MS, D_A), lambda i, j: (i, 0, 0)),
            pl.BlockSpec((1, MP + MS, D_B), lambda i, j: (i, 0, 0)),
            pl.BlockSpec((None, D_A + D_B, tn), lambda i, j: (0, 0, jnp.where(i == 0, j, n_n - 1))),
        ],
        out_specs=[
            pl.BlockSpec((MP, tn), lambda i, j: (i, j)),
            pl.BlockSpec((1, MS, tn), lambda i, j: (i, 0, j)),
        ],
        out_shape=[
            jax.ShapeDtypeStruct((N_MIX_BLOCKS * MP, D_MODEL), F32),
            jax.ShapeDtypeStruct((N_MIX_BLOCKS, MS, D_MODEL), F32),
        ],
        scratch_shapes=[pltpu.VMEM((n_n, D_A + D_B, tn), BF16)],
        compiler_params=pltpu.CompilerParams(
            dimension_semantics=("arbitrary", "arbitrary"), vmem_limit_bytes=VMEM_LIMIT),
        name="outproj",
    )(hp, hs, ap, a_s, yb_c, w3)


def kernel(x_prompt, x_sample, state_mlstm_C, state_mlstm_n, state_mlstm_m, state_conv, norm_ffn1, ffn1_gate, ffn1_up, ffn1_down, norm_mix, w_in, b_gates, conv_w, conv_b, norm_mlstm, norm_conv, w_out, norm_ffn2, ffn2_gate, ffn2_up, ffn2_down, norm_final):
    batch, seq, _ = x_prompt.shape
    nb = x_sample.shape[0]
    assert batch == N_MIX_BLOCKS and seq == MP and nb == N_MIX_BLOCKS * MS
    assert norm_ffn1.shape[0] == 1, "single-layer trunk"

    xp = x_prompt.reshape(batch * seq, D_MODEL)
    xs = x_sample.reshape(N_FFN_BLOCKS, TM_S, D_MODEL)

    h1p, h1s, xn_c = _ffn(xp, xs, norm_ffn1, ffn1_gate, ffn1_up, ffn1_down, norm_mix)

    w_t = jnp.swapaxes(w_in[0], 0, 1)
    za_p, za_s, g_p, g_s = _proj(xn_c, w_t)
    za_s = za_s.reshape(nb, 4 * D_A)
    yb_c, conv_p, conv_s, qc_s = _conv(xn_c, w_t, conv_w[0], conv_b, norm_conv,
                                       state_conv.reshape(N_MIX_BLOCKS, MS, 2, D_B), za_s, state_mlstm_C)

    bias_row = jnp.zeros((1, LANE), F32).at[0, :2 * H_A].set(b_gates[0].astype(F32))
    ha_p, c_p, n_p, m_p = _mlstm_prompt(za_p, g_p, bias_row, norm_mlstm, batch, seq)
    ha_s, m_s, n_s, wk_s, dec_s = _mlstm_sample(
        za_s, qc_s, g_s.reshape(nb, LANE), bias_row, norm_mlstm,
        state_mlstm_m.reshape(nb, H_A), state_mlstm_n.reshape(nb, H_A * DK))

    h2p, h2s = _outproj(h1p, h1s.reshape(N_MIX_BLOCKS, MS, D_MODEL), ha_p.reshape(batch * seq, D_A),
                        ha_s.reshape(N_MIX_BLOCKS, MS, D_A), yb_c, w_out)

    yp, ys, c_s = _ffn(h2p, h2s.reshape(N_FFN_BLOCKS, TM_S, D_MODEL), norm_ffn2, ffn2_gate, ffn2_up,
                       ffn2_down, norm_final.reshape(1, D_MODEL),
                       state=(state_mlstm_C, wk_s, za_s, dec_s))

    return (
        yp.reshape(batch, seq, D_MODEL),
        ys.reshape(nb, 1, D_MODEL),
        c_p,
        n_p,
        m_p[:, 0, :H_A].reshape(1, batch, H_A),
        conv_p,
        c_s,
        n_s.reshape(1, nb, H_A, DK),
        m_s.reshape(1, nb, H_A),
        conv_s.reshape(1, nb, 2, D_B),
    )
```

```python
import functools

import jax
import jax.numpy as jnp
from jax import lax
from jax.experimental import pallas as pl
from jax.experimental.pallas import tpu as pltpu

F32 = jnp.float32
BF16 = jnp.bfloat16

D_MODEL = 2048
D_A = 1024
D_B = 1024
H_A = 4
DK = 256
DV = 256
G_B = 8
D_FF = 5504
GATE_CAP = 15.0
EPS = 1e-6
GATE_ROW0 = 4 * D_A
CONV_ROW0 = 4 * D_A + 2 * H_A

LANE = 128
N_FFN_BLOCKS = 8
TM_P = 1024
TM_S = 16
TM = TM_P + TM_S
TF = 256
X_CHUNK = 128
N_X_CHUNKS = TM_P // X_CHUNK
X_AHEAD0 = 2
N_MIX_BLOCKS = N_FFN_BLOCKS // 2
MP = 2 * TM_P
MS = 2 * TM_S
PROJ_TN = 1024
CONV_TC = 256
OUT_TN = 512
MLSTM_CHUNK = 256
MLSTM_SEQS = 2
SAMPLE_BS = 8
VMEM_LIMIT = 62 * 1024 * 1024


def _rms(x, g):
    return x * lax.rsqrt(jnp.mean(x * x, axis=-1, keepdims=True) + EPS) * g


def _soft_cap(x):
    return GATE_CAP * jnp.tanh(x / GATE_CAP)


def _log_sigmoid(x):
    return -jax.nn.softplus(-x)


def _bdot(a, b):
    return jnp.dot(a, b, preferred_element_type=F32)


def _bdot_t(a, b):
    return lax.dot_general(a, b, (((1,), (1,)), ((), ())), preferred_element_type=F32)


def _sample_state_update(f, c_ref, wk_ref, v_ref, dec_ref, c_out):
    r = jnp.minimum(f, TM_S - 1)
    seq = pl.program_id(0) * TM_S + r
    lane = lax.broadcasted_iota(jnp.int32, (D_A, LANE), 1)
    wk_col = jnp.sum(jnp.where(lane == seq, wk_ref[...], 0.0), axis=1, keepdims=True)
    v_row = v_ref[pl.ds(r, 1), :]
    dec = dec_ref[pl.ds(r, 1), :]
    for h in range(H_A):
        hs = slice(h * DK, (h + 1) * DK)
        c_out[0, 0, h] = dec[:, h:h + 1] * c_ref[0, 0, h] + wk_col[hs, :] * v_row[:, hs]


def _x_chunk_copy(x_hbm, xp_ref, sem, blk, c):
    dst0 = pl.multiple_of(c * X_CHUNK, X_CHUNK)
    src0 = pl.multiple_of(blk * TM_P + c * X_CHUNK, X_CHUNK)
    return pltpu.make_async_copy(
        x_hbm.at[pl.ds(src0, X_CHUNK), :], xp_ref.at[pl.ds(dst0, X_CHUNK), :], sem.at[c])


def _ffn_kernel(x_hbm, xs_ref, g_ref, wg_ref, wu_ref, wd_ref, g2_ref, *rest,
                n_f, last_valid, final_norm):
    if final_norm:
        c_ref, wk_ref, v_ref, dec_ref, op_ref, os_ref, c_out, xn_ref, xp_ref, x_sem = rest
    else:
        op_ref, os_ref, nxt_ref, xn_ref, xp_ref, x_sem = rest
    i = pl.program_id(0)
    f = pl.program_id(1)

    @pl.when(jnp.logical_and(i == 0, f == 0))
    def _():
        for c in range(N_X_CHUNKS):
            _x_chunk_copy(x_hbm, xp_ref, x_sem, 0, c).start()

    @pl.when(f == 0)
    def _():
        for c in range(N_X_CHUNKS):
            _x_chunk_copy(x_hbm, xp_ref, x_sem, i, c).wait()

    @pl.when(jnp.logical_and(jnp.logical_and(f >= X_AHEAD0, f < X_AHEAD0 + N_X_CHUNKS),
                             i < N_FFN_BLOCKS - 1))
    def _():
        _x_chunk_copy(x_hbm, xp_ref, x_sem, i + 1, f - X_AHEAD0).start()

    def step(valid, first, with_state):
        if first:
            g = g_ref[...]
            xn_ref[:TM_P, :] = _rms(xp_ref[...], g).astype(BF16)
            xn_ref[TM_P:, :] = _rms(xs_ref[0], g).astype(BF16)
        xn = xn_ref[...]
        a = _bdot(xn, wg_ref[:, :valid].astype(BF16))
        b = _bdot(xn, wu_ref[:, :valid].astype(BF16))
        hid = (a * jax.nn.sigmoid(a) * b * 0.5).astype(BF16)
        r = _bdot(hid, wd_ref[:valid, :].astype(BF16))
        if first:
            op_ref[...] = xp_ref[...] + r[:TM_P]
            os_ref[0] = xs_ref[0] + r[TM_P:]
        else:
            op_ref[...] += r[:TM_P]
            os_ref[0] += r[TM_P:]
        if with_state:
            _sample_state_update(f, c_ref, wk_ref, v_ref, dec_ref, c_out)

    assert last_valid < TF and n_f - 1 >= TM_S, "state update rides the full-width steps"

    @pl.when(f == 0)
    def _():
        step(TF, True, final_norm)

    @pl.when(jnp.logical_and(f > 0, f < n_f - 1))
    def _():
        step(TF, False, final_norm)

    @pl.when(f == n_f - 1)
    def _():
        step(last_valid, False, False)
        g2 = g2_ref[...]
        if final_norm:
            op_ref[...] = _rms(op_ref[...], g2)
            os_ref[0] = _rms(os_ref[0], g2)
        else:
            nxt_ref[0, :TM_P, :] = _rms(op_ref[...], g2).astype(BF16)
            nxt_ref[0, TM_P:, :] = _rms(os_ref[0], g2).astype(BF16)


def _ffn(xp, xs, g, wg, wu, wd, g2, state=None):
    final_norm = state is not None
    n_f = pl.cdiv(D_FF, TF)
    last_valid = D_FF - (n_f - 1) * TF
    kern = functools.partial(_ffn_kernel, n_f=n_f, last_valid=last_valid, final_norm=final_norm)
    assert X_AHEAD0 + N_X_CHUNKS <= n_f and N_X_CHUNKS * X_CHUNK == TM_P
    in_specs = [
        pl.BlockSpec(memory_space=pl.ANY),
        pl.BlockSpec((1, TM_S, D_MODEL), lambda i, f: (i, 0, 0)),
        pl.BlockSpec((1, D_MODEL), lambda i, f: (0, 0)),
        pl.BlockSpec((None, D_MODEL, TF), lambda i, f: (0, 0, f)),
        pl.BlockSpec((None, D_MODEL, TF), lambda i, f: (0, 0, f)),
        pl.BlockSpec((None, TF, D_MODEL), lambda i, f: (0, f, 0)),
        pl.BlockSpec((1, D_MODEL), lambda i, f: (0, 0)),
    ]
    out_specs = [
        pl.BlockSpec((TM_P, D_MODEL), lambda i, f: (i, 0)),
        pl.BlockSpec((1, TM_S, D_MODEL), lambda i, f: (i, 0, 0)),
    ]
    out_shape = [
        jax.ShapeDtypeStruct((N_FFN_BLOCKS * TM_P, D_MODEL), F32),
        jax.ShapeDtypeStruct((N_FFN_BLOCKS, TM_S, D_MODEL), F32),
    ]
    args = [xp, xs, g, wg, wu, wd, g2]
    if final_norm:
        c0, wk, za_s, dec = state
        seq = lambda i, f: i * TM_S + jnp.minimum(f, TM_S - 1)
        cspec = pl.BlockSpec((1, 1, H_A, DK, DV), lambda i, f: (0, seq(i, f), 0, 0, 0))
        in_specs += [
            cspec,
            pl.BlockSpec((D_A, LANE), lambda i, f: (0, 0)),
            pl.BlockSpec((TM_S, D_A), lambda i, f: (i, 2)),
            pl.BlockSpec((TM_S, LANE), lambda i, f: (i, 0)),
        ]
        out_specs.append(cspec)
        out_shape.append(jax.ShapeDtypeStruct(c0.shape, F32))
        args += [c0, wk, za_s, dec]
    else:
        out_specs.append(pl.BlockSpec((1, TM, D_MODEL), lambda i, f: (i, 0, 0)))
        out_shape.append(jax.ShapeDtypeStruct((N_FFN_BLOCKS, TM, D_MODEL), BF16))
    return pl.pallas_call(
        kern,
        grid=(N_FFN_BLOCKS, n_f),
        in_specs=in_specs,
        out_specs=out_specs,
        out_shape=out_shape,
        scratch_shapes=[pltpu.VMEM((TM, D_MODEL), BF16),
                        pltpu.VMEM((TM_P, D_MODEL), F32),
                        pltpu.SemaphoreType.DMA((N_X_CHUNKS,))],
        compiler_params=pltpu.CompilerParams(
            dimension_semantics=("arbitrary", "arbitrary"), vmem_limit_bytes=VMEM_LIMIT),
        name="ffn_final" if final_norm else "ffn",
    )(*args)


def _split_rows(z0, z1, p_ref, s_ref):
    p_ref[:TM_P, :] = z0[:TM_P]
    p_ref[TM_P:, :] = z1[:TM_P]
    s_ref[0, :TM_S, :] = z0[TM_P:]
    s_ref[0, TM_S:, :] = z1[TM_P:]


def _proj_kernel(xn_ref, w_ref, wg_ref, zp_ref, zs_ref, gp_ref, gs_ref):
    w = w_ref[...].astype(BF16)
    _split_rows(_bdot_t(xn_ref[0], w), _bdot_t(xn_ref[1], w), zp_ref, zs_ref)

    @pl.when(pl.program_id(1) == 0)
    def _():
        wg = wg_ref[...].astype(BF16)
        _split_rows(_bdot_t(xn_ref[0], wg), _bdot_t(xn_ref[1], wg), gp_ref, gs_ref)


def _proj(xn_c, w_t):
    n_n = (4 * D_A) // PROJ_TN
    return pl.pallas_call(
        _proj_kernel,
        grid=(N_MIX_BLOCKS, n_n),
        in_specs=[
            pl.BlockSpec((2, TM, D_MODEL), lambda i, j: (i, 0, 0)),
            pl.BlockSpec((PROJ_TN, D_MODEL), lambda i, j: (j, 0)),
            pl.BlockSpec((LANE, D_MODEL), lambda i, j: (GATE_ROW0 // LANE, 0)),
        ],
        out_specs=[
            pl.BlockSpec((MP, PROJ_TN), lambda i, j: (i, j)),
            pl.BlockSpec((1, MS, PROJ_TN), lambda i, j: (i, 0, j)),
            pl.BlockSpec((MP, LANE), lambda i, j: (i, 0)),
            pl.BlockSpec((1, MS, LANE), lambda i, j: (i, 0, 0)),
        ],
        out_shape=[
            jax.ShapeDtypeStruct((N_MIX_BLOCKS * MP, 4 * D_A), F32),
            jax.ShapeDtypeStruct((N_MIX_BLOCKS, MS, 4 * D_A), F32),
            jax.ShapeDtypeStruct((N_MIX_BLOCKS * MP, LANE), F32),
            jax.ShapeDtypeStruct((N_MIX_BLOCKS, MS, LANE), F32),
        ],
        compiler_params=pltpu.CompilerParams(
            dimension_semantics=("parallel", "arbitrary"), vmem_limit_bytes=VMEM_LIMIT),
        name="proj_qkvo",
    )(xn_c, w_t, w_t)


def _group_norm(yb, nrm):
    gw = D_B // G_B
    parts = []
    for g in range(yb.shape[1] // gw):
        seg = yb[:, g * gw:(g + 1) * gw]
        parts.append(seg * lax.rsqrt(jnp.mean(seg * seg, axis=-1, keepdims=True) + EPS))
    return jnp.concatenate(parts, axis=1) * nrm


def _conv_kernel(xn_ref, wgb_ref, wgc_ref, wxc_ref, cw_ref, cb_ref, nrm_ref, buf_ref, q_ref, c_ref,
                 yb_ref, cp_ref, cs_ref, qc_ref):
    q_t = jnp.concatenate([q_ref[...], jnp.zeros((LANE - SAMPLE_BS, D_A), F32)], axis=0).T
    for h in range(H_A):
        hs = slice(h * DK, (h + 1) * DK)
        for j in range(SAMPLE_BS):
            qc_ref[j:j + 1, hs] = jnp.sum(q_t[hs, j:j + 1] * c_ref[0, j, h], axis=0, keepdims=True)

    wgb = wgb_ref[...].astype(BF16)
    wgc = wgc_ref[...].astype(BF16)
    wxc = wxc_ref[...].astype(BF16)
    cw0 = cw_ref[0:1, :]
    cw1 = cw_ref[1:2, :]
    cw2 = cw_ref[2:3, :]
    cb = cb_ref[...]
    nrm = nrm_ref[...]
    row = lax.broadcasted_iota(jnp.int32, (TM_P, wgb.shape[0]), 0)
    zero_row = jnp.zeros((1, wgb.shape[0]), F32)
    um2, um1 = zero_row, zero_row

    for t in range(2):
        xn = xn_ref[t]
        u = _bdot_t(xn, wgc) * _bdot_t(xn, wxc)
        gb = _bdot_t(xn, wgb)

        up = u[:TM_P]
        u1 = jnp.where(row < 1, um1, pltpu.roll(up, 1, 0))
        u2 = jnp.where(row < 1, um2, jnp.where(row < 2, um1, pltpu.roll(up, 2, 0)))
        yc = cw0 * u2 + cw1 * u1 + cw2 * up + cb
        yb_ref[0, t * TM_P:(t + 1) * TM_P, :] = _group_norm(gb[:TM_P] * yc, nrm).astype(yb_ref.dtype)
        um2, um1 = up[TM_P - 2:TM_P - 1, :], up[TM_P - 1:TM_P, :]

        ss = slice(t * TM_S, (t + 1) * TM_S)
        us = u[TM_P:]
        b0 = buf_ref[0, ss, 0, :]
        b1 = buf_ref[0, ss, 1, :]
        ycs = cw0 * b0 + cw1 * b1 + cw2 * us + cb
        yb_ref[0, MP + t * TM_S:MP + (t + 1) * TM_S, :] = _group_norm(gb[TM_P:] * ycs, nrm).astype(yb_ref.dtype)
        cs_ref[0, ss, 0, :] = b1
        cs_ref[0, ss, 1, :] = us

    cp_ref[0, 0, 0:1, :] = um2
    cp_ref[0, 0, 1:2, :] = um1


def _conv(xn_c, w_t, cw, cb_row, nrm_row, buf, za_s, c0):
    tc = CONV_TC
    n_c = D_B // tc
    assert N_MIX_BLOCKS * n_c * SAMPLE_BS == za_s.shape[0], "one sample group per grid step"
    wspec = lambda base: pl.BlockSpec(
        (pl.Element(tc), pl.Element(D_MODEL)), lambda i, c: (pl.multiple_of(base + c * tc, 8), 0))
    grp = lambda i, c: i * n_c + c
    return pl.pallas_call(
        _conv_kernel,
        grid=(N_MIX_BLOCKS, n_c),
        in_specs=[
            pl.BlockSpec((2, TM, D_MODEL), lambda i, c: (i, 0, 0)),
            wspec(CONV_ROW0),
            wspec(CONV_ROW0 + D_B),
            wspec(CONV_ROW0 + 2 * D_B),
            pl.BlockSpec((3, tc), lambda i, c: (0, c)),
            pl.BlockSpec((1, tc), lambda i, c: (0, c)),
            pl.BlockSpec((1, tc), lambda i, c: (0, c)),
            pl.BlockSpec((1, MS, 2, tc), lambda i, c: (i, 0, 0, c)),
            pl.BlockSpec((SAMPLE_BS, D_A), lambda i, c: (grp(i, c), 0)),
            pl.BlockSpec((1, SAMPLE_BS, H_A, DK, DV), lambda i, c: (0, grp(i, c), 0, 0, 0)),
        ],
        out_specs=[
            pl.BlockSpec((1, MP + MS, tc), lambda i, c: (i, 0, c)),
            pl.BlockSpec((1, 1, 2, tc), lambda i, c: (0, i, 0, c)),
            pl.BlockSpec((1, MS, 2, tc), lambda i, c: (i, 0, 0, c)),
            pl.BlockSpec((SAMPLE_BS, D_A), lambda i, c: (grp(i, c), 0)),
        ],
        out_shape=[
            jax.ShapeDtypeStruct((N_MIX_BLOCKS, MP + MS, D_B), BF16),
            jax.ShapeDtypeStruct((1, N_MIX_BLOCKS, 2, D_B), F32),
            jax.ShapeDtypeStruct((N_MIX_BLOCKS, MS, 2, D_B), F32),
            jax.ShapeDtypeStruct((za_s.shape[0], D_A), F32),
        ],
        compiler_params=pltpu.CompilerParams(
            dimension_semantics=("parallel", "arbitrary"), vmem_limit_bytes=VMEM_LIMIT),
        name="proj_conv",
    )(xn_c, w_t, w_t, w_t, cw, cb_row, nrm_row, buf, za_s, c0)


def _mlstm_prompt_kernel(q_ref, k_ref, v_ref, o_ref, g_ref, bias_ref, nrm_ref,
                         ha_ref, c_ref, n_ref, m_ref, cx_s, m_s, *, n_chunks):
    L = MLSTM_CHUNK
    c = pl.program_id(1)

    @pl.when(c == 0)
    def _():
        cx_s[...] = jnp.zeros_like(cx_s)
        m_s[...] = jnp.zeros_like(m_s)

    row = lax.broadcasted_iota(jnp.int32, (L, L), 0)
    col = lax.broadcasted_iota(jnp.int32, (L, L), 1)
    causal = row >= col
    tri = causal.astype(BF16)

    for b in range(MLSTM_SEQS):
        _mlstm_prompt_seq(b, causal, tri, q_ref, k_ref, v_ref, o_ref, g_ref, bias_ref, nrm_ref,
                          ha_ref, cx_s, m_s)

    @pl.when(c == n_chunks - 1)
    def _():
        for b in range(MLSTM_SEQS):
            m_ref[b] = m_s[b]
            for h in range(H_A):
                cx = cx_s[b * H_A + h]
                c_ref[0, b, h] = cx[:, :DV]
                n_ref[0, b, h:h + 1, :] = cx[:, DV:].T[h:h + 1, :]


def _prefix_max_rows(x):
    rows = x.shape[0]
    row = lax.broadcasted_iota(jnp.int32, x.shape, 0)
    k = 1
    while k < rows:
        x = jnp.where(row >= k, jnp.maximum(x, pltpu.roll(x, k, 0)), x)
        k *= 2
    return x


def _mlstm_prompt_seq(b, causal, tri, q_ref, k_ref, v_ref, o_ref, g_ref, bias_ref, nrm_ref,
                      ha_ref, cx_s, m_s):
    L = MLSTM_CHUNK
    lane = lax.broadcasted_iota(jnp.int32, (L, LANE), 1)
    lane_dv = lax.broadcasted_iota(jnp.int32, (DV, LANE), 1)
    capped = _soft_cap(g_ref[b] + bias_ref[...])
    lf = _log_sigmoid(capped)
    hi = lf.astype(BF16)
    r1 = lf - hi.astype(F32)
    mid = r1.astype(BF16)
    lo = (r1 - mid.astype(F32)).astype(BF16)
    parts = _bdot(tri, jnp.concatenate([hi, mid, lo], axis=1))
    bc = parts[:, :LANE] + parts[:, LANE:2 * LANE] + parts[:, 2 * LANE:]
    bh = pltpu.roll(bc, LANE - H_A, 1)
    a = capped - bh
    a_t = a.T
    m_prev = m_s[b]
    big_m = jnp.maximum(_prefix_max_rows(a), m_prev[0:1, :])
    sc_all = jnp.exp(m_prev[0:1, :] - big_m)
    emt_all = jnp.exp(-(bh + big_m))
    m_last = big_m[L - 1:L, :]
    w_all = jnp.exp(a - m_last)
    decay_all = sc_all[L - 1:L, :]
    m_s[b] = jnp.broadcast_to(bh[L - 1:L, :] + m_last, (8, LANE))

    nums = []
    nd_tail = jnp.zeros((L, LANE), F32)
    sq_tail = jnp.zeros((L, LANE), F32)
    for h in range(H_A):
        hs = slice(h * DK, (h + 1) * DK)
        onehot = (lane == h).astype(BF16)
        qb = q_ref[b, :, hs].astype(BF16)
        kf = k_ref[b, :, hs] * (DK ** -0.5)
        kb = kf.astype(BF16)
        vx = jnp.concatenate([v_ref[b, :, hs].astype(BF16), onehot], axis=1)
        cx_prev = cx_s[b * H_A + h]

        p = jnp.exp(jnp.where(causal, a_t[h:h + 1, :] - big_m[:, h:h + 1], -jnp.inf))
        s = _bdot_t(qb, kb) * p
        nd = sc_all[:, h:h + 1] * _bdot(qb, cx_prev.astype(BF16)) + _bdot(s.astype(BF16), vx)
        num = nd[:, :DV]
        nums.append(num)
        nd_tail = nd_tail + nd[:, DV:]
        sq_tail = sq_tail + _bdot((num * num).astype(BF16), (lane_dv == h).astype(BF16))

        wk = (w_all[:, h:h + 1] * kf).astype(BF16)
        cx_s[b * H_A + h] = decay_all[:, h:h + 1] * cx_prev + lax.dot_general(
            wk, vx, (((0,), (0,)), ((), ())), preferred_element_type=F32)

    inv = 1.0 / jnp.maximum(jnp.abs(nd_tail), emt_all)
    ms = inv * inv * sq_tail * (1.0 / DV)
    scale_all = inv * lax.rsqrt(ms + EPS)
    for h in range(H_A):
        hs = slice(h * DK, (h + 1) * DK)
        hn = nums[h] * scale_all[:, h:h + 1]
        ha_ref[b, :, hs] = (hn * nrm_ref[:, hs] * jax.nn.sigmoid(o_ref[b, :, hs])).astype(ha_ref.dtype)


def _mlstm_prompt(za_p, g_p, bias_row, nrm_row, batch, seq):
    L = MLSTM_CHUNK
    nsq = MLSTM_SEQS
    n_chunks = seq // L
    kern = functools.partial(_mlstm_prompt_kernel, n_chunks=n_chunks)
    za3 = za_p.reshape(batch, seq, 4 * D_A)
    g3 = g_p.reshape(batch, seq, LANE)
    zspec = lambda part: pl.BlockSpec((nsq, L, D_A), lambda b, c: (b, c, part))
    return pl.pallas_call(
        kern,
        grid=(batch // nsq, n_chunks),
        in_specs=[
            zspec(0), zspec(1), zspec(2), zspec(3),
            pl.BlockSpec((nsq, L, LANE), lambda b, c: (b, c, 0)),
            pl.BlockSpec((1, LANE), lambda b, c: (0, 0)),
            pl.BlockSpec((1, D_A), lambda b, c: (0, 0)),
        ],
        out_specs=[
            pl.BlockSpec((nsq, L, D_A), lambda b, c: (b, c, 0)),
            pl.BlockSpec((1, nsq, H_A, DK, DV), lambda b, c: (0, b, 0, 0, 0)),
            pl.BlockSpec((1, nsq, H_A, DK), lambda b, c: (0, b, 0, 0)),
            pl.BlockSpec((nsq, 8, LANE), lambda b, c: (b, 0, 0)),
        ],
        out_shape=[
            jax.ShapeDtypeStruct((batch, seq, D_A), BF16),
            jax.ShapeDtypeStruct((1, batch, H_A, DK, DV), F32),
            jax.ShapeDtypeStruct((1, batch, H_A, DK), F32),
            jax.ShapeDtypeStruct((batch, 8, LANE), F32),
        ],
        scratch_shapes=[pltpu.VMEM((nsq * H_A, DK, DV + LANE), F32),
                        pltpu.VMEM((nsq, 8, LANE), F32)],
        compiler_params=pltpu.CompilerParams(
            dimension_semantics=("parallel", "arbitrary"), vmem_limit_bytes=VMEM_LIMIT),
        name="mlstm_prompt",
    )(za3, za3, za3, za3, g3, bias_row, nrm_row)


def _expand_heads(x, width):
    rows = x.shape[0]
    return jnp.concatenate([jnp.broadcast_to(x[:, h:h + 1], (rows, width)) for h in range(H_A)], axis=1)


def _head_sums(x):
    return jnp.concatenate(
        [jnp.sum(x[:, h * DK:(h + 1) * DK], axis=-1, keepdims=True) for h in range(H_A)], axis=1)


def _mlstm_sample_kernel(za_ref, qc_ref, g_ref, bias_ref, nrm_ref, m_ref, n_ref,
                         ha_ref, m_out, n_out, wk_out, dec_out):
    nb = za_ref.shape[0]
    capped = _soft_cap(g_ref[...] + bias_ref[...])
    logi = capped[:, 0:H_A]
    logf = _log_sigmoid(capped)[:, H_A:2 * H_A]
    m_prev = m_ref[...]
    m_inter = logf + m_prev
    m_t = jnp.maximum(m_inter, logi)
    sc = jnp.exp(m_inter - m_t)
    ei = jnp.exp(logi - m_t)
    emt = jnp.exp(-m_t)

    q = za_ref[:, 0:D_A]
    k = za_ref[:, D_A:2 * D_A] * (DK ** -0.5)
    v = za_ref[:, 2 * D_A:3 * D_A]
    o = za_ref[:, 3 * D_A:4 * D_A]
    n_prev = n_ref[...]

    s = _head_sums(q * k) * ei
    den = sc * _head_sums(q * n_prev) + s
    denom = jnp.maximum(jnp.abs(den), emt)
    num = _expand_heads(sc, DV) * qc_ref[...] + _expand_heads(s, DV) * v
    hh = num / _expand_heads(denom, DV)
    ms = _head_sums(hh * hh) * (1.0 / DV)
    hn = hh * lax.rsqrt(_expand_heads(ms, DV) + EPS)
    ha_ref[...] = (hn * nrm_ref[...] * jax.nn.sigmoid(o)).astype(ha_ref.dtype)
    n_out[...] = _expand_heads(sc, DK) * n_prev + _expand_heads(ei, DK) * k
    m_out[...] = m_t
    pad = jnp.zeros((nb, LANE - H_A), F32)
    dec_out[...] = jnp.concatenate([sc, pad], axis=1)

    k_t = k.T
    ei_t = jnp.concatenate([ei, pad], axis=1).T
    for h in range(H_A):
        hs = slice(h * DK, (h + 1) * DK)
        wk_out[hs, :] = k_t[hs, :] * ei_t[h:h + 1, :]


def _mlstm_sample(za_s, qc, g_s, bias_row, nrm_row, m0, n0):
    nb = za_s.shape[0]
    assert nb == LANE, "the sequence axis is transposed onto the lanes"
    return pl.pallas_call(
        _mlstm_sample_kernel,
        out_shape=[
            jax.ShapeDtypeStruct((nb, D_A), BF16),
            jax.ShapeDtypeStruct((nb, H_A), F32),
            jax.ShapeDtypeStruct((nb, H_A * DK), F32),
            jax.ShapeDtypeStruct((D_A, nb), F32),
            jax.ShapeDtypeStruct((nb, LANE), F32),
        ],
        compiler_params=pltpu.CompilerParams(vmem_limit_bytes=VMEM_LIMIT),
        name="mlstm_sample",
    )(za_s, qc, g_s, bias_row, nrm_row, m0, n0)


def _outproj_kernel(hp_ref, hs_ref, ap_ref, as_ref, yb_ref, w_ref, op_ref, os_ref, wc_ref):
    j = pl.program_id(1)

    @pl.when(pl.program_id(0) == 0)
    def _():
        wc_ref[j] = w_ref[...].astype(BF16)

    wa = wc_ref[j, :D_A, :]
    wb = wc_ref[j, D_A:, :]
    op_ref[...] = hp_ref[...] + _bdot(ap_ref[...], wa) + _bdot(yb_ref[0, :MP, :], wb)
    os_ref[0] = hs_ref[0] + _bdot(as_ref[0], wa) + _bdot(yb_ref[0, MP:, :], wb)


def _outproj(hp, hs, ap, a_s, yb_c, w3):
    tn = OUT_TN
    n_n = D_MODEL // tn
    return pl.pallas_call(
        _outproj_kernel,
        grid=(N_MIX_BLOCKS, n_n),
        in_specs=[
            pl.BlockSpec((MP, tn), lambda i, j: (i, j)),
            pl.BlockSpec((1, MS, tn), lambda i, j: (i, 0, j)),
            pl.BlockSpec((MP, D_A), lambda i, j: (i, 0)),
            pl.BlockSpec((1, MS, D_A), lambda i, j: (i, 0, 0)),
            pl.BlockSpec((1, MP + MS, D_B), lambda i, j: (i, 0, 0)),
            pl.BlockSpec((None, D_A + D_B, tn), lambda i, j: (0, 0, jnp.where(i == 0, j, n_n - 1))),
        ],
        out_specs=[
            pl.BlockSpec((MP, tn), lambda i, j: (i, j)),
            pl.BlockSpec((1, MS, tn), lambda i, j: (i, 0, j)),
        ],
        out_shape=[
            jax.ShapeDtypeStruct((N_MIX_BLOCKS * MP, D_MODEL), F32),
            jax.ShapeDtypeStruct((N_MIX_BLOCKS, MS, D_MODEL), F32),
        ],
        scratch_shapes=[pltpu.VMEM((n_n, D_A + D_B, tn), BF16)],
        compiler_params=pltpu.CompilerParams(
            dimension_semantics=("arbitrary", "arbitrary"), vmem_limit_bytes=VMEM_LIMIT),
        name="outproj",
    )(hp, hs, ap, a_s, yb_c, w3)


def kernel(x_prompt, x_sample, state_mlstm_C, state_mlstm_n, state_mlstm_m, state_conv, norm_ffn1, ffn1_gate, ffn1_up, ffn1_down, norm_mix, w_in, b_gates, conv_w, conv_b, norm_mlstm, norm_conv, w_out, norm_ffn2, ffn2_gate, ffn2_up, ffn2_down, norm_final):
    batch, seq, _ = x_prompt.shape
    nb = x_sample.shape[0]
    assert batch == N_MIX_BLOCKS and seq == MP and nb == N_MIX_BLOCKS * MS
    assert norm_ffn1.shape[0] == 1, "single-layer trunk"

    xp = x_prompt.reshape(batch * seq, D_MODEL)
    xs = x_sample.reshape(N_FFN_BLOCKS, TM_S, D_MODEL)

    h1p, h1s, xn_c = _ffn(xp, xs, norm_ffn1, ffn1_gate, ffn1_up, ffn1_down, norm_mix)

    w_t = jnp.swapaxes(w_in[0], 0, 1)
    za_p, za_s, g_p, g_s = _proj(xn_c, w_t)
    za_s = za_s.reshape(nb, 4 * D_A)
    yb_c, conv_p, conv_s, qc_s = _conv(xn_c, w_t, conv_w[0], conv_b, norm_conv,
                                       state_conv.reshape(N_MIX_BLOCKS, MS, 2, D_B), za_s, state_mlstm_C)

    bias_row = jnp.zeros((1, LANE), F32).at[0, :2 * H_A].set(b_gates[0].astype(F32))
    ha_p, c_p, n_p, m_p = _mlstm_prompt(za_p, g_p, bias_row, norm_mlstm, batch, seq)
    ha_s, m_s, n_s, wk_s, dec_s = _mlstm_sample(
        za_s, qc_s, g_s.reshape(nb, LANE), bias_row, norm_mlstm,
        state_mlstm_m.reshape(nb, H_A), state_mlstm_n.reshape(nb, H_A * DK))

    h2p, h2s = _outproj(h1p, h1s.reshape(N_MIX_BLOCKS, MS, D_MODEL), ha_p.reshape(batch * seq, D_A),
                        ha_s.reshape(N_MIX_BLOCKS, MS, D_A), yb_c, w_out)

    yp, ys, c_s = _ffn(h2p, h2s.reshape(N_FFN_BLOCKS, TM_S, D_MODEL), norm_ffn2, ffn2_gate, ffn2_up,
                       ffn2_down, norm_final.reshape(1, D_MODEL),
                       state=(state_mlstm_C, wk_s, za_s, dec_s))

    return (
        yp.reshape(batch, seq, D_MODEL),
        ys.reshape(nb, 1, D_MODEL),
        c_p,
        n_p,
        m_p[:, 0, :H_A].reshape(1, batch, H_A),
        conv_p,
        c_s,
        n_s.reshape(1, nb, H_A, DK),
        m_s.reshape(1, nb, H_A),
        conv_s.reshape(1, nb, 2, D_B),
    )
```

```python
import functools

import jax
import jax.numpy as jnp
from jax import lax
from jax.experimental import pallas as pl
from jax.experimental.pallas import tpu as pltpu

F32 = jnp.float32
BF16 = jnp.bfloat16

D_MODEL = 2048
D_A = 1024
D_B = 1024
H_A = 4
DK = 256
DV = 256
G_B = 8
D_FF = 5504
GATE_CAP = 15.0
EPS = 1e-6
GATE_ROW0 = 4 * D_A
CONV_ROW0 = 4 * D_A + 2 * H_A

LANE = 128
N_FFN_BLOCKS = 8
TM_P = 1024
TM_S = 16
TM = TM_P + TM_S
TF = 256
X_CHUNK = 128
N_X_CHUNKS = TM_P // X_CHUNK
X_AHEAD0 = 2
N_MIX_BLOCKS = N_FFN_BLOCKS // 2
MP = 2 * TM_P
MS = 2 * TM_S
PROJ_TN = 1024
CONV_TC = 256
OUT_TN = 512
MLSTM_CHUNK = 256
MLSTM_SEQS = 2
SAMPLE_BS = 8
VMEM_LIMIT = 62 * 1024 * 1024


def _rms(x, g):
    return x * lax.rsqrt(jnp.mean(x * x, axis=-1, keepdims=True) + EPS) * g


def _soft_cap(x):
    return GATE_CAP * jnp.tanh(x / GATE_CAP)


def _log_sigmoid(x):
    return -jax.nn.softplus(-x)


def _bdot(a, b):
    return jnp.dot(a, b, preferred_element_type=F32)


def _bdot_t(a, b):
    return lax.dot_general(a, b, (((1,), (1,)), ((), ())), preferred_element_type=F32)


def _sample_state_update(f, c_ref, wk_ref, v_ref, dec_ref, c_out):
    r = jnp.minimum(f, TM_S - 1)
    seq = pl.program_id(0) * TM_S + r
    lane = lax.broadcasted_iota(jnp.int32, (D_A, LANE), 1)
    wk_col = jnp.sum(jnp.where(lane == seq, wk_ref[...], 0.0), axis=1, keepdims=True)
    v_row = v_ref[pl.ds(r, 1), :]
    dec = dec_ref[pl.ds(r, 1), :]
    for h in range(H_A):
        hs = slice(h * DK, (h + 1) * DK)
        c_out[0, 0, h] = dec[:, h:h + 1] * c_ref[0, 0, h] + wk_col[hs, :] * v_row[:, hs]


def _x_chunk_copy(x_hbm, xp_ref, sem, blk, c):
    dst0 = pl.multiple_of(c * X_CHUNK, X_CHUNK)
    src0 = pl.multiple_of(blk * TM_P + c * X_CHUNK, X_CHUNK)
    return pltpu.make_async_copy(
        x_hbm.at[pl.ds(src0, X_CHUNK), :], xp_ref.at[pl.ds(dst0, X_CHUNK), :], sem.at[c])


def _ffn_kernel(x_hbm, xs_ref, g_ref, wg_ref, wu_ref, wd_ref, g2_ref, *rest,
                n_f, last_valid, final_norm):
    if final_norm:
        c_ref, wk_ref, v_ref, dec_ref, out_hbm, os_ref, c_out, xn_ref, xp_ref, x_sem, acc_ref, o_sem = rest
    else:
        out_hbm, os_ref, nxt_ref, xn_ref, xp_ref, x_sem, acc_ref, o_sem = rest
    i = pl.program_id(0)
    f = pl.program_id(1)
    slot = lax.rem(i, 2)
    op_ref = acc_ref.at[slot]

    def out_copy(blk, s):
        row0 = pl.multiple_of(blk * TM_P, TM_P)
        return pltpu.make_async_copy(acc_ref.at[s], out_hbm.at[pl.ds(row0, TM_P), :], o_sem.at[s])

    @pl.when(jnp.logical_and(i == 0, f == 0))
    def _():
        for c in range(N_X_CHUNKS):
            _x_chunk_copy(x_hbm, xp_ref, x_sem, 0, c).start()

    @pl.when(f == 0)
    def _():
        for c in range(N_X_CHUNKS):
            _x_chunk_copy(x_hbm, xp_ref, x_sem, i, c).wait()

    @pl.when(jnp.logical_and(jnp.logical_and(f >= X_AHEAD0, f < X_AHEAD0 + N_X_CHUNKS),
                             i < N_FFN_BLOCKS - 1))
    def _():
        _x_chunk_copy(x_hbm, xp_ref, x_sem, i + 1, f - X_AHEAD0).start()

    def step(valid, first, with_state):
        if first:
            g = g_ref[...]
            xn_ref[:TM_P, :] = _rms(xp_ref[...], g).astype(BF16)
            xn_ref[TM_P:, :] = _rms(xs_ref[0], g).astype(BF16)
        xn = xn_ref[...]
        a = _bdot(xn, wg_ref[:, :valid].astype(BF16))
        b = _bdot(xn, wu_ref[:, :valid].astype(BF16))
        hid = (a * jax.nn.sigmoid(a) * b * 0.5).astype(BF16)
        r = _bdot(hid, wd_ref[:valid, :].astype(BF16))
        if first:
            op_ref[...] = xp_ref[...] + r[:TM_P]
            os_ref[0] = xs_ref[0] + r[TM_P:]
        else:
            op_ref[...] += r[:TM_P]
            os_ref[0] += r[TM_P:]
        if with_state:
            _sample_state_update(f, c_ref, wk_ref, v_ref, dec_ref, c_out)

    assert last_valid < TF and n_f - 1 >= TM_S, "state update rides the full-width steps"

    @pl.when(f == 0)
    def _():
        step(TF, True, final_norm)

    @pl.when(jnp.logical_and(f > 0, f < n_f - 1))
    def _():
        step(TF, False, final_norm)

    @pl.when(f == n_f - 1)
    def _():
        step(last_valid, False, False)
        g2 = g2_ref[...]
        if final_norm:
            op_ref[...] = _rms(op_ref[...], g2)
            os_ref[0] = _rms(os_ref[0], g2)
        else:
            nxt_ref[0, :TM_P, :] = _rms(op_ref[...], g2).astype(BF16)
            nxt_ref[0, TM_P:, :] = _rms(os_ref[0], g2).astype(BF16)

        @pl.when(i > 0)
        def _():
            out_copy(i - 1, 1 - slot).wait()

        out_copy(i, slot).start()

        @pl.when(i == N_FFN_BLOCKS - 1)
        def _():
            out_copy(i, slot).wait()


def _ffn(xp, xs, g, wg, wu, wd, g2, state=None):
    final_norm = state is not None
    n_f = pl.cdiv(D_FF, TF)
    last_valid = D_FF - (n_f - 1) * TF
    kern = functools.partial(_ffn_kernel, n_f=n_f, last_valid=last_valid, final_norm=final_norm)
    assert X_AHEAD0 + N_X_CHUNKS <= n_f and N_X_CHUNKS * X_CHUNK == TM_P
    in_specs = [
        pl.BlockSpec(memory_space=pl.ANY),
        pl.BlockSpec((1, TM_S, D_MODEL), lambda i, f: (i, 0, 0)),
        pl.BlockSpec((1, D_MODEL), lambda i, f: (0, 0)),
        pl.BlockSpec((None, D_MODEL, TF), lambda i, f: (0, 0, f)),
        pl.BlockSpec((None, D_MODEL, TF), lambda i, f: (0, 0, f)),
        pl.BlockSpec((None, TF, D_MODEL), lambda i, f: (0, f, 0)),
        pl.BlockSpec((1, D_MODEL), lambda i, f: (0, 0)),
    ]
    out_specs = [
        pl.BlockSpec(memory_space=pl.ANY),
        pl.BlockSpec((1, TM_S, D_MODEL), lambda i, f: (i, 0, 0)),
    ]
    out_shape = [
        jax.ShapeDtypeStruct((N_FFN_BLOCKS * TM_P, D_MODEL), F32),
        jax.ShapeDtypeStruct((N_FFN_BLOCKS, TM_S, D_MODEL), F32),
    ]
    args = [xp, xs, g, wg, wu, wd, g2]
    if final_norm:
        c0, wk, za_s, dec = state
        seq = lambda i, f: i * TM_S + jnp.minimum(f, TM_S - 1)
        cspec = pl.BlockSpec((1, 1, H_A, DK, DV), lambda i, f: (0, seq(i, f), 0, 0, 0))
        in_specs += [
            cspec,
            pl.BlockSpec((D_A, LANE), lambda i, f: (0, 0)),
            pl.BlockSpec((TM_S, D_A), lambda i, f: (i, 2)),
            pl.BlockSpec((TM_S, LANE), lambda i, f: (i, 0)),
        ]
        out_specs.append(cspec)
        out_shape.append(jax.ShapeDtypeStruct(c0.shape, F32))
        args += [c0, wk, za_s, dec]
    else:
        out_specs.append(pl.BlockSpec((1, TM, D_MODEL), lambda i, f: (i, 0, 0)))
        out_shape.append(jax.ShapeDtypeStruct((N_FFN_BLOCKS, TM, D_MODEL), BF16))
    return pl.pallas_call(
        kern,
        grid=(N_FFN_BLOCKS, n_f),
        in_specs=in_specs,
        out_specs=out_specs,
        out_shape=out_shape,
        scratch_shapes=[pltpu.VMEM((TM, D_MODEL), BF16),
                        pltpu.VMEM((TM_P, D_MODEL), F32),
                        pltpu.SemaphoreType.DMA((N_X_CHUNKS,)),
                        pltpu.VMEM((2, TM_P, D_MODEL), F32),
                        pltpu.SemaphoreType.DMA((2,))],
        compiler_params=pltpu.CompilerParams(
            dimension_semantics=("arbitrary", "arbitrary"), vmem_limit_bytes=VMEM_LIMIT),
        name="ffn_final" if final_norm else "ffn",
    )(*args)


def _split_rows(z0, z1, p_ref, s_ref):
    p_ref[:TM_P, :] = z0[:TM_P]
    p_ref[TM_P:, :] = z1[:TM_P]
    s_ref[0, :TM_S, :] = z0[TM_P:]
    s_ref[0, TM_S:, :] = z1[TM_P:]


def _proj_kernel(xn_ref, w_ref, wg_ref, zp_ref, zs_ref, gp_ref, gs_ref):
    w = w_ref[...].astype(BF16)
    _split_rows(_bdot_t(xn_ref[0], w), _bdot_t(xn_ref[1], w), zp_ref, zs_ref)

    @pl.when(pl.program_id(1) == 0)
    def _():
        wg = wg_ref[...].astype(BF16)
        _split_rows(_bdot_t(xn_ref[0], wg), _bdot_t(xn_ref[1], wg), gp_ref, gs_ref)


def _proj(xn_c, w_t):
    n_n = (4 * D_A) // PROJ_TN
    return pl.pallas_call(
        _proj_kernel,
        grid=(N_MIX_BLOCKS, n_n),
        in_specs=[
            pl.BlockSpec((2, TM, D_MODEL), lambda i, j: (i, 0, 0)),
            pl.BlockSpec((PROJ_TN, D_MODEL), lambda i, j: (j, 0)),
            pl.BlockSpec((LANE, D_MODEL), lambda i, j: (GATE_ROW0 // LANE, 0)),
        ],
        out_specs=[
            pl.BlockSpec((MP, PROJ_TN), lambda i, j: (i, j)),
            pl.BlockSpec((1, MS, PROJ_TN), lambda i, j: (i, 0, j)),
            pl.BlockSpec((MP, LANE), lambda i, j: (i, 0)),
            pl.BlockSpec((1, MS, LANE), lambda i, j: (i, 0, 0)),
        ],
        out_shape=[
            jax.ShapeDtypeStruct((N_MIX_BLOCKS * MP, 4 * D_A), F32),
            jax.ShapeDtypeStruct((N_MIX_BLOCKS, MS, 4 * D_A), F32),
            jax.ShapeDtypeStruct((N_MIX_BLOCKS * MP, LANE), F32),
            jax.ShapeDtypeStruct((N_MIX_BLOCKS, MS, LANE), F32),
        ],
        compiler_params=pltpu.CompilerParams(
            dimension_semantics=("parallel", "arbitrary"), vmem_limit_bytes=VMEM_LIMIT),
        name="proj_qkvo",
    )(xn_c, w_t, w_t)


def _group_norm(yb, nrm):
    gw = D_B // G_B
    parts = []
    for g in range(yb.shape[1] // gw):
        seg = yb[:, g * gw:(g + 1) * gw]
        parts.append(seg * lax.rsqrt(jnp.mean(seg * seg, axis=-1, keepdims=True) + EPS))
    return jnp.concatenate(parts, axis=1) * nrm


def _conv_kernel(xn_ref, wgb_ref, wgc_ref, wxc_ref, cw_ref, cb_ref, nrm_ref, buf_ref, q_ref, c_ref,
                 yb_ref, cp_ref, cs_ref, qc_ref):
    q_t = jnp.concatenate([q_ref[...], jnp.zeros((LANE - SAMPLE_BS, D_A), F32)], axis=0).T
    for h in range(H_A):
        hs = slice(h * DK, (h + 1) * DK)
        for j in range(SAMPLE_BS):
            qc_ref[j:j + 1, hs] = jnp.sum(q_t[hs, j:j + 1] * c_ref[0, j, h], axis=0, keepdims=True)

    wgb = wgb_ref[...].astype(BF16)
    wgc = wgc_ref[...].astype(BF16)
    wxc = wxc_ref[...].astype(BF16)
    cw0 = cw_ref[0:1, :]
    cw1 = cw_ref[1:2, :]
    cw2 = cw_ref[2:3, :]
    cb = cb_ref[...]
    nrm = nrm_ref[...]
    row = lax.broadcasted_iota(jnp.int32, (TM_P, wgb.shape[0]), 0)
    zero_row = jnp.zeros((1, wgb.shape[0]), F32)
    um2, um1 = zero_row, zero_row

    for t in range(2):
        xn = xn_ref[t]
        u = _bdot_t(xn, wgc) * _bdot_t(xn, wxc)
        gb = _bdot_t(xn, wgb)

        up = u[:TM_P]
        u1 = jnp.where(row < 1, um1, pltpu.roll(up, 1, 0))
        u2 = jnp.where(row < 1, um2, jnp.where(row < 2, um1, pltpu.roll(up, 2, 0)))
        yc = cw0 * u2 + cw1 * u1 + cw2 * up + cb
        yb_ref[0, t * TM_P:(t + 1) * TM_P, :] = _group_norm(gb[:TM_P] * yc, nrm).astype(yb_ref.dtype)
        um2, um1 = up[TM_P - 2:TM_P - 1, :], up[TM_P - 1:TM_P, :]

        ss = slice(t * TM_S, (t + 1) * TM_S)
        us = u[TM_P:]
        b0 = buf_ref[0, ss, 0, :]
        b1 = buf_ref[0, ss, 1, :]
        ycs = cw0 * b0 + cw1 * b1 + cw2 * us + cb
        yb_ref[0, MP + t * TM_S:MP + (t + 1) * TM_S, :] = _group_norm(gb[TM_P:] * ycs, nrm).astype(yb_ref.dtype)
        cs_ref[0, ss, 0, :] = b1
        cs_ref[0, ss, 1, :] = us

    cp_ref[0, 0, 0:1, :] = um2
    cp_ref[0, 0, 1:2, :] = um1


def _conv(xn_c, w_t, cw, cb_row, nrm_row, buf, za_s, c0):
    tc = CONV_TC
    n_c = D_B // tc
    assert N_MIX_BLOCKS * n_c * SAMPLE_BS == za_s.shape[0], "one sample group per grid step"
    wspec = lambda base: pl.BlockSpec(
        (pl.Element(tc), pl.Element(D_MODEL)), lambda i, c: (pl.multiple_of(base + c * tc, 8), 0))
    grp = lambda i, c: i * n_c + c
    return pl.pallas_call(
        _conv_kernel,
        grid=(N_MIX_BLOCKS, n_c),
        in_specs=[
            pl.BlockSpec((2, TM, D_MODEL), lambda i, c: (i, 0, 0)),
            wspec(CONV_ROW0),
            wspec(CONV_ROW0 + D_B),
            wspec(CONV_ROW0 + 2 * D_B),
            pl.BlockSpec((3, tc), lambda i, c: (0, c)),
            pl.BlockSpec((1, tc), lambda i, c: (0, c)),
            pl.BlockSpec((1, tc), lambda i, c: (0, c)),
            pl.BlockSpec((1, MS, 2, tc), lambda i, c: (i, 0, 0, c)),
            pl.BlockSpec((SAMPLE_BS, D_A), lambda i, c: (grp(i, c), 0)),
            pl.BlockSpec((1, SAMPLE_BS, H_A, DK, DV), lambda i, c: (0, grp(i, c), 0, 0, 0)),
        ],
        out_specs=[
            pl.BlockSpec((1, MP + MS, tc), lambda i, c: (i, 0, c)),
            pl.BlockSpec((1, 1, 2, tc), lambda i, c: (0, i, 0, c)),
            pl.BlockSpec((1, MS, 2, tc), lambda i, c: (i, 0, 0, c)),
            pl.BlockSpec((SAMPLE_BS, D_A), lambda i, c: (grp(i, c), 0)),
        ],
        out_shape=[
            jax.ShapeDtypeStruct((N_MIX_BLOCKS, MP + MS, D_B), BF16),
            jax.ShapeDtypeStruct((1, N_MIX_BLOCKS, 2, D_B), F32),
            jax.ShapeDtypeStruct((N_MIX_BLOCKS, MS, 2, D_B), F32),
            jax.ShapeDtypeStruct((za_s.shape[0], D_A), F32),
        ],
        compiler_params=pltpu.CompilerParams(
            dimension_semantics=("parallel", "arbitrary"), vmem_limit_bytes=VMEM_LIMIT),
        name="proj_conv",
    )(xn_c, w_t, w_t, w_t, cw, cb_row, nrm_row, buf, za_s, c0)


def _mlstm_prompt_kernel(q_ref, k_ref, v_ref, o_ref, g_ref, bias_ref, nrm_ref,
                         ha_ref, c_ref, n_ref, m_ref, cx_s, m_s, *, n_chunks):
    L = MLSTM_CHUNK
    c = pl.program_id(1)

    @pl.when(c == 0)
    def _():
        cx_s[...] = jnp.zeros_like(cx_s)
        m_s[...] = jnp.zeros_like(m_s)

    row = lax.broadcasted_iota(jnp.int32, (L, L), 0)
    col = lax.broadcasted_iota(jnp.int32, (L, L), 1)
    causal = row >= col
    tri = causal.astype(BF16)

    for b in range(MLSTM_SEQS):
        _mlstm_prompt_seq(b, causal, tri, q_ref, k_ref, v_ref, o_ref, g_ref, bias_ref, nrm_ref,
                          ha_ref, cx_s, m_s)

    @pl.when(c == n_chunks - 1)
    def _():
        for b in range(MLSTM_SEQS):
            m_ref[b] = m_s[b]
            for h in range(H_A):
                cx = cx_s[b * H_A + h]
                c_ref[0, b, h] = cx[:, :DV]
                n_ref[0, b, h:h + 1, :] = cx[:, DV:].T[h:h + 1, :]


def _prefix_max_rows(x):
    rows = x.shape[0]
    row = lax.broadcasted_iota(jnp.int32, x.shape, 0)
    k = 1
    while k < rows:
        x = jnp.where(row >= k, jnp.maximum(x, pltpu.roll(x, k, 0)), x)
        k *= 2
    return x


def _mlstm_prompt_seq(b, causal, tri, q_ref, k_ref, v_ref, o_ref, g_ref, bias_ref, nrm_ref,
                      ha_ref, cx_s, m_s):
    L = MLSTM_CHUNK
    lane = lax.broadcasted_iota(jnp.int32, (L, LANE), 1)
    lane_dv = lax.broadcasted_iota(jnp.int32, (DV, LANE), 1)
    capped = _soft_cap(g_ref[b] + bias_ref[...])
    lf = _log_sigmoid(capped)
    hi = lf.astype(BF16)
    r1 = lf - hi.astype(F32)
    mid = r1.astype(BF16)
    lo = (r1 - mid.astype(F32)).astype(BF16)
    parts = _bdot(tri, jnp.concatenate([hi, mid, lo], axis=1))
    bc = parts[:, :LANE] + parts[:, LANE:2 * LANE] + parts[:, 2 * LANE:]
    bh = pltpu.roll(bc, LANE - H_A, 1)
    a = capped - bh
    a_t = a.T
    m_prev = m_s[b]
    big_m = jnp.maximum(_prefix_max_rows(a), m_prev[0:1, :])
    sc_all = jnp.exp(m_prev[0:1, :] - big_m)
    emt_all = jnp.exp(-(bh + big_m))
    m_last = big_m[L - 1:L, :]
    w_all = jnp.exp(a - m_last)
    decay_all = sc_all[L - 1:L, :]
    m_s[b] = jnp.broadcast_to(bh[L - 1:L, :] + m_last, (8, LANE))

    nums = []
    nd_tail = jnp.zeros((L, LANE), F32)
    sq_tail = jnp.zeros((L, LANE), F32)
    for h in range(H_A):
        hs = slice(h * DK, (h + 1) * DK)
        onehot = (lane == h).astype(BF16)
        qb = q_ref[b, :, hs].astype(BF16)
        kf = k_ref[b, :, hs] * (DK ** -0.5)
        kb = kf.astype(BF16)
        vx = jnp.concatenate([v_ref[b, :, hs].astype(BF16), onehot], axis=1)
        cx_prev = cx_s[b * H_A + h]

        p = jnp.exp(jnp.where(causal, a_t[h:h + 1, :] - big_m[:, h:h + 1], -jnp.inf))
        s = _bdot_t(qb, kb) * p
        nd = sc_all[:, h:h + 1] * _bdot(qb, cx_prev.astype(BF16)) + _bdot(s.astype(BF16), vx)
        num = nd[:, :DV]
        nums.append(num)
        nd_tail = nd_tail + nd[:, DV:]
        sq_tail = sq_tail + _bdot((num * num).astype(BF16), (lane_dv == h).astype(BF16))

        wk = (w_all[:, h:h + 1] * kf).astype(BF16)
        cx_s[b * H_A + h] = decay_all[:, h:h + 1] * cx_prev + lax.dot_general(
            wk, vx, (((0,), (0,)), ((), ())), preferred_element_type=F32)

    inv = 1.0 / jnp.maximum(jnp.abs(nd_tail), emt_all)
    ms = inv * inv * sq_tail * (1.0 / DV)
    scale_all = inv * lax.rsqrt(ms + EPS)
    for h in range(H_A):
        hs = slice(h * DK, (h + 1) * DK)
        hn = nums[h] * scale_all[:, h:h + 1]
        ha_ref[b, :, hs] = (hn * nrm_ref[:, hs] * jax.nn.sigmoid(o_ref[b, :, hs])).astype(ha_ref.dtype)


def _mlstm_prompt(za_p, g_p, bias_row, nrm_row, batch, seq):
    L = MLSTM_CHUNK
    nsq = MLSTM_SEQS
    n_chunks = seq // L
    kern = functools.partial(_mlstm_prompt_kernel, n_chunks=n_chunks)
    za3 = za_p.reshape(batch, seq, 4 * D_A)
    g3 = g_p.reshape(batch, seq, LANE)
    zspec = lambda part: pl.BlockSpec((nsq, L, D_A), lambda b, c: (b, c, part))
    return pl.pallas_call(
        kern,
        grid=(batch // nsq, n_chunks),
        in_specs=[
            zspec(0), zspec(1), zspec(2), zspec(3),
            pl.BlockSpec((nsq, L, LANE), lambda b, c: (b, c, 0)),
            pl.BlockSpec((1, LANE), lambda b, c: (0, 0)),
            pl.BlockSpec((1, D_A), lambda b, c: (0, 0)),
        ],
        out_specs=[
            pl.BlockSpec((nsq, L, D_A), lambda b, c: (b, c, 0)),
            pl.BlockSpec((1, nsq, H_A, DK, DV), lambda b, c: (0, b, 0, 0, 0)),
            pl.BlockSpec((1, nsq, H_A, DK), lambda b, c: (0, b, 0, 0)),
            pl.BlockSpec((nsq, 8, LANE), lambda b, c: (b, 0, 0)),
        ],
        out_shape=[
            jax.ShapeDtypeStruct((batch, seq, D_A), BF16),
            jax.ShapeDtypeStruct((1, batch, H_A, DK, DV), F32),
            jax.ShapeDtypeStruct((1, batch, H_A, DK), F32),
            jax.ShapeDtypeStruct((batch, 8, LANE), F32),
        ],
        scratch_shapes=[pltpu.VMEM((nsq * H_A, DK, DV + LANE), F32),
                        pltpu.VMEM((nsq, 8, LANE), F32)],
        compiler_params=pltpu.CompilerParams(
            dimension_semantics=("parallel", "arbitrary"), vmem_limit_bytes=VMEM_LIMIT),
        name="mlstm_prompt",
    )(za3, za3, za3, za3, g3, bias_row, nrm_row)


def _expand_heads(x, width):
    rows = x.shape[0]
    return jnp.concatenate([jnp.broadcast_to(x[:, h:h + 1], (rows, width)) for h in range(H_A)], axis=1)


def _head_sums(x):
    return jnp.concatenate(
        [jnp.sum(x[:, h * DK:(h + 1) * DK], axis=-1, keepdims=True) for h in range(H_A)], axis=1)


def _mlstm_sample_kernel(za_ref, qc_ref, g_ref, bias_ref, nrm_ref, m_ref, n_ref,
                         ha_ref, m_out, n_out, wk_out, dec_out):
    nb = za_ref.shape[0]
    capped = _soft_cap(g_ref[...] + bias_ref[...])
    logi = capped[:, 0:H_A]
    logf = _log_sigmoid(capped)[:, H_A:2 * H_A]
    m_prev = m_ref[...]
    m_inter = logf + m_prev
    m_t = jnp.maximum(m_inter, logi)
    sc = jnp.exp(m_inter - m_t)
    ei = jnp.exp(logi - m_t)
    emt = jnp.exp(-m_t)

    q = za_ref[:, 0:D_A]
    k = za_ref[:, D_A:2 * D_A] * (DK ** -0.5)
    v = za_ref[:, 2 * D_A:3 * D_A]
    o = za_ref[:, 3 * D_A:4 * D_A]
    n_prev = n_ref[...]

    s = _head_sums(q * k) * ei
    den = sc * _head_sums(q * n_prev) + s
    denom = jnp.maximum(jnp.abs(den), emt)
    num = _expand_heads(sc, DV) * qc_ref[...] + _expand_heads(s, DV) * v
    hh = num / _expand_heads(denom, DV)
    ms = _head_sums(hh * hh) * (1.0 / DV)
    hn = hh * lax.rsqrt(_expand_heads(ms, DV) + EPS)
    ha_ref[...] = (hn * nrm_ref[...] * jax.nn.sigmoid(o)).astype(ha_ref.dtype)
    n_out[...] = _expand_heads(sc, DK) * n_prev + _expand_heads(ei, DK) * k
    m_out[...] = m_t
    pad = jnp.zeros((nb, LANE - H_A), F32)
    dec_out[...] = jnp.concatenate([sc, pad], axis=1)

    k_t = k.T
    ei_t = jnp.concatenate([ei, pad], axis=1).T
    for h in range(H_A):
        hs = slice(h * DK, (h + 1) * DK)
        wk_out[hs, :] = k_t[hs, :] * ei_t[h:h + 1, :]


def _mlstm_sample(za_s, qc, g_s, bias_row, nrm_row, m0, n0):
    nb = za_s.shape[0]
    assert nb == LANE, "the sequence axis is transposed onto the lanes"
    return pl.pallas_call(
        _mlstm_sample_kernel,
        out_shape=[
            jax.ShapeDtypeStruct((nb, D_A), BF16),
            jax.ShapeDtypeStruct((nb, H_A), F32),
            jax.ShapeDtypeStruct((nb, H_A * DK), F32),
            jax.ShapeDtypeStruct((D_A, nb), F32),
            jax.ShapeDtypeStruct((nb, LANE), F32),
        ],
        compiler_params=pltpu.CompilerParams(vmem_limit_bytes=VMEM_LIMIT),
        name="mlstm_sample",
    )(za_s, qc, g_s, bias_row, nrm_row, m0, n0)


def _outproj_kernel(hp_ref, hs_ref, ap_ref, as_ref, yb_ref, w_ref, op_ref, os_ref, wc_ref):
    j = pl.program_id(1)

    @pl.when(pl.program_id(0) == 0)
    def _():
        wc_ref[j] = w_ref[...].astype(BF16)

    wa = wc_ref[j, :D_A, :]
    wb = wc_ref[j, D_A:, :]
    op_ref[...] = hp_ref[...] + _bdot(ap_ref[...], wa) + _bdot(yb_ref[0, :MP, :], wb)
    os_ref[0] = hs_ref[0] + _bdot(as_ref[0], wa) + _bdot(yb_ref[0, MP:, :], wb)


def _outproj(hp, hs, ap, a_s, yb_c, w3):
    tn = OUT_TN
    n_n = D_MODEL // tn
    return pl.pallas_call(
        _outproj_kernel,
        grid=(N_MIX_BLOCKS, n_n),
        in_specs=[
            pl.BlockSpec((MP, tn), lambda i, j: (i, j)),
            pl.BlockSpec((1, MS, tn), lambda i, j: (i, 0, j)),
            pl.BlockSpec((MP, D_A), lambda i, j: (i, 0)),
            pl.BlockSpec((1, MS, D_A), lambda i, j: (i, 0, 0)),
            pl.BlockSpec((1, MP + MS, D_B), lambda i, j: (i, 0, 0)),
            pl.BlockSpec((None, D_A + D_B, tn), lambda i, j: (0, 0, jnp.where(i == 0, j, n_n - 1))),
        ],
        out_specs=[
            pl.BlockSpec((MP, tn), lambda i, j: (i, j)),
            pl.BlockSpec((1, MS, tn), lambda i, j: (i, 0, j)),
        ],
        out_shape=[
            jax.ShapeDtypeStruct((N_MIX_BLOCKS * MP, D_MODEL), F32),
            jax.ShapeDtypeStruct((N_MIX_BLOCKS, MS, D_MODEL), F32),
        ],
        scratch_shapes=[pltpu.VMEM((n_n, D_A + D_B, tn), BF16)],
        compiler_params=pltpu.CompilerParams(
            dimension_semantics=("arbitrary", "arbitrary"), vmem_limit_bytes=VMEM_LIMIT),
        name="outproj",
    )(hp, hs, ap, a_s, yb_c, w3)


def kernel(x_prompt, x_sample, state_mlstm_C, state_mlstm_n, state_mlstm_m, state_conv, norm_ffn1, ffn1_gate, ffn1_up, ffn1_down, norm_mix, w_in, b_gates, conv_w, conv_b, norm_mlstm, norm_conv, w_out, norm_ffn2, ffn2_gate, ffn2_up, ffn2_down, norm_final):
    batch, seq, _ = x_prompt.shape
    nb = x_sample.shape[0]
    assert batch == N_MIX_BLOCKS and seq == MP and nb == N_MIX_BLOCKS * MS
    assert norm_ffn1.shape[0] == 1, "single-layer trunk"

    xp = x_prompt.reshape(batch * seq, D_MODEL)
    xs = x_sample.reshape(N_FFN_BLOCKS, TM_S, D_MODEL)

    h1p, h1s, xn_c = _ffn(xp, xs, norm_ffn1, ffn1_gate, ffn1_up, ffn1_down, norm_mix)

    w_t = jnp.swapaxes(w_in[0], 0, 1)
    za_p, za_s, g_p, g_s = _proj(xn_c, w_t)
    za_s = za_s.reshape(nb, 4 * D_A)
    yb_c, conv_p, conv_s, qc_s = _conv(xn_c, w_t, conv_w[0], conv_b, norm_conv,
                                       state_conv.reshape(N_MIX_BLOCKS, MS, 2, D_B), za_s, state_mlstm_C)

    bias_row = jnp.zeros((1, LANE), F32).at[0, :2 * H_A].set(b_gates[0].astype(F32))
    ha_p, c_p, n_p, m_p = _mlstm_prompt(za_p, g_p, bias_row, norm_mlstm, batch, seq)
    ha_s, m_s, n_s, wk_s, dec_s = _mlstm_sample(
        za_s, qc_s, g_s.reshape(nb, LANE), bias_row, norm_mlstm,
        state_mlstm_m.reshape(nb, H_A), state_mlstm_n.reshape(nb, H_A * DK))

    h2p, h2s = _outproj(h1p, h1s.reshape(N_MIX_BLOCKS, MS, D_MODEL), ha_p.reshape(batch * seq, D_A),
                        ha_s.reshape(N_MIX_BLOCKS, MS, D_A), yb_c, w_out)

    yp, ys, c_s = _ffn(h2p, h2s.reshape(N_FFN_BLOCKS, TM_S, D_MODEL), norm_ffn2, ffn2_gate, ffn2_up,
                       ffn2_down, norm_final.reshape(1, D_MODEL),
                       state=(state_mlstm_C, wk_s, za_s, dec_s))

    return (
        yp.reshape(batch, seq, D_MODEL),
        ys.reshape(nb, 1, D_MODEL),
        c_p,
        n_p,
        m_p[:, 0, :H_A].reshape(1, batch, H_A),
        conv_p,
        c_s,
        n_s.reshape(1, nb, H_A, DK),
        m_s.reshape(1, nb, H_A),
        conv_s.reshape(1, nb, 2, D_B),
    )
```

```python
import functools

import jax
import jax.numpy as jnp
from jax import lax
from jax.experimental import pallas as pl
from jax.experimental.pallas import tpu as pltpu

F32 = jnp.float32
BF16 = jnp.bfloat16

D_MODEL = 2048
D_A = 1024
D_B = 1024
H_A = 4
DK = 256
DV = 256
G_B = 8
D_FF = 5504
GATE_CAP = 15.0
EPS = 1e-6
GATE_ROW0 = 4 * D_A
CONV_ROW0 = 4 * D_A + 2 * H_A

LANE = 128
N_FFN_BLOCKS = 8
TM_P = 1024
TM_S = 16
TM = TM_P + TM_S
TF = 256
N_MIX_BLOCKS = N_FFN_BLOCKS // 2
MP = 2 * TM_P
MS = 2 * TM_S
PROJ_TN = 1024
CONV_TC = 256
OUT_TN = 512
MLSTM_CHUNK = 256
MLSTM_SEQS = 2
SAMPLE_BS = 8
VMEM_LIMIT = 62 * 1024 * 1024


def _rms(x, g):
    return x * lax.rsqrt(jnp.mean(x * x, axis=-1, keepdims=True) + EPS) * g


def _soft_cap(x):
    return GATE_CAP * jnp.tanh(x / GATE_CAP)


def _log_sigmoid(x):
    return -jax.nn.softplus(-x)


def _bdot(a, b):
    return jnp.dot(a, b, preferred_element_type=F32)


def _bdot_t(a, b):
    return lax.dot_general(a, b, (((1,), (1,)), ((), ())), preferred_element_type=F32)


def _sample_state_update(f, c_ref, wk_ref, v_ref, dec_ref, c_out):
    r = jnp.minimum(f, TM_S - 1)
    seq = pl.program_id(0) * TM_S + r
    lane = lax.broadcasted_iota(jnp.int32, (D_A, LANE), 1)
    wk_col = jnp.sum(jnp.where(lane == seq, wk_ref[...], 0.0), axis=1, keepdims=True)
    v_row = v_ref[pl.ds(r, 1), :]
    dec = dec_ref[pl.ds(r, 1), :]
    for h in range(H_A):
        hs = slice(h * DK, (h + 1) * DK)
        c_out[0, 0, h] = dec[:, h:h + 1] * c_ref[0, 0, h] + wk_col[hs, :] * v_row[:, hs]


def _ffn_kernel(xp_ref, xs_ref, g_ref, wg_ref, wu_ref, wd_ref, g2_ref, *rest,
                n_f, last_valid, final_norm):
    if final_norm:
        c_ref, wk_ref, v_ref, dec_ref, op_ref, os_ref, c_out, xn_ref = rest
    else:
        op_ref, os_ref, nxt_ref, xn_ref = rest
    f = pl.program_id(1)

    def step(first, with_state):
        if first:
            g = g_ref[...]
            xn_ref[:TM_P, :] = _rms(xp_ref[...], g).astype(BF16)
            xn_ref[TM_P:, :] = _rms(xs_ref[0], g).astype(BF16)
        xn = xn_ref[...]
        a = _bdot(xn, wg_ref[...].astype(BF16))
        b = _bdot(xn, wu_ref[...].astype(BF16))
        hid = a * jax.nn.sigmoid(a) * b * 0.5
        wd = wd_ref[...]
        if not first:
            valid = jnp.where(f == n_f - 1, last_valid, TF)
            hid = jnp.where(lax.broadcasted_iota(jnp.int32, (1, TF), 1) < valid, hid, 0.0)
            wd = jnp.where(lax.broadcasted_iota(jnp.int32, (TF, 1), 0) < valid, wd, 0.0)
        r = _bdot(hid.astype(BF16), wd.astype(BF16))
        if first:
            op_ref[...] = xp_ref[...] + r[:TM_P]
            os_ref[0] = xs_ref[0] + r[TM_P:]
        else:
            op_ref[...] += r[:TM_P]
            os_ref[0] += r[TM_P:]
        if with_state:
            _sample_state_update(f, c_ref, wk_ref, v_ref, dec_ref, c_out)

    assert n_f - 1 >= TM_S, "one sample sequence per step after the first"

    @pl.when(f == 0)
    def _():
        step(True, final_norm)

    @pl.when(f > 0)
    def _():
        step(False, final_norm)

    @pl.when(f == n_f - 1)
    def _():
        g2 = g2_ref[...]
        if final_norm:
            op_ref[...] = _rms(op_ref[...], g2)
            os_ref[0] = _rms(os_ref[0], g2)
        else:
            nxt_ref[0, :TM_P, :] = _rms(op_ref[...], g2).astype(BF16)
            nxt_ref[0, TM_P:, :] = _rms(os_ref[0], g2).astype(BF16)


def _ffn(xp, xs, g, wg, wu, wd, g2, state=None):
    final_norm = state is not None
    n_f = pl.cdiv(D_FF, TF)
    last_valid = D_FF - (n_f - 1) * TF
    kern = functools.partial(_ffn_kernel, n_f=n_f, last_valid=last_valid, final_norm=final_norm)
    in_specs = [
        pl.BlockSpec((TM_P, D_MODEL), lambda i, f: (i, 0)),
        pl.BlockSpec((1, TM_S, D_MODEL), lambda i, f: (i, 0, 0)),
        pl.BlockSpec((1, D_MODEL), lambda i, f: (0, 0)),
        pl.BlockSpec((None, D_MODEL, TF), lambda i, f: (0, 0, f)),
        pl.BlockSpec((None, D_MODEL, TF), lambda i, f: (0, 0, f)),
        pl.BlockSpec((None, TF, D_MODEL), lambda i, f: (0, f, 0)),
        pl.BlockSpec((1, D_MODEL), lambda i, f: (0, 0)),
    ]
    out_specs = [
        pl.BlockSpec((TM_P, D_MODEL), lambda i, f: (i, 0)),
        pl.BlockSpec((1, TM_S, D_MODEL), lambda i, f: (i, 0, 0)),
    ]
    out_shape = [
        jax.ShapeDtypeStruct((N_FFN_BLOCKS * TM_P, D_MODEL), F32),
        jax.ShapeDtypeStruct((N_FFN_BLOCKS, TM_S, D_MODEL), F32),
    ]
    args = [xp, xs, g, wg, wu, wd, g2]
    if final_norm:
        c0, wk, za_s, dec = state
        seq = lambda i, f: i * TM_S + jnp.minimum(f, TM_S - 1)
        cspec = pl.BlockSpec((1, 1, H_A, DK, DV), lambda i, f: (0, seq(i, f), 0, 0, 0))
        in_specs += [
            cspec,
            pl.BlockSpec((D_A, LANE), lambda i, f: (0, 0)),
            pl.BlockSpec((TM_S, D_A), lambda i, f: (i, 2)),
            pl.BlockSpec((TM_S, LANE), lambda i, f: (i, 0)),
        ]
        out_specs.append(cspec)
        out_shape.append(jax.ShapeDtypeStruct(c0.shape, F32))
        args += [c0, wk, za_s, dec]
    else:
        out_specs.append(pl.BlockSpec((1, TM, D_MODEL), lambda i, f: (i, 0, 0)))
        out_shape.append(jax.ShapeDtypeStruct((N_FFN_BLOCKS, TM, D_MODEL), BF16))
    return pl.pallas_call(
        kern,
        grid=(N_FFN_BLOCKS, n_f),
        in_specs=in_specs,
        out_specs=out_specs,
        out_shape=out_shape,
        scratch_shapes=[pltpu.VMEM((TM, D_MODEL), BF16)],
        compiler_params=pltpu.CompilerParams(
            dimension_semantics=("parallel", "arbitrary"), vmem_limit_bytes=VMEM_LIMIT),
        name="ffn_final" if final_norm else "ffn",
    )(*args)


def _split_rows(z0, z1, p_ref, s_ref):
    p_ref[:TM_P, :] = z0[:TM_P]
    p_ref[TM_P:, :] = z1[:TM_P]
    s_ref[0, :TM_S, :] = z0[TM_P:]
    s_ref[0, TM_S:, :] = z1[TM_P:]


def _proj_kernel(xn_ref, w_ref, wg_ref, zp_ref, zs_ref, gp_ref, gs_ref):
    w = w_ref[...].astype(BF16)
    _split_rows(_bdot_t(xn_ref[0], w), _bdot_t(xn_ref[1], w), zp_ref, zs_ref)

    @pl.when(pl.program_id(1) == 0)
    def _():
        wg = wg_ref[...].astype(BF16)
        _split_rows(_bdot_t(xn_ref[0], wg), _bdot_t(xn_ref[1], wg), gp_ref, gs_ref)


def _proj(xn_c, w_t):
    n_n = (4 * D_A) // PROJ_TN
    return pl.pallas_call(
        _proj_kernel,
        grid=(N_MIX_BLOCKS, n_n),
        in_specs=[
            pl.BlockSpec((2, TM, D_MODEL), lambda i, j: (i, 0, 0)),
            pl.BlockSpec((PROJ_TN, D_MODEL), lambda i, j: (j, 0)),
            pl.BlockSpec((LANE, D_MODEL), lambda i, j: (GATE_ROW0 // LANE, 0)),
        ],
        out_specs=[
            pl.BlockSpec((MP, PROJ_TN), lambda i, j: (i, j)),
            pl.BlockSpec((1, MS, PROJ_TN), lambda i, j: (i, 0, j)),
            pl.BlockSpec((MP, LANE), lambda i, j: (i, 0)),
            pl.BlockSpec((1, MS, LANE), lambda i, j: (i, 0, 0)),
        ],
        out_shape=[
            jax.ShapeDtypeStruct((N_MIX_BLOCKS * MP, 4 * D_A), F32),
            jax.ShapeDtypeStruct((N_MIX_BLOCKS, MS, 4 * D_A), F32),
            jax.ShapeDtypeStruct((N_MIX_BLOCKS * MP, LANE), F32),
            jax.ShapeDtypeStruct((N_MIX_BLOCKS, MS, LANE), F32),
        ],
        compiler_params=pltpu.CompilerParams(
            dimension_semantics=("parallel", "arbitrary"), vmem_limit_bytes=VMEM_LIMIT),
        name="proj_qkvo",
    )(xn_c, w_t, w_t)


def _group_norm(yb, nrm):
    gw = D_B // G_B
    parts = []
    for g in range(yb.shape[1] // gw):
        seg = yb[:, g * gw:(g + 1) * gw]
        parts.append(seg * lax.rsqrt(jnp.mean(seg * seg, axis=-1, keepdims=True) + EPS))
    return jnp.concatenate(parts, axis=1) * nrm


def _conv_kernel(xn_ref, wgb_ref, wgc_ref, wxc_ref, cw_ref, cb_ref, nrm_ref, buf_ref, q_ref, c_ref,
                 yb_ref, cp_ref, cs_ref, qc_ref):
    q_t = jnp.concatenate([q_ref[...], jnp.zeros((LANE - SAMPLE_BS, D_A), F32)], axis=0).T
    for h in range(H_A):
        hs = slice(h * DK, (h + 1) * DK)
        for j in range(SAMPLE_BS):
            qc_ref[j:j + 1, hs] = jnp.sum(q_t[hs, j:j + 1] * c_ref[0, j, h], axis=0, keepdims=True)

    wgb = wgb_ref[...].astype(BF16)
    wgc = wgc_ref[...].astype(BF16)
    wxc = wxc_ref[...].astype(BF16)
    cw0 = cw_ref[0:1, :]
    cw1 = cw_ref[1:2, :]
    cw2 = cw_ref[2:3, :]
    cb = cb_ref[...]
    nrm = nrm_ref[...]
    row = lax.broadcasted_iota(jnp.int32, (TM_P, wgb.shape[0]), 0)
    zero_row = jnp.zeros((1, wgb.shape[0]), F32)
    um2, um1 = zero_row, zero_row

    for t in range(2):
        xn = xn_ref[t]
        u = _bdot_t(xn, wgc) * _bdot_t(xn, wxc)
        gb = _bdot_t(xn, wgb)

        up = u[:TM_P]
        u1 = jnp.where(row < 1, um1, pltpu.roll(up, 1, 0))
        u2 = jnp.where(row < 1, um2, jnp.where(row < 2, um1, pltpu.roll(up, 2, 0)))
        yc = cw0 * u2 + cw1 * u1 + cw2 * up + cb
        yb_ref[0, t * TM_P:(t + 1) * TM_P, :] = _group_norm(gb[:TM_P] * yc, nrm).astype(yb_ref.dtype)
        um2, um1 = up[TM_P - 2:TM_P - 1, :], up[TM_P - 1:TM_P, :]

        ss = slice(t * TM_S, (t + 1) * TM_S)
        us = u[TM_P:]
        b0 = buf_ref[0, ss, 0, :]
        b1 = buf_ref[0, ss, 1, :]
        ycs = cw0 * b0 + cw1 * b1 + cw2 * us + cb
        yb_ref[0, MP + t * TM_S:MP + (t + 1) * TM_S, :] = _group_norm(gb[TM_P:] * ycs, nrm).astype(yb_ref.dtype)
        cs_ref[0, ss, 0, :] = b1
        cs_ref[0, ss, 1, :] = us

    cp_ref[0, 0, 0:1, :] = um2
    cp_ref[0, 0, 1:2, :] = um1


def _conv(xn_c, w_t, cw, cb_row, nrm_row, buf, za_s, c0):
    tc = CONV_TC
    n_c = D_B // tc
    assert N_MIX_BLOCKS * n_c * SAMPLE_BS == za_s.shape[0], "one sample group per grid step"
    wspec = lambda base: pl.BlockSpec(
        (pl.Element(tc), pl.Element(D_MODEL)), lambda i, c: (pl.multiple_of(base + c * tc, 8), 0))
    grp = lambda i, c: i * n_c + c
    return pl.pallas_call(
        _conv_kernel,
        grid=(N_MIX_BLOCKS, n_c),
        in_specs=[
            pl.BlockSpec((2, TM, D_MODEL), lambda i, c: (i, 0, 0)),
            wspec(CONV_ROW0),
            wspec(CONV_ROW0 + D_B),
            wspec(CONV_ROW0 + 2 * D_B),
            pl.BlockSpec((3, tc), lambda i, c: (0, c)),
            pl.BlockSpec((1, tc), lambda i, c: (0, c)),
            pl.BlockSpec((1, tc), lambda i, c: (0, c)),
            pl.BlockSpec((1, MS, 2, tc), lambda i, c: (i, 0, 0, c)),
            pl.BlockSpec((SAMPLE_BS, D_A), lambda i, c: (grp(i, c), 0)),
            pl.BlockSpec((1, SAMPLE_BS, H_A, DK, DV), lambda i, c: (0, grp(i, c), 0, 0, 0)),
        ],
        out_specs=[
            pl.BlockSpec((1, MP + MS, tc), lambda i, c: (i, 0, c)),
            pl.BlockSpec((1, 1, 2, tc), lambda i, c: (0, i, 0, c)),
            pl.BlockSpec((1, MS, 2, tc), lambda i, c: (i, 0, 0, c)),
            pl.BlockSpec((SAMPLE_BS, D_A), lambda i, c: (grp(i, c), 0)),
        ],
        out_shape=[
            jax.ShapeDtypeStruct((N_MIX_BLOCKS, MP + MS, D_B), BF16),
            jax.ShapeDtypeStruct((1, N_MIX_BLOCKS, 2, D_B), F32),
            jax.ShapeDtypeStruct((N_MIX_BLOCKS, MS, 2, D_B), F32),
            jax.ShapeDtypeStruct((za_s.shape[0], D_A), F32),
        ],
        compiler_params=pltpu.CompilerParams(
            dimension_semantics=("parallel", "arbitrary"), vmem_limit_bytes=VMEM_LIMIT),
        name="proj_conv",
    )(xn_c, w_t, w_t, w_t, cw, cb_row, nrm_row, buf, za_s, c0)


def _mlstm_prompt_kernel(q_ref, k_ref, v_ref, o_ref, g_ref, bias_ref, nrm_ref,
                         ha_ref, c_ref, n_ref, m_ref, cx_s, m_s, *, n_chunks):
    L = MLSTM_CHUNK
    c = pl.program_id(1)

    @pl.when(c == 0)
    def _():
        cx_s[...] = jnp.zeros_like(cx_s)
        m_s[...] = jnp.zeros_like(m_s)

    row = lax.broadcasted_iota(jnp.int32, (L, L), 0)
    col = lax.broadcasted_iota(jnp.int32, (L, L), 1)
    causal = row >= col
    tri = causal.astype(BF16)

    for b in range(MLSTM_SEQS):
        _mlstm_prompt_seq(b, causal, tri, q_ref, k_ref, v_ref, o_ref, g_ref, bias_ref, nrm_ref,
                          ha_ref, cx_s, m_s)

    @pl.when(c == n_chunks - 1)
    def _():
        for b in range(MLSTM_SEQS):
            m_ref[b] = m_s[b]
            for h in range(H_A):
                cx = cx_s[b * H_A + h]
                c_ref[0, b, h] = cx[:, :DV]
                n_ref[0, b, h:h + 1, :] = cx[:, DV:].T[h:h + 1, :]


def _prefix_max_rows(x):
    rows = x.shape[0]
    row = lax.broadcasted_iota(jnp.int32, x.shape, 0)
    k = 1
    while k < rows:
        x = jnp.where(row >= k, jnp.maximum(x, pltpu.roll(x, k, 0)), x)
        k *= 2
    return x


def _mlstm_prompt_seq(b, causal, tri, q_ref, k_ref, v_ref, o_ref, g_ref, bias_ref, nrm_ref,
                      ha_ref, cx_s, m_s):
    L = MLSTM_CHUNK
    lane = lax.broadcasted_iota(jnp.int32, (L, LANE), 1)
    lane_dv = lax.broadcasted_iota(jnp.int32, (DV, LANE), 1)
    capped = _soft_cap(g_ref[b] + bias_ref[...])
    lf = _log_sigmoid(capped)
    hi = lf.astype(BF16)
    r1 = lf - hi.astype(F32)
    mid = r1.astype(BF16)
    lo = (r1 - mid.astype(F32)).astype(BF16)
    parts = _bdot(tri, jnp.concatenate([hi, mid, lo], axis=1))
    bc = parts[:, :LANE] + parts[:, LANE:2 * LANE] + parts[:, 2 * LANE:]
    bh = pltpu.roll(bc, LANE - H_A, 1)
    a = capped - bh
    a_t = a.T
    m_prev = m_s[b]
    big_m = jnp.maximum(_prefix_max_rows(a), m_prev[0:1, :])
    sc_all = jnp.exp(m_prev[0:1, :] - big_m)
    emt_all = jnp.exp(-(bh + big_m))
    m_last = big_m[L - 1:L, :]
    w_all = jnp.exp(a - m_last)
    decay_all = sc_all[L - 1:L, :]
    m_s[b] = jnp.broadcast_to(bh[L - 1:L, :] + m_last, (8, LANE))

    nums = []
    nd_tail = jnp.zeros((L, LANE), F32)
    sq_tail = jnp.zeros((L, LANE), F32)
    for h in range(H_A):
        hs = slice(h * DK, (h + 1) * DK)
        onehot = (lane == h).astype(BF16)
        qb = q_ref[b, :, hs].astype(BF16)
        kf = k_ref[b, :, hs] * (DK ** -0.5)
        kb = kf.astype(BF16)
        vx = jnp.concatenate([v_ref[b, :, hs].astype(BF16), onehot], axis=1)
        cx_prev = cx_s[b * H_A + h]

        p = jnp.exp(jnp.where(causal, a_t[h:h + 1, :] - big_m[:, h:h + 1], -jnp.inf))
        s = _bdot_t(qb, kb) * p
        nd = sc_all[:, h:h + 1] * _bdot(qb, cx_prev.astype(BF16)) + _bdot(s.astype(BF16), vx)
        num = nd[:, :DV]
        nums.append(num)
        nd_tail = nd_tail + nd[:, DV:]
        sq_tail = sq_tail + _bdot((num * num).astype(BF16), (lane_dv == h).astype(BF16))

        wk = (w_all[:, h:h + 1] * kf).astype(BF16)
        cx_s[b * H_A + h] = decay_all[:, h:h + 1] * cx_prev + lax.dot_general(
            wk, vx, (((0,), (0,)), ((), ())), preferred_element_type=F32)

    inv = 1.0 / jnp.maximum(jnp.abs(nd_tail), emt_all)
    ms = inv * inv * sq_tail * (1.0 / DV)
    scale_all = inv * lax.rsqrt(ms + EPS)
    for h in range(H_A):
        hs = slice(h * DK, (h + 1) * DK)
        hn = nums[h] * scale_all[:, h:h + 1]
        ha_ref[b, :, hs] = (hn * nrm_ref[:, hs] * jax.nn.sigmoid(o_ref[b, :, hs])).astype(ha_ref.dtype)


def _mlstm_prompt(za_p, g_p, bias_row, nrm_row, batch, seq):
    L = MLSTM_CHUNK
    nsq = MLSTM_SEQS
    n_chunks = seq // L
    kern = functools.partial(_mlstm_prompt_kernel, n_chunks=n_chunks)
    za3 = za_p.reshape(batch, seq, 4 * D_A)
    g3 = g_p.reshape(batch, seq, LANE)
    zspec = lambda part: pl.BlockSpec((nsq, L, D_A), lambda b, c: (b, c, part))
    return pl.pallas_call(
        kern,
        grid=(batch // nsq, n_chunks),
        in_specs=[
            zspec(0), zspec(1), zspec(2), zspec(3),
            pl.BlockSpec((nsq, L, LANE), lambda b, c: (b, c, 0)),
            pl.BlockSpec((1, LANE), lambda b, c: (0, 0)),
            pl.BlockSpec((1, D_A), lambda b, c: (0, 0)),
        ],
        out_specs=[
            pl.BlockSpec((nsq, L, D_A), lambda b, c: (b, c, 0)),
            pl.BlockSpec((1, nsq, H_A, DK, DV), lambda b, c: (0, b, 0, 0, 0)),
            pl.BlockSpec((1, nsq, H_A, DK), lambda b, c: (0, b, 0, 0)),
            pl.BlockSpec((nsq, 8, LANE), lambda b, c: (b, 0, 0)),
        ],
        out_shape=[
            jax.ShapeDtypeStruct((batch, seq, D_A), BF16),
            jax.ShapeDtypeStruct((1, batch, H_A, DK, DV), F32),
            jax.ShapeDtypeStruct((1, batch, H_A, DK), F32),
            jax.ShapeDtypeStruct((batch, 8, LANE), F32),
        ],
        scratch_shapes=[pltpu.VMEM((nsq * H_A, DK, DV + LANE), F32),
                        pltpu.VMEM((nsq, 8, LANE), F32)],
        compiler_params=pltpu.CompilerParams(
            dimension_semantics=("parallel", "arbitrary"), vmem_limit_bytes=VMEM_LIMIT),
        name="mlstm_prompt",
    )(za3, za3, za3, za3, g3, bias_row, nrm_row)


def _expand_heads(x, width):
    rows = x.shape[0]
    return jnp.concatenate([jnp.broadcast_to(x[:, h:h + 1], (rows, width)) for h in range(H_A)], axis=1)


def _head_sums(x):
    return jnp.concatenate(
        [jnp.sum(x[:, h * DK:(h + 1) * DK], axis=-1, keepdims=True) for h in range(H_A)], axis=1)


def _mlstm_sample_kernel(za_ref, qc_ref, g_ref, bias_ref, nrm_ref, m_ref, n_ref,
                         ha_ref, m_out, n_out, wk_out, dec_out):
    nb = za_ref.shape[0]
    capped = _soft_cap(g_ref[...] + bias_ref[...])
    logi = capped[:, 0:H_A]
    logf = _log_sigmoid(capped)[:, H_A:2 * H_A]
    m_prev = m_ref[...]
    m_inter = logf + m_prev
    m_t = jnp.maximum(m_inter, logi)
    sc = jnp.exp(m_inter - m_t)
    ei = jnp.exp(logi - m_t)
    emt = jnp.exp(-m_t)

    q = za_ref[:, 0:D_A]
    k = za_ref[:, D_A:2 * D_A] * (DK ** -0.5)
    v = za_ref[:, 2 * D_A:3 * D_A]
    o = za_ref[:, 3 * D_A:4 * D_A]
    n_prev = n_ref[...]

    s = _head_sums(q * k) * ei
    den = sc * _head_sums(q * n_prev) + s
    denom = jnp.maximum(jnp.abs(den), emt)
    num = _expand_heads(sc, DV) * qc_ref[...] + _expand_heads(s, DV) * v
    hh = num / _expand_heads(denom, DV)
    ms = _head_sums(hh * hh) * (1.0 / DV)
    hn = hh * lax.rsqrt(_expand_heads(ms, DV) + EPS)
    ha_ref[...] = (hn * nrm_ref[...] * jax.nn.sigmoid(o)).astype(ha_ref.dtype)
    n_out[...] = _expand_heads(sc, DK) * n_prev + _expand_heads(ei, DK) * k
    m_out[...] = m_t
    pad = jnp.zeros((nb, LANE - H_A), F32)
    dec_out[...] = jnp.concatenate([sc, pad], axis=1)

    k_t = k.T
    ei_t = jnp.concatenate([ei, pad], axis=1).T
    for h in range(H_A):
        hs = slice(h * DK, (h + 1) * DK)
        wk_out[hs, :] = k_t[hs, :] * ei_t[h:h + 1, :]


def _mlstm_sample(za_s, qc, g_s, bias_row, nrm_row, m0, n0):
    nb = za_s.shape[0]
    assert nb == LANE, "the sequence axis is transposed onto the lanes"
    return pl.pallas_call(
        _mlstm_sample_kernel,
        out_shape=[
            jax.ShapeDtypeStruct((nb, D_A), BF16),
            jax.ShapeDtypeStruct((nb, H_A), F32),
            jax.ShapeDtypeStruct((nb, H_A * DK), F32),
            jax.ShapeDtypeStruct((D_A, nb), F32),
            jax.ShapeDtypeStruct((nb, LANE), F32),
        ],
        compiler_params=pltpu.CompilerParams(vmem_limit_bytes=VMEM_LIMIT),
        name="mlstm_sample",
    )(za_s, qc, g_s, bias_row, nrm_row, m0, n0)


def _outproj_kernel(hp_ref, hs_ref, ap_ref, as_ref, yb_ref, w_ref, op_ref, os_ref, wc_ref):
    j = pl.program_id(1)

    @pl.when(pl.program_id(0) == 0)
    def _():
        wc_ref[j] = w_ref[...].astype(BF16)

    wa = wc_ref[j, :D_A, :]
    wb = wc_ref[j, D_A:, :]
    op_ref[...] = hp_ref[...] + _bdot(ap_ref[...], wa) + _bdot(yb_ref[0, :MP, :], wb)
    os_ref[0] = hs_ref[0] + _bdot(as_ref[0], wa) + _bdot(yb_ref[0, MP:, :], wb)


def _outproj(hp, hs, ap, a_s, yb_c, w3):
    tn = OUT_TN
    n_n = D_MODEL // tn
    return pl.pallas_call(
        _outproj_kernel,
        grid=(N_MIX_BLOCKS, n_n),
        in_specs=[
            pl.BlockSpec((MP, tn), lambda i, j: (i, j)),
            pl.BlockSpec((1, MS, tn), lambda i, j: (i, 0, j)),
            pl.BlockSpec((MP, D_A), lambda i, j: (i, 0)),
            pl.BlockSpec((1, MS, D_A), lambda i, j: (i, 0, 0)),
            pl.BlockSpec((1, MP + MS, D_B), lambda i, j: (i, 0, 0)),
            pl.BlockSpec((None, D_A + D_B, tn), lambda i, j: (0, 0, jnp.where(i == 0, j, n_n - 1))),
        ],
        out_specs=[
            pl.BlockSpec((MP, tn), lambda i, j: (i, j)),
            pl.BlockSpec((1, MS, tn), lambda i, j: (i, 0, j)),
        ],
        out_shape=[
            jax.ShapeDtypeStruct((N_MIX_BLOCKS * MP, D_MODEL), F32),
            jax.ShapeDtypeStruct((N_MIX_BLOCKS, MS, D_MODEL), F32),
        ],
        scratch_shapes=[pltpu.VMEM((n_n, D_A + D_B, tn), BF16)],
        compiler_params=pltpu.CompilerParams(
            dimension_semantics=("arbitrary", "arbitrary"), vmem_limit_bytes=VMEM_LIMIT),
        name="outproj",
    )(hp, hs, ap, a_s, yb_c, w3)


def kernel(x_prompt, x_sample, state_mlstm_C, state_mlstm_n, state_mlstm_m, state_conv, norm_ffn1, ffn1_gate, ffn1_up, ffn1_down, norm_mix, w_in, b_gates, conv_w, conv_b, norm_mlstm, norm_conv, w_out, norm_ffn2, ffn2_gate, ffn2_up, ffn2_down, norm_final):
    batch, seq, _ = x_prompt.shape
    nb = x_sample.shape[0]
    assert batch == N_MIX_BLOCKS and seq == MP and nb == N_MIX_BLOCKS * MS
    assert norm_ffn1.shape[0] == 1, "single-layer trunk"

    xp = x_prompt.reshape(batch * seq, D_MODEL)
    xs = x_sample.reshape(N_FFN_BLOCKS, TM_S, D_MODEL)

    h1p, h1s, xn_c = _ffn(xp, xs, norm_ffn1, ffn1_gate, ffn1_up, ffn1_down, norm_mix)

    w_t = jnp.swapaxes(w_in[0], 0, 1)
    za_p, za_s, g_p, g_s = _proj(xn_c, w_t)
    za_s = za_s.reshape(nb, 4 * D_A)
    yb_c, conv_p, conv_s, qc_s = _conv(xn_c, w_t, conv_w[0], conv_b, norm_conv,
                                       state_conv.reshape(N_MIX_BLOCKS, MS, 2, D_B), za_s, state_mlstm_C)

    bias_row = jnp.zeros((1, LANE), F32).at[0, :2 * H_A].set(b_gates[0].astype(F32))
    ha_p, c_p, n_p, m_p = _mlstm_prompt(za_p, g_p, bias_row, norm_mlstm, batch, seq)
    ha_s, m_s, n_s, wk_s, dec_s = _mlstm_sample(
        za_s, qc_s, g_s.reshape(nb, LANE), bias_row, norm_mlstm,
        state_mlstm_m.reshape(nb, H_A), state_mlstm_n.reshape(nb, H_A * DK))

    h2p, h2s = _outproj(h1p, h1s.reshape(N_MIX_BLOCKS, MS, D_MODEL), ha_p.reshape(batch * seq, D_A),
                        ha_s.reshape(N_MIX_BLOCKS, MS, D_A), yb_c, w_out)

    yp, ys, c_s = _ffn(h2p, h2s.reshape(N_FFN_BLOCKS, TM_S, D_MODEL), norm_ffn2, ffn2_gate, ffn2_up,
                       ffn2_down, norm_final.reshape(1, D_MODEL),
                       state=(state_mlstm_C, wk_s, za_s, dec_s))

    return (
        yp.reshape(batch, seq, D_MODEL),
        ys.reshape(nb, 1, D_MODEL),
        c_p,
        n_p,
        m_p[:, 0, :H_A].reshape(1, batch, H_A),
        conv_p,
        c_s,
        n_s.reshape(1, nb, H_A, DK),
        m_s.reshape(1, nb, H_A),
        conv_s.reshape(1, nb, 2, D_B),
    )
```

```python
import functools

import jax
import jax.numpy as jnp
from jax import lax
from jax.experimental import pallas as pl
from jax.experimental.pallas import tpu as pltpu

F32 = jnp.float32
BF16 = jnp.bfloat16

D_MODEL = 2048
D_A = 1024
D_B = 1024
H_A = 4
DK = 256
DV = 256
G_B = 8
D_FF = 5504
GATE_CAP = 15.0
EPS = 1e-6
GATE_ROW0 = 4 * D_A
CONV_ROW0 = 4 * D_A + 2 * H_A

LANE = 128
N_FFN_BLOCKS = 8
TM_P = 1024
TM_S = 16
TM = TM_P + TM_S
TF1 = 512
TF2 = 256
X_CHUNK = 128
N_X_CHUNKS = TM_P // X_CHUNK
X_AHEAD0 = 2
N_MIX_BLOCKS = N_FFN_BLOCKS // 2
MP = 2 * TM_P
MS = 2 * TM_S
PROJ_TN = 1024
CONV_TC = 256
OUT_TN = 512
MLSTM_CHUNK = 256
MLSTM_SEQS = 2
SAMPLE_BS = 8
VMEM_LIMIT = 62 * 1024 * 1024


def _rms(x, g):
    return x * lax.rsqrt(jnp.mean(x * x, axis=-1, keepdims=True) + EPS) * g


def _soft_cap(x):
    return GATE_CAP * jnp.tanh(x / GATE_CAP)


def _log_sigmoid(x):
    return -jax.nn.softplus(-x)


def _bdot(a, b):
    return jnp.dot(a, b, preferred_element_type=F32)


def _bdot_t(a, b):
    return lax.dot_general(a, b, (((1,), (1,)), ((), ())), preferred_element_type=F32)


def _sample_state_update(f, c_ref, wk_ref, v_ref, dec_ref, c_out):
    r = jnp.minimum(f, TM_S - 1)
    seq = pl.program_id(0) * TM_S + r
    lane = lax.broadcasted_iota(jnp.int32, (D_A, LANE), 1)
    wk_col = jnp.sum(jnp.where(lane == seq, wk_ref[...], 0.0), axis=1, keepdims=True)
    v_row = v_ref[pl.ds(r, 1), :]
    dec = dec_ref[pl.ds(r, 1), :]
    for h in range(H_A):
        hs = slice(h * DK, (h + 1) * DK)
        c_out[0, 0, h] = dec[:, h:h + 1] * c_ref[0, 0, h] + wk_col[hs, :] * v_row[:, hs]


def _ffn_kernel(x_hbm, xs_ref, g_ref, wg_ref, wu_ref, wd_ref, g2_ref, *rest,
                tf, n_f, last_valid, final_norm):
    if final_norm:
        c_ref, wk_ref, v_ref, dec_ref, out_hbm, os_ref, c_out, xn_ref, acc_ref, x_sem, o_sem = rest
    else:
        out_hbm, os_ref, nxt_ref, xn_ref, acc_ref, x_sem, o_sem = rest
    i = pl.program_id(0)
    f = pl.program_id(1)
    slot = lax.rem(i, 2)
    op_ref = acc_ref.at[slot]

    def x_copy(blk, s, c):
        dst0 = pl.multiple_of(c * X_CHUNK, X_CHUNK)
        src0 = pl.multiple_of(blk * TM_P + c * X_CHUNK, X_CHUNK)
        return pltpu.make_async_copy(x_hbm.at[pl.ds(src0, X_CHUNK), :],
                                     acc_ref.at[s, pl.ds(dst0, X_CHUNK), :], x_sem.at[c])

    def out_copy(blk, s):
        row0 = pl.multiple_of(blk * TM_P, TM_P)
        return pltpu.make_async_copy(acc_ref.at[s], out_hbm.at[pl.ds(row0, TM_P), :], o_sem.at[s])

    @pl.when(jnp.logical_and(i == 0, f == 0))
    def _():
        for c in range(N_X_CHUNKS):
            x_copy(0, 0, c).start()

    @pl.when(f == 0)
    def _():
        for c in range(N_X_CHUNKS):
            x_copy(i, slot, c).wait()

    @pl.when(jnp.logical_and(i > 0, f == X_AHEAD0 - 1))
    def _():
        out_copy(i - 1, 1 - slot).wait()

    @pl.when(jnp.logical_and(jnp.logical_and(f >= X_AHEAD0, f < X_AHEAD0 + N_X_CHUNKS),
                             i < N_FFN_BLOCKS - 1))
    def _():
        x_copy(i + 1, 1 - slot, f - X_AHEAD0).start()

    def step(valid, first, with_state):
        if first:
            g = g_ref[...]
            xn_ref[:TM_P, :] = _rms(op_ref[...], g).astype(BF16)
            xn_ref[TM_P:, :] = _rms(xs_ref[0], g).astype(BF16)
        xn = xn_ref[...]
        a = _bdot(xn, wg_ref[:, :valid].astype(BF16))
        b = _bdot(xn, wu_ref[:, :valid].astype(BF16))
        hid = (a * jax.nn.sigmoid(a) * b * 0.5).astype(BF16)
        r = _bdot(hid, wd_ref[:valid, :].astype(BF16))
        op_ref[...] += r[:TM_P]
        if first:
            os_ref[0] = xs_ref[0] + r[TM_P:]
        else:
            os_ref[0] += r[TM_P:]
        if with_state:
            _sample_state_update(f, c_ref, wk_ref, v_ref, dec_ref, c_out)

    assert last_valid < tf and (not final_norm or n_f - 1 >= TM_S), "state update rides the full-width steps"

    @pl.when(f == 0)
    def _():
        step(tf, True, final_norm)

    @pl.when(jnp.logical_and(f > 0, f < n_f - 1))
    def _():
        step(tf, False, final_norm)

    @pl.when(f == n_f - 1)
    def _():
        step(last_valid, False, False)
        g2 = g2_ref[...]
        if final_norm:
            op_ref[...] = _rms(op_ref[...], g2)
            os_ref[0] = _rms(os_ref[0], g2)
        else:
            nxt_ref[0, :TM_P, :] = _rms(op_ref[...], g2).astype(BF16)
            nxt_ref[0, TM_P:, :] = _rms(os_ref[0], g2).astype(BF16)

        out_copy(i, slot).start()

        @pl.when(i == N_FFN_BLOCKS - 1)
        def _():
            out_copy(i, slot).wait()


def _ffn(xp, xs, g, wg, wu, wd, g2, tf, state=None):
    final_norm = state is not None
    n_f = pl.cdiv(D_FF, tf)
    last_valid = D_FF - (n_f - 1) * tf
    kern = functools.partial(_ffn_kernel, tf=tf, n_f=n_f, last_valid=last_valid, final_norm=final_norm)
    assert X_AHEAD0 + N_X_CHUNKS <= n_f and N_X_CHUNKS * X_CHUNK == TM_P
    in_specs = [
        pl.BlockSpec(memory_space=pl.ANY),
        pl.BlockSpec((1, TM_S, D_MODEL), lambda i, f: (i, 0, 0)),
        pl.BlockSpec((1, D_MODEL), lambda i, f: (0, 0)),
        pl.BlockSpec((None, D_MODEL, tf), lambda i, f: (0, 0, f)),
        pl.BlockSpec((None, D_MODEL, tf), lambda i, f: (0, 0, f)),
        pl.BlockSpec((None, tf, D_MODEL), lambda i, f: (0, f, 0)),
        pl.BlockSpec((1, D_MODEL), lambda i, f: (0, 0)),
    ]
    out_specs = [
        pl.BlockSpec(memory_space=pl.ANY),
        pl.BlockSpec((1, TM_S, D_MODEL), lambda i, f: (i, 0, 0)),
    ]
    out_shape = [
        jax.ShapeDtypeStruct((N_FFN_BLOCKS * TM_P, D_MODEL), F32),
        jax.ShapeDtypeStruct((N_FFN_BLOCKS, TM_S, D_MODEL), F32),
    ]
    args = [xp, xs, g, wg, wu, wd, g2]
    if final_norm:
        c0, wk, za_s, dec = state
        seq = lambda i, f: i * TM_S + jnp.minimum(f, TM_S - 1)
        cspec = pl.BlockSpec((1, 1, H_A, DK, DV), lambda i, f: (0, seq(i, f), 0, 0, 0))
        in_specs += [
            cspec,
            pl.BlockSpec((D_A, LANE), lambda i, f: (0, 0)),
            pl.BlockSpec((TM_S, D_A), lambda i, f: (i, 2)),
            pl.BlockSpec((TM_S, LANE), lambda i, f: (i, 0)),
        ]
        out_specs.append(cspec)
        out_shape.append(jax.ShapeDtypeStruct(c0.shape, F32))
        args += [c0, wk, za_s, dec]
    else:
        out_specs.append(pl.BlockSpec((1, TM, D_MODEL), lambda i, f: (i, 0, 0)))
        out_shape.append(jax.ShapeDtypeStruct((N_FFN_BLOCKS, TM, D_MODEL), BF16))
    return pl.pallas_call(
        kern,
        grid=(N_FFN_BLOCKS, n_f),
        in_specs=in_specs,
        out_specs=out_specs,
        out_shape=out_shape,
        scratch_shapes=[pltpu.VMEM((TM, D_MODEL), BF16),
                        pltpu.VMEM((2, TM_P, D_MODEL), F32),
                        pltpu.SemaphoreType.DMA((N_X_CHUNKS,)),
                        pltpu.SemaphoreType.DMA((2,))],
        compiler_params=pltpu.CompilerParams(
            dimension_semantics=("arbitrary", "arbitrary"), vmem_limit_bytes=VMEM_LIMIT),
        name="ffn_final" if final_norm else "ffn",
    )(*args)


def _split_rows(z0, z1, p_ref, s_ref):
    p_ref[:TM_P, :] = z0[:TM_P]
    p_ref[TM_P:, :] = z1[:TM_P]
    s_ref[0, :TM_S, :] = z0[TM_P:]
    s_ref[0, TM_S:, :] = z1[TM_P:]


def _proj_kernel(xn_ref, w_ref, wg_ref, zp_ref, zs_ref, gp_ref, gs_ref):
    w = w_ref[...].astype(BF16)
    _split_rows(_bdot_t(xn_ref[0], w), _bdot_t(xn_ref[1], w), zp_ref, zs_ref)

    @pl.when(pl.program_id(1) == 0)
    def _():
        wg = wg_ref[...].astype(BF16)
        _split_rows(_bdot_t(xn_ref[0], wg), _bdot_t(xn_ref[1], wg), gp_ref, gs_ref)


def _proj(xn_c, w_t):
    n_n = (4 * D_A) // PROJ_TN
    return pl.pallas_call(
        _proj_kernel,
        grid=(N_MIX_BLOCKS, n_n),
        in_specs=[
            pl.BlockSpec((2, TM, D_MODEL), lambda i, j: (i, 0, 0)),
            pl.BlockSpec((PROJ_TN, D_MODEL), lambda i, j: (j, 0)),
            pl.BlockSpec((LANE, D_MODEL), lambda i, j: (GATE_ROW0 // LANE, 0)),
        ],
        out_specs=[
            pl.BlockSpec((MP, PROJ_TN), lambda i, j: (i, j)),
            pl.BlockSpec((1, MS, PROJ_TN), lambda i, j: (i, 0, j)),
            pl.BlockSpec((MP, LANE), lambda i, j: (i, 0)),
            pl.BlockSpec((1, MS, LANE), lambda i, j: (i, 0, 0)),
        ],
        out_shape=[
            jax.ShapeDtypeStruct((N_MIX_BLOCKS * MP, 4 * D_A), F32),
            jax.ShapeDtypeStruct((N_MIX_BLOCKS, MS, 4 * D_A), F32),
            jax.ShapeDtypeStruct((N_MIX_BLOCKS * MP, LANE), F32),
            jax.ShapeDtypeStruct((N_MIX_BLOCKS, MS, LANE), F32),
        ],
        compiler_params=pltpu.CompilerParams(
            dimension_semantics=("parallel", "arbitrary"), vmem_limit_bytes=VMEM_LIMIT),
        name="proj_qkvo",
    )(xn_c, w_t, w_t)


def _group_norm(yb, nrm):
    gw = D_B // G_B
    parts = []
    for g in range(yb.shape[1] // gw):
        seg = yb[:, g * gw:(g + 1) * gw]
        parts.append(seg * lax.rsqrt(jnp.mean(seg * seg, axis=-1, keepdims=True) + EPS))
    return jnp.concatenate(parts, axis=1) * nrm


def _conv_kernel(xn_ref, wgb_ref, wgc_ref, wxc_ref, cw_ref, cb_ref, nrm_ref, buf_ref, q_ref, c_ref,
                 yb_ref, cp_ref, cs_ref, qc_ref):
    q_t = jnp.concatenate([q_ref[...], jnp.zeros((LANE - SAMPLE_BS, D_A), F32)], axis=0).T
    for h in range(H_A):
        hs = slice(h * DK, (h + 1) * DK)
        for j in range(SAMPLE_BS):
            qc_ref[j:j + 1, hs] = jnp.sum(q_t[hs, j:j + 1] * c_ref[0, j, h], axis=0, keepdims=True)

    wgb = wgb_ref[...].astype(BF16)
    wgc = wgc_ref[...].astype(BF16)
    wxc = wxc_ref[...].astype(BF16)
    cw0 = cw_ref[0:1, :]
    cw1 = cw_ref[1:2, :]
    cw2 = cw_ref[2:3, :]
    cb = cb_ref[...]
    nrm = nrm_ref[...]
    row = lax.broadcasted_iota(jnp.int32, (TM_P, wgb.shape[0]), 0)
    zero_row = jnp.zeros((1, wgb.shape[0]), F32)
    um2, um1 = zero_row, zero_row

    for t in range(2):
        xn = xn_ref[t]
        u = _bdot_t(xn, wgc) * _bdot_t(xn, wxc)
        gb = _bdot_t(xn, wgb)

        up = u[:TM_P]
        u1 = jnp.where(row < 1, um1, pltpu.roll(up, 1, 0))
        u2 = jnp.where(row < 1, um2, jnp.where(row < 2, um1, pltpu.roll(up, 2, 0)))
        yc = cw0 * u2 + cw1 * u1 + cw2 * up + cb
        yb_ref[0, t * TM_P:(t + 1) * TM_P, :] = _group_norm(gb[:TM_P] * yc, nrm).astype(yb_ref.dtype)
        um2, um1 = up[TM_P - 2:TM_P - 1, :], up[TM_P - 1:TM_P, :]

        ss = slice(t * TM_S, (t + 1) * TM_S)
        us = u[TM_P:]
        b0 = buf_ref[0, ss, 0, :]
        b1 = buf_ref[0, ss, 1, :]
        ycs = cw0 * b0 + cw1 * b1 + cw2 * us + cb
        yb_ref[0, MP + t * TM_S:MP + (t + 1) * TM_S, :] = _group_norm(gb[TM_P:] * ycs, nrm).astype(yb_ref.dtype)
        cs_ref[0, ss, 0, :] = b1
        cs_ref[0, ss, 1, :] = us

    cp_ref[0, 0, 0:1, :] = um2
    cp_ref[0, 0, 1:2, :] = um1


def _conv(xn_c, w_t, cw, cb_row, nrm_row, buf, za_s, c0):
    tc = CONV_TC
    n_c = D_B // tc
    assert N_MIX_BLOCKS * n_c * SAMPLE_BS == za_s.shape[0], "one sample group per grid step"
    wspec = lambda base: pl.BlockSpec(
        (pl.Element(tc), pl.Element(D_MODEL)), lambda i, c: (pl.multiple_of(base + c * tc, 8), 0))
    grp = lambda i, c: i * n_c + c
    return pl.pallas_call(
        _conv_kernel,
        grid=(N_MIX_BLOCKS, n_c),
        in_specs=[
            pl.BlockSpec((2, TM, D_MODEL), lambda i, c: (i, 0, 0)),
            wspec(CONV_ROW0),
            wspec(CONV_ROW0 + D_B),
            wspec(CONV_ROW0 + 2 * D_B),
            pl.BlockSpec((3, tc), lambda i, c: (0, c)),
            pl.BlockSpec((1, tc), lambda i, c: (0, c)),
            pl.BlockSpec((1, tc), lambda i, c: (0, c)),
            pl.BlockSpec((1, MS, 2, tc), lambda i, c: (i, 0, 0, c)),
            pl.BlockSpec((SAMPLE_BS, D_A), lambda i, c: (grp(i, c), 0)),
            pl.BlockSpec((1, SAMPLE_BS, H_A, DK, DV), lambda i, c: (0, grp(i, c), 0, 0, 0)),
        ],
        out_specs=[
            pl.BlockSpec((1, MP + MS, tc), lambda i, c: (i, 0, c)),
            pl.BlockSpec((1, 1, 2, tc), lambda i, c: (0, i, 0, c)),
            pl.BlockSpec((1, MS, 2, tc), lambda i, c: (i, 0, 0, c)),
            pl.BlockSpec((SAMPLE_BS, D_A), lambda i, c: (grp(i, c), 0)),
        ],
        out_shape=[
            jax.ShapeDtypeStruct((N_MIX_BLOCKS, MP + MS, D_B), BF16),
            jax.ShapeDtypeStruct((1, N_MIX_BLOCKS, 2, D_B), F32),
            jax.ShapeDtypeStruct((N_MIX_BLOCKS, MS, 2, D_B), F32),
            jax.ShapeDtypeStruct((za_s.shape[0], D_A), F32),
        ],
        compiler_params=pltpu.CompilerParams(
            dimension_semantics=("parallel", "arbitrary"), vmem_limit_bytes=VMEM_LIMIT),
        name="proj_conv",
    )(xn_c, w_t, w_t, w_t, cw, cb_row, nrm_row, buf, za_s, c0)


def _mlstm_prompt_kernel(q_ref, k_ref, v_ref, o_ref, g_ref, bias_ref, nrm_ref,
                         ha_ref, c_ref, n_ref, m_ref, cx_s, m_s, *, n_chunks):
    L = MLSTM_CHUNK
    c = pl.program_id(1)

    @pl.when(c == 0)
    def _():
        cx_s[...] = jnp.zeros_like(cx_s)
        m_s[...] = jnp.zeros_like(m_s)

    row = lax.broadcasted_iota(jnp.int32, (L, L), 0)
    col = lax.broadcasted_iota(jnp.int32, (L, L), 1)
    causal = row >= col
    tri = causal.astype(BF16)

    for b in range(MLSTM_SEQS):
        _mlstm_prompt_seq(b, causal, tri, q_ref, k_ref, v_ref, o_ref, g_ref, bias_ref, nrm_ref,
                          ha_ref, cx_s, m_s)

    @pl.when(c == n_chunks - 1)
    def _():
        for b in range(MLSTM_SEQS):
            m_ref[b] = m_s[b]
            for h in range(H_A):
                cx = cx_s[b * H_A + h]
                c_ref[0, b, h] = cx[:, :DV]
                n_ref[0, b, h:h + 1, :] = cx[:, DV:].T[h:h + 1, :]


def _prefix_max_rows(x):
    rows = x.shape[0]
    row = lax.broadcasted_iota(jnp.int32, x.shape, 0)
    k = 1
    while k < rows:
        x = jnp.where(row >= k, jnp.maximum(x, pltpu.roll(x, k, 0)), x)
        k *= 2
    return x


def _mlstm_prompt_seq(b, causal, tri, q_ref, k_ref, v_ref, o_ref, g_ref, bias_ref, nrm_ref,
                      ha_ref, cx_s, m_s):
    L = MLSTM_CHUNK
    lane = lax.broadcasted_iota(jnp.int32, (L, LANE), 1)
    lane_dv = lax.broadcasted_iota(jnp.int32, (DV, LANE), 1)
    capped = _soft_cap(g_ref[b] + bias_ref[...])
    lf = _log_sigmoid(capped)
    hi = lf.astype(BF16)
    r1 = lf - hi.astype(F32)
    mid = r1.astype(BF16)
    lo = (r1 - mid.astype(F32)).astype(BF16)
    parts = _bdot(tri, jnp.concatenate([hi, mid, lo], axis=1))
    bc = parts[:, :LANE] + parts[:, LANE:2 * LANE] + parts[:, 2 * LANE:]
    bh = pltpu.roll(bc, LANE - H_A, 1)
    a = capped - bh
    a_t = a.T
    m_prev = m_s[b]
    big_m = jnp.maximum(_prefix_max_rows(a), m_prev[0:1, :])
    sc_all = jnp.exp(m_prev[0:1, :] - big_m)
    emt_all = jnp.exp(-(bh + big_m))
    m_last = big_m[L - 1:L, :]
    w_all = jnp.exp(a - m_last)
    decay_all = sc_all[L - 1:L, :]
    m_s[b] = jnp.broadcast_to(bh[L - 1:L, :] + m_last, (8, LANE))

    nums = []
    nd_tail = jnp.zeros((L, LANE), F32)
    sq_tail = jnp.zeros((L, LANE), F32)
    for h in range(H_A):
        hs = slice(h * DK, (h + 1) * DK)
        onehot = (lane == h).astype(BF16)
        qb = q_ref[b, :, hs].astype(BF16)
        kf = k_ref[b, :, hs] * (DK ** -0.5)
        kb = kf.astype(BF16)
        vx = jnp.concatenate([v_ref[b, :, hs].astype(BF16), onehot], axis=1)
        cx_prev = cx_s[b * H_A + h]

        p = jnp.exp(jnp.where(causal, a_t[h:h + 1, :] - big_m[:, h:h + 1], -jnp.inf))
        s = _bdot_t(qb, kb) * p
        nd = sc_all[:, h:h + 1] * _bdot(qb, cx_prev.astype(BF16)) + _bdot(s.astype(BF16), vx)
        num = nd[:, :DV]
        nums.append(num)
        nd_tail = nd_tail + nd[:, DV:]
        sq_tail = sq_tail + _bdot((num * num).astype(BF16), (lane_dv == h).astype(BF16))

        wk = (w_all[:, h:h + 1] * kf).astype(BF16)
        cx_s[b * H_A + h] = decay_all[:, h:h + 1] * cx_prev + lax.dot_general(
            wk, vx, (((0,), (0,)), ((), ())), preferred_element_type=F32)

    inv = 1.0 / jnp.maximum(jnp.abs(nd_tail), emt_all)
    ms = inv * inv * sq_tail * (1.0 / DV)
    scale_all = inv * lax.rsqrt(ms + EPS)
    for h in range(H_A):
        hs = slice(h * DK, (h + 1) * DK)
        hn = nums[h] * scale_all[:, h:h + 1]
        ha_ref[b, :, hs] = (hn * nrm_ref[:, hs] * jax.nn.sigmoid(o_ref[b, :, hs])).astype(ha_ref.dtype)


def _mlstm_prompt(za_p, g_p, bias_row, nrm_row, batch, seq):
    L = MLSTM_CHUNK
    nsq = MLSTM_SEQS
    n_chunks = seq // L
    kern = functools.partial(_mlstm_prompt_kernel, n_chunks=n_chunks)
    za3 = za_p.reshape(batch, seq, 4 * D_A)
    g3 = g_p.reshape(batch, seq, LANE)
    zspec = lambda part: pl.BlockSpec((nsq, L, D_A), lambda b, c: (b, c, part))
    return pl.pallas_call(
        kern,
        grid=(batch // nsq, n_chunks),
        in_specs=[
            zspec(0), zspec(1), zspec(2), zspec(3),
            pl.BlockSpec((nsq, L, LANE), lambda b, c: (b, c, 0)),
            pl.BlockSpec((1, LANE), lambda b, c: (0, 0)),
            pl.BlockSpec((1, D_A), lambda b, c: (0, 0)),
        ],
        out_specs=[
            pl.BlockSpec((nsq, L, D_A), lambda b, c: (b, c, 0)),
            pl.BlockSpec((1, nsq, H_A, DK, DV), lambda b, c: (0, b, 0, 0, 0)),
            pl.BlockSpec((1, nsq, H_A, DK), lambda b, c: (0, b, 0, 0)),
            pl.BlockSpec((nsq, 8, LANE), lambda b, c: (b, 0, 0)),
        ],
        out_shape=[
            jax.ShapeDtypeStruct((batch, seq, D_A), BF16),
            jax.ShapeDtypeStruct((1, batch, H_A, DK, DV), F32),
            jax.ShapeDtypeStruct((1, batch, H_A, DK), F32),
            jax.ShapeDtypeStruct((batch, 8, LANE), F32),
        ],
        scratch_shapes=[pltpu.VMEM((nsq * H_A, DK, DV + LANE), F32),
                        pltpu.VMEM((nsq, 8, LANE), F32)],
        compiler_params=pltpu.CompilerParams(
            dimension_semantics=("parallel", "arbitrary"), vmem_limit_bytes=VMEM_LIMIT),
        name="mlstm_prompt",
    )(za3, za3, za3, za3, g3, bias_row, nrm_row)


def _expand_heads(x, width):
    rows = x.shape[0]
    return jnp.concatenate([jnp.broadcast_to(x[:, h:h + 1], (rows, width)) for h in range(H_A)], axis=1)


def _head_sums(x):
    return jnp.concatenate(
        [jnp.sum(x[:, h * DK:(h + 1) * DK], axis=-1, keepdims=True) for h in range(H_A)], axis=1)


def _mlstm_sample_kernel(za_ref, qc_ref, g_ref, bias_ref, nrm_ref, m_ref, n_ref,
                         ha_ref, m_out, n_out, wk_out, dec_out):
    nb = za_ref.shape[0]
    capped = _soft_cap(g_ref[...] + bias_ref[...])
    logi = capped[:, 0:H_A]
    logf = _log_sigmoid(capped)[:, H_A:2 * H_A]
    m_prev = m_ref[...]
    m_inter = logf + m_prev
    m_t = jnp.maximum(m_inter, logi)
    sc = jnp.exp(m_inter - m_t)
    ei = jnp.exp(logi - m_t)
    emt = jnp.exp(-m_t)

    q = za_ref[:, 0:D_A]
    k = za_ref[:, D_A:2 * D_A] * (DK ** -0.5)
    v = za_ref[:, 2 * D_A:3 * D_A]
    o = za_ref[:, 3 * D_A:4 * D_A]
    n_prev = n_ref[...]

    s = _head_sums(q * k) * ei
    den = sc * _head_sums(q * n_prev) + s
    denom = jnp.maximum(jnp.abs(den), emt)
    num = _expand_heads(sc, DV) * qc_ref[...] + _expand_heads(s, DV) * v
    hh = num / _expand_heads(denom, DV)
    ms = _head_sums(hh * hh) * (1.0 / DV)
    hn = hh * lax.rsqrt(_expand_heads(ms, DV) + EPS)
    ha_ref[...] = (hn * nrm_ref[...] * jax.nn.sigmoid(o)).astype(ha_ref.dtype)
    n_out[...] = _expand_heads(sc, DK) * n_prev + _expand_heads(ei, DK) * k
    m_out[...] = m_t
    pad = jnp.zeros((nb, LANE - H_A), F32)
    dec_out[...] = jnp.concatenate([sc, pad], axis=1)

    k_t = k.T
    ei_t = jnp.concatenate([ei, pad], axis=1).T
    for h in range(H_A):
        hs = slice(h * DK, (h + 1) * DK)
        wk_out[hs, :] = k_t[hs, :] * ei_t[h:h + 1, :]


def _mlstm_sample(za_s, qc, g_s, bias_row, nrm_row, m0, n0):
    nb = za_s.shape[0]
    assert nb == LANE, "the sequence axis is transposed onto the lanes"
    return pl.pallas_call(
        _mlstm_sample_kernel,
        out_shape=[
            jax.ShapeDtypeStruct((nb, D_A), BF16),
            jax.ShapeDtypeStruct((nb, H_A), F32),
            jax.ShapeDtypeStruct((nb, H_A * DK), F32),
            jax.ShapeDtypeStruct((D_A, nb), F32),
            jax.ShapeDtypeStruct((nb, LANE), F32),
        ],
        compiler_params=pltpu.CompilerParams(vmem_limit_bytes=VMEM_LIMIT),
        name="mlstm_sample",
    )(za_s, qc, g_s, bias_row, nrm_row, m0, n0)


def _outproj_kernel(hp_ref, hs_ref, ap_ref, as_ref, yb_ref, w_ref, op_ref, os_ref, wc_ref):
    j = pl.program_id(1)

    @pl.when(pl.program_id(0) == 0)
    def _():
        wc_ref[j] = w_ref[...].astype(BF16)

    wa = wc_ref[j, :D_A, :]
    wb = wc_ref[j, D_A:, :]
    op_ref[...] = hp_ref[...] + _bdot(ap_ref[...], wa) + _bdot(yb_ref[0, :MP, :], wb)
    os_ref[0] = hs_ref[0] + _bdot(as_ref[0], wa) + _bdot(yb_ref[0, MP:, :], wb)


def _outproj(hp, hs, ap, a_s, yb_c, w3):
    tn = OUT_TN
    n_n = D_MODEL // tn
    return pl.pallas_call(
        _outproj_kernel,
        grid=(N_MIX_BLOCKS, n_n),
        in_specs=[
            pl.BlockSpec((MP, tn), lambda i, j: (i, j)),
            pl.BlockSpec((1, MS, tn), lambda i, j: (i, 0, j)),
            pl.BlockSpec((MP, D_A), lambda i, j: (i, 0)),
            pl.BlockSpec((1, MS, D_A), lambda i, j: (i, 0, 0)),
            pl.BlockSpec((1, MP + MS, D_B), lambda i, j: (i, 0, 0)),
            pl.BlockSpec((None, D_A + D_B, tn), lambda i, j: (0, 0, jnp.where(i == 0, j, n_n - 1))),
        ],
        out_specs=[
            pl.BlockSpec((MP, tn), lambda i, j: (i, j)),
            pl.BlockSpec((1, MS, tn), lambda i, j: (i, 0, j)),
        ],
        out_shape=[
            jax.ShapeDtypeStruct((N_MIX_BLOCKS * MP, D_MODEL), F32),
            jax.ShapeDtypeStruct((N_MIX_BLOCKS, MS, D_MODEL), F32),
        ],
        scratch_shapes=[pltpu.VMEM((n_n, D_A + D_B, tn), BF16)],
        compiler_params=pltpu.CompilerParams(
            dimension_semantics=("arbitrary", "arbitrary"), vmem_limit_bytes=VMEM_LIMIT),
        name="outproj",
    )(hp, hs, ap, a_s, yb_c, w3)


def kernel(x_prompt, x_sample, state_mlstm_C, state_mlstm_n, state_mlstm_m, state_conv, norm_ffn1, ffn1_gate, ffn1_up, ffn1_down, norm_mix, w_in, b_gates, conv_w, conv_b, norm_mlstm, norm_conv, w_out, norm_ffn2, ffn2_gate, ffn2_up, ffn2_down, norm_final):
    batch, seq, _ = x_prompt.shape
    nb = x_sample.shape[0]
    assert batch == N_MIX_BLOCKS and seq == MP and nb == N_MIX_BLOCKS * MS
    assert norm_ffn1.shape[0] == 1, "single-layer trunk"

    xp = x_prompt.reshape(batch * seq, D_MODEL)
    xs = x_sample.reshape(N_FFN_BLOCKS, TM_S, D_MODEL)

    h1p, h1s, xn_c = _ffn(xp, xs, norm_ffn1, ffn1_gate, ffn1_up, ffn1_down, norm_mix, TF1)

    w_t = jnp.swapaxes(w_in[0], 0, 1)
    za_p, za_s, g_p, g_s = _proj(xn_c, w_t)
    za_s = za_s.reshape(nb, 4 * D_A)
    yb_c, conv_p, conv_s, qc_s = _conv(xn_c, w_t, conv_w[0], conv_b, norm_conv,
                                       state_conv.reshape(N_MIX_BLOCKS, MS, 2, D_B), za_s, state_mlstm_C)

    bias_row = jnp.zeros((1, LANE), F32).at[0, :2 * H_A].set(b_gates[0].astype(F32))
    ha_p, c_p, n_p, m_p = _mlstm_prompt(za_p, g_p, bias_row, norm_mlstm, batch, seq)
    ha_s, m_s, n_s, wk_s, dec_s = _mlstm_sample(
        za_s, qc_s, g_s.reshape(nb, LANE), bias_row, norm_mlstm,
        state_mlstm_m.reshape(nb, H_A), state_mlstm_n.reshape(nb, H_A * DK))

    h2p, h2s = _outproj(h1p, h1s.reshape(N_MIX_BLOCKS, MS, D_MODEL), ha_p.reshape(batch * seq, D_A),
                        ha_s.reshape(N_MIX_BLOCKS, MS, D_A), yb_c, w_out)

    yp, ys, c_s = _ffn(h2p, h2s.reshape(N_FFN_BLOCKS, TM_S, D_MODEL), norm_ffn2, ffn2_gate, ffn2_up,
                       ffn2_down, norm_final.reshape(1, D_MODEL), TF2,
                       state=(state_mlstm_C, wk_s, za_s, dec_s))

    return (
        yp.reshape(batch, seq, D_MODEL),
        ys.reshape(nb, 1, D_MODEL),
        c_p,
        n_p,
        m_p[:, 0, :H_A].reshape(1, batch, H_A),
        conv_p,
        c_s,
        n_s.reshape(1, nb, H_A, DK),
        m_s.reshape(1, nb, H_A),
        conv_s.reshape(1, nb, 2, D_B),
    )
```

```python
import functools

import jax
import jax.numpy as jnp
from jax import lax
from jax.experimental import pallas as pl
from jax.experimental.pallas import tpu as pltpu

F32 = jnp.float32
BF16 = jnp.bfloat16

D_MODEL = 2048
D_A = 1024
D_B = 1024
H_A = 4
DK = 256
DV = 256
G_B = 8
D_FF = 5504
GATE_CAP = 15.0
EPS = 1e-6
GATE_ROW0 = 4 * D_A
CONV_ROW0 = 4 * D_A + 2 * H_A

LANE = 128
N_FFN_BLOCKS = 8
TM_P = 1024
TM_S = 16
TM = TM_P + TM_S
TF1 = 512
TF2 = 512
STATE_SEQS = 2
X_CHUNK = 128
N_X_CHUNKS = TM_P // X_CHUNK
X_AHEAD0 = 2
N_MIX_BLOCKS = N_FFN_BLOCKS // 2
MP = 2 * TM_P
MS = 2 * TM_S
PROJ_TN = 1024
CONV_TC = 256
OUT_TN = 512
MLSTM_CHUNK = 256
MLSTM_SEQS = 2
SAMPLE_BS = 8
VMEM_LIMIT = 62 * 1024 * 1024


def _rms(x, g):
    return x * lax.rsqrt(jnp.mean(x * x, axis=-1, keepdims=True) + EPS) * g


def _soft_cap(x):
    return GATE_CAP * jnp.tanh(x / GATE_CAP)


def _log_sigmoid(x):
    return -jax.nn.softplus(-x)


def _bdot(a, b):
    return jnp.dot(a, b, preferred_element_type=F32)


def _bdot_t(a, b):
    return lax.dot_general(a, b, (((1,), (1,)), ((), ())), preferred_element_type=F32)


def _sample_state_update(f, c_ref, wk_ref, v_ref, dec_ref, c_out):
    grp = jnp.minimum(f, TM_S // STATE_SEQS - 1)
    lane = lax.broadcasted_iota(jnp.int32, (D_A, LANE), 1)
    for u in range(STATE_SEQS):
        r = grp * STATE_SEQS + u
        seq = pl.program_id(0) * TM_S + r
        wk_col = jnp.sum(jnp.where(lane == seq, wk_ref[...], 0.0), axis=1, keepdims=True)
        v_row = v_ref[pl.ds(r, 1), :]
        dec = dec_ref[pl.ds(r, 1), :]
        for h in range(H_A):
            hs = slice(h * DK, (h + 1) * DK)
            c_out[0, u, h] = dec[:, h:h + 1] * c_ref[0, u, h] + wk_col[hs, :] * v_row[:, hs]


def _ffn_kernel(x_hbm, xs_ref, g_ref, wg_ref, wu_ref, wd_ref, g2_ref, *rest,
                tf, n_f, last_valid, final_norm):
    if final_norm:
        c_ref, wk_ref, v_ref, dec_ref, out_hbm, os_ref, c_out, xn_ref, acc_ref, x_sem, o_sem = rest
    else:
        out_hbm, os_ref, nxt_ref, xn_ref, acc_ref, x_sem, o_sem = rest
    i = pl.program_id(0)
    f = pl.program_id(1)
    slot = lax.rem(i, 2)
    op_ref = acc_ref.at[slot]

    def x_copy(blk, s, c):
        dst0 = pl.multiple_of(c * X_CHUNK, X_CHUNK)
        src0 = pl.multiple_of(blk * TM_P + c * X_CHUNK, X_CHUNK)
        return pltpu.make_async_copy(x_hbm.at[pl.ds(src0, X_CHUNK), :],
                                     acc_ref.at[s, pl.ds(dst0, X_CHUNK), :], x_sem.at[c])

    def out_copy(blk, s):
        row0 = pl.multiple_of(blk * TM_P, TM_P)
        return pltpu.make_async_copy(acc_ref.at[s], out_hbm.at[pl.ds(row0, TM_P), :], o_sem.at[s])

    @pl.when(jnp.logical_and(i == 0, f == 0))
    def _():
        for c in range(N_X_CHUNKS):
            x_copy(0, 0, c).start()

    @pl.when(f == 0)
    def _():
        for c in range(N_X_CHUNKS):
            x_copy(i, slot, c).wait()

    @pl.when(jnp.logical_and(i > 0, f == X_AHEAD0 - 1))
    def _():
        out_copy(i - 1, 1 - slot).wait()

    @pl.when(jnp.logical_and(jnp.logical_and(f >= X_AHEAD0, f < X_AHEAD0 + N_X_CHUNKS),
                             i < N_FFN_BLOCKS - 1))
    def _():
        x_copy(i + 1, 1 - slot, f - X_AHEAD0).start()

    def step(valid, first, with_state):
        if first:
            g = g_ref[...]
            xn_ref[:TM_P, :] = _rms(op_ref[...], g).astype(BF16)
            xn_ref[TM_P:, :] = _rms(xs_ref[0], g).astype(BF16)
        xn = xn_ref[...]
        a = _bdot(xn, wg_ref[:, :valid].astype(BF16))
        b = _bdot(xn, wu_ref[:, :valid].astype(BF16))
        hid = (a * jax.nn.sigmoid(a) * b * 0.5).astype(BF16)
        r = _bdot(hid, wd_ref[:valid, :].astype(BF16))
        op_ref[...] += r[:TM_P]
        if first:
            os_ref[0] = xs_ref[0] + r[TM_P:]
        else:
            os_ref[0] += r[TM_P:]
        if with_state:
            _sample_state_update(f, c_ref, wk_ref, v_ref, dec_ref, c_out)

    assert last_valid < tf and (not final_norm or (n_f - 1) * STATE_SEQS >= TM_S), \
        "state updates ride the full-width steps"

    @pl.when(f == 0)
    def _():
        step(tf, True, final_norm)

    @pl.when(jnp.logical_and(f > 0, f < n_f - 1))
    def _():
        step(tf, False, final_norm)

    @pl.when(f == n_f - 1)
    def _():
        step(last_valid, False, False)
        g2 = g2_ref[...]
        if final_norm:
            op_ref[...] = _rms(op_ref[...], g2)
            os_ref[0] = _rms(os_ref[0], g2)
        else:
            nxt_ref[0, :TM_P, :] = _rms(op_ref[...], g2).astype(BF16)
            nxt_ref[0, TM_P:, :] = _rms(os_ref[0], g2).astype(BF16)

        out_copy(i, slot).start()

        @pl.when(i == N_FFN_BLOCKS - 1)
        def _():
            out_copy(i, slot).wait()


def _ffn(xp, xs, g, wg, wu, wd, g2, tf, state=None):
    final_norm = state is not None
    n_f = pl.cdiv(D_FF, tf)
    last_valid = D_FF - (n_f - 1) * tf
    kern = functools.partial(_ffn_kernel, tf=tf, n_f=n_f, last_valid=last_valid, final_norm=final_norm)
    assert X_AHEAD0 + N_X_CHUNKS <= n_f and N_X_CHUNKS * X_CHUNK == TM_P
    in_specs = [
        pl.BlockSpec(memory_space=pl.ANY),
        pl.BlockSpec((1, TM_S, D_MODEL), lambda i, f: (i, 0, 0)),
        pl.BlockSpec((1, D_MODEL), lambda i, f: (0, 0)),
        pl.BlockSpec((None, D_MODEL, tf), lambda i, f: (0, 0, f)),
        pl.BlockSpec((None, D_MODEL, tf), lambda i, f: (0, 0, f)),
        pl.BlockSpec((None, tf, D_MODEL), lambda i, f: (0, f, 0)),
        pl.BlockSpec((1, D_MODEL), lambda i, f: (0, 0)),
    ]
    out_specs = [
        pl.BlockSpec(memory_space=pl.ANY),
        pl.BlockSpec((1, TM_S, D_MODEL), lambda i, f: (i, 0, 0)),
    ]
    out_shape = [
        jax.ShapeDtypeStruct((N_FFN_BLOCKS * TM_P, D_MODEL), F32),
        jax.ShapeDtypeStruct((N_FFN_BLOCKS, TM_S, D_MODEL), F32),
    ]
    args = [xp, xs, g, wg, wu, wd, g2]
    if final_norm:
        c0, wk, za_s, dec = state
        groups = TM_S // STATE_SEQS
        grp = lambda i, f: i * groups + jnp.minimum(f, groups - 1)
        cspec = pl.BlockSpec((1, STATE_SEQS, H_A, DK, DV), lambda i, f: (0, grp(i, f), 0, 0, 0))
        in_specs += [
            cspec,
            pl.BlockSpec((D_A, LANE), lambda i, f: (0, 0)),
            pl.BlockSpec((TM_S, D_A), lambda i, f: (i, 2)),
            pl.BlockSpec((TM_S, LANE), lambda i, f: (i, 0)),
        ]
        out_specs.append(cspec)
        out_shape.append(jax.ShapeDtypeStruct(c0.shape, F32))
        args += [c0, wk, za_s, dec]
    else:
        out_specs.append(pl.BlockSpec((1, TM, D_MODEL), lambda i, f: (i, 0, 0)))
        out_shape.append(jax.ShapeDtypeStruct((N_FFN_BLOCKS, TM, D_MODEL), BF16))
    return pl.pallas_call(
        kern,
        grid=(N_FFN_BLOCKS, n_f),
        in_specs=in_specs,
        out_specs=out_specs,
        out_shape=out_shape,
        scratch_shapes=[pltpu.VMEM((TM, D_MODEL), BF16),
                        pltpu.VMEM((2, TM_P, D_MODEL), F32),
                        pltpu.SemaphoreType.DMA((N_X_CHUNKS,)),
                        pltpu.SemaphoreType.DMA((2,))],
        compiler_params=pltpu.CompilerParams(
            dimension_semantics=("arbitrary", "arbitrary"), vmem_limit_bytes=VMEM_LIMIT),
        name="ffn_final" if final_norm else "ffn",
    )(*args)


def _split_rows(z0, z1, p_ref, s_ref):
    p_ref[:TM_P, :] = z0[:TM_P]
    p_ref[TM_P:, :] = z1[:TM_P]
    s_ref[0, :TM_S, :] = z0[TM_P:]
    s_ref[0, TM_S:, :] = z1[TM_P:]


def _proj_kernel(xn_ref, w_ref, wg_ref, zp_ref, zs_ref, gp_ref, gs_ref):
    w = w_ref[...].astype(BF16)
    _split_rows(_bdot_t(xn_ref[0], w), _bdot_t(xn_ref[1], w), zp_ref, zs_ref)

    @pl.when(pl.program_id(1) == 0)
    def _():
        wg = wg_ref[...].astype(BF16)
        _split_rows(_bdot_t(xn_ref[0], wg), _bdot_t(xn_ref[1], wg), gp_ref, gs_ref)


def _proj(xn_c, w_t):
    n_n = (4 * D_A) // PROJ_TN
    return pl.pallas_call(
        _proj_kernel,
        grid=(N_MIX_BLOCKS, n_n),
        in_specs=[
            pl.BlockSpec((2, TM, D_MODEL), lambda i, j: (i, 0, 0)),
            pl.BlockSpec((PROJ_TN, D_MODEL), lambda i, j: (j, 0)),
            pl.BlockSpec((LANE, D_MODEL), lambda i, j: (GATE_ROW0 // LANE, 0)),
        ],
        out_specs=[
            pl.BlockSpec((MP, PROJ_TN), lambda i, j: (i, j)),
            pl.BlockSpec((1, MS, PROJ_TN), lambda i, j: (i, 0, j)),
            pl.BlockSpec((MP, LANE), lambda i, j: (i, 0)),
            pl.BlockSpec((1, MS, LANE), lambda i, j: (i, 0, 0)),
        ],
        out_shape=[
            jax.ShapeDtypeStruct((N_MIX_BLOCKS * MP, 4 * D_A), F32),
            jax.ShapeDtypeStruct((N_MIX_BLOCKS, MS, 4 * D_A), F32),
            jax.ShapeDtypeStruct((N_MIX_BLOCKS * MP, LANE), F32),
            jax.ShapeDtypeStruct((N_MIX_BLOCKS, MS, LANE), F32),
        ],
        compiler_params=pltpu.CompilerParams(
            dimension_semantics=("parallel", "arbitrary"), vmem_limit_bytes=VMEM_LIMIT),
        name="proj_qkvo",
    )(xn_c, w_t, w_t)


def _group_norm(yb, nrm):
    gw = D_B // G_B
    parts = []
    for g in range(yb.shape[1] // gw):
        seg = yb[:, g * gw:(g + 1) * gw]
        parts.append(seg * lax.rsqrt(jnp.mean(seg * seg, axis=-1, keepdims=True) + EPS))
    return jnp.concatenate(parts, axis=1) * nrm


def _conv_kernel(xn_ref, wgb_ref, wgc_ref, wxc_ref, cw_ref, cb_ref, nrm_ref, buf_ref, q_ref, c_ref,
                 yb_ref, cp_ref, cs_ref, qc_ref):
    q_t = jnp.concatenate([q_ref[...], jnp.zeros((LANE - SAMPLE_BS, D_A), F32)], axis=0).T
    for h in range(H_A):
        hs = slice(h * DK, (h + 1) * DK)
        for j in range(SAMPLE_BS):
            qc_ref[j:j + 1, hs] = jnp.sum(q_t[hs, j:j + 1] * c_ref[0, j, h], axis=0, keepdims=True)

    wgb = wgb_ref[...].astype(BF16)
    wgc = wgc_ref[...].astype(BF16)
    wxc = wxc_ref[...].astype(BF16)
    cw0 = cw_ref[0:1, :]
    cw1 = cw_ref[1:2, :]
    cw2 = cw_ref[2:3, :]
    cb = cb_ref[...]
    nrm = nrm_ref[...]
    row = lax.broadcasted_iota(jnp.int32, (TM_P, wgb.shape[0]), 0)
    zero_row = jnp.zeros((1, wgb.shape[0]), F32)
    um2, um1 = zero_row, zero_row

    for t in range(2):
        xn = xn_ref[t]
        u = _bdot_t(xn, wgc) * _bdot_t(xn, wxc)
        gb = _bdot_t(xn, wgb)

        up = u[:TM_P]
        u1 = jnp.where(row < 1, um1, pltpu.roll(up, 1, 0))
        u2 = jnp.where(row < 1, um2, jnp.where(row < 2, um1, pltpu.roll(up, 2, 0)))
        yc = cw0 * u2 + cw1 * u1 + cw2 * up + cb
        yb_ref[0, t * TM_P:(t + 1) * TM_P, :] = _group_norm(gb[:TM_P] * yc, nrm).astype(yb_ref.dtype)
        um2, um1 = up[TM_P - 2:TM_P - 1, :], up[TM_P - 1:TM_P, :]

        ss = slice(t * TM_S, (t + 1) * TM_S)
        us = u[TM_P:]
        b0 = buf_ref[0, ss, 0, :]
        b1 = buf_ref[0, ss, 1, :]
        ycs = cw0 * b0 + cw1 * b1 + cw2 * us + cb
        yb_ref[0, MP + t * TM_S:MP + (t + 1) * TM_S, :] = _group_norm(gb[TM_P:] * ycs, nrm).astype(yb_ref.dtype)
        cs_ref[0, ss, 0, :] = b1
        cs_ref[0, ss, 1, :] = us

    cp_ref[0, 0, 0:1, :] = um2
    cp_ref[0, 0, 1:2, :] = um1


def _conv(xn_c, w_t, cw, cb_row, nrm_row, buf, za_s, c0):
    tc = CONV_TC
    n_c = D_B // tc
    assert N_MIX_BLOCKS * n_c * SAMPLE_BS == za_s.shape[0], "one sample group per grid step"
    wspec = lambda base: pl.BlockSpec(
        (pl.Element(tc), pl.Element(D_MODEL)), lambda i, c: (pl.multiple_of(base + c * tc, 8), 0))
    grp = lambda i, c: i * n_c + c
    return pl.pallas_call(
        _conv_kernel,
        grid=(N_MIX_BLOCKS, n_c),
        in_specs=[
            pl.BlockSpec((2, TM, D_MODEL), lambda i, c: (i, 0, 0)),
            wspec(CONV_ROW0),
            wspec(CONV_ROW0 + D_B),
            wspec(CONV_ROW0 + 2 * D_B),
            pl.BlockSpec((3, tc), lambda i, c: (0, c)),
            pl.BlockSpec((1, tc), lambda i, c: (0, c)),
            pl.BlockSpec((1, tc), lambda i, c: (0, c)),
            pl.BlockSpec((1, MS, 2, tc), lambda i, c: (i, 0, 0, c)),
            pl.BlockSpec((SAMPLE_BS, D_A), lambda i, c: (grp(i, c), 0)),
            pl.BlockSpec((1, SAMPLE_BS, H_A, DK, DV), lambda i, c: (0, grp(i, c), 0, 0, 0)),
        ],
        out_specs=[
            pl.BlockSpec((1, MP + MS, tc), lambda i, c: (i, 0, c)),
            pl.BlockSpec((1, 1, 2, tc), lambda i, c: (0, i, 0, c)),
            pl.BlockSpec((1, MS, 2, tc), lambda i, c: (i, 0, 0, c)),
            pl.BlockSpec((SAMPLE_BS, D_A), lambda i, c: (grp(i, c), 0)),
        ],
        out_shape=[
            jax.ShapeDtypeStruct((N_MIX_BLOCKS, MP + MS, D_B), BF16),
            jax.ShapeDtypeStruct((1, N_MIX_BLOCKS, 2, D_B), F32),
            jax.ShapeDtypeStruct((N_MIX_BLOCKS, MS, 2, D_B), F32),
            jax.ShapeDtypeStruct((za_s.shape[0], D_A), F32),
        ],
        compiler_params=pltpu.CompilerParams(
            dimension_semantics=("parallel", "arbitrary"), vmem_limit_bytes=VMEM_LIMIT),
        name="proj_conv",
    )(xn_c, w_t, w_t, w_t, cw, cb_row, nrm_row, buf, za_s, c0)


def _mlstm_prompt_kernel(q_ref, k_ref, v_ref, o_ref, g_ref, bias_ref, nrm_ref,
                         ha_ref, c_ref, n_ref, m_ref, cx_s, m_s, *, n_chunks):
    L = MLSTM_CHUNK
    c = pl.program_id(1)

    @pl.when(c == 0)
    def _():
        cx_s[...] = jnp.zeros_like(cx_s)
        m_s[...] = jnp.zeros_like(m_s)

    row = lax.broadcasted_iota(jnp.int32, (L, L), 0)
    col = lax.broadcasted_iota(jnp.int32, (L, L), 1)
    causal = row >= col
    tri = causal.astype(BF16)

    for b in range(MLSTM_SEQS):
        _mlstm_prompt_seq(b, causal, tri, q_ref, k_ref, v_ref, o_ref, g_ref, bias_ref, nrm_ref,
                          ha_ref, cx_s, m_s)

    @pl.when(c == n_chunks - 1)
    def _():
        for b in range(MLSTM_SEQS):
            m_ref[b] = m_s[b]
            for h in range(H_A):
                cx = cx_s[b * H_A + h]
                c_ref[0, b, h] = cx[:, :DV]
                n_ref[0, b, h:h + 1, :] = cx[:, DV:].T[h:h + 1, :]


def _prefix_max_rows(x):
    rows = x.shape[0]
    row = lax.broadcasted_iota(jnp.int32, x.shape, 0)
    k = 1
    while k < rows:
        x = jnp.where(row >= k, jnp.maximum(x, pltpu.roll(x, k, 0)), x)
        k *= 2
    return x


def _mlstm_prompt_seq(b, causal, tri, q_ref, k_ref, v_ref, o_ref, g_ref, bias_ref, nrm_ref,
                      ha_ref, cx_s, m_s):
    L = MLSTM_CHUNK
    lane = lax.broadcasted_iota(jnp.int32, (L, LANE), 1)
    lane_dv = lax.broadcasted_iota(jnp.int32, (DV, LANE), 1)
    capped = _soft_cap(g_ref[b] + bias_ref[...])
    lf = _log_sigmoid(capped)
    hi = lf.astype(BF16)
    r1 = lf - hi.astype(F32)
    mid = r1.astype(BF16)
    lo = (r1 - mid.astype(F32)).astype(BF16)
    parts = _bdot(tri, jnp.concatenate([hi, mid, lo], axis=1))
    bc = parts[:, :LANE] + parts[:, LANE:2 * LANE] + parts[:, 2 * LANE:]
    bh = pltpu.roll(bc, LANE - H_A, 1)
    a = capped - bh
    a_t = a.T
    m_prev = m_s[b]
    big_m = jnp.maximum(_prefix_max_rows(a), m_prev[0:1, :])
    sc_all = jnp.exp(m_prev[0:1, :] - big_m)
    emt_all = jnp.exp(-(bh + big_m))
    m_last = big_m[L - 1:L, :]
    w_all = jnp.exp(a - m_last)
    decay_all = sc_all[L - 1:L, :]
    m_s[b] = jnp.broadcast_to(bh[L - 1:L, :] + m_last, (8, LANE))

    nums = []
    nd_tail = jnp.zeros((L, LANE), F32)
    sq_tail = jnp.zeros((L, LANE), F32)
    for h in range(H_A):
        hs = slice(h * DK, (h + 1) * DK)
        onehot = (lane == h).astype(BF16)
        qb = q_ref[b, :, hs].astype(BF16)
        kf = k_ref[b, :, hs] * (DK ** -0.5)
        kb = kf.astype(BF16)
        vx = jnp.concatenate([v_ref[b, :, hs].astype(BF16), onehot], axis=1)
        cx_prev = cx_s[b * H_A + h]

        p = jnp.exp(jnp.where(causal, a_t[h:h + 1, :] - big_m[:, h:h + 1], -jnp.inf))
        s = _bdot_t(qb, kb) * p
        nd = sc_all[:, h:h + 1] * _bdot(qb, cx_prev.astype(BF16)) + _bdot(s.astype(BF16), vx)
        num = nd[:, :DV]
        nums.append(num)
        nd_tail = nd_tail + nd[:, DV:]
        sq_tail = sq_tail + _bdot((num * num).astype(BF16), (lane_dv == h).astype(BF16))

        wk = (w_all[:, h:h + 1] * kf).astype(BF16)
        cx_s[b * H_A + h] = decay_all[:, h:h + 1] * cx_prev + lax.dot_general(
            wk, vx, (((0,), (0,)), ((), ())), preferred_element_type=F32)

    inv = 1.0 / jnp.maximum(jnp.abs(nd_tail), emt_all)
    ms = inv * inv * sq_tail * (1.0 / DV)
    scale_all = inv * lax.rsqrt(ms + EPS)
    for h in range(H_A):
        hs = slice(h * DK, (h + 1) * DK)
        hn = nums[h] * scale_all[:, h:h + 1]
        ha_ref[b, :, hs] = (hn * nrm_ref[:, hs] * jax.nn.sigmoid(o_ref[b, :, hs])).astype(ha_ref.dtype)


def _mlstm_prompt(za_p, g_p, bias_row, nrm_row, batch, seq):
    L = MLSTM_CHUNK
    nsq = MLSTM_SEQS
    n_chunks = seq // L
    kern = functools.partial(_mlstm_prompt_kernel, n_chunks=n_chunks)
    za3 = za_p.reshape(batch, seq, 4 * D_A)
    g3 = g_p.reshape(batch, seq, LANE)
    zspec = lambda part: pl.BlockSpec((nsq, L, D_A), lambda b, c: (b, c, part))
    return pl.pallas_call(
        kern,
        grid=(batch // nsq, n_chunks),
        in_specs=[
            zspec(0), zspec(1), zspec(2), zspec(3),
            pl.BlockSpec((nsq, L, LANE), lambda b, c: (b, c, 0)),
            pl.BlockSpec((1, LANE), lambda b, c: (0, 0)),
            pl.BlockSpec((1, D_A), lambda b, c: (0, 0)),
        ],
        out_specs=[
            pl.BlockSpec((nsq, L, D_A), lambda b, c: (b, c, 0)),
            pl.BlockSpec((1, nsq, H_A, DK, DV), lambda b, c: (0, b, 0, 0, 0)),
            pl.BlockSpec((1, nsq, H_A, DK), lambda b, c: (0, b, 0, 0)),
            pl.BlockSpec((nsq, 8, LANE), lambda b, c: (b, 0, 0)),
        ],
        out_shape=[
            jax.ShapeDtypeStruct((batch, seq, D_A), BF16),
            jax.ShapeDtypeStruct((1, batch, H_A, DK, DV), F32),
            jax.ShapeDtypeStruct((1, batch, H_A, DK), F32),
            jax.ShapeDtypeStruct((batch, 8, LANE), F32),
        ],
        scratch_shapes=[pltpu.VMEM((nsq * H_A, DK, DV + LANE), F32),
                        pltpu.VMEM((nsq, 8, LANE), F32)],
        compiler_params=pltpu.CompilerParams(
            dimension_semantics=("parallel", "arbitrary"), vmem_limit_bytes=VMEM_LIMIT),
        name="mlstm_prompt",
    )(za3, za3, za3, za3, g3, bias_row, nrm_row)


def _expand_heads(x, width):
    rows = x.shape[0]
    return jnp.concatenate([jnp.broadcast_to(x[:, h:h + 1], (rows, width)) for h in range(H_A)], axis=1)


def _head_sums(x):
    return jnp.concatenate(
        [jnp.sum(x[:, h * DK:(h + 1) * DK], axis=-1, keepdims=True) for h in range(H_A)], axis=1)


def _mlstm_sample_kernel(za_ref, qc_ref, g_ref, bias_ref, nrm_ref, m_ref, n_ref,
                         ha_ref, m_out, n_out, wk_out, dec_out):
    nb = za_ref.shape[0]
    capped = _soft_cap(g_ref[...] + bias_ref[...])
    logi = capped[:, 0:H_A]
    logf = _log_sigmoid(capped)[:, H_A:2 * H_A]
    m_prev = m_ref[...]
    m_inter = logf + m_prev
    m_t = jnp.maximum(m_inter, logi)
    sc = jnp.exp(m_inter - m_t)
    ei = jnp.exp(logi - m_t)
    emt = jnp.exp(-m_t)

    q = za_ref[:, 0:D_A]
    k = za_ref[:, D_A:2 * D_A] * (DK ** -0.5)
    v = za_ref[:, 2 * D_A:3 * D_A]
    o = za_ref[:, 3 * D_A:4 * D_A]
    n_prev = n_ref[...]

    s = _head_sums(q * k) * ei
    den = sc * _head_sums(q * n_prev) + s
    denom = jnp.maximum(jnp.abs(den), emt)
    num = _expand_heads(sc, DV) * qc_ref[...] + _expand_heads(s, DV) * v
    hh = num / _expand_heads(denom, DV)
    ms = _head_sums(hh * hh) * (1.0 / DV)
    hn = hh * lax.rsqrt(_expand_heads(ms, DV) + EPS)
    ha_ref[...] = (hn * nrm_ref[...] * jax.nn.sigmoid(o)).astype(ha_ref.dtype)
    n_out[...] = _expand_heads(sc, DK) * n_prev + _expand_heads(ei, DK) * k
    m_out[...] = m_t
    pad = jnp.zeros((nb, LANE - H_A), F32)
    dec_out[...] = jnp.concatenate([sc, pad], axis=1)

    k_t = k.T
    ei_t = jnp.concatenate([ei, pad], axis=1).T
    for h in range(H_A):
        hs = slice(h * DK, (h + 1) * DK)
        wk_out[hs, :] = k_t[hs, :] * ei_t[h:h + 1, :]


def _mlstm_sample(za_s, qc, g_s, bias_row, nrm_row, m0, n0):
    nb = za_s.shape[0]
    assert nb == LANE, "the sequence axis is transposed onto the lanes"
    return pl.pallas_call(
        _mlstm_sample_kernel,
        out_shape=[
            jax.ShapeDtypeStruct((nb, D_A), BF16),
            jax.ShapeDtypeStruct((nb, H_A), F32),
            jax.ShapeDtypeStruct((nb, H_A * DK), F32),
            jax.ShapeDtypeStruct((D_A, nb), F32),
            jax.ShapeDtypeStruct((nb, LANE), F32),
        ],
        compiler_params=pltpu.CompilerParams(vmem_limit_bytes=VMEM_LIMIT),
        name="mlstm_sample",
    )(za_s, qc, g_s, bias_row, nrm_row, m0, n0)


def _outproj_kernel(hp_ref, hs_ref, ap_ref, as_ref, yb_ref, w_ref, op_ref, os_ref, wc_ref):
    j = pl.program_id(1)

    @pl.when(pl.program_id(0) == 0)
    def _():
        wc_ref[j] = w_ref[...].astype(BF16)

    wa = wc_ref[j, :D_A, :]
    wb = wc_ref[j, D_A:, :]
    op_ref[...] = hp_ref[...] + _bdot(ap_ref[...], wa) + _bdot(yb_ref[0, :MP, :], wb)
    os_ref[0] = hs_ref[0] + _bdot(as_ref[0], wa) + _bdot(yb_ref[0, MP:, :], wb)


def _outproj(hp, hs, ap, a_s, yb_c, w3):
    tn = OUT_TN
    n_n = D_MODEL // tn
    return pl.pallas_call(
        _outproj_kernel,
        grid=(N_MIX_BLOCKS, n_n),
        in_specs=[
            pl.BlockSpec((MP, tn), lambda i, j: (i, j)),
            pl.BlockSpec((1, MS, tn), lambda i, j: (i, 0, j)),
            pl.BlockSpec((MP, D_A), lambda i, j: (i, 0)),
            pl.BlockSpec((1, MS, D_A), lambda i, j: (i, 0, 0)),
            pl.BlockSpec((1, MP + MS, D_B), lambda i, j: (i, 0, 0)),
            pl.BlockSpec((None, D_A + D_B, tn), lambda i, j: (0, 0, jnp.where(i == 0, j, n_n - 1))),
        ],
        out_specs=[
            pl.BlockSpec((MP, tn), lambda i, j: (i, j)),
            pl.BlockSpec((1, MS, tn), lambda i, j: (i, 0, j)),
        ],
        out_shape=[
            jax.ShapeDtypeStruct((N_MIX_BLOCKS * MP, D_MODEL), F32),
            jax.ShapeDtypeStruct((N_MIX_BLOCKS, MS, D_MODEL), F32),
        ],
        scratch_shapes=[pltpu.VMEM((n_n, D_A + D_B, tn), BF16)],
        compiler_params=pltpu.CompilerParams(
            dimension_semantics=("arbitrary", "arbitrary"), vmem_limit_bytes=VMEM_LIMIT),
        name="outproj",
    )(hp, hs, ap, a_s, yb_c, w3)


def kernel(x_prompt, x_sample, state_mlstm_C, state_mlstm_n, state_mlstm_m, state_conv, norm_ffn1, ffn1_gate, ffn1_up, ffn1_down, norm_mix, w_in, b_gates, conv_w, conv_b, norm_mlstm, norm_conv, w_out, norm_ffn2, ffn2_gate, ffn2_up, ffn2_down, norm_final):
    batch, seq, _ = x_prompt.shape
    nb = x_sample.shape[0]
    assert batch == N_MIX_BLOCKS and seq == MP and nb == N_MIX_BLOCKS * MS
    assert norm_ffn1.shape[0] == 1, "single-layer trunk"

    xp = x_prompt.reshape(batch * seq, D_MODEL)
    xs = x_sample.reshape(N_FFN_BLOCKS, TM_S, D_MODEL)

    h1p, h1s, xn_c = _ffn(xp, xs, norm_ffn1, ffn1_gate, ffn1_up, ffn1_down, norm_mix, TF1)

    w_t = jnp.swapaxes(w_in[0], 0, 1)
    za_p, za_s, g_p, g_s = _proj(xn_c, w_t)
    za_s = za_s.reshape(nb, 4 * D_A)
    yb_c, conv_p, conv_s, qc_s = _conv(xn_c, w_t, conv_w[0], conv_b, norm_conv,
                                       state_conv.reshape(N_MIX_BLOCKS, MS, 2, D_B), za_s, state_mlstm_C)

    bias_row = jnp.zeros((1, LANE), F32).at[0, :2 * H_A].set(b_gates[0].astype(F32))
    ha_p, c_p, n_p, m_p = _mlstm_prompt(za_p, g_p, bias_row, norm_mlstm, batch, seq)
    ha_s, m_s, n_s, wk_s, dec_s = _mlstm_sample(
        za_s, qc_s, g_s.reshape(nb, LANE), bias_row, norm_mlstm,
        state_mlstm_m.reshape(nb, H_A), state_mlstm_n.reshape(nb, H_A * DK))

    h2p, h2s = _outproj(h1p, h1s.reshape(N_MIX_BLOCKS, MS, D_MODEL), ha_p.reshape(batch * seq, D_A),
                        ha_s.reshape(N_MIX_BLOCKS, MS, D_A), yb_c, w_out)

    yp, ys, c_s = _ffn(h2p, h2s.reshape(N_FFN_BLOCKS, TM_S, D_MODEL), norm_ffn2, ffn2_gate, ffn2_up,
                       ffn2_down, norm_final.reshape(1, D_MODEL), TF2,
                       state=(state_mlstm_C, wk_s, za_s, dec_s))

    return (
        yp.reshape(batch, seq, D_MODEL),
        ys.reshape(nb, 1, D_MODEL),
        c_p,
        n_p,
        m_p[:, 0, :H_A].reshape(1, batch, H_A),
        conv_p,
        c_s,
        n_s.reshape(1, nb, H_A, DK),
        m_s.reshape(1, nb, H_A),
        conv_s.reshape(1, nb, 2, D_B),
    )
```

```python
import functools

import jax
import jax.numpy as jnp
from jax import lax
from jax.experimental import pallas as pl
from jax.experimental.pallas import tpu as pltpu

F32 = jnp.float32
BF16 = jnp.bfloat16

D_MODEL = 2048
D_A = 1024
D_B = 1024
H_A = 4
DK = 256
DV = 256
G_B = 8
D_FF = 5504
GATE_CAP = 15.0
EPS = 1e-6
GATE_ROW0 = 4 * D_A
CONV_ROW0 = 4 * D_A + 2 * H_A

LANE = 128
N_FFN_BLOCKS = 8
TM_P = 1024
TM_S = 16
TM = TM_P + TM_S
TF1 = 512
TF2 = 512
STATE_SEQS = 2
X_CHUNK = 128
N_X_CHUNKS = TM_P // X_CHUNK
X_AHEAD0 = 2
N_MIX_BLOCKS = N_FFN_BLOCKS // 2
MP = 2 * TM_P
MS = 2 * TM_S
PROJ_TN = 1024
CONV_TC = 256
OUT_TN = 512
MLSTM_CHUNK = 256
MLSTM_SEQS = 2
SAMPLE_BS = 8
VMEM_LIMIT = 62 * 1024 * 1024


def _rms(x, g):
    return x * lax.rsqrt(jnp.mean(x * x, axis=-1, keepdims=True) + EPS) * g


def _soft_cap(x):
    return GATE_CAP * jnp.tanh(x / GATE_CAP)


def _log_sigmoid(x):
    return -jax.nn.softplus(-x)


def _bdot(a, b):
    return jnp.dot(a, b, preferred_element_type=F32)


def _bdot_t(a, b):
    return lax.dot_general(a, b, (((1,), (1,)), ((), ())), preferred_element_type=F32)


def _sample_state_update(f, c_ref, wk_ref, v_ref, dec_ref, c_out):
    grp = jnp.minimum(f, TM_S // STATE_SEQS - 1)
    lane = lax.broadcasted_iota(jnp.int32, (D_A, LANE), 1)
    for u in range(STATE_SEQS):
        r = grp * STATE_SEQS + u
        seq = pl.program_id(0) * TM_S + r
        wk_col = jnp.sum(jnp.where(lane == seq, wk_ref[...], 0.0), axis=1, keepdims=True)
        v_row = v_ref[pl.ds(r, 1), :]
        dec = dec_ref[pl.ds(r, 1), :]
        for h in range(H_A):
            hs = slice(h * DK, (h + 1) * DK)
            c_out[0, u, h] = dec[:, h:h + 1] * c_ref[0, u, h] + wk_col[hs, :] * v_row[:, hs]


def _ffn_kernel(x_hbm, xs_ref, g_ref, wg_ref, wu_ref, wd_ref, g2_ref, *rest,
                tf, n_f, last_valid, final_norm):
    if final_norm:
        c_ref, wk_ref, v_ref, dec_ref, out_hbm, os_ref, c_out, xn_ref, acc_ref, x_sem, o_sem = rest
    else:
        out_hbm, os_ref, nxt_ref, xn_ref, acc_ref, x_sem, o_sem = rest
    i = pl.program_id(0)
    f = pl.program_id(1)
    slot = lax.rem(i, 2)
    op_ref = acc_ref.at[slot]

    def x_copy(blk, s, c):
        dst0 = pl.multiple_of(c * X_CHUNK, X_CHUNK)
        src0 = pl.multiple_of(blk * TM_P + c * X_CHUNK, X_CHUNK)
        return pltpu.make_async_copy(x_hbm.at[pl.ds(src0, X_CHUNK), :],
                                     acc_ref.at[s, pl.ds(dst0, X_CHUNK), :], x_sem.at[c])

    def out_copy(blk, s):
        row0 = pl.multiple_of(blk * TM_P, TM_P)
        return pltpu.make_async_copy(acc_ref.at[s], out_hbm.at[pl.ds(row0, TM_P), :], o_sem.at[s])

    @pl.when(jnp.logical_and(i == 0, f == 0))
    def _():
        for c in range(N_X_CHUNKS):
            x_copy(0, 0, c).start()

    @pl.when(f == 0)
    def _():
        for c in range(N_X_CHUNKS):
            x_copy(i, slot, c).wait()

    @pl.when(jnp.logical_and(i > 0, f == X_AHEAD0 - 1))
    def _():
        out_copy(i - 1, 1 - slot).wait()

    @pl.when(jnp.logical_and(jnp.logical_and(f >= X_AHEAD0, f < X_AHEAD0 + N_X_CHUNKS),
                             i < N_FFN_BLOCKS - 1))
    def _():
        x_copy(i + 1, 1 - slot, f - X_AHEAD0).start()

    def step(valid, first, with_state):
        if first:
            g = g_ref[...]
            xn_ref[:TM_P, :] = _rms(op_ref[...], g).astype(BF16)
            xn_ref[TM_P:, :] = _rms(xs_ref[0], g).astype(BF16)
        xn = xn_ref[...]
        a = _bdot(xn, wg_ref[:, :valid].astype(BF16))
        b = _bdot(xn, wu_ref[:, :valid].astype(BF16))
        hid = (a * jax.nn.sigmoid(a) * b * 0.5).astype(BF16)
        r = _bdot(hid, wd_ref[:valid, :].astype(BF16))
        op_ref[...] += r[:TM_P]
        if first:
            os_ref[0] = xs_ref[0] + r[TM_P:]
        else:
            os_ref[0] += r[TM_P:]
        if with_state:
            _sample_state_update(f, c_ref, wk_ref, v_ref, dec_ref, c_out)

    assert last_valid < tf and (not final_norm or n_f * STATE_SEQS >= TM_S), "state updates ride the steps"
    mid = n_f // 2

    @pl.when(f == 0)
    def _():
        step(tf, True, final_norm)

    @pl.when(jnp.logical_and(jnp.logical_and(f > 0, f < n_f - 1), f != mid))
    def _():
        step(tf, False, final_norm)

    @pl.when(f == mid)
    def _():
        step(last_valid, False, final_norm)

    @pl.when(f == n_f - 1)
    def _():
        step(tf, False, final_norm)
        g2 = g2_ref[...]
        if final_norm:
            op_ref[...] = _rms(op_ref[...], g2)
            os_ref[0] = _rms(os_ref[0], g2)
        else:
            nxt_ref[0, :TM_P, :] = _rms(op_ref[...], g2).astype(BF16)
            nxt_ref[0, TM_P:, :] = _rms(os_ref[0], g2).astype(BF16)

        out_copy(i, slot).start()

        @pl.when(i == N_FFN_BLOCKS - 1)
        def _():
            out_copy(i, slot).wait()


def _ffn(xp, xs, g, wg, wu, wd, g2, tf, state=None):
    final_norm = state is not None
    n_f = pl.cdiv(D_FF, tf)
    last_valid = D_FF - (n_f - 1) * tf
    kern = functools.partial(_ffn_kernel, tf=tf, n_f=n_f, last_valid=last_valid, final_norm=final_norm)
    assert X_AHEAD0 + N_X_CHUNKS <= n_f and N_X_CHUNKS * X_CHUNK == TM_P
    assert n_f % 2 == 1, "the narrow tile sits at the middle step of both traversal directions"
    mid = n_f // 2

    def tile(i, f):
        p = jnp.where(lax.rem(i, 2) == 1, n_f - 1 - f, f)
        return jnp.where(p == mid, n_f - 1, jnp.where(p > mid, p - 1, p))

    in_specs = [
        pl.BlockSpec(memory_space=pl.ANY),
        pl.BlockSpec((1, TM_S, D_MODEL), lambda i, f: (i, 0, 0)),
        pl.BlockSpec((1, D_MODEL), lambda i, f: (0, 0)),
        pl.BlockSpec((None, D_MODEL, tf), lambda i, f: (0, 0, tile(i, f))),
        pl.BlockSpec((None, D_MODEL, tf), lambda i, f: (0, 0, tile(i, f))),
        pl.BlockSpec((None, tf, D_MODEL), lambda i, f: (0, tile(i, f), 0)),
        pl.BlockSpec((1, D_MODEL), lambda i, f: (0, 0)),
    ]
    out_specs = [
        pl.BlockSpec(memory_space=pl.ANY),
        pl.BlockSpec((1, TM_S, D_MODEL), lambda i, f: (i, 0, 0)),
    ]
    out_shape = [
        jax.ShapeDtypeStruct((N_FFN_BLOCKS * TM_P, D_MODEL), F32),
        jax.ShapeDtypeStruct((N_FFN_BLOCKS, TM_S, D_MODEL), F32),
    ]
    args = [xp, xs, g, wg, wu, wd, g2]
    if final_norm:
        c0, wk, za_s, dec = state
        groups = TM_S // STATE_SEQS
        grp = lambda i, f: i * groups + jnp.minimum(f, groups - 1)
        cspec = pl.BlockSpec((1, STATE_SEQS, H_A, DK, DV), lambda i, f: (0, grp(i, f), 0, 0, 0))
        in_specs += [
            cspec,
            pl.BlockSpec((D_A, LANE), lambda i, f: (0, 0)),
            pl.BlockSpec((TM_S, D_A), lambda i, f: (i, 2)),
            pl.BlockSpec((TM_S, LANE), lambda i, f: (i, 0)),
        ]
        out_specs.append(cspec)
        out_shape.append(jax.ShapeDtypeStruct(c0.shape, F32))
        args += [c0, wk, za_s, dec]
    else:
        out_specs.append(pl.BlockSpec((1, TM, D_MODEL), lambda i, f: (i, 0, 0)))
        out_shape.append(jax.ShapeDtypeStruct((N_FFN_BLOCKS, TM, D_MODEL), BF16))
    return pl.pallas_call(
        kern,
        grid=(N_FFN_BLOCKS, n_f),
        in_specs=in_specs,
        out_specs=out_specs,
        out_shape=out_shape,
        scratch_shapes=[pltpu.VMEM((TM, D_MODEL), BF16),
                        pltpu.VMEM((2, TM_P, D_MODEL), F32),
                        pltpu.SemaphoreType.DMA((N_X_CHUNKS,)),
                        pltpu.SemaphoreType.DMA((2,))],
        compiler_params=pltpu.CompilerParams(
            dimension_semantics=("arbitrary", "arbitrary"), vmem_limit_bytes=VMEM_LIMIT),
        name="ffn_final" if final_norm else "ffn",
    )(*args)


def _split_rows(z0, z1, p_ref, s_ref):
    p_ref[:TM_P, :] = z0[:TM_P]
    p_ref[TM_P:, :] = z1[:TM_P]
    s_ref[0, :TM_S, :] = z0[TM_P:]
    s_ref[0, TM_S:, :] = z1[TM_P:]


def _proj_kernel(xn_ref, w_ref, wg_ref, zp_ref, zs_ref, gp_ref, gs_ref):
    w = w_ref[...].astype(BF16)
    _split_rows(_bdot_t(xn_ref[0], w), _bdot_t(xn_ref[1], w), zp_ref, zs_ref)

    @pl.when(pl.program_id(1) == 0)
    def _():
        wg = wg_ref[...].astype(BF16)
        _split_rows(_bdot_t(xn_ref[0], wg), _bdot_t(xn_ref[1], wg), gp_ref, gs_ref)


def _proj(xn_c, w_t):
    n_n = (4 * D_A) // PROJ_TN
    return pl.pallas_call(
        _proj_kernel,
        grid=(N_MIX_BLOCKS, n_n),
        in_specs=[
            pl.BlockSpec((2, TM, D_MODEL), lambda i, j: (i, 0, 0)),
            pl.BlockSpec((PROJ_TN, D_MODEL), lambda i, j: (j, 0)),
            pl.BlockSpec((LANE, D_MODEL), lambda i, j: (GATE_ROW0 // LANE, 0)),
        ],
        out_specs=[
            pl.BlockSpec((MP, PROJ_TN), lambda i, j: (i, j)),
            pl.BlockSpec((1, MS, PROJ_TN), lambda i, j: (i, 0, j)),
            pl.BlockSpec((MP, LANE), lambda i, j: (i, 0)),
            pl.BlockSpec((1, MS, LANE), lambda i, j: (i, 0, 0)),
        ],
        out_shape=[
            jax.ShapeDtypeStruct((N_MIX_BLOCKS * MP, 4 * D_A), F32),
            jax.ShapeDtypeStruct((N_MIX_BLOCKS, MS, 4 * D_A), F32),
            jax.ShapeDtypeStruct((N_MIX_BLOCKS * MP, LANE), F32),
            jax.ShapeDtypeStruct((N_MIX_BLOCKS, MS, LANE), F32),
        ],
        compiler_params=pltpu.CompilerParams(
            dimension_semantics=("parallel", "arbitrary"), vmem_limit_bytes=VMEM_LIMIT),
        name="proj_qkvo",
    )(xn_c, w_t, w_t)


def _group_norm(yb, nrm):
    gw = D_B // G_B
    parts = []
    for g in range(yb.shape[1] // gw):
        seg = yb[:, g * gw:(g + 1) * gw]
        parts.append(seg * lax.rsqrt(jnp.mean(seg * seg, axis=-1, keepdims=True) + EPS))
    return jnp.concatenate(parts, axis=1) * nrm


def _conv_kernel(xn_ref, wgb_ref, wgc_ref, wxc_ref, cw_ref, cb_ref, nrm_ref, buf_ref, q_ref, c_ref,
                 yb_ref, cp_ref, cs_ref, qc_ref):
    q_t = jnp.concatenate([q_ref[...], jnp.zeros((LANE - SAMPLE_BS, D_A), F32)], axis=0).T
    for h in range(H_A):
        hs = slice(h * DK, (h + 1) * DK)
        for j in range(SAMPLE_BS):
            qc_ref[j:j + 1, hs] = jnp.sum(q_t[hs, j:j + 1] * c_ref[0, j, h], axis=0, keepdims=True)

    wgb = wgb_ref[...].astype(BF16)
    wgc = wgc_ref[...].astype(BF16)
    wxc = wxc_ref[...].astype(BF16)
    cw0 = cw_ref[0:1, :]
    cw1 = cw_ref[1:2, :]
    cw2 = cw_ref[2:3, :]
    cb = cb_ref[...]
    nrm = nrm_ref[...]
    row = lax.broadcasted_iota(jnp.int32, (TM_P, wgb.shape[0]), 0)
    zero_row = jnp.zeros((1, wgb.shape[0]), F32)
    um2, um1 = zero_row, zero_row

    for t in range(2):
        xn = xn_ref[t]
        u = _bdot_t(xn, wgc) * _bdot_t(xn, wxc)
        gb = _bdot_t(xn, wgb)

        up = u[:TM_P]
        u1 = jnp.where(row < 1, um1, pltpu.roll(up, 1, 0))
        u2 = jnp.where(row < 1, um2, jnp.where(row < 2, um1, pltpu.roll(up, 2, 0)))
        yc = cw0 * u2 + cw1 * u1 + cw2 * up + cb
        yb_ref[0, t * TM_P:(t + 1) * TM_P, :] = _group_norm(gb[:TM_P] * yc, nrm).astype(yb_ref.dtype)
        um2, um1 = up[TM_P - 2:TM_P - 1, :], up[TM_P - 1:TM_P, :]

        ss = slice(t * TM_S, (t + 1) * TM_S)
        us = u[TM_P:]
        b0 = buf_ref[0, ss, 0, :]
        b1 = buf_ref[0, ss, 1, :]
        ycs = cw0 * b0 + cw1 * b1 + cw2 * us + cb
        yb_ref[0, MP + t * TM_S:MP + (t + 1) * TM_S, :] = _group_norm(gb[TM_P:] * ycs, nrm).astype(yb_ref.dtype)
        cs_ref[0, ss, 0, :] = b1
        cs_ref[0, ss, 1, :] = us

    cp_ref[0, 0, 0:1, :] = um2
    cp_ref[0, 0, 1:2, :] = um1


def _conv(xn_c, w_t, cw, cb_row, nrm_row, buf, za_s, c0):
    tc = CONV_TC
    n_c = D_B // tc
    assert N_MIX_BLOCKS * n_c * SAMPLE_BS == za_s.shape[0], "one sample group per grid step"
    wspec = lambda base: pl.BlockSpec(
        (pl.Element(tc), pl.Element(D_MODEL)), lambda i, c: (pl.multiple_of(base + c * tc, 8), 0))
    grp = lambda i, c: i * n_c + c
    return pl.pallas_call(
        _conv_kernel,
        grid=(N_MIX_BLOCKS, n_c),
        in_specs=[
            pl.BlockSpec((2, TM, D_MODEL), lambda i, c: (i, 0, 0)),
            wspec(CONV_ROW0),
            wspec(CONV_ROW0 + D_B),
            wspec(CONV_ROW0 + 2 * D_B),
            pl.BlockSpec((3, tc), lambda i, c: (0, c)),
            pl.BlockSpec((1, tc), lambda i, c: (0, c)),
            pl.BlockSpec((1, tc), lambda i, c: (0, c)),
            pl.BlockSpec((1, MS, 2, tc), lambda i, c: (i, 0, 0, c)),
            pl.BlockSpec((SAMPLE_BS, D_A), lambda i, c: (grp(i, c), 0)),
            pl.BlockSpec((1, SAMPLE_BS, H_A, DK, DV), lambda i, c: (0, grp(i, c), 0, 0, 0)),
        ],
        out_specs=[
            pl.BlockSpec((1, MP + MS, tc), lambda i, c: (i, 0, c)),
            pl.BlockSpec((1, 1, 2, tc), lambda i, c: (0, i, 0, c)),
            pl.BlockSpec((1, MS, 2, tc), lambda i, c: (i, 0, 0, c)),
            pl.BlockSpec((SAMPLE_BS, D_A), lambda i, c: (grp(i, c), 0)),
        ],
        out_shape=[
            jax.ShapeDtypeStruct((N_MIX_BLOCKS, MP + MS, D_B), BF16),
            jax.ShapeDtypeStruct((1, N_MIX_BLOCKS, 2, D_B), F32),
            jax.ShapeDtypeStruct((N_MIX_BLOCKS, MS, 2, D_B), F32),
            jax.ShapeDtypeStruct((za_s.shape[0], D_A), F32),
        ],
        compiler_params=pltpu.CompilerParams(
            dimension_semantics=("parallel", "arbitrary"), vmem_limit_bytes=VMEM_LIMIT),
        name="proj_conv",
    )(xn_c, w_t, w_t, w_t, cw, cb_row, nrm_row, buf, za_s, c0)


def _mlstm_prompt_kernel(q_ref, k_ref, v_ref, o_ref, g_ref, bias_ref, nrm_ref,
                         ha_ref, c_ref, n_ref, m_ref, cx_s, m_s, *, n_chunks):
    L = MLSTM_CHUNK
    c = pl.program_id(1)

    @pl.when(c == 0)
    def _():
        cx_s[...] = jnp.zeros_like(cx_s)
        m_s[...] = jnp.zeros_like(m_s)

    row = lax.broadcasted_iota(jnp.int32, (L, L), 0)
    col = lax.broadcasted_iota(jnp.int32, (L, L), 1)
    causal = row >= col
    tri = causal.astype(BF16)

    for b in range(MLSTM_SEQS):
        _mlstm_prompt_seq(b, causal, tri, q_ref, k_ref, v_ref, o_ref, g_ref, bias_ref, nrm_ref,
                          ha_ref, cx_s, m_s)

    @pl.when(c == n_chunks - 1)
    def _():
        for b in range(MLSTM_SEQS):
            m_ref[b] = m_s[b]
            for h in range(H_A):
                cx = cx_s[b * H_A + h]
                c_ref[0, b, h] = cx[:, :DV]
                n_ref[0, b, h:h + 1, :] = cx[:, DV:].T[h:h + 1, :]


def _prefix_max_rows(x):
    rows = x.shape[0]
    row = lax.broadcasted_iota(jnp.int32, x.shape, 0)
    k = 1
    while k < rows:
        x = jnp.where(row >= k, jnp.maximum(x, pltpu.roll(x, k, 0)), x)
        k *= 2
    return x


def _mlstm_prompt_seq(b, causal, tri, q_ref, k_ref, v_ref, o_ref, g_ref, bias_ref, nrm_ref,
                      ha_ref, cx_s, m_s):
    L = MLSTM_CHUNK
    lane = lax.broadcasted_iota(jnp.int32, (L, LANE), 1)
    lane_dv = lax.broadcasted_iota(jnp.int32, (DV, LANE), 1)
    capped = _soft_cap(g_ref[b] + bias_ref[...])
    lf = _log_sigmoid(capped)
    hi = lf.astype(BF16)
    r1 = lf - hi.astype(F32)
    mid = r1.astype(BF16)
    lo = (r1 - mid.astype(F32)).astype(BF16)
    parts = _bdot(tri, jnp.concatenate([hi, mid, lo], axis=1))
    bc = parts[:, :LANE] + parts[:, LANE:2 * LANE] + parts[:, 2 * LANE:]
    bh = pltpu.roll(bc, LANE - H_A, 1)
    a = capped - bh
    a_t = a.T
    m_prev = m_s[b]
    big_m = jnp.maximum(_prefix_max_rows(a), m_prev[0:1, :])
    sc_all = jnp.exp(m_prev[0:1, :] - big_m)
    emt_all = jnp.exp(-(bh + big_m))
    m_last = big_m[L - 1:L, :]
    w_all = jnp.exp(a - m_last)
    decay_all = sc_all[L - 1:L, :]
    m_s[b] = jnp.broadcast_to(bh[L - 1:L, :] + m_last, (8, LANE))

    nums = []
    nd_tail = jnp.zeros((L, LANE), F32)
    sq_tail = jnp.zeros((L, LANE), F32)
    for h in range(H_A):
        hs = slice(h * DK, (h + 1) * DK)
        onehot = (lane == h).astype(BF16)
        qb = q_ref[b, :, hs].astype(BF16)
        kf = k_ref[b, :, hs] * (DK ** -0.5)
        kb = kf.astype(BF16)
        vx = jnp.concatenate([v_ref[b, :, hs].astype(BF16), onehot], axis=1)
        cx_prev = cx_s[b * H_A + h]

        p = jnp.exp(jnp.where(causal, a_t[h:h + 1, :] - big_m[:, h:h + 1], -jnp.inf))
        s = _bdot_t(qb, kb) * p
        nd = sc_all[:, h:h + 1] * _bdot(qb, cx_prev.astype(BF16)) + _bdot(s.astype(BF16), vx)
        num = nd[:, :DV]
        nums.append(num)
        nd_tail = nd_tail + nd[:, DV:]
        sq_tail = sq_tail + _bdot((num * num).astype(BF16), (lane_dv == h).astype(BF16))

        wk = (w_all[:, h:h + 1] * kf).astype(BF16)
        cx_s[b * H_A + h] = decay_all[:, h:h + 1] * cx_prev + lax.dot_general(
            wk, vx, (((0,), (0,)), ((), ())), preferred_element_type=F32)

    inv = 1.0 / jnp.maximum(jnp.abs(nd_tail), emt_all)
    ms = inv * inv * sq_tail * (1.0 / DV)
    scale_all = inv * lax.rsqrt(ms + EPS)
    for h in range(H_A):
        hs = slice(h * DK, (h + 1) * DK)
        hn = nums[h] * scale_all[:, h:h + 1]
        ha_ref[b, :, hs] = (hn * nrm_ref[:, hs] * jax.nn.sigmoid(o_ref[b, :, hs])).astype(ha_ref.dtype)


def _mlstm_prompt(za_p, g_p, bias_row, nrm_row, batch, seq):
    L = MLSTM_CHUNK
    nsq = MLSTM_SEQS
    n_chunks = seq // L
    kern = functools.partial(_mlstm_prompt_kernel, n_chunks=n_chunks)
    za3 = za_p.reshape(batch, seq, 4 * D_A)
    g3 = g_p.reshape(batch, seq, LANE)
    zspec = lambda part: pl.BlockSpec((nsq, L, D_A), lambda b, c: (b, c, part))
    return pl.pallas_call(
        kern,
        grid=(batch // nsq, n_chunks),
        in_specs=[
            zspec(0), zspec(1), zspec(2), zspec(3),
            pl.BlockSpec((nsq, L, LANE), lambda b, c: (b, c, 0)),
            pl.BlockSpec((1, LANE), lambda b, c: (0, 0)),
            pl.BlockSpec((1, D_A), lambda b, c: (0, 0)),
        ],
        out_specs=[
            pl.BlockSpec((nsq, L, D_A), lambda b, c: (b, c, 0)),
            pl.BlockSpec((1, nsq, H_A, DK, DV), lambda b, c: (0, b, 0, 0, 0)),
            pl.BlockSpec((1, nsq, H_A, DK), lambda b, c: (0, b, 0, 0)),
            pl.BlockSpec((nsq, 8, LANE), lambda b, c: (b, 0, 0)),
        ],
        out_shape=[
            jax.ShapeDtypeStruct((batch, seq, D_A), BF16),
            jax.ShapeDtypeStruct((1, batch, H_A, DK, DV), F32),
            jax.ShapeDtypeStruct((1, batch, H_A, DK), F32),
            jax.ShapeDtypeStruct((batch, 8, LANE), F32),
        ],
        scratch_shapes=[pltpu.VMEM((nsq * H_A, DK, DV + LANE), F32),
                        pltpu.VMEM((nsq, 8, LANE), F32)],
        compiler_params=pltpu.CompilerParams(
            dimension_semantics=("parallel", "arbitrary"), vmem_limit_bytes=VMEM_LIMIT),
        name="mlstm_prompt",
    )(za3, za3, za3, za3, g3, bias_row, nrm_row)


def _expand_heads(x, width):
    rows = x.shape[0]
    return jnp.concatenate([jnp.broadcast_to(x[:, h:h + 1], (rows, width)) for h in range(H_A)], axis=1)


def _head_sums(x):
    return jnp.concatenate(
        [jnp.sum(x[:, h * DK:(h + 1) * DK], axis=-1, keepdims=True) for h in range(H_A)], axis=1)


def _mlstm_sample_kernel(za_ref, qc_ref, g_ref, bias_ref, nrm_ref, m_ref, n_ref,
                         ha_ref, m_out, n_out, wk_out, dec_out):
    nb = za_ref.shape[0]
    capped = _soft_cap(g_ref[...] + bias_ref[...])
    logi = capped[:, 0:H_A]
    logf = _log_sigmoid(capped)[:, H_A:2 * H_A]
    m_prev = m_ref[...]
    m_inter = logf + m_prev
    m_t = jnp.maximum(m_inter, logi)
    sc = jnp.exp(m_inter - m_t)
    ei = jnp.exp(logi - m_t)
    emt = jnp.exp(-m_t)

    q = za_ref[:, 0:D_A]
    k = za_ref[:, D_A:2 * D_A] * (DK ** -0.5)
    v = za_ref[:, 2 * D_A:3 * D_A]
    o = za_ref[:, 3 * D_A:4 * D_A]
    n_prev = n_ref[...]

    s = _head_sums(q * k) * ei
    den = sc * _head_sums(q * n_prev) + s
    denom = jnp.maximum(jnp.abs(den), emt)
    num = _expand_heads(sc, DV) * qc_ref[...] + _expand_heads(s, DV) * v
    hh = num / _expand_heads(denom, DV)
    ms = _head_sums(hh * hh) * (1.0 / DV)
    hn = hh * lax.rsqrt(_expand_heads(ms, DV) + EPS)
    ha_ref[...] = (hn * nrm_ref[...] * jax.nn.sigmoid(o)).astype(ha_ref.dtype)
    n_out[...] = _expand_heads(sc, DK) * n_prev + _expand_heads(ei, DK) * k
    m_out[...] = m_t
    pad = jnp.zeros((nb, LANE - H_A), F32)
    dec_out[...] = jnp.concatenate([sc, pad], axis=1)

    k_t = k.T
    ei_t = jnp.concatenate([ei, pad], axis=1).T
    for h in range(H_A):
        hs = slice(h * DK, (h + 1) * DK)
        wk_out[hs, :] = k_t[hs, :] * ei_t[h:h + 1, :]


def _mlstm_sample(za_s, qc, g_s, bias_row, nrm_row, m0, n0):
    nb = za_s.shape[0]
    assert nb == LANE, "the sequence axis is transposed onto the lanes"
    return pl.pallas_call(
        _mlstm_sample_kernel,
        out_shape=[
            jax.ShapeDtypeStruct((nb, D_A), BF16),
            jax.ShapeDtypeStruct((nb, H_A), F32),
            jax.ShapeDtypeStruct((nb, H_A * DK), F32),
            jax.ShapeDtypeStruct((D_A, nb), F32),
            jax.ShapeDtypeStruct((nb, LANE), F32),
        ],
        compiler_params=pltpu.CompilerParams(vmem_limit_bytes=VMEM_LIMIT),
        name="mlstm_sample",
    )(za_s, qc, g_s, bias_row, nrm_row, m0, n0)


def _outproj_kernel(hp_ref, hs_ref, ap_ref, as_ref, yb_ref, w_ref, op_ref, os_ref, wc_ref):
    j = pl.program_id(1)

    @pl.when(pl.program_id(0) == 0)
    def _():
        wc_ref[j] = w_ref[...].astype(BF16)

    wa = wc_ref[j, :D_A, :]
    wb = wc_ref[j, D_A:, :]
    op_ref[...] = hp_ref[...] + _bdot(ap_ref[...], wa) + _bdot(yb_ref[0, :MP, :], wb)
    os_ref[0] = hs_ref[0] + _bdot(as_ref[0], wa) + _bdot(yb_ref[0, MP:, :], wb)


def _outproj(hp, hs, ap, a_s, yb_c, w3):
    tn = OUT_TN
    n_n = D_MODEL // tn
    return pl.pallas_call(
        _outproj_kernel,
        grid=(N_MIX_BLOCKS, n_n),
        in_specs=[
            pl.BlockSpec((MP, tn), lambda i, j: (i, j)),
            pl.BlockSpec((1, MS, tn), lambda i, j: (i, 0, j)),
            pl.BlockSpec((MP, D_A), lambda i, j: (i, 0)),
            pl.BlockSpec((1, MS, D_A), lambda i, j: (i, 0, 0)),
            pl.BlockSpec((1, MP + MS, D_B), lambda i, j: (i, 0, 0)),
            pl.BlockSpec((None, D_A + D_B, tn), lambda i, j: (0, 0, jnp.where(i == 0, j, n_n - 1))),
        ],
        out_specs=[
            pl.BlockSpec((MP, tn), lambda i, j: (i, j)),
            pl.BlockSpec((1, MS, tn), lambda i, j: (i, 0, j)),
        ],
        out_shape=[
            jax.ShapeDtypeStruct((N_MIX_BLOCKS * MP, D_MODEL), F32),
            jax.ShapeDtypeStruct((N_MIX_BLOCKS, MS, D_MODEL), F32),
        ],
        scratch_shapes=[pltpu.VMEM((n_n, D_A + D_B, tn), BF16)],
        compiler_params=pltpu.CompilerParams(
            dimension_semantics=("arbitrary", "arbitrary"), vmem_limit_bytes=VMEM_LIMIT),
        name="outproj",
    )(hp, hs, ap, a_s, yb_c, w3)


def kernel(x_prompt, x_sample, state_mlstm_C, state_mlstm_n, state_mlstm_m, state_conv, norm_ffn1, ffn1_gate, ffn1_up, ffn1_down, norm_mix, w_in, b_gates, conv_w, conv_b, norm_mlstm, norm_conv, w_out, norm_ffn2, ffn2_gate, ffn2_up, ffn2_down, norm_final):
    batch, seq, _ = x_prompt.shape
    nb = x_sample.shape[0]
    assert batch == N_MIX_BLOCKS and seq == MP and nb == N_MIX_BLOCKS * MS
    assert norm_ffn1.shape[0] == 1, "single-layer trunk"

    xp = x_prompt.reshape(batch * seq, D_MODEL)
    xs = x_sample.reshape(N_FFN_BLOCKS, TM_S, D_MODEL)

    h1p, h1s, xn_c = _ffn(xp, xs, norm_ffn1, ffn1_gate, ffn1_up, ffn1_down, norm_mix, TF1)

    w_t = jnp.swapaxes(w_in[0], 0, 1)
    za_p, za_s, g_p, g_s = _proj(xn_c, w_t)
    za_s = za_s.reshape(nb, 4 * D_A)
    yb_c, conv_p, conv_s, qc_s = _conv(xn_c, w_t, conv_w[0], conv_b, norm_conv,
                                       state_conv.reshape(N_MIX_BLOCKS, MS, 2, D_B), za_s, state_mlstm_C)

    bias_row = jnp.zeros((1, LANE), F32).at[0, :2 * H_A].set(b_gates[0].astype(F32))
    ha_p, c_p, n_p, m_p = _mlstm_prompt(za_p, g_p, bias_row, norm_mlstm, batch, seq)
    ha_s, m_s, n_s, wk_s, dec_s = _mlstm_sample(
        za_s, qc_s, g_s.reshape(nb, LANE), bias_row, norm_mlstm,
        state_mlstm_m.reshape(nb, H_A), state_mlstm_n.reshape(nb, H_A * DK))

    h2p, h2s = _outproj(h1p, h1s.reshape(N_MIX_BLOCKS, MS, D_MODEL), ha_p.reshape(batch * seq, D_A),
                        ha_s.reshape(N_MIX_BLOCKS, MS, D_A), yb_c, w_out)

    yp, ys, c_s = _ffn(h2p, h2s.reshape(N_FFN_BLOCKS, TM_S, D_MODEL), norm_ffn2, ffn2_gate, ffn2_up,
                       ffn2_down, norm_final.reshape(1, D_MODEL), TF2,
                       state=(state_mlstm_C, wk_s, za_s, dec_s))

    return (
        yp.reshape(batch, seq, D_MODEL),
        ys.reshape(nb, 1, D_MODEL),
        c_p,
        n_p,
        m_p[:, 0, :H_A].reshape(1, batch, H_A),
        conv_p,
        c_s,
        n_s.reshape(1, nb, H_A, DK),
        m_s.reshape(1, nb, H_A),
        conv_s.reshape(1, nb, 2, D_B),
    )
```

```python
import functools

import jax
import jax.numpy as jnp
from jax import lax
from jax.experimental import pallas as pl
from jax.experimental.pallas import tpu as pltpu

F32 = jnp.float32
BF16 = jnp.bfloat16

D_MODEL = 2048
D_A = 1024
D_B = 1024
H_A = 4
DK = 256
DV = 256
G_B = 8
D_FF = 5504
GATE_CAP = 15.0
EPS = 1e-6
GATE_ROW0 = 4 * D_A
CONV_ROW0 = 4 * D_A + 2 * H_A

LANE = 128
N_FFN_BLOCKS = 8
TM_P = 1024
TM_S = 16
TM = TM_P + TM_S
TF1 = 512
TF2 = 512
STATE_SEQS = 2
X_CHUNK = 128
N_X_CHUNKS = TM_P // X_CHUNK
X_AHEAD0 = 2
N_MIX_BLOCKS = N_FFN_BLOCKS // 2
MP = 2 * TM_P
MS = 2 * TM_S
PROJ_TN = 1024
CONV_TC = 256
OUT_TN = 512
MLSTM_CHUNK = 256
MLSTM_SEQS = 2
SAMPLE_BS = 8
VMEM_LIMIT = 62 * 1024 * 1024


def _rms(x, g):
    return x * lax.rsqrt(jnp.mean(x * x, axis=-1, keepdims=True) + EPS) * g


def _soft_cap(x):
    return GATE_CAP * jnp.tanh(x / GATE_CAP)


def _log_sigmoid(x):
    return -jax.nn.softplus(-x)


def _bdot(a, b):
    return jnp.dot(a, b, preferred_element_type=F32)


def _bdot_t(a, b):
    return lax.dot_general(a, b, (((1,), (1,)), ((), ())), preferred_element_type=F32)


def _sample_state_update(f, c_ref, wk_ref, v_ref, dec_ref, c_out):
    grp = jnp.minimum(f, TM_S // STATE_SEQS - 1)
    lane = lax.broadcasted_iota(jnp.int32, (D_A, LANE), 1)
    for u in range(STATE_SEQS):
        r = grp * STATE_SEQS + u
        seq = pl.program_id(0) * TM_S + r
        wk_col = jnp.sum(jnp.where(lane == seq, wk_ref[...], 0.0), axis=1, keepdims=True)
        v_row = v_ref[pl.ds(r, 1), :]
        dec = dec_ref[pl.ds(r, 1), :]
        for h in range(H_A):
            hs = slice(h * DK, (h + 1) * DK)
            c_out[0, u, h] = dec[:, h:h + 1] * c_ref[0, u, h] + wk_col[hs, :] * v_row[:, hs]


def _ffn_kernel(x_hbm, xs_ref, g_ref, wg_ref, wu_ref, wd_ref, g2_ref, *rest,
                tf, n_f, last_valid, final_norm):
    if final_norm:
        c_ref, wk_ref, v_ref, dec_ref, out_hbm, os_ref, c_out, xn_ref, acc_ref, x_sem, o_sem = rest
    else:
        wgate_ref, out_hbm, os_ref, nxt_ref, gp_ref, gs_ref, xn_ref, acc_ref, x_sem, o_sem = rest
    i = pl.program_id(0)
    f = pl.program_id(1)
    slot = lax.rem(i, 2)
    op_ref = acc_ref.at[slot]

    def x_copy(blk, s, c):
        dst0 = pl.multiple_of(c * X_CHUNK, X_CHUNK)
        src0 = pl.multiple_of(blk * TM_P + c * X_CHUNK, X_CHUNK)
        return pltpu.make_async_copy(x_hbm.at[pl.ds(src0, X_CHUNK), :],
                                     acc_ref.at[s, pl.ds(dst0, X_CHUNK), :], x_sem.at[c])

    def out_copy(blk, s):
        row0 = pl.multiple_of(blk * TM_P, TM_P)
        return pltpu.make_async_copy(acc_ref.at[s], out_hbm.at[pl.ds(row0, TM_P), :], o_sem.at[s])

    @pl.when(jnp.logical_and(i == 0, f == 0))
    def _():
        for c in range(N_X_CHUNKS):
            x_copy(0, 0, c).start()

    @pl.when(f == 0)
    def _():
        for c in range(N_X_CHUNKS):
            x_copy(i, slot, c).wait()

    @pl.when(jnp.logical_and(i > 0, f == X_AHEAD0 - 1))
    def _():
        out_copy(i - 1, 1 - slot).wait()

    @pl.when(jnp.logical_and(jnp.logical_and(f >= X_AHEAD0, f < X_AHEAD0 + N_X_CHUNKS),
                             i < N_FFN_BLOCKS - 1))
    def _():
        x_copy(i + 1, 1 - slot, f - X_AHEAD0).start()

    def step(valid, first, with_state):
        if first:
            g = g_ref[...]
            xn_ref[:TM_P, :] = _rms(op_ref[...], g).astype(BF16)
            xn_ref[TM_P:, :] = _rms(xs_ref[0], g).astype(BF16)
        xn = xn_ref[...]
        a = _bdot(xn, wg_ref[:, :valid].astype(BF16))
        b = _bdot(xn, wu_ref[:, :valid].astype(BF16))
        hid = (a * jax.nn.sigmoid(a) * b * 0.5).astype(BF16)
        r = _bdot(hid, wd_ref[:valid, :].astype(BF16))
        op_ref[...] += r[:TM_P]
        if first:
            os_ref[0] = xs_ref[0] + r[TM_P:]
        else:
            os_ref[0] += r[TM_P:]
        if with_state:
            _sample_state_update(f, c_ref, wk_ref, v_ref, dec_ref, c_out)

    assert last_valid < tf and (not final_norm or (n_f - 1) * STATE_SEQS >= TM_S), \
        "state updates ride the full-width steps"

    @pl.when(f == 0)
    def _():
        step(tf, True, final_norm)

    @pl.when(jnp.logical_and(f > 0, f < n_f - 1))
    def _():
        step(tf, False, final_norm)

    @pl.when(f == n_f - 1)
    def _():
        step(last_valid, False, False)
        g2 = g2_ref[...]
        if final_norm:
            op_ref[...] = _rms(op_ref[...], g2)
            os_ref[0] = _rms(os_ref[0], g2)
        else:
            nx_p = _rms(op_ref[...], g2).astype(BF16)
            nx_s = _rms(os_ref[0], g2).astype(BF16)
            nxt_ref[0, :TM_P, :] = nx_p
            nxt_ref[0, TM_P:, :] = nx_s
            wgt = wgate_ref[...].astype(BF16)
            half = TM_P // 2
            gp_ref[:half, :] = _bdot_t(nx_p[:half], wgt)
            gp_ref[half:, :] = _bdot_t(nx_p[half:], wgt)
            gs_ref[0] = _bdot_t(nx_s, wgt)

        out_copy(i, slot).start()

        @pl.when(i == N_FFN_BLOCKS - 1)
        def _():
            out_copy(i, slot).wait()


def _ffn(xp, xs, g, wg, wu, wd, g2, tf, state=None, w_gate_t=None):
    final_norm = state is not None
    n_f = pl.cdiv(D_FF, tf)
    last_valid = D_FF - (n_f - 1) * tf
    kern = functools.partial(_ffn_kernel, tf=tf, n_f=n_f, last_valid=last_valid, final_norm=final_norm)
    assert X_AHEAD0 + N_X_CHUNKS <= n_f and N_X_CHUNKS * X_CHUNK == TM_P
    in_specs = [
        pl.BlockSpec(memory_space=pl.ANY),
        pl.BlockSpec((1, TM_S, D_MODEL), lambda i, f: (i, 0, 0)),
        pl.BlockSpec((1, D_MODEL), lambda i, f: (0, 0)),
        pl.BlockSpec((None, D_MODEL, tf), lambda i, f: (0, 0, f)),
        pl.BlockSpec((None, D_MODEL, tf), lambda i, f: (0, 0, f)),
        pl.BlockSpec((None, tf, D_MODEL), lambda i, f: (0, f, 0)),
        pl.BlockSpec((1, D_MODEL), lambda i, f: (0, 0)),
    ]
    out_specs = [
        pl.BlockSpec(memory_space=pl.ANY),
        pl.BlockSpec((1, TM_S, D_MODEL), lambda i, f: (i, 0, 0)),
    ]
    out_shape = [
        jax.ShapeDtypeStruct((N_FFN_BLOCKS * TM_P, D_MODEL), F32),
        jax.ShapeDtypeStruct((N_FFN_BLOCKS, TM_S, D_MODEL), F32),
    ]
    args = [xp, xs, g, wg, wu, wd, g2]
    if final_norm:
        c0, wk, za_s, dec = state
        groups = TM_S // STATE_SEQS
        grp = lambda i, f: i * groups + jnp.minimum(f, groups - 1)
        cspec = pl.BlockSpec((1, STATE_SEQS, H_A, DK, DV), lambda i, f: (0, grp(i, f), 0, 0, 0))
        in_specs += [
            cspec,
            pl.BlockSpec((D_A, LANE), lambda i, f: (0, 0)),
            pl.BlockSpec((TM_S, D_A), lambda i, f: (i, 2)),
            pl.BlockSpec((TM_S, LANE), lambda i, f: (i, 0)),
        ]
        out_specs.append(cspec)
        out_shape.append(jax.ShapeDtypeStruct(c0.shape, F32))
        args += [c0, wk, za_s, dec]
    else:
        in_specs.append(pl.BlockSpec((LANE, D_MODEL), lambda i, f: (GATE_ROW0 // LANE, 0)))
        args.append(w_gate_t)
        out_specs += [
            pl.BlockSpec((1, TM, D_MODEL), lambda i, f: (i, 0, 0)),
            pl.BlockSpec((TM_P, LANE), lambda i, f: (i, 0)),
            pl.BlockSpec((1, TM_S, LANE), lambda i, f: (i, 0, 0)),
        ]
        out_shape += [
            jax.ShapeDtypeStruct((N_FFN_BLOCKS, TM, D_MODEL), BF16),
            jax.ShapeDtypeStruct((N_FFN_BLOCKS * TM_P, LANE), F32),
            jax.ShapeDtypeStruct((N_FFN_BLOCKS, TM_S, LANE), F32),
        ]
    return pl.pallas_call(
        kern,
        grid=(N_FFN_BLOCKS, n_f),
        in_specs=in_specs,
        out_specs=out_specs,
        out_shape=out_shape,
        scratch_shapes=[pltpu.VMEM((TM, D_MODEL), BF16),
                        pltpu.VMEM((2, TM_P, D_MODEL), F32),
                        pltpu.SemaphoreType.DMA((N_X_CHUNKS,)),
                        pltpu.SemaphoreType.DMA((2,))],
        compiler_params=pltpu.CompilerParams(
            dimension_semantics=("arbitrary", "arbitrary"), vmem_limit_bytes=VMEM_LIMIT),
        name="ffn_final" if final_norm else "ffn",
    )(*args)


def _split_rows(z0, z1, p_ref, s_ref):
    p_ref[:TM_P, :] = z0[:TM_P]
    p_ref[TM_P:, :] = z1[:TM_P]
    s_ref[0, :TM_S, :] = z0[TM_P:]
    s_ref[0, TM_S:, :] = z1[TM_P:]


def _proj_kernel(xn_ref, w_ref, zp_ref, zs_ref):
    w = w_ref[...].astype(BF16)
    _split_rows(_bdot_t(xn_ref[0], w), _bdot_t(xn_ref[1], w), zp_ref, zs_ref)


def _proj(xn_c, w_t):
    n_n = (4 * D_A) // PROJ_TN
    return pl.pallas_call(
        _proj_kernel,
        grid=(N_MIX_BLOCKS, n_n),
        in_specs=[
            pl.BlockSpec((2, TM, D_MODEL), lambda i, j: (i, 0, 0)),
            pl.BlockSpec((PROJ_TN, D_MODEL), lambda i, j: (j, 0)),
        ],
        out_specs=[
            pl.BlockSpec((MP, PROJ_TN), lambda i, j: (i, j)),
            pl.BlockSpec((1, MS, PROJ_TN), lambda i, j: (i, 0, j)),
        ],
        out_shape=[
            jax.ShapeDtypeStruct((N_MIX_BLOCKS * MP, 4 * D_A), F32),
            jax.ShapeDtypeStruct((N_MIX_BLOCKS, MS, 4 * D_A), F32),
        ],
        compiler_params=pltpu.CompilerParams(
            dimension_semantics=("parallel", "arbitrary"), vmem_limit_bytes=VMEM_LIMIT),
        name="proj_qkvo",
    )(xn_c, w_t)


def _group_norm(yb, nrm):
    gw = D_B // G_B
    parts = []
    for g in range(yb.shape[1] // gw):
        seg = yb[:, g * gw:(g + 1) * gw]
        parts.append(seg * lax.rsqrt(jnp.mean(seg * seg, axis=-1, keepdims=True) + EPS))
    return jnp.concatenate(parts, axis=1) * nrm


def _conv_kernel(xn_ref, wgb_ref, wgc_ref, wxc_ref, cw_ref, cb_ref, nrm_ref, buf_ref, q_ref, c_ref,
                 yb_ref, cp_ref, cs_ref, qc_ref):
    q_t = jnp.concatenate([q_ref[...], jnp.zeros((LANE - SAMPLE_BS, D_A), F32)], axis=0).T
    for h in range(H_A):
        hs = slice(h * DK, (h + 1) * DK)
        for j in range(SAMPLE_BS):
            qc_ref[j:j + 1, hs] = jnp.sum(q_t[hs, j:j + 1] * c_ref[0, j, h], axis=0, keepdims=True)

    wgb = wgb_ref[...].astype(BF16)
    wgc = wgc_ref[...].astype(BF16)
    wxc = wxc_ref[...].astype(BF16)
    cw0 = cw_ref[0:1, :]
    cw1 = cw_ref[1:2, :]
    cw2 = cw_ref[2:3, :]
    cb = cb_ref[...]
    nrm = nrm_ref[...]
    row = lax.broadcasted_iota(jnp.int32, (TM_P, wgb.shape[0]), 0)
    zero_row = jnp.zeros((1, wgb.shape[0]), F32)
    um2, um1 = zero_row, zero_row

    for t in range(2):
        xn = xn_ref[t]
        u = _bdot_t(xn, wgc) * _bdot_t(xn, wxc)
        gb = _bdot_t(xn, wgb)

        up = u[:TM_P]
        u1 = jnp.where(row < 1, um1, pltpu.roll(up, 1, 0))
        u2 = jnp.where(row < 1, um2, jnp.where(row < 2, um1, pltpu.roll(up, 2, 0)))
        yc = cw0 * u2 + cw1 * u1 + cw2 * up + cb
        yb_ref[0, t * TM_P:(t + 1) * TM_P, :] = _group_norm(gb[:TM_P] * yc, nrm).astype(yb_ref.dtype)
        um2, um1 = up[TM_P - 2:TM_P - 1, :], up[TM_P - 1:TM_P, :]

        ss = slice(t * TM_S, (t + 1) * TM_S)
        us = u[TM_P:]
        b0 = buf_ref[0, ss, 0, :]
        b1 = buf_ref[0, ss, 1, :]
        ycs = cw0 * b0 + cw1 * b1 + cw2 * us + cb
        yb_ref[0, MP + t * TM_S:MP + (t + 1) * TM_S, :] = _group_norm(gb[TM_P:] * ycs, nrm).astype(yb_ref.dtype)
        cs_ref[0, ss, 0, :] = b1
        cs_ref[0, ss, 1, :] = us

    cp_ref[0, 0, 0:1, :] = um2
    cp_ref[0, 0, 1:2, :] = um1


def _conv(xn_c, w_t, cw, cb_row, nrm_row, buf, za_s, c0):
    tc = CONV_TC
    n_c = D_B // tc
    assert N_MIX_BLOCKS * n_c * SAMPLE_BS == za_s.shape[0], "one sample group per grid step"
    wspec = lambda base: pl.BlockSpec(
        (pl.Element(tc), pl.Element(D_MODEL)), lambda i, c: (pl.multiple_of(base + c * tc, 8), 0))
    grp = lambda i, c: i * n_c + c
    return pl.pallas_call(
        _conv_kernel,
        grid=(N_MIX_BLOCKS, n_c),
        in_specs=[
            pl.BlockSpec((2, TM, D_MODEL), lambda i, c: (i, 0, 0)),
            wspec(CONV_ROW0),
            wspec(CONV_ROW0 + D_B),
            wspec(CONV_ROW0 + 2 * D_B),
            pl.BlockSpec((3, tc), lambda i, c: (0, c)),
            pl.BlockSpec((1, tc), lambda i, c: (0, c)),
            pl.BlockSpec((1, tc), lambda i, c: (0, c)),
            pl.BlockSpec((1, MS, 2, tc), lambda i, c: (i, 0, 0, c)),
            pl.BlockSpec((SAMPLE_BS, D_A), lambda i, c: (grp(i, c), 0)),
            pl.BlockSpec((1, SAMPLE_BS, H_A, DK, DV), lambda i, c: (0, grp(i, c), 0, 0, 0)),
        ],
        out_specs=[
            pl.BlockSpec((1, MP + MS, tc), lambda i, c: (i, 0, c)),
            pl.BlockSpec((1, 1, 2, tc), lambda i, c: (0, i, 0, c)),
            pl.BlockSpec((1, MS, 2, tc), lambda i, c: (i, 0, 0, c)),
            pl.BlockSpec((SAMPLE_BS, D_A), lambda i, c: (grp(i, c), 0)),
        ],
        out_shape=[
            jax.ShapeDtypeStruct((N_MIX_BLOCKS, MP + MS, D_B), BF16),
            jax.ShapeDtypeStruct((1, N_MIX_BLOCKS, 2, D_B), F32),
            jax.ShapeDtypeStruct((N_MIX_BLOCKS, MS, 2, D_B), F32),
            jax.ShapeDtypeStruct((za_s.shape[0], D_A), F32),
        ],
        compiler_params=pltpu.CompilerParams(
            dimension_semantics=("parallel", "arbitrary"), vmem_limit_bytes=VMEM_LIMIT),
        name="proj_conv",
    )(xn_c, w_t, w_t, w_t, cw, cb_row, nrm_row, buf, za_s, c0)


def _mlstm_prompt_kernel(q_ref, k_ref, v_ref, o_ref, g_ref, bias_ref, nrm_ref,
                         ha_ref, c_ref, n_ref, m_ref, cx_s, m_s, *, n_chunks):
    L = MLSTM_CHUNK
    c = pl.program_id(1)

    @pl.when(c == 0)
    def _():
        cx_s[...] = jnp.zeros_like(cx_s)
        m_s[...] = jnp.zeros_like(m_s)

    row = lax.broadcasted_iota(jnp.int32, (L, L), 0)
    col = lax.broadcasted_iota(jnp.int32, (L, L), 1)
    causal = row >= col
    tri = causal.astype(BF16)

    for b in range(MLSTM_SEQS):
        _mlstm_prompt_seq(b, causal, tri, q_ref, k_ref, v_ref, o_ref, g_ref, bias_ref, nrm_ref,
                          ha_ref, cx_s, m_s)

    @pl.when(c == n_chunks - 1)
    def _():
        for b in range(MLSTM_SEQS):
            m_ref[b] = m_s[b]
            for h in range(H_A):
                cx = cx_s[b * H_A + h]
                c_ref[0, b, h] = cx[:, :DV]
                n_ref[0, b, h:h + 1, :] = cx[:, DV:].T[h:h + 1, :]


def _prefix_max_rows(x):
    rows = x.shape[0]
    row = lax.broadcasted_iota(jnp.int32, x.shape, 0)
    k = 1
    while k < rows:
        x = jnp.where(row >= k, jnp.maximum(x, pltpu.roll(x, k, 0)), x)
        k *= 2
    return x


def _mlstm_prompt_seq(b, causal, tri, q_ref, k_ref, v_ref, o_ref, g_ref, bias_ref, nrm_ref,
                      ha_ref, cx_s, m_s):
    L = MLSTM_CHUNK
    lane = lax.broadcasted_iota(jnp.int32, (L, LANE), 1)
    lane_dv = lax.broadcasted_iota(jnp.int32, (DV, LANE), 1)
    capped = _soft_cap(g_ref[b] + bias_ref[...])
    lf = _log_sigmoid(capped)
    hi = lf.astype(BF16)
    r1 = lf - hi.astype(F32)
    mid = r1.astype(BF16)
    lo = (r1 - mid.astype(F32)).astype(BF16)
    parts = _bdot(tri, jnp.concatenate([hi, mid, lo], axis=1))
    bc = parts[:, :LANE] + parts[:, LANE:2 * LANE] + parts[:, 2 * LANE:]
    bh = pltpu.roll(bc, LANE - H_A, 1)
    a = capped - bh
    a_t = a.T
    m_prev = m_s[b]
    big_m = jnp.maximum(_prefix_max_rows(a), m_prev[0:1, :])
    sc_all = jnp.exp(m_prev[0:1, :] - big_m)
    emt_all = jnp.exp(-(bh + big_m))
    m_last = big_m[L - 1:L, :]
    w_all = jnp.exp(a - m_last)
    decay_all = sc_all[L - 1:L, :]
    m_s[b] = jnp.broadcast_to(bh[L - 1:L, :] + m_last, (8, LANE))

    nums = []
    nd_tail = jnp.zeros((L, LANE), F32)
    sq_tail = jnp.zeros((L, LANE), F32)
    for h in range(H_A):
        hs = slice(h * DK, (h + 1) * DK)
        onehot = (lane == h).astype(BF16)
        qb = q_ref[b, :, hs].astype(BF16)
        kf = k_ref[b, :, hs] * (DK ** -0.5)
        kb = kf.astype(BF16)
        vx = jnp.concatenate([v_ref[b, :, hs].astype(BF16), onehot], axis=1)
        cx_prev = cx_s[b * H_A + h]

        p = jnp.exp(jnp.where(causal, a_t[h:h + 1, :] - big_m[:, h:h + 1], -jnp.inf))
        s = _bdot_t(qb, kb) * p
        nd = sc_all[:, h:h + 1] * _bdot(qb, cx_prev.astype(BF16)) + _bdot(s.astype(BF16), vx)
        num = nd[:, :DV]
        nums.append(num)
        nd_tail = nd_tail + nd[:, DV:]
        sq_tail = sq_tail + _bdot((num * num).astype(BF16), (lane_dv == h).astype(BF16))

        wk = (w_all[:, h:h + 1] * kf).astype(BF16)
        cx_s[b * H_A + h] = decay_all[:, h:h + 1] * cx_prev + lax.dot_general(
            wk, vx, (((0,), (0,)), ((), ())), preferred_element_type=F32)

    inv = 1.0 / jnp.maximum(jnp.abs(nd_tail), emt_all)
    ms = inv * inv * sq_tail * (1.0 / DV)
    scale_all = inv * lax.rsqrt(ms + EPS)
    for h in range(H_A):
        hs = slice(h * DK, (h + 1) * DK)
        hn = nums[h] * scale_all[:, h:h + 1]
        ha_ref[b, :, hs] = (hn * nrm_ref[:, hs] * jax.nn.sigmoid(o_ref[b, :, hs])).astype(ha_ref.dtype)


def _mlstm_prompt(za_p, g_p, bias_row, nrm_row, batch, seq):
    L = MLSTM_CHUNK
    nsq = MLSTM_SEQS
    n_chunks = seq // L
    kern = functools.partial(_mlstm_prompt_kernel, n_chunks=n_chunks)
    za3 = za_p.reshape(batch, seq, 4 * D_A)
    g3 = g_p.reshape(batch, seq, LANE)
    zspec = lambda part: pl.BlockSpec((nsq, L, D_A), lambda b, c: (b, c, part))
    return pl.pallas_call(
        kern,
        grid=(batch // nsq, n_chunks),
        in_specs=[
            zspec(0), zspec(1), zspec(2), zspec(3),
            pl.BlockSpec((nsq, L, LANE), lambda b, c: (b, c, 0)),
            pl.BlockSpec((1, LANE), lambda b, c: (0, 0)),
            pl.BlockSpec((1, D_A), lambda b, c: (0, 0)),
        ],
        out_specs=[
            pl.BlockSpec((nsq, L, D_A), lambda b, c: (b, c, 0)),
            pl.BlockSpec((1, nsq, H_A, DK, DV), lambda b, c: (0, b, 0, 0, 0)),
            pl.BlockSpec((1, nsq, H_A, DK), lambda b, c: (0, b, 0, 0)),
            pl.BlockSpec((nsq, 8, LANE), lambda b, c: (b, 0, 0)),
        ],
        out_shape=[
            jax.ShapeDtypeStruct((batch, seq, D_A), BF16),
            jax.ShapeDtypeStruct((1, batch, H_A, DK, DV), F32),
            jax.ShapeDtypeStruct((1, batch, H_A, DK), F32),
            jax.ShapeDtypeStruct((batch, 8, LANE), F32),
        ],
        scratch_shapes=[pltpu.VMEM((nsq * H_A, DK, DV + LANE), F32),
                        pltpu.VMEM((nsq, 8, LANE), F32)],
        compiler_params=pltpu.CompilerParams(
            dimension_semantics=("parallel", "arbitrary"), vmem_limit_bytes=VMEM_LIMIT),
        name="mlstm_prompt",
    )(za3, za3, za3, za3, g3, bias_row, nrm_row)


def _expand_heads(x, width):
    rows = x.shape[0]
    return jnp.concatenate([jnp.broadcast_to(x[:, h:h + 1], (rows, width)) for h in range(H_A)], axis=1)


def _head_sums(x):
    return jnp.concatenate(
        [jnp.sum(x[:, h * DK:(h + 1) * DK], axis=-1, keepdims=True) for h in range(H_A)], axis=1)


def _mlstm_sample_kernel(za_ref, qc_ref, g_ref, bias_ref, nrm_ref, m_ref, n_ref,
                         ha_ref, m_out, n_out, wk_out, dec_out):
    nb = za_ref.shape[0]
    capped = _soft_cap(g_ref[...] + bias_ref[...])
    logi = capped[:, 0:H_A]
    logf = _log_sigmoid(capped)[:, H_A:2 * H_A]
    m_prev = m_ref[...]
    m_inter = logf + m_prev
    m_t = jnp.maximum(m_inter, logi)
    sc = jnp.exp(m_inter - m_t)
    ei = jnp.exp(logi - m_t)
    emt = jnp.exp(-m_t)

    q = za_ref[:, 0:D_A]
    k = za_ref[:, D_A:2 * D_A] * (DK ** -0.5)
    v = za_ref[:, 2 * D_A:3 * D_A]
    o = za_ref[:, 3 * D_A:4 * D_A]
    n_prev = n_ref[...]

    s = _head_sums(q * k) * ei
    den = sc * _head_sums(q * n_prev) + s
    denom = jnp.maximum(jnp.abs(den), emt)
    num = _expand_heads(sc, DV) * qc_ref[...] + _expand_heads(s, DV) * v
    hh = num / _expand_heads(denom, DV)
    ms = _head_sums(hh * hh) * (1.0 / DV)
    hn = hh * lax.rsqrt(_expand_heads(ms, DV) + EPS)
    ha_ref[...] = (hn * nrm_ref[...] * jax.nn.sigmoid(o)).astype(ha_ref.dtype)
    n_out[...] = _expand_heads(sc, DK) * n_prev + _expand_heads(ei, DK) * k
    m_out[...] = m_t
    pad = jnp.zeros((nb, LANE - H_A), F32)
    dec_out[...] = jnp.concatenate([sc, pad], axis=1)

    k_t = k.T
    ei_t = jnp.concatenate([ei, pad], axis=1).T
    for h in range(H_A):
        hs = slice(h * DK, (h + 1) * DK)
        wk_out[hs, :] = k_t[hs, :] * ei_t[h:h + 1, :]


def _mlstm_sample(za_s, qc, g_s, bias_row, nrm_row, m0, n0):
    nb = za_s.shape[0]
    assert nb == LANE, "the sequence axis is transposed onto the lanes"
    return pl.pallas_call(
        _mlstm_sample_kernel,
        out_shape=[
            jax.ShapeDtypeStruct((nb, D_A), BF16),
            jax.ShapeDtypeStruct((nb, H_A), F32),
            jax.ShapeDtypeStruct((nb, H_A * DK), F32),
            jax.ShapeDtypeStruct((D_A, nb), F32),
            jax.ShapeDtypeStruct((nb, LANE), F32),
        ],
        compiler_params=pltpu.CompilerParams(vmem_limit_bytes=VMEM_LIMIT),
        name="mlstm_sample",
    )(za_s, qc, g_s, bias_row, nrm_row, m0, n0)


def _outproj_kernel(hp_ref, hs_ref, ap_ref, as_ref, yb_ref, w_ref, op_ref, os_ref, wc_ref):
    j = pl.program_id(1)

    @pl.when(pl.program_id(0) == 0)
    def _():
        wc_ref[j] = w_ref[...].astype(BF16)

    wa = wc_ref[j, :D_A, :]
    wb = wc_ref[j, D_A:, :]
    op_ref[...] = hp_ref[...] + _bdot(ap_ref[...], wa) + _bdot(yb_ref[0, :MP, :], wb)
    os_ref[0] = hs_ref[0] + _bdot(as_ref[0], wa) + _bdot(yb_ref[0, MP:, :], wb)


def _outproj(hp, hs, ap, a_s, yb_c, w3):
    tn = OUT_TN
    n_n = D_MODEL // tn
    return pl.pallas_call(
        _outproj_kernel,
        grid=(N_MIX_BLOCKS, n_n),
        in_specs=[
            pl.BlockSpec((MP, tn), lambda i, j: (i, j)),
            pl.BlockSpec((1, MS, tn), lambda i, j: (i, 0, j)),
            pl.BlockSpec((MP, D_A), lambda i, j: (i, 0)),
            pl.BlockSpec((1, MS, D_A), lambda i, j: (i, 0, 0)),
            pl.BlockSpec((1, MP + MS, D_B), lambda i, j: (i, 0, 0)),
            pl.BlockSpec((None, D_A + D_B, tn), lambda i, j: (0, 0, jnp.where(i == 0, j, n_n - 1))),
        ],
        out_specs=[
            pl.BlockSpec((MP, tn), lambda i, j: (i, j)),
            pl.BlockSpec((1, MS, tn), lambda i, j: (i, 0, j)),
        ],
        out_shape=[
            jax.ShapeDtypeStruct((N_MIX_BLOCKS * MP, D_MODEL), F32),
            jax.ShapeDtypeStruct((N_MIX_BLOCKS, MS, D_MODEL), F32),
        ],
        scratch_shapes=[pltpu.VMEM((n_n, D_A + D_B, tn), BF16)],
        compiler_params=pltpu.CompilerParams(
            dimension_semantics=("arbitrary", "arbitrary"), vmem_limit_bytes=VMEM_LIMIT),
        name="outproj",
    )(hp, hs, ap, a_s, yb_c, w3)


def kernel(x_prompt, x_sample, state_mlstm_C, state_mlstm_n, state_mlstm_m, state_conv, norm_ffn1, ffn1_gate, ffn1_up, ffn1_down, norm_mix, w_in, b_gates, conv_w, conv_b, norm_mlstm, norm_conv, w_out, norm_ffn2, ffn2_gate, ffn2_up, ffn2_down, norm_final):
    batch, seq, _ = x_prompt.shape
    nb = x_sample.shape[0]
    assert batch == N_MIX_BLOCKS and seq == MP and nb == N_MIX_BLOCKS * MS
    assert norm_ffn1.shape[0] == 1, "single-layer trunk"

    xp = x_prompt.reshape(batch * seq, D_MODEL)
    xs = x_sample.reshape(N_FFN_BLOCKS, TM_S, D_MODEL)

    w_t = jnp.swapaxes(w_in[0], 0, 1)

    h1p, h1s, xn_c, g_p, g_s = _ffn(xp, xs, norm_ffn1, ffn1_gate, ffn1_up, ffn1_down, norm_mix, TF1,
                                    w_gate_t=w_t)
    za_p, za_s = _proj(xn_c, w_t)
    za_s = za_s.reshape(nb, 4 * D_A)
    yb_c, conv_p, conv_s, qc_s = _conv(xn_c, w_t, conv_w[0], conv_b, norm_conv,
                                       state_conv.reshape(N_MIX_BLOCKS, MS, 2, D_B), za_s, state_mlstm_C)

    bias_row = jnp.zeros((1, LANE), F32).at[0, :2 * H_A].set(b_gates[0].astype(F32))
    ha_p, c_p, n_p, m_p = _mlstm_prompt(za_p, g_p, bias_row, norm_mlstm, batch, seq)
    ha_s, m_s, n_s, wk_s, dec_s = _mlstm_sample(
        za_s, qc_s, g_s.reshape(nb, LANE), bias_row, norm_mlstm,
        state_mlstm_m.reshape(nb, H_A), state_mlstm_n.reshape(nb, H_A * DK))

    h2p, h2s = _outproj(h1p, h1s.reshape(N_MIX_BLOCKS, MS, D_MODEL), ha_p.reshape(batch * seq, D_A),
                        ha_s.reshape(N_MIX_BLOCKS, MS, D_A), yb_c, w_out)

    yp, ys, c_s = _ffn(h2p, h2s.reshape(N_FFN_BLOCKS, TM_S, D_MODEL), norm_ffn2, ffn2_gate, ffn2_up,
                       ffn2_down, norm_final.reshape(1, D_MODEL), TF2,
                       state=(state_mlstm_C, wk_s, za_s, dec_s))

    return (
        yp.reshape(batch, seq, D_MODEL),
        ys.reshape(nb, 1, D_MODEL),
        c_p,
        n_p,
        m_p[:, 0, :H_A].reshape(1, batch, H_A),
        conv_p,
        c_s,
        n_s.reshape(1, nb, H_A, DK),
        m_s.reshape(1, nb, H_A),
        conv_s.reshape(1, nb, 2, D_B),
    )
```

```python
import functools

import jax
import jax.numpy as jnp
from jax import lax
from jax.experimental import pallas as pl
from jax.experimental.pallas import tpu as pltpu

F32 = jnp.float32
BF16 = jnp.bfloat16

D_MODEL = 2048
D_A = 1024
D_B = 1024
H_A = 4
DK = 256
DV = 256
G_B = 8
D_FF = 5504
GATE_CAP = 15.0
EPS = 1e-6
GATE_ROW0 = 4 * D_A
CONV_ROW0 = 4 * D_A + 2 * H_A

LANE = 128
N_FFN_BLOCKS = 8
TM_P = 1024
TM_S = 16
TM = TM_P + TM_S
TF1 = 512
TF2 = 512
STATE_SEQS = 2
X_CHUNK = 128
N_X_CHUNKS = TM_P // X_CHUNK
X_AHEAD0 = 2
N_MIX_BLOCKS = N_FFN_BLOCKS // 2
MP = 2 * TM_P
MS = 2 * TM_S
PROJ_TN = 1024
CONV_TC = 256
OUT_TN = 512
MLSTM_CHUNK = 256
MLSTM_SEQS = 2
SAMPLE_BS = 8
VMEM_LIMIT = 62 * 1024 * 1024


def _rms(x, g):
    return x * lax.rsqrt(jnp.mean(x * x, axis=-1, keepdims=True) + EPS) * g


def _soft_cap(x):
    return GATE_CAP * jnp.tanh(x / GATE_CAP)


def _log_sigmoid(x):
    return -jax.nn.softplus(-x)


def _bdot(a, b):
    return jnp.dot(a, b, preferred_element_type=F32)


def _bdot_t(a, b):
    return lax.dot_general(a, b, (((1,), (1,)), ((), ())), preferred_element_type=F32)


def _sample_state_update(f, c_ref, wk_ref, v_ref, dec_ref, c_out):
    grp = jnp.minimum(f, TM_S // STATE_SEQS - 1)
    lane = lax.broadcasted_iota(jnp.int32, (D_A, LANE), 1)
    for u in range(STATE_SEQS):
        r = grp * STATE_SEQS + u
        seq = pl.program_id(0) * TM_S + r
        wk_col = jnp.sum(jnp.where(lane == seq, wk_ref[...], 0.0), axis=1, keepdims=True)
        v_row = v_ref[pl.ds(r, 1), :]
        dec = dec_ref[pl.ds(r, 1), :]
        for h in range(H_A):
            hs = slice(h * DK, (h + 1) * DK)
            c_out[0, u, h] = dec[:, h:h + 1] * c_ref[0, u, h] + wk_col[hs, :] * v_row[:, hs]


def _ffn_kernel(x_hbm, xs_ref, g_ref, wg_ref, wu_ref, wd_ref, g2_ref, *rest,
                tf, n_f, last_valid, final_norm):
    if final_norm:
        c_ref, wk_ref, v_ref, dec_ref, out_hbm, os_ref, c_out, xn_ref, acc_ref, x_sem, o_sem = rest
    else:
        wgate_ref, out_hbm, os_ref, nxt_ref, gp_ref, gs_ref, xn_ref, acc_ref, x_sem, o_sem = rest
    i = pl.program_id(0)
    f = pl.program_id(1)
    slot = lax.rem(i, 2)
    op_ref = acc_ref.at[slot]

    def x_copy(blk, s, c):
        dst0 = pl.multiple_of(c * X_CHUNK, X_CHUNK)
        src0 = pl.multiple_of(blk * TM_P + c * X_CHUNK, X_CHUNK)
        return pltpu.make_async_copy(x_hbm.at[pl.ds(src0, X_CHUNK), :],
                                     acc_ref.at[s, pl.ds(dst0, X_CHUNK), :], x_sem.at[c])

    def out_copy(blk, s):
        row0 = pl.multiple_of(blk * TM_P, TM_P)
        return pltpu.make_async_copy(acc_ref.at[s], out_hbm.at[pl.ds(row0, TM_P), :], o_sem.at[s])

    @pl.when(jnp.logical_and(i == 0, f == 0))
    def _():
        for c in range(N_X_CHUNKS):
            x_copy(0, 0, c).start()

    @pl.when(f == 0)
    def _():
        for c in range(N_X_CHUNKS):
            x_copy(i, slot, c).wait()

    @pl.when(jnp.logical_and(i > 0, f == X_AHEAD0 - 1))
    def _():
        out_copy(i - 1, 1 - slot).wait()

    @pl.when(jnp.logical_and(jnp.logical_and(f >= X_AHEAD0, f < X_AHEAD0 + N_X_CHUNKS),
                             i < N_FFN_BLOCKS - 1))
    def _():
        x_copy(i + 1, 1 - slot, f - X_AHEAD0).start()

    def step(valid, first, with_state):
        if first:
            g = g_ref[...]
            xn_ref[:TM_P, :] = _rms(op_ref[...], g).astype(BF16)
            xn_ref[TM_P:, :] = _rms(xs_ref[0], g).astype(BF16)
        xn = xn_ref[...]
        a = _bdot(xn, wg_ref[:, :valid].astype(BF16))
        b = _bdot(xn, wu_ref[:, :valid].astype(BF16))
        hid = (a * jax.nn.sigmoid(a) * b * 0.5).astype(BF16)
        r = _bdot(hid, wd_ref[:valid, :].astype(BF16))
        op_ref[...] += r[:TM_P]
        if first:
            os_ref[0] = xs_ref[0] + r[TM_P:]
        else:
            os_ref[0] += r[TM_P:]
        if with_state:
            _sample_state_update(f, c_ref, wk_ref, v_ref, dec_ref, c_out)

    assert last_valid < tf and (not final_norm or (n_f - 1) * STATE_SEQS >= TM_S), \
        "state updates ride the full-width steps"

    @pl.when(f == 0)
    def _():
        step(tf, True, final_norm)

    @pl.when(jnp.logical_and(f > 0, f < n_f - 1))
    def _():
        step(tf, False, final_norm)

    @pl.when(f == n_f - 1)
    def _():
        step(last_valid, False, False)
        g2 = g2_ref[...]
        if final_norm:
            op_ref[...] = _rms(op_ref[...], g2)
            os_ref[0] = _rms(os_ref[0], g2)
        else:
            nx_p = _rms(op_ref[...], g2).astype(BF16)
            nx_s = _rms(os_ref[0], g2).astype(BF16)
            nxt_ref[0, :TM_P, :] = nx_p
            nxt_ref[0, TM_P:, :] = nx_s
            wgt = wgate_ref[...].astype(BF16)
            half = TM_P // 2
            gp_ref[:half, :] = _bdot_t(nx_p[:half], wgt)
            gp_ref[half:, :] = _bdot_t(nx_p[half:], wgt)
            gs_ref[0] = _bdot_t(nx_s, wgt)

        out_copy(i, slot).start()

        @pl.when(i == N_FFN_BLOCKS - 1)
        def _():
            out_copy(i, slot).wait()


def _ffn(xp, xs, g, wg, wu, wd, g2, tf, state=None, w_gate_t=None):
    final_norm = state is not None
    n_f = pl.cdiv(D_FF, tf)
    last_valid = D_FF - (n_f - 1) * tf
    kern = functools.partial(_ffn_kernel, tf=tf, n_f=n_f, last_valid=last_valid, final_norm=final_norm)
    assert X_AHEAD0 + N_X_CHUNKS <= n_f and N_X_CHUNKS * X_CHUNK == TM_P
    in_specs = [
        pl.BlockSpec(memory_space=pl.ANY),
        pl.BlockSpec((1, TM_S, D_MODEL), lambda i, f: (i, 0, 0)),
        pl.BlockSpec((1, D_MODEL), lambda i, f: (0, 0)),
        pl.BlockSpec((None, D_MODEL, tf), lambda i, f: (0, 0, f)),
        pl.BlockSpec((None, D_MODEL, tf), lambda i, f: (0, 0, f)),
        pl.BlockSpec((None, tf, D_MODEL), lambda i, f: (0, f, 0)),
        pl.BlockSpec((1, D_MODEL), lambda i, f: (0, 0)),
    ]
    out_specs = [
        pl.BlockSpec(memory_space=pl.ANY),
        pl.BlockSpec((1, TM_S, D_MODEL), lambda i, f: (i, 0, 0)),
    ]
    out_shape = [
        jax.ShapeDtypeStruct((N_FFN_BLOCKS * TM_P, D_MODEL), F32),
        jax.ShapeDtypeStruct((N_FFN_BLOCKS, TM_S, D_MODEL), F32),
    ]
    args = [xp, xs, g, wg, wu, wd, g2]
    if final_norm:
        c0, wk, za_s, dec = state
        groups = TM_S // STATE_SEQS
        grp = lambda i, f: i * groups + jnp.minimum(f, groups - 1)
        cspec = pl.BlockSpec((1, STATE_SEQS, H_A, DK, DV), lambda i, f: (0, grp(i, f), 0, 0, 0))
        in_specs += [
            cspec,
            pl.BlockSpec((D_A, LANE), lambda i, f: (0, 0)),
            pl.BlockSpec((TM_S, D_A), lambda i, f: (i, 2)),
            pl.BlockSpec((TM_S, LANE), lambda i, f: (i, 0)),
        ]
        out_specs.append(cspec)
        out_shape.append(jax.ShapeDtypeStruct(c0.shape, F32))
        args += [c0, wk, za_s, dec]
    else:
        in_specs.append(pl.BlockSpec((LANE, D_MODEL), lambda i, f: (GATE_ROW0 // LANE, 0)))
        args.append(w_gate_t)
        out_specs += [
            pl.BlockSpec((1, TM, D_MODEL), lambda i, f: (i, 0, 0)),
            pl.BlockSpec((TM_P, LANE), lambda i, f: (i, 0)),
            pl.BlockSpec((1, TM_S, LANE), lambda i, f: (i, 0, 0)),
        ]
        out_shape += [
            jax.ShapeDtypeStruct((N_FFN_BLOCKS, TM, D_MODEL), BF16),
            jax.ShapeDtypeStruct((N_FFN_BLOCKS * TM_P, LANE), F32),
            jax.ShapeDtypeStruct((N_FFN_BLOCKS, TM_S, LANE), F32),
        ]
    return pl.pallas_call(
        kern,
        grid=(N_FFN_BLOCKS, n_f),
        in_specs=in_specs,
        out_specs=out_specs,
        out_shape=out_shape,
        scratch_shapes=[pltpu.VMEM((TM, D_MODEL), BF16),
                        pltpu.VMEM((2, TM_P, D_MODEL), F32),
                        pltpu.SemaphoreType.DMA((N_X_CHUNKS,)),
                        pltpu.SemaphoreType.DMA((2,))],
        compiler_params=pltpu.CompilerParams(
            dimension_semantics=("arbitrary", "arbitrary"), vmem_limit_bytes=VMEM_LIMIT),
        name="ffn_final" if final_norm else "ffn",
    )(*args)


def _split_rows(z0, z1, p_ref, s_ref):
    p_ref[:TM_P, :] = z0[:TM_P]
    p_ref[TM_P:, :] = z1[:TM_P]
    s_ref[0, :TM_S, :] = z0[TM_P:]
    s_ref[0, TM_S:, :] = z1[TM_P:]


def _proj_kernel(xn_ref, w_ref, zp_ref, zs_ref):
    w = w_ref[...].astype(BF16)
    _split_rows(_bdot_t(xn_ref[0], w), _bdot_t(xn_ref[1], w), zp_ref, zs_ref)


def _proj(xn_c, w_t):
    n_n = (4 * D_A) // PROJ_TN
    return pl.pallas_call(
        _proj_kernel,
        grid=(N_MIX_BLOCKS, n_n),
        in_specs=[
            pl.BlockSpec((2, TM, D_MODEL), lambda i, j: (i, 0, 0)),
            pl.BlockSpec((PROJ_TN, D_MODEL), lambda i, j: (j, 0)),
        ],
        out_specs=[
            pl.BlockSpec((MP, PROJ_TN), lambda i, j: (i, j)),
            pl.BlockSpec((1, MS, PROJ_TN), lambda i, j: (i, 0, j)),
        ],
        out_shape=[
            jax.ShapeDtypeStruct((N_MIX_BLOCKS * MP, 4 * D_A), F32),
            jax.ShapeDtypeStruct((N_MIX_BLOCKS, MS, 4 * D_A), F32),
        ],
        compiler_params=pltpu.CompilerParams(
            dimension_semantics=("parallel", "arbitrary"), vmem_limit_bytes=VMEM_LIMIT),
        name="proj_qkvo",
    )(xn_c, w_t)


def _group_norm(yb, nrm):
    gw = D_B // G_B
    parts = []
    for g in range(yb.shape[1] // gw):
        seg = yb[:, g * gw:(g + 1) * gw]
        parts.append(seg * lax.rsqrt(jnp.mean(seg * seg, axis=-1, keepdims=True) + EPS))
    return jnp.concatenate(parts, axis=1) * nrm


def _conv_kernel(xn_ref, wgb_ref, wgc_ref, wxc_ref, cw_ref, cb_ref, nrm_ref, buf_ref, q_ref, c_ref,
                 yb_ref, cp_ref, cs_ref, qc_ref):
    q_t = jnp.concatenate([q_ref[...], jnp.zeros((LANE - SAMPLE_BS, D_A), F32)], axis=0).T
    for h in range(H_A):
        hs = slice(h * DK, (h + 1) * DK)
        for j in range(SAMPLE_BS):
            qc_ref[j:j + 1, hs] = jnp.sum(q_t[hs, j:j + 1] * c_ref[0, j, h], axis=0, keepdims=True)

    wgb = wgb_ref[...].astype(BF16)
    wgc = wgc_ref[...].astype(BF16)
    wxc = wxc_ref[...].astype(BF16)
    cw0 = cw_ref[0:1, :]
    cw1 = cw_ref[1:2, :]
    cw2 = cw_ref[2:3, :]
    cb = cb_ref[...]
    nrm = nrm_ref[...]
    row = lax.broadcasted_iota(jnp.int32, (TM_P, wgb.shape[0]), 0)
    zero_row = jnp.zeros((1, wgb.shape[0]), F32)
    um2, um1 = zero_row, zero_row

    for t in range(2):
        xn = xn_ref[t]
        u = _bdot_t(xn, wgc) * _bdot_t(xn, wxc)
        gb = _bdot_t(xn, wgb)

        up = u[:TM_P]
        u1 = jnp.where(row < 1, um1, pltpu.roll(up, 1, 0))
        u2 = jnp.where(row < 1, um2, jnp.where(row < 2, um1, pltpu.roll(up, 2, 0)))
        yc = cw0 * u2 + cw1 * u1 + cw2 * up + cb
        yb_ref[0, t * TM_P:(t + 1) * TM_P, :] = _group_norm(gb[:TM_P] * yc, nrm).astype(yb_ref.dtype)
        um2, um1 = up[TM_P - 2:TM_P - 1, :], up[TM_P - 1:TM_P, :]

        ss = slice(t * TM_S, (t + 1) * TM_S)
        us = u[TM_P:]
        b0 = buf_ref[0, ss, 0, :]
        b1 = buf_ref[0, ss, 1, :]
        ycs = cw0 * b0 + cw1 * b1 + cw2 * us + cb
        yb_ref[0, MP + t * TM_S:MP + (t + 1) * TM_S, :] = _group_norm(gb[TM_P:] * ycs, nrm).astype(yb_ref.dtype)
        cs_ref[0, ss, 0, :] = b1
        cs_ref[0, ss, 1, :] = us

    cp_ref[0, 0, 0:1, :] = um2
    cp_ref[0, 0, 1:2, :] = um1


def _conv(xn_c, w_t, cw, cb_row, nrm_row, buf, za_s, c0):
    tc = CONV_TC
    n_c = D_B // tc
    assert N_MIX_BLOCKS * n_c * SAMPLE_BS == za_s.shape[0], "one sample group per grid step"
    wspec = lambda base: pl.BlockSpec(
        (pl.Element(tc), pl.Element(D_MODEL)), lambda i, c: (pl.multiple_of(base + c * tc, 8), 0))
    grp = lambda i, c: i * n_c + c
    return pl.pallas_call(
        _conv_kernel,
        grid=(N_MIX_BLOCKS, n_c),
        in_specs=[
            pl.BlockSpec((2, TM, D_MODEL), lambda i, c: (i, 0, 0)),
            wspec(CONV_ROW0),
            wspec(CONV_ROW0 + D_B),
            wspec(CONV_ROW0 + 2 * D_B),
            pl.BlockSpec((3, tc), lambda i, c: (0, c)),
            pl.BlockSpec((1, tc), lambda i, c: (0, c)),
            pl.BlockSpec((1, tc), lambda i, c: (0, c)),
            pl.BlockSpec((1, MS, 2, tc), lambda i, c: (i, 0, 0, c)),
            pl.BlockSpec((SAMPLE_BS, D_A), lambda i, c: (grp(i, c), 0)),
            pl.BlockSpec((1, SAMPLE_BS, H_A, DK, DV), lambda i, c: (0, grp(i, c), 0, 0, 0)),
        ],
        out_specs=[
            pl.BlockSpec((1, MP + MS, tc), lambda i, c: (i, 0, c)),
            pl.BlockSpec((1, 1, 2, tc), lambda i, c: (0, i, 0, c)),
            pl.BlockSpec((1, MS, 2, tc), lambda i, c: (i, 0, 0, c)),
            pl.BlockSpec((SAMPLE_BS, D_A), lambda i, c: (grp(i, c), 0)),
        ],
        out_shape=[
            jax.ShapeDtypeStruct((N_MIX_BLOCKS, MP + MS, D_B), BF16),
            jax.ShapeDtypeStruct((1, N_MIX_BLOCKS, 2, D_B), F32),
            jax.ShapeDtypeStruct((N_MIX_BLOCKS, MS, 2, D_B), F32),
            jax.ShapeDtypeStruct((za_s.shape[0], D_A), F32),
        ],
        compiler_params=pltpu.CompilerParams(
            dimension_semantics=("parallel", "arbitrary"), vmem_limit_bytes=VMEM_LIMIT),
        name="proj_conv",
    )(xn_c, w_t, w_t, w_t, cw, cb_row, nrm_row, buf, za_s, c0)


def _mlstm_prompt_kernel(z_ref, g_ref, bias_ref, nrm_ref,
                         ha_ref, c_ref, n_ref, m_ref, cx_s, m_s, *, n_chunks):
    L = MLSTM_CHUNK
    c = pl.program_id(1)

    @pl.when(c == 0)
    def _():
        cx_s[...] = jnp.zeros_like(cx_s)
        m_s[...] = jnp.zeros_like(m_s)

    row = lax.broadcasted_iota(jnp.int32, (L, L), 0)
    col = lax.broadcasted_iota(jnp.int32, (L, L), 1)
    causal = row >= col
    tri = causal.astype(BF16)

    for b in range(MLSTM_SEQS):
        _mlstm_prompt_seq(b, causal, tri, z_ref, g_ref, bias_ref, nrm_ref, ha_ref, cx_s, m_s)

    @pl.when(c == n_chunks - 1)
    def _():
        for b in range(MLSTM_SEQS):
            m_ref[b] = m_s[b]
            for h in range(H_A):
                cx = cx_s[b * H_A + h]
                c_ref[0, b, h] = cx[:, :DV]
                n_ref[0, b, h:h + 1, :] = cx[:, DV:].T[h:h + 1, :]


def _prefix_max_rows(x):
    rows = x.shape[0]
    row = lax.broadcasted_iota(jnp.int32, x.shape, 0)
    k = 1
    while k < rows:
        x = jnp.where(row >= k, jnp.maximum(x, pltpu.roll(x, k, 0)), x)
        k *= 2
    return x


def _mlstm_prompt_seq(b, causal, tri, z_ref, g_ref, bias_ref, nrm_ref, ha_ref, cx_s, m_s):
    L = MLSTM_CHUNK
    lane = lax.broadcasted_iota(jnp.int32, (L, LANE), 1)
    lane_dv = lax.broadcasted_iota(jnp.int32, (DV, LANE), 1)
    capped = _soft_cap(g_ref[b] + bias_ref[...])
    lf = _log_sigmoid(capped)
    hi = lf.astype(BF16)
    r1 = lf - hi.astype(F32)
    mid = r1.astype(BF16)
    lo = (r1 - mid.astype(F32)).astype(BF16)
    parts = _bdot(tri, jnp.concatenate([hi, mid, lo], axis=1))
    bc = parts[:, :LANE] + parts[:, LANE:2 * LANE] + parts[:, 2 * LANE:]
    bh = pltpu.roll(bc, LANE - H_A, 1)
    a = capped - bh
    a_t = a.T
    m_prev = m_s[b]
    big_m = jnp.maximum(_prefix_max_rows(a), m_prev[0:1, :])
    sc_all = jnp.exp(m_prev[0:1, :] - big_m)
    emt_all = jnp.exp(-(bh + big_m))
    m_last = big_m[L - 1:L, :]
    w_all = jnp.exp(a - m_last)
    decay_all = sc_all[L - 1:L, :]
    m_s[b] = jnp.broadcast_to(bh[L - 1:L, :] + m_last, (8, LANE))

    nums = []
    nd_tail = jnp.zeros((L, LANE), F32)
    sq_tail = jnp.zeros((L, LANE), F32)
    for h in range(H_A):
        hs = slice(h * DK, (h + 1) * DK)
        onehot = (lane == h).astype(BF16)
        qb = z_ref[b, :, hs].astype(BF16)
        kf = z_ref[b, :, slice(D_A + h * DK, D_A + (h + 1) * DK)] * (DK ** -0.5)
        kb = kf.astype(BF16)
        v_h = z_ref[b, :, slice(2 * D_A + h * DK, 2 * D_A + (h + 1) * DK)]
        vx = jnp.concatenate([v_h.astype(BF16), onehot], axis=1)
        cx_prev = cx_s[b * H_A + h]

        p = jnp.exp(jnp.where(causal, a_t[h:h + 1, :] - big_m[:, h:h + 1], -jnp.inf))
        s = _bdot_t(qb, kb) * p
        nd = sc_all[:, h:h + 1] * _bdot(qb, cx_prev.astype(BF16)) + _bdot(s.astype(BF16), vx)
        num = nd[:, :DV]
        nums.append(num)
        nd_tail = nd_tail + nd[:, DV:]
        sq_tail = sq_tail + _bdot((num * num).astype(BF16), (lane_dv == h).astype(BF16))

        wk = (w_all[:, h:h + 1] * kf).astype(BF16)
        cx_s[b * H_A + h] = decay_all[:, h:h + 1] * cx_prev + lax.dot_general(
            wk, vx, (((0,), (0,)), ((), ())), preferred_element_type=F32)

    inv = 1.0 / jnp.maximum(jnp.abs(nd_tail), emt_all)
    ms = inv * inv * sq_tail * (1.0 / DV)
    scale_all = inv * lax.rsqrt(ms + EPS)
    for h in range(H_A):
        hs = slice(h * DK, (h + 1) * DK)
        hn = nums[h] * scale_all[:, h:h + 1]
        o_h = z_ref[b, :, slice(3 * D_A + h * DK, 3 * D_A + (h + 1) * DK)]
        ha_ref[b, :, hs] = (hn * nrm_ref[:, hs] * jax.nn.sigmoid(o_h)).astype(ha_ref.dtype)


def _mlstm_prompt(za_p, g_p, bias_row, nrm_row, batch, seq):
    L = MLSTM_CHUNK
    nsq = MLSTM_SEQS
    n_chunks = seq // L
    kern = functools.partial(_mlstm_prompt_kernel, n_chunks=n_chunks)
    za3 = za_p.reshape(batch, seq, 4 * D_A)
    g3 = g_p.reshape(batch, seq, LANE)
    return pl.pallas_call(
        kern,
        grid=(batch // nsq, n_chunks),
        in_specs=[
            pl.BlockSpec((nsq, L, 4 * D_A), lambda b, c: (b, c, 0)),
            pl.BlockSpec((nsq, L, LANE), lambda b, c: (b, c, 0)),
            pl.BlockSpec((1, LANE), lambda b, c: (0, 0)),
            pl.BlockSpec((1, D_A), lambda b, c: (0, 0)),
        ],
        out_specs=[
            pl.BlockSpec((nsq, L, D_A), lambda b, c: (b, c, 0)),
            pl.BlockSpec((1, nsq, H_A, DK, DV), lambda b, c: (0, b, 0, 0, 0)),
            pl.BlockSpec((1, nsq, H_A, DK), lambda b, c: (0, b, 0, 0)),
            pl.BlockSpec((nsq, 8, LANE), lambda b, c: (b, 0, 0)),
        ],
        out_shape=[
            jax.ShapeDtypeStruct((batch, seq, D_A), BF16),
            jax.ShapeDtypeStruct((1, batch, H_A, DK, DV), F32),
            jax.ShapeDtypeStruct((1, batch, H_A, DK), F32),
            jax.ShapeDtypeStruct((batch, 8, LANE), F32),
        ],
        scratch_shapes=[pltpu.VMEM((nsq * H_A, DK, DV + LANE), F32),
                        pltpu.VMEM((nsq, 8, LANE), F32)],
        compiler_params=pltpu.CompilerParams(
            dimension_semantics=("parallel", "arbitrary"), vmem_limit_bytes=VMEM_LIMIT),
        name="mlstm_prompt",
    )(za3, g3, bias_row, nrm_row)


def _expand_heads(x, width):
    rows = x.shape[0]
    return jnp.concatenate([jnp.broadcast_to(x[:, h:h + 1], (rows, width)) for h in range(H_A)], axis=1)


def _head_sums(x):
    return jnp.concatenate(
        [jnp.sum(x[:, h * DK:(h + 1) * DK], axis=-1, keepdims=True) for h in range(H_A)], axis=1)


def _mlstm_sample_kernel(za_ref, qc_ref, g_ref, bias_ref, nrm_ref, m_ref, n_ref,
                         ha_ref, m_out, n_out, wk_out, dec_out):
    nb = za_ref.shape[0]
    capped = _soft_cap(g_ref[...] + bias_ref[...])
    logi = capped[:, 0:H_A]
    logf = _log_sigmoid(capped)[:, H_A:2 * H_A]
    m_prev = m_ref[...]
    m_inter = logf + m_prev
    m_t = jnp.maximum(m_inter, logi)
    sc = jnp.exp(m_inter - m_t)
    ei = jnp.exp(logi - m_t)
    emt = jnp.exp(-m_t)

    q = za_ref[:, 0:D_A]
    k = za_ref[:, D_A:2 * D_A] * (DK ** -0.5)
    v = za_ref[:, 2 * D_A:3 * D_A]
    o = za_ref[:, 3 * D_A:4 * D_A]
    n_prev = n_ref[...]

    s = _head_sums(q * k) * ei
    den = sc * _head_sums(q * n_prev) + s
    denom = jnp.maximum(jnp.abs(den), emt)
    num = _expand_heads(sc, DV) * qc_ref[...] + _expand_heads(s, DV) * v
    hh = num / _expand_heads(denom, DV)
    ms = _head_sums(hh * hh) * (1.0 / DV)
    hn = hh * lax.rsqrt(_expand_heads(ms, DV) + EPS)
    ha_ref[...] = (hn * nrm_ref[...] * jax.nn.sigmoid(o)).astype(ha_ref.dtype)
    n_out[...] = _expand_heads(sc, DK) * n_prev + _expand_heads(ei, DK) * k
    m_out[...] = m_t
    pad = jnp.zeros((nb, LANE - H_A), F32)
    dec_out[...] = jnp.concatenate([sc, pad], axis=1)

    k_t = k.T
    ei_t = jnp.concatenate([ei, pad], axis=1).T
    for h in range(H_A):
        hs = slice(h * DK, (h + 1) * DK)
        wk_out[hs, :] = k_t[hs, :] * ei_t[h:h + 1, :]


def _mlstm_sample(za_s, qc, g_s, bias_row, nrm_row, m0, n0):
    nb = za_s.shape[0]
    assert nb == LANE, "the sequence axis is transposed onto the lanes"
    return pl.pallas_call(
        _mlstm_sample_kernel,
        out_shape=[
            jax.ShapeDtypeStruct((nb, D_A), BF16),
            jax.ShapeDtypeStruct((nb, H_A), F32),
            jax.ShapeDtypeStruct((nb, H_A * DK), F32),
            jax.ShapeDtypeStruct((D_A, nb), F32),
            jax.ShapeDtypeStruct((nb, LANE), F32),
        ],
        compiler_params=pltpu.CompilerParams(vmem_limit_bytes=VMEM_LIMIT),
        name="mlstm_sample",
    )(za_s, qc, g_s, bias_row, nrm_row, m0, n0)


def _outproj_kernel(hp_ref, hs_ref, ap_ref, as_ref, yb_ref, w_ref, op_ref, os_ref, wc_ref):
    j = pl.program_id(1)

    @pl.when(pl.program_id(0) == 0)
    def _():
        wc_ref[j] = w_ref[...].astype(BF16)

    wa = wc_ref[j, :D_A, :]
    wb = wc_ref[j, D_A:, :]
    op_ref[...] = hp_ref[...] + _bdot(ap_ref[...], wa) + _bdot(yb_ref[0, :MP, :], wb)
    os_ref[0] = hs_ref[0] + _bdot(as_ref[0], wa) + _bdot(yb_ref[0, MP:, :], wb)


def _outproj(hp, hs, ap, a_s, yb_c, w3):
    tn = OUT_TN
    n_n = D_MODEL // tn
    return pl.pallas_call(
        _outproj_kernel,
        grid=(N_MIX_BLOCKS, n_n),
        in_specs=[
            pl.BlockSpec((MP, tn), lambda i, j: (i, j)),
            pl.BlockSpec((1, MS, tn), lambda i, j: (i, 0, j)),
            pl.BlockSpec((MP, D_A), lambda i, j: (i, 0)),
            pl.BlockSpec((1, MS, D_A), lambda i, j: (i, 0, 0)),
            pl.BlockSpec((1, MP + MS, D_B), lambda i, j: (i, 0, 0)),
            pl.BlockSpec((None, D_A + D_B, tn), lambda i, j: (0, 0, jnp.where(i == 0, j, n_n - 1))),
        ],
        out_specs=[
            pl.BlockSpec((MP, tn), lambda i, j: (i, j)),
            pl.BlockSpec((1, MS, tn), lambda i, j: (i, 0, j)),
        ],
        out_shape=[
            jax.ShapeDtypeStruct((N_MIX_BLOCKS * MP, D_MODEL), F32),
            jax.ShapeDtypeStruct((N_MIX_BLOCKS, MS, D_MODEL), F32),
        ],
        scratch_shapes=[pltpu.VMEM((n_n, D_A + D_B, tn), BF16)],
        compiler_params=pltpu.CompilerParams(
            dimension_semantics=("arbitrary", "arbitrary"), vmem_limit_bytes=VMEM_LIMIT),
        name="outproj",
    )(hp, hs, ap, a_s, yb_c, w3)


def kernel(x_prompt, x_sample, state_mlstm_C, state_mlstm_n, state_mlstm_m, state_conv, norm_ffn1, ffn1_gate, ffn1_up, ffn1_down, norm_mix, w_in, b_gates, conv_w, conv_b, norm_mlstm, norm_conv, w_out, norm_ffn2, ffn2_gate, ffn2_up, ffn2_down, norm_final):
    batch, seq, _ = x_prompt.shape
    nb = x_sample.shape[0]
    assert batch == N_MIX_BLOCKS and seq == MP and nb == N_MIX_BLOCKS * MS
    assert norm_ffn1.shape[0] == 1, "single-layer trunk"

    xp = x_prompt.reshape(batch * seq, D_MODEL)
    xs = x_sample.reshape(N_FFN_BLOCKS, TM_S, D_MODEL)

    w_t = jnp.swapaxes(w_in[0], 0, 1)

    h1p, h1s, xn_c, g_p, g_s = _ffn(xp, xs, norm_ffn1, ffn1_gate, ffn1_up, ffn1_down, norm_mix, TF1,
                                    w_gate_t=w_t)
    za_p, za_s = _proj(xn_c, w_t)
    za_s = za_s.reshape(nb, 4 * D_A)
    yb_c, conv_p, conv_s, qc_s = _conv(xn_c, w_t, conv_w[0], conv_b, norm_conv,
                                       state_conv.reshape(N_MIX_BLOCKS, MS, 2, D_B), za_s, state_mlstm_C)

    bias_row = jnp.zeros((1, LANE), F32).at[0, :2 * H_A].set(b_gates[0].astype(F32))
    ha_p, c_p, n_p, m_p = _mlstm_prompt(za_p, g_p, bias_row, norm_mlstm, batch, seq)
    ha_s, m_s, n_s, wk_s, dec_s = _mlstm_sample(
        za_s, qc_s, g_s.reshape(nb, LANE), bias_row, norm_mlstm,
        state_mlstm_m.reshape(nb, H_A), state_mlstm_n.reshape(nb, H_A * DK))

    h2p, h2s = _outproj(h1p, h1s.reshape(N_MIX_BLOCKS, MS, D_MODEL), ha_p.reshape(batch * seq, D_A),
                        ha_s.reshape(N_MIX_BLOCKS, MS, D_A), yb_c, w_out)

    yp, ys, c_s = _ffn(h2p, h2s.reshape(N_FFN_BLOCKS, TM_S, D_MODEL), norm_ffn2, ffn2_gate, ffn2_up,
                       ffn2_down, norm_final.reshape(1, D_MODEL), TF2,
                       state=(state_mlstm_C, wk_s, za_s, dec_s))

    return (
        yp.reshape(batch, seq, D_MODEL),
        ys.reshape(nb, 1, D_MODEL),
        c_p,
        n_p,
        m_p[:, 0, :H_A].reshape(1, batch, H_A),
        conv_p,
        c_s,
        n_s.reshape(1, nb, H_A, DK),
        m_s.reshape(1, nb, H_A),
        conv_s.reshape(1, nb, 2, D_B),
    )
```

```python
import functools

import jax
import jax.numpy as jnp
from jax import lax
from jax.experimental import pallas as pl
from jax.experimental.pallas import tpu as pltpu

F32 = jnp.float32
BF16 = jnp.bfloat16

D_MODEL = 2048
D_A = 1024
D_B = 1024
H_A = 4
DK = 256
DV = 256
G_B = 8
D_FF = 5504
GATE_CAP = 15.0
EPS = 1e-6
GATE_ROW0 = 4 * D_A
CONV_ROW0 = 4 * D_A + 2 * H_A

LANE = 128
N_FFN_BLOCKS = 8
TM_P = 1024
TM_S = 16
TM = TM_P + TM_S
TF1 = 512
TF2 = 512
STATE_SEQS = 2
X_CHUNK = 128
N_X_CHUNKS = TM_P // X_CHUNK
X_AHEAD0 = 2
N_MIX_BLOCKS = N_FFN_BLOCKS // 2
MP = 2 * TM_P
MS = 2 * TM_S
PROJ_TN = 1024
CONV_TC = 256
OUT_TN = 512
MLSTM_CHUNK = 256
MLSTM_SEQS = 2
SAMPLE_BS = 8
VMEM_LIMIT = 62 * 1024 * 1024


def _rms(x, g):
    return x * lax.rsqrt(jnp.mean(x * x, axis=-1, keepdims=True) + EPS) * g


def _soft_cap(x):
    return GATE_CAP * jnp.tanh(x / GATE_CAP)


def _log_sigmoid(x):
    return -jax.nn.softplus(-x)


def _bdot(a, b):
    return jnp.dot(a, b, preferred_element_type=F32)


def _bdot_t(a, b):
    return lax.dot_general(a, b, (((1,), (1,)), ((), ())), preferred_element_type=F32)


def _sample_state_update(f, c_ref, wk_ref, v_ref, dec_ref, c_out):
    grp = jnp.minimum(f, TM_S // STATE_SEQS - 1)
    lane = lax.broadcasted_iota(jnp.int32, (D_A, LANE), 1)
    for u in range(STATE_SEQS):
        r = grp * STATE_SEQS + u
        seq = pl.program_id(0) * TM_S + r
        wk_col = jnp.sum(jnp.where(lane == seq, wk_ref[...], 0.0), axis=1, keepdims=True)
        v_row = v_ref[pl.ds(r, 1), :]
        dec = dec_ref[pl.ds(r, 1), :]
        for h in range(H_A):
            hs = slice(h * DK, (h + 1) * DK)
            c_out[0, u, h] = dec[:, h:h + 1] * c_ref[0, u, h] + wk_col[hs, :] * v_row[:, hs]


def _ffn_kernel(x_hbm, xs_ref, g_ref, wg_ref, wu_ref, wd_ref, g2_ref, *rest,
                tf, n_f, last_valid, final_norm):
    if final_norm:
        c_ref, wk_ref, v_ref, dec_ref, out_hbm, os_ref, c_out, xn_ref, acc_ref, x_sem, o_sem = rest
    else:
        wgate_ref, out_hbm, os_ref, nxt_ref, gp_ref, gs_ref, xn_ref, acc_ref, x_sem, o_sem = rest
    i = pl.program_id(0)
    f = pl.program_id(1)
    slot = lax.rem(i, 2)
    op_ref = acc_ref.at[slot]

    def x_copy(blk, s, c):
        dst0 = pl.multiple_of(c * X_CHUNK, X_CHUNK)
        src0 = pl.multiple_of(blk * TM_P + c * X_CHUNK, X_CHUNK)
        return pltpu.make_async_copy(x_hbm.at[pl.ds(src0, X_CHUNK), :],
                                     acc_ref.at[s, pl.ds(dst0, X_CHUNK), :], x_sem.at[c])

    def out_copy(blk, s):
        row0 = pl.multiple_of(blk * TM_P, TM_P)
        return pltpu.make_async_copy(acc_ref.at[s], out_hbm.at[pl.ds(row0, TM_P), :], o_sem.at[s])

    def first_step_dma():
        @pl.when(i == 0)
        def _():
            for c in range(N_X_CHUNKS):
                x_copy(0, 0, c).start()

        for c in range(N_X_CHUNKS):
            x_copy(i, slot, c).wait()

    def middle_step_dma():
        @pl.when(jnp.logical_and(i > 0, f == X_AHEAD0 - 1))
        def _():
            out_copy(i - 1, 1 - slot).wait()

        @pl.when(jnp.logical_and(jnp.logical_and(f >= X_AHEAD0, f < X_AHEAD0 + N_X_CHUNKS),
                                 i < N_FFN_BLOCKS - 1))
        def _():
            x_copy(i + 1, 1 - slot, f - X_AHEAD0).start()

    def step(valid, first, with_state):
        if first:
            g = g_ref[...]
            xn_ref[:TM_P, :] = _rms(op_ref[...], g).astype(BF16)
            xn_ref[TM_P:, :] = _rms(xs_ref[0], g).astype(BF16)
        xn = xn_ref[...]
        a = _bdot(xn, wg_ref[:, :valid].astype(BF16))
        b = _bdot(xn, wu_ref[:, :valid].astype(BF16))
        hid = (a * jax.nn.sigmoid(a) * b * 0.5).astype(BF16)
        r = _bdot(hid, wd_ref[:valid, :].astype(BF16))
        op_ref[...] += r[:TM_P]
        if first:
            os_ref[0] = xs_ref[0] + r[TM_P:]
        else:
            os_ref[0] += r[TM_P:]
        if with_state:
            _sample_state_update(f, c_ref, wk_ref, v_ref, dec_ref, c_out)

    assert last_valid < tf and (not final_norm or (n_f - 1) * STATE_SEQS >= TM_S), \
        "state updates ride the full-width steps"

    assert X_AHEAD0 >= 2 and X_AHEAD0 + N_X_CHUNKS <= n_f - 1, "hand DMAs are issued from the middle steps"

    @pl.when(f == 0)
    def _():
        first_step_dma()
        step(tf, True, final_norm)

    @pl.when(jnp.logical_and(f > 0, f < n_f - 1))
    def _():
        middle_step_dma()
        step(tf, False, final_norm)

    @pl.when(f == n_f - 1)
    def _():
        step(last_valid, False, False)
        g2 = g2_ref[...]
        if final_norm:
            op_ref[...] = _rms(op_ref[...], g2)
            os_ref[0] = _rms(os_ref[0], g2)
        else:
            nx_p = _rms(op_ref[...], g2).astype(BF16)
            nx_s = _rms(os_ref[0], g2).astype(BF16)
            nxt_ref[0, :TM_P, :] = nx_p
            nxt_ref[0, TM_P:, :] = nx_s
            wgt = wgate_ref[...].astype(BF16)
            half = TM_P // 2
            gp_ref[:half, :] = _bdot_t(nx_p[:half], wgt)
            gp_ref[half:, :] = _bdot_t(nx_p[half:], wgt)
            gs_ref[0] = _bdot_t(nx_s, wgt)

        out_copy(i, slot).start()

        @pl.when(i == N_FFN_BLOCKS - 1)
        def _():
            out_copy(i, slot).wait()


def _ffn(xp, xs, g, wg, wu, wd, g2, tf, state=None, w_gate_t=None):
    final_norm = state is not None
    n_f = pl.cdiv(D_FF, tf)
    last_valid = D_FF - (n_f - 1) * tf
    kern = functools.partial(_ffn_kernel, tf=tf, n_f=n_f, last_valid=last_valid, final_norm=final_norm)
    assert X_AHEAD0 + N_X_CHUNKS <= n_f and N_X_CHUNKS * X_CHUNK == TM_P
    in_specs = [
        pl.BlockSpec(memory_space=pl.ANY),
        pl.BlockSpec((1, TM_S, D_MODEL), lambda i, f: (i, 0, 0)),
        pl.BlockSpec((1, D_MODEL), lambda i, f: (0, 0)),
        pl.BlockSpec((None, D_MODEL, tf), lambda i, f: (0, 0, f)),
        pl.BlockSpec((None, D_MODEL, tf), lambda i, f: (0, 0, f)),
        pl.BlockSpec((None, tf, D_MODEL), lambda i, f: (0, f, 0)),
        pl.BlockSpec((1, D_MODEL), lambda i, f: (0, 0)),
    ]
    out_specs = [
        pl.BlockSpec(memory_space=pl.ANY),
        pl.BlockSpec((1, TM_S, D_MODEL), lambda i, f: (i, 0, 0)),
    ]
    out_shape = [
        jax.ShapeDtypeStruct((N_FFN_BLOCKS * TM_P, D_MODEL), F32),
        jax.ShapeDtypeStruct((N_FFN_BLOCKS, TM_S, D_MODEL), F32),
    ]
    args = [xp, xs, g, wg, wu, wd, g2]
    if final_norm:
        c0, wk, za_s, dec = state
        groups = TM_S // STATE_SEQS
        grp = lambda i, f: i * groups + jnp.minimum(f, groups - 1)
        cspec = pl.BlockSpec((1, STATE_SEQS, H_A, DK, DV), lambda i, f: (0, grp(i, f), 0, 0, 0))
        in_specs += [
            cspec,
            pl.BlockSpec((D_A, LANE), lambda i, f: (0, 0)),
            pl.BlockSpec((TM_S, D_A), lambda i, f: (i, 2)),
            pl.BlockSpec((TM_S, LANE), lambda i, f: (i, 0)),
        ]
        out_specs.append(cspec)
        out_shape.append(jax.ShapeDtypeStruct(c0.shape, F32))
        args += [c0, wk, za_s, dec]
    else:
        in_specs.append(pl.BlockSpec((LANE, D_MODEL), lambda i, f: (GATE_ROW0 // LANE, 0)))
        args.append(w_gate_t)
        out_specs += [
            pl.BlockSpec((1, TM, D_MODEL), lambda i, f: (i, 0, 0)),
            pl.BlockSpec((TM_P, LANE), lambda i, f: (i, 0)),
            pl.BlockSpec((1, TM_S, LANE), lambda i, f: (i, 0, 0)),
        ]
        out_shape += [
            jax.ShapeDtypeStruct((N_FFN_BLOCKS, TM, D_MODEL), BF16),
            jax.ShapeDtypeStruct((N_FFN_BLOCKS * TM_P, LANE), F32),
            jax.ShapeDtypeStruct((N_FFN_BLOCKS, TM_S, LANE), F32),
        ]
    return pl.pallas_call(
        kern,
        grid=(N_FFN_BLOCKS, n_f),
        in_specs=in_specs,
        out_specs=out_specs,
        out_shape=out_shape,
        scratch_shapes=[pltpu.VMEM((TM, D_MODEL), BF16),
                        pltpu.VMEM((2, TM_P, D_MODEL), F32),
                        pltpu.SemaphoreType.DMA((N_X_CHUNKS,)),
                        pltpu.SemaphoreType.DMA((2,))],
        compiler_params=pltpu.CompilerParams(
            dimension_semantics=("arbitrary", "arbitrary"), vmem_limit_bytes=VMEM_LIMIT),
        name="ffn_final" if final_norm else "ffn",
    )(*args)


def _split_rows(z0, z1, p_ref, s_ref):
    p_ref[:TM_P, :] = z0[:TM_P]
    p_ref[TM_P:, :] = z1[:TM_P]
    s_ref[0, :TM_S, :] = z0[TM_P:]
    s_ref[0, TM_S:, :] = z1[TM_P:]


def _proj_kernel(xn_ref, w_ref, zp_ref, zs_ref):
    w = w_ref[...].astype(BF16)
    _split_rows(_bdot_t(xn_ref[0], w), _bdot_t(xn_ref[1], w), zp_ref, zs_ref)


def _proj(xn_c, w_t):
    n_n = (4 * D_A) // PROJ_TN
    return pl.pallas_call(
        _proj_kernel,
        grid=(N_MIX_BLOCKS, n_n),
        in_specs=[
            pl.BlockSpec((2, TM, D_MODEL), lambda i, j: (i, 0, 0)),
            pl.BlockSpec((PROJ_TN, D_MODEL), lambda i, j: (j, 0)),
        ],
        out_specs=[
            pl.BlockSpec((MP, PROJ_TN), lambda i, j: (i, j)),
            pl.BlockSpec((1, MS, PROJ_TN), lambda i, j: (i, 0, j)),
        ],
        out_shape=[
            jax.ShapeDtypeStruct((N_MIX_BLOCKS * MP, 4 * D_A), F32),
            jax.ShapeDtypeStruct((N_MIX_BLOCKS, MS, 4 * D_A), F32),
        ],
        compiler_params=pltpu.CompilerParams(
            dimension_semantics=("parallel", "arbitrary"), vmem_limit_bytes=VMEM_LIMIT),
        name="proj_qkvo",
    )(xn_c, w_t)


def _group_norm(yb, nrm):
    gw = D_B // G_B
    parts = []
    for g in range(yb.shape[1] // gw):
        seg = yb[:, g * gw:(g + 1) * gw]
        parts.append(seg * lax.rsqrt(jnp.mean(seg * seg, axis=-1, keepdims=True) + EPS))
    return jnp.concatenate(parts, axis=1) * nrm


def _conv_kernel(xn_ref, wgb_ref, wgc_ref, wxc_ref, cw_ref, cb_ref, nrm_ref, buf_ref, q_ref, c_ref,
                 yb_ref, cp_ref, cs_ref, qc_ref):
    q_t = jnp.concatenate([q_ref[...], jnp.zeros((LANE - SAMPLE_BS, D_A), F32)], axis=0).T
    for h in range(H_A):
        hs = slice(h * DK, (h + 1) * DK)
        for j in range(SAMPLE_BS):
            qc_ref[j:j + 1, hs] = jnp.sum(q_t[hs, j:j + 1] * c_ref[0, j, h], axis=0, keepdims=True)

    wgb = wgb_ref[...].astype(BF16)
    wgc = wgc_ref[...].astype(BF16)
    wxc = wxc_ref[...].astype(BF16)
    cw0 = cw_ref[0:1, :]
    cw1 = cw_ref[1:2, :]
    cw2 = cw_ref[2:3, :]
    cb = cb_ref[...]
    nrm = nrm_ref[...]
    row = lax.broadcasted_iota(jnp.int32, (TM_P, wgb.shape[0]), 0)
    zero_row = jnp.zeros((1, wgb.shape[0]), F32)
    um2, um1 = zero_row, zero_row

    for t in range(2):
        xn = xn_ref[t]
        u = _bdot_t(xn, wgc) * _bdot_t(xn, wxc)
        gb = _bdot_t(xn, wgb)

        up = u[:TM_P]
        u1 = jnp.where(row < 1, um1, pltpu.roll(up, 1, 0))
        u2 = jnp.where(row < 1, um2, jnp.where(row < 2, um1, pltpu.roll(up, 2, 0)))
        yc = cw0 * u2 + cw1 * u1 + cw2 * up + cb
        yb_ref[0, t * TM_P:(t + 1) * TM_P, :] = _group_norm(gb[:TM_P] * yc, nrm).astype(yb_ref.dtype)
        um2, um1 = up[TM_P - 2:TM_P - 1, :], up[TM_P - 1:TM_P, :]

        ss = slice(t * TM_S, (t + 1) * TM_S)
        us = u[TM_P:]
        b0 = buf_ref[0, ss, 0, :]
        b1 = buf_ref[0, ss, 1, :]
        ycs = cw0 * b0 + cw1 * b1 + cw2 * us + cb
        yb_ref[0, MP + t * TM_S:MP + (t + 1) * TM_S, :] = _group_norm(gb[TM_P:] * ycs, nrm).astype(yb_ref.dtype)
        cs_ref[0, ss, 0, :] = b1
        cs_ref[0, ss, 1, :] = us

    cp_ref[0, 0, 0:1, :] = um2
    cp_ref[0, 0, 1:2, :] = um1


def _conv(xn_c, w_t, cw, cb_row, nrm_row, buf, za_s, c0):
    tc = CONV_TC
    n_c = D_B // tc
    assert N_MIX_BLOCKS * n_c * SAMPLE_BS == za_s.shape[0], "one sample group per grid step"
    wspec = lambda base: pl.BlockSpec(
        (pl.Element(tc), pl.Element(D_MODEL)), lambda i, c: (pl.multiple_of(base + c * tc, 8), 0))
    grp = lambda i, c: i * n_c + c
    return pl.pallas_call(
        _conv_kernel,
        grid=(N_MIX_BLOCKS, n_c),
        in_specs=[
            pl.BlockSpec((2, TM, D_MODEL), lambda i, c: (i, 0, 0)),
            wspec(CONV_ROW0),
            wspec(CONV_ROW0 + D_B),
            wspec(CONV_ROW0 + 2 * D_B),
            pl.BlockSpec((3, tc), lambda i, c: (0, c)),
            pl.BlockSpec((1, tc), lambda i, c: (0, c)),
            pl.BlockSpec((1, tc), lambda i, c: (0, c)),
            pl.BlockSpec((1, MS, 2, tc), lambda i, c: (i, 0, 0, c)),
            pl.BlockSpec((SAMPLE_BS, D_A), lambda i, c: (grp(i, c), 0)),
            pl.BlockSpec((1, SAMPLE_BS, H_A, DK, DV), lambda i, c: (0, grp(i, c), 0, 0, 0)),
        ],
        out_specs=[
            pl.BlockSpec((1, MP + MS, tc), lambda i, c: (i, 0, c)),
            pl.BlockSpec((1, 1, 2, tc), lambda i, c: (0, i, 0, c)),
            pl.BlockSpec((1, MS, 2, tc), lambda i, c: (i, 0, 0, c)),
            pl.BlockSpec((SAMPLE_BS, D_A), lambda i, c: (grp(i, c), 0)),
        ],
        out_shape=[
            jax.ShapeDtypeStruct((N_MIX_BLOCKS, MP + MS, D_B), BF16),
            jax.ShapeDtypeStruct((1, N_MIX_BLOCKS, 2, D_B), F32),
            jax.ShapeDtypeStruct((N_MIX_BLOCKS, MS, 2, D_B), F32),
            jax.ShapeDtypeStruct((za_s.shape[0], D_A), F32),
        ],
        compiler_params=pltpu.CompilerParams(
            dimension_semantics=("parallel", "arbitrary"), vmem_limit_bytes=VMEM_LIMIT),
        name="proj_conv",
    )(xn_c, w_t, w_t, w_t, cw, cb_row, nrm_row, buf, za_s, c0)


def _mlstm_prompt_kernel(z_ref, g_ref, bias_ref, nrm_ref,
                         ha_ref, c_ref, n_ref, m_ref, cx_s, m_s, *, n_chunks):
    L = MLSTM_CHUNK
    c = pl.program_id(1)

    @pl.when(c == 0)
    def _():
        cx_s[...] = jnp.zeros_like(cx_s)
        m_s[...] = jnp.zeros_like(m_s)

    row = lax.broadcasted_iota(jnp.int32, (L, L), 0)
    col = lax.broadcasted_iota(jnp.int32, (L, L), 1)
    causal = row >= col
    tri = causal.astype(BF16)

    for b in range(MLSTM_SEQS):
        _mlstm_prompt_seq(b, causal, tri, z_ref, g_ref, bias_ref, nrm_ref, ha_ref, cx_s, m_s)

    @pl.when(c == n_chunks - 1)
    def _():
        for b in range(MLSTM_SEQS):
            m_ref[b] = m_s[b]
            for h in range(H_A):
                cx = cx_s[b * H_A + h]
                c_ref[0, b, h] = cx[:, :DV]
                n_ref[0, b, h:h + 1, :] = cx[:, DV:].T[h:h + 1, :]


def _prefix_max_rows(x):
    rows = x.shape[0]
    row = lax.broadcasted_iota(jnp.int32, x.shape, 0)
    k = 1
    while k < rows:
        x = jnp.where(row >= k, jnp.maximum(x, pltpu.roll(x, k, 0)), x)
        k *= 2
    return x


def _mlstm_prompt_seq(b, causal, tri, z_ref, g_ref, bias_ref, nrm_ref, ha_ref, cx_s, m_s):
    L = MLSTM_CHUNK
    lane = lax.broadcasted_iota(jnp.int32, (L, LANE), 1)
    lane_dv = lax.broadcasted_iota(jnp.int32, (DV, LANE), 1)
    capped = _soft_cap(g_ref[b] + bias_ref[...])
    lf = _log_sigmoid(capped)
    hi = lf.astype(BF16)
    r1 = lf - hi.astype(F32)
    mid = r1.astype(BF16)
    lo = (r1 - mid.astype(F32)).astype(BF16)
    parts = _bdot(tri, jnp.concatenate([hi, mid, lo], axis=1))
    bc = parts[:, :LANE] + parts[:, LANE:2 * LANE] + parts[:, 2 * LANE:]
    bh = pltpu.roll(bc, LANE - H_A, 1)
    a = capped - bh
    a_t = a.T
    m_prev = m_s[b]
    big_m = jnp.maximum(_prefix_max_rows(a), m_prev[0:1, :])
    sc_all = jnp.exp(m_prev[0:1, :] - big_m)
    emt_all = jnp.exp(-(bh + big_m))
    m_last = big_m[L - 1:L, :]
    w_all = jnp.exp(a - m_last)
    decay_all = sc_all[L - 1:L, :]
    m_s[b] = jnp.broadcast_to(bh[L - 1:L, :] + m_last, (8, LANE))

    nums = []
    nd_tail = jnp.zeros((L, LANE), F32)
    sq_tail = jnp.zeros((L, LANE), F32)
    for h in range(H_A):
        hs = slice(h * DK, (h + 1) * DK)
        onehot = (lane == h).astype(BF16)
        qb = z_ref[b, :, hs].astype(BF16)
        kf = z_ref[b, :, slice(D_A + h * DK, D_A + (h + 1) * DK)] * (DK ** -0.5)
        kb = kf.astype(BF16)
        v_h = z_ref[b, :, slice(2 * D_A + h * DK, 2 * D_A + (h + 1) * DK)]
        vx = jnp.concatenate([v_h.astype(BF16), onehot], axis=1)
        cx_prev = cx_s[b * H_A + h]

        p = jnp.exp(jnp.where(causal, a_t[h:h + 1, :] - big_m[:, h:h + 1], -jnp.inf))
        s = _bdot_t(qb, kb) * p
        nd = sc_all[:, h:h + 1] * _bdot(qb, cx_prev.astype(BF16)) + _bdot(s.astype(BF16), vx)
        num = nd[:, :DV]
        nums.append(num)
        nd_tail = nd_tail + nd[:, DV:]
        sq_tail = sq_tail + _bdot((num * num).astype(BF16), (lane_dv == h).astype(BF16))

        wk = (w_all[:, h:h + 1] * kf).astype(BF16)
        cx_s[b * H_A + h] = decay_all[:, h:h + 1] * cx_prev + lax.dot_general(
            wk, vx, (((0,), (0,)), ((), ())), preferred_element_type=F32)

    inv = 1.0 / jnp.maximum(jnp.abs(nd_tail), emt_all)
    ms = inv * inv * sq_tail * (1.0 / DV)
    scale_all = inv * lax.rsqrt(ms + EPS)
    for h in range(H_A):
        hs = slice(h * DK, (h + 1) * DK)
        hn = nums[h] * scale_all[:, h:h + 1]
        o_h = z_ref[b, :, slice(3 * D_A + h * DK, 3 * D_A + (h + 1) * DK)]
        ha_ref[b, :, hs] = (hn * nrm_ref[:, hs] * jax.nn.sigmoid(o_h)).astype(ha_ref.dtype)


def _mlstm_prompt(za_p, g_p, bias_row, nrm_row, batch, seq):
    L = MLSTM_CHUNK
    nsq = MLSTM_SEQS
    n_chunks = seq // L
    kern = functools.partial(_mlstm_prompt_kernel, n_chunks=n_chunks)
    za3 = za_p.reshape(batch, seq, 4 * D_A)
    g3 = g_p.reshape(batch, seq, LANE)
    return pl.pallas_call(
        kern,
        grid=(batch // nsq, n_chunks),
        in_specs=[
            pl.BlockSpec((nsq, L, 4 * D_A), lambda b, c: (b, c, 0)),
            pl.BlockSpec((nsq, L, LANE), lambda b, c: (b, c, 0)),
            pl.BlockSpec((1, LANE), lambda b, c: (0, 0)),
            pl.BlockSpec((1, D_A), lambda b, c: (0, 0)),
        ],
        out_specs=[
            pl.BlockSpec((nsq, L, D_A), lambda b, c: (b, c, 0)),
            pl.BlockSpec((1, nsq, H_A, DK, DV), lambda b, c: (0, b, 0, 0, 0)),
            pl.BlockSpec((1, nsq, H_A, DK), lambda b, c: (0, b, 0, 0)),
            pl.BlockSpec((nsq, 8, LANE), lambda b, c: (b, 0, 0)),
        ],
        out_shape=[
            jax.ShapeDtypeStruct((batch, seq, D_A), BF16),
            jax.ShapeDtypeStruct((1, batch, H_A, DK, DV), F32),
            jax.ShapeDtypeStruct((1, batch, H_A, DK), F32),
            jax.ShapeDtypeStruct((batch, 8, LANE), F32),
        ],
        scratch_shapes=[pltpu.VMEM((nsq * H_A, DK, DV + LANE), F32),
                        pltpu.VMEM((nsq, 8, LANE), F32)],
        compiler_params=pltpu.CompilerParams(
            dimension_semantics=("parallel", "arbitrary"), vmem_limit_bytes=VMEM_LIMIT),
        name="mlstm_prompt",
    )(za3, g3, bias_row, nrm_row)


def _expand_heads(x, width):
    rows = x.shape[0]
    return jnp.concatenate([jnp.broadcast_to(x[:, h:h + 1], (rows, width)) for h in range(H_A)], axis=1)


def _head_sums(x):
    return jnp.concatenate(
        [jnp.sum(x[:, h * DK:(h + 1) * DK], axis=-1, keepdims=True) for h in range(H_A)], axis=1)


def _mlstm_sample_kernel(za_ref, qc_ref, g_ref, bias_ref, nrm_ref, m_ref, n_ref,
                         ha_ref, m_out, n_out, wk_out, dec_out):
    nb = za_ref.shape[0]
    capped = _soft_cap(g_ref[...] + bias_ref[...])
    logi = capped[:, 0:H_A]
    logf = _log_sigmoid(capped)[:, H_A:2 * H_A]
    m_prev = m_ref[...]
    m_inter = logf + m_prev
    m_t = jnp.maximum(m_inter, logi)
    sc = jnp.exp(m_inter - m_t)
    ei = jnp.exp(logi - m_t)
    emt = jnp.exp(-m_t)

    q = za_ref[:, 0:D_A]
    k = za_ref[:, D_A:2 * D_A] * (DK ** -0.5)
    v = za_ref[:, 2 * D_A:3 * D_A]
    o = za_ref[:, 3 * D_A:4 * D_A]
    n_prev = n_ref[...]

    s = _head_sums(q * k) * ei
    den = sc * _head_sums(q * n_prev) + s
    denom = jnp.maximum(jnp.abs(den), emt)
    num = _expand_heads(sc, DV) * qc_ref[...] + _expand_heads(s, DV) * v
    hh = num / _expand_heads(denom, DV)
    ms = _head_sums(hh * hh) * (1.0 / DV)
    hn = hh * lax.rsqrt(_expand_heads(ms, DV) + EPS)
    ha_ref[...] = (hn * nrm_ref[...] * jax.nn.sigmoid(o)).astype(ha_ref.dtype)
    n_out[...] = _expand_heads(sc, DK) * n_prev + _expand_heads(ei, DK) * k
    m_out[...] = m_t
    pad = jnp.zeros((nb, LANE - H_A), F32)
    dec_out[...] = jnp.concatenate([sc, pad], axis=1)

    k_t = k.T
    ei_t = jnp.concatenate([ei, pad], axis=1).T
    for h in range(H_A):
        hs = slice(h * DK, (h + 1) * DK)
        wk_out[hs, :] = k_t[hs, :] * ei_t[h:h + 1, :]


def _mlstm_sample(za_s, qc, g_s, bias_row, nrm_row, m0, n0):
    nb = za_s.shape[0]
    assert nb == LANE, "the sequence axis is transposed onto the lanes"
    return pl.pallas_call(
        _mlstm_sample_kernel,
        out_shape=[
            jax.ShapeDtypeStruct((nb, D_A), BF16),
            jax.ShapeDtypeStruct((nb, H_A), F32),
            jax.ShapeDtypeStruct((nb, H_A * DK), F32),
            jax.ShapeDtypeStruct((D_A, nb), F32),
            jax.ShapeDtypeStruct((nb, LANE), F32),
        ],
        compiler_params=pltpu.CompilerParams(vmem_limit_bytes=VMEM_LIMIT),
        name="mlstm_sample",
    )(za_s, qc, g_s, bias_row, nrm_row, m0, n0)


def _outproj_kernel(hp_ref, hs_ref, ap_ref, as_ref, yb_ref, w_ref, op_ref, os_ref, wc_ref):
    j = pl.program_id(1)

    @pl.when(pl.program_id(0) == 0)
    def _():
        wc_ref[j] = w_ref[...].astype(BF16)

    wa = wc_ref[j, :D_A, :]
    wb = wc_ref[j, D_A:, :]
    op_ref[...] = hp_ref[...] + _bdot(ap_ref[...], wa) + _bdot(yb_ref[0, :MP, :], wb)
    os_ref[0] = hs_ref[0] + _bdot(as_ref[0], wa) + _bdot(yb_ref[0, MP:, :], wb)


def _outproj(hp, hs, ap, a_s, yb_c, w3):
    tn = OUT_TN
    n_n = D_MODEL // tn
    return pl.pallas_call(
        _outproj_kernel,
        grid=(N_MIX_BLOCKS, n_n),
        in_specs=[
            pl.BlockSpec((MP, tn), lambda i, j: (i, j)),
            pl.BlockSpec((1, MS, tn), lambda i, j: (i, 0, j)),
            pl.BlockSpec((MP, D_A), lambda i, j: (i, 0)),
            pl.BlockSpec((1, MS, D_A), lambda i, j: (i, 0, 0)),
            pl.BlockSpec((1, MP + MS, D_B), lambda i, j: (i, 0, 0)),
            pl.BlockSpec((None, D_A + D_B, tn), lambda i, j: (0, 0, jnp.where(i == 0, j, n_n - 1))),
        ],
        out_specs=[
            pl.BlockSpec((MP, tn), lambda i, j: (i, j)),
            pl.BlockSpec((1, MS, tn), lambda i, j: (i, 0, j)),
        ],
        out_shape=[
            jax.ShapeDtypeStruct((N_MIX_BLOCKS * MP, D_MODEL), F32),
            jax.ShapeDtypeStruct((N_MIX_BLOCKS, MS, D_MODEL), F32),
        ],
        scratch_shapes=[pltpu.VMEM((n_n, D_A + D_B, tn), BF16)],
        compiler_params=pltpu.CompilerParams(
            dimension_semantics=("arbitrary", "arbitrary"), vmem_limit_bytes=VMEM_LIMIT),
        name="outproj",
    )(hp, hs, ap, a_s, yb_c, w3)


def kernel(x_prompt, x_sample, state_mlstm_C, state_mlstm_n, state_mlstm_m, state_conv, norm_ffn1, ffn1_gate, ffn1_up, ffn1_down, norm_mix, w_in, b_gates, conv_w, conv_b, norm_mlstm, norm_conv, w_out, norm_ffn2, ffn2_gate, ffn2_up, ffn2_down, norm_final):
    batch, seq, _ = x_prompt.shape
    nb = x_sample.shape[0]
    assert batch == N_MIX_BLOCKS and seq == MP and nb == N_MIX_BLOCKS * MS
    assert norm_ffn1.shape[0] == 1, "single-layer trunk"

    xp = x_prompt.reshape(batch * seq, D_MODEL)
    xs = x_sample.reshape(N_FFN_BLOCKS, TM_S, D_MODEL)

    w_t = jnp.swapaxes(w_in[0], 0, 1)

    h1p, h1s, xn_c, g_p, g_s = _ffn(xp, xs, norm_ffn1, ffn1_gate, ffn1_up, ffn1_down, norm_mix, TF1,
                                    w_gate_t=w_t)
    za_p, za_s = _proj(xn_c, w_t)
    za_s = za_s.reshape(nb, 4 * D_A)
    yb_c, conv_p, conv_s, qc_s = _conv(xn_c, w_t, conv_w[0], conv_b, norm_conv,
                                       state_conv.reshape(N_MIX_BLOCKS, MS, 2, D_B), za_s, state_mlstm_C)

    bias_row = jnp.zeros((1, LANE), F32).at[0, :2 * H_A].set(b_gates[0].astype(F32))
    ha_p, c_p, n_p, m_p = _mlstm_prompt(za_p, g_p, bias_row, norm_mlstm, batch, seq)
    ha_s, m_s, n_s, wk_s, dec_s = _mlstm_sample(
        za_s, qc_s, g_s.reshape(nb, LANE), bias_row, norm_mlstm,
        state_mlstm_m.reshape(nb, H_A), state_mlstm_n.reshape(nb, H_A * DK))

    h2p, h2s = _outproj(h1p, h1s.reshape(N_MIX_BLOCKS, MS, D_MODEL), ha_p.reshape(batch * seq, D_A),
                        ha_s.reshape(N_MIX_BLOCKS, MS, D_A), yb_c, w_out)

    yp, ys, c_s = _ffn(h2p, h2s.reshape(N_FFN_BLOCKS, TM_S, D_MODEL), norm_ffn2, ffn2_gate, ffn2_up,
                       ffn2_down, norm_final.reshape(1, D_MODEL), TF2,
                       state=(state_mlstm_C, wk_s, za_s, dec_s))

    return (
        yp.reshape(batch, seq, D_MODEL),
        ys.reshape(nb, 1, D_MODEL),
        c_p,
        n_p,
        m_p[:, 0, :H_A].reshape(1, batch, H_A),
        conv_p,
        c_s,
        n_s.reshape(1, nb, H_A, DK),
        m_s.reshape(1, nb, H_A),
        conv_s.reshape(1, nb, 2, D_B),
    )
```

```python
import functools

import jax
import jax.numpy as jnp
from jax import lax
from jax.experimental import pallas as pl
from jax.experimental.pallas import tpu as pltpu

F32 = jnp.float32
BF16 = jnp.bfloat16

D_MODEL = 2048
D_A = 1024
D_B = 1024
H_A = 4
DK = 256
DV = 256
G_B = 8
D_FF = 5504
GATE_CAP = 15.0
EPS = 1e-6
GATE_ROW0 = 4 * D_A
CONV_ROW0 = 4 * D_A + 2 * H_A

LANE = 128
N_FFN_BLOCKS = 8
TM_P = 1024
TM_S = 16
TM = TM_P + TM_S
TF1 = 512
TF2 = 512
STATE_SEQS = 2
X_CHUNK = 128
N_X_CHUNKS = TM_P // X_CHUNK
X_AHEAD0 = 2
N_MIX_BLOCKS = N_FFN_BLOCKS // 2
MP = 2 * TM_P
MS = 2 * TM_S
PROJ_TN = 1024
CONV_TC = 256
OUT_TN = 512
MLSTM_CHUNK = 256
MLSTM_SEQS = 2
SAMPLE_BS = 8
VMEM_LIMIT = 62 * 1024 * 1024


def _rms(x, g):
    return x * lax.rsqrt(jnp.mean(x * x, axis=-1, keepdims=True) + EPS) * g


def _soft_cap(x):
    return GATE_CAP * jnp.tanh(x / GATE_CAP)


def _log_sigmoid(x):
    return -jax.nn.softplus(-x)


def _bdot(a, b):
    return jnp.dot(a, b, preferred_element_type=F32)


def _bdot_t(a, b):
    return lax.dot_general(a, b, (((1,), (1,)), ((), ())), preferred_element_type=F32)


def _sample_state_update(f, c_ref, wk_ref, v_ref, dec_ref, c_out):
    grp = jnp.minimum(f, TM_S // STATE_SEQS - 1)
    lane = lax.broadcasted_iota(jnp.int32, (D_A, LANE), 1)
    for u in range(STATE_SEQS):
        r = grp * STATE_SEQS + u
        seq = pl.program_id(0) * TM_S + r
        wk_col = jnp.sum(jnp.where(lane == seq, wk_ref[...], 0.0), axis=1, keepdims=True)
        v_row = v_ref[pl.ds(r, 1), :]
        dec = dec_ref[pl.ds(r, 1), :]
        for h in range(H_A):
            hs = slice(h * DK, (h + 1) * DK)
            c_out[0, u, h] = dec[:, h:h + 1] * c_ref[0, u, h] + wk_col[hs, :] * v_row[:, hs]


def _ffn_kernel(x_hbm, xs_ref, g_ref, wg_ref, wu_ref, wd_ref, g2_ref, *rest,
                tf, n_f, last_valid, final_norm):
    if final_norm:
        c_ref, wk_ref, v_ref, dec_ref, out_hbm, os_ref, c_out, xn_ref, acc_ref, x_sem, o_sem = rest
    else:
        wgate_ref, out_hbm, os_ref, nxt_ref, gp_ref, gs_ref, xn_ref, acc_ref, x_sem, o_sem = rest
    i = pl.program_id(0)
    f = pl.program_id(1)
    slot = lax.rem(i, 2)
    op_ref = acc_ref.at[slot]

    def x_copy(blk, s, c):
        dst0 = pl.multiple_of(c * X_CHUNK, X_CHUNK)
        src0 = pl.multiple_of(blk * TM_P + c * X_CHUNK, X_CHUNK)
        return pltpu.make_async_copy(x_hbm.at[pl.ds(src0, X_CHUNK), :],
                                     acc_ref.at[s, pl.ds(dst0, X_CHUNK), :], x_sem.at[c])

    def out_copy(blk, s):
        row0 = pl.multiple_of(blk * TM_P, TM_P)
        return pltpu.make_async_copy(acc_ref.at[s], out_hbm.at[pl.ds(row0, TM_P), :], o_sem.at[s])

    def first_step_dma():
        @pl.when(i == 0)
        def _():
            for c in range(N_X_CHUNKS):
                x_copy(0, 0, c).start()

        for c in range(N_X_CHUNKS):
            x_copy(i, slot, c).wait()

    def middle_step_dma():
        @pl.when(jnp.logical_and(i > 0, f == X_AHEAD0 - 1))
        def _():
            out_copy(i - 1, 1 - slot).wait()

        @pl.when(jnp.logical_and(jnp.logical_and(f >= X_AHEAD0, f < X_AHEAD0 + N_X_CHUNKS),
                                 i < N_FFN_BLOCKS - 1))
        def _():
            x_copy(i + 1, 1 - slot, f - X_AHEAD0).start()

    xs_idx = (0,) if final_norm else (slice(None), 0)
    os_idx = (slice(None), 0) if final_norm else (0,)

    def step(valid, first, with_state):
        if first:
            g = g_ref[...]
            xn_ref[:TM_P, :] = _rms(op_ref[...], g).astype(BF16)
            xn_ref[TM_P:, :] = _rms(xs_ref[xs_idx], g).astype(BF16)
        xn = xn_ref[...]
        a = _bdot(xn, wg_ref[:, :valid].astype(BF16))
        b = _bdot(xn, wu_ref[:, :valid].astype(BF16))
        hid = (a * jax.nn.sigmoid(a) * b * 0.5).astype(BF16)
        r = _bdot(hid, wd_ref[:valid, :].astype(BF16))
        op_ref[...] += r[:TM_P]
        if first:
            os_ref[os_idx] = xs_ref[xs_idx] + r[TM_P:]
        else:
            os_ref[os_idx] += r[TM_P:]
        if with_state:
            _sample_state_update(f, c_ref, wk_ref, v_ref, dec_ref, c_out)

    assert last_valid < tf and (not final_norm or (n_f - 1) * STATE_SEQS >= TM_S), \
        "state updates ride the full-width steps"

    assert X_AHEAD0 >= 2 and X_AHEAD0 + N_X_CHUNKS <= n_f - 1, "hand DMAs are issued from the middle steps"

    @pl.when(f == 0)
    def _():
        first_step_dma()
        step(tf, True, final_norm)

    @pl.when(jnp.logical_and(f > 0, f < n_f - 1))
    def _():
        middle_step_dma()
        step(tf, False, final_norm)

    @pl.when(f == n_f - 1)
    def _():
        step(last_valid, False, False)
        g2 = g2_ref[...]
        if final_norm:
            op_ref[...] = _rms(op_ref[...], g2)
            os_ref[os_idx] = _rms(os_ref[os_idx], g2)
        else:
            nx_p = _rms(op_ref[...], g2).astype(BF16)
            nx_s = _rms(os_ref[os_idx], g2).astype(BF16)
            nxt_ref[0, :TM_P, :] = nx_p
            nxt_ref[0, TM_P:, :] = nx_s
            wgt = wgate_ref[...].astype(BF16)
            half = TM_P // 2
            gp_ref[:half, :] = _bdot_t(nx_p[:half], wgt)
            gp_ref[half:, :] = _bdot_t(nx_p[half:], wgt)
            gs_ref[0] = _bdot_t(nx_s, wgt)

        out_copy(i, slot).start()

        @pl.when(i == N_FFN_BLOCKS - 1)
        def _():
            out_copy(i, slot).wait()


def _ffn(xp, xs, g, wg, wu, wd, g2, tf, state=None, w_gate_t=None):
    final_norm = state is not None
    n_f = pl.cdiv(D_FF, tf)
    last_valid = D_FF - (n_f - 1) * tf
    kern = functools.partial(_ffn_kernel, tf=tf, n_f=n_f, last_valid=last_valid, final_norm=final_norm)
    assert X_AHEAD0 + N_X_CHUNKS <= n_f and N_X_CHUNKS * X_CHUNK == TM_P
    blocked_s = pl.BlockSpec((1, TM_S, D_MODEL), lambda i, f: (i, 0, 0))
    native_s = pl.BlockSpec((TM_S, 1, D_MODEL), lambda i, f: (i, 0, 0))
    in_specs = [
        pl.BlockSpec(memory_space=pl.ANY),
        blocked_s if final_norm else native_s,
        pl.BlockSpec((1, D_MODEL), lambda i, f: (0, 0)),
        pl.BlockSpec((None, D_MODEL, tf), lambda i, f: (0, 0, f)),
        pl.BlockSpec((None, D_MODEL, tf), lambda i, f: (0, 0, f)),
        pl.BlockSpec((None, tf, D_MODEL), lambda i, f: (0, f, 0)),
        pl.BlockSpec((1, D_MODEL), lambda i, f: (0, 0)),
    ]
    out_specs = [
        pl.BlockSpec(memory_space=pl.ANY),
        native_s if final_norm else blocked_s,
    ]
    out_shape = [
        jax.ShapeDtypeStruct((N_FFN_BLOCKS * TM_P, D_MODEL), F32),
        jax.ShapeDtypeStruct((N_FFN_BLOCKS * TM_S, 1, D_MODEL) if final_norm
                             else (N_FFN_BLOCKS, TM_S, D_MODEL), F32),
    ]
    args = [xp, xs, g, wg, wu, wd, g2]
    if final_norm:
        c0, wk, za_s, dec = state
        groups = TM_S // STATE_SEQS
        grp = lambda i, f: i * groups + jnp.minimum(f, groups - 1)
        cspec = pl.BlockSpec((1, STATE_SEQS, H_A, DK, DV), lambda i, f: (0, grp(i, f), 0, 0, 0))
        in_specs += [
            cspec,
            pl.BlockSpec((D_A, LANE), lambda i, f: (0, 0)),
            pl.BlockSpec((TM_S, D_A), lambda i, f: (i, 2)),
            pl.BlockSpec((TM_S, LANE), lambda i, f: (i, 0)),
        ]
        out_specs.append(cspec)
        out_shape.append(jax.ShapeDtypeStruct(c0.shape, F32))
        args += [c0, wk, za_s, dec]
    else:
        in_specs.append(pl.BlockSpec((LANE, D_MODEL), lambda i, f: (GATE_ROW0 // LANE, 0)))
        args.append(w_gate_t)
        out_specs += [
            pl.BlockSpec((1, TM, D_MODEL), lambda i, f: (i, 0, 0)),
            pl.BlockSpec((TM_P, LANE), lambda i, f: (i, 0)),
            pl.BlockSpec((1, TM_S, LANE), lambda i, f: (i, 0, 0)),
        ]
        out_shape += [
            jax.ShapeDtypeStruct((N_FFN_BLOCKS, TM, D_MODEL), BF16),
            jax.ShapeDtypeStruct((N_FFN_BLOCKS * TM_P, LANE), F32),
            jax.ShapeDtypeStruct((N_FFN_BLOCKS, TM_S, LANE), F32),
        ]
    return pl.pallas_call(
        kern,
        grid=(N_FFN_BLOCKS, n_f),
        in_specs=in_specs,
        out_specs=out_specs,
        out_shape=out_shape,
        scratch_shapes=[pltpu.VMEM((TM, D_MODEL), BF16),
                        pltpu.VMEM((2, TM_P, D_MODEL), F32),
                        pltpu.SemaphoreType.DMA((N_X_CHUNKS,)),
                        pltpu.SemaphoreType.DMA((2,))],
        compiler_params=pltpu.CompilerParams(
            dimension_semantics=("arbitrary", "arbitrary"), vmem_limit_bytes=VMEM_LIMIT),
        name="ffn_final" if final_norm else "ffn",
    )(*args)


def _split_rows(z0, z1, p_ref, s_ref):
    p_ref[:TM_P, :] = z0[:TM_P]
    p_ref[TM_P:, :] = z1[:TM_P]
    s_ref[0, :TM_S, :] = z0[TM_P:]
    s_ref[0, TM_S:, :] = z1[TM_P:]


def _proj_kernel(xn_ref, w_ref, zp_ref, zs_ref):
    w = w_ref[...].astype(BF16)
    _split_rows(_bdot_t(xn_ref[0], w), _bdot_t(xn_ref[1], w), zp_ref, zs_ref)


def _proj(xn_c, w_t):
    n_n = (4 * D_A) // PROJ_TN
    return pl.pallas_call(
        _proj_kernel,
        grid=(N_MIX_BLOCKS, n_n),
        in_specs=[
            pl.BlockSpec((2, TM, D_MODEL), lambda i, j: (i, 0, 0)),
            pl.BlockSpec((PROJ_TN, D_MODEL), lambda i, j: (j, 0)),
        ],
        out_specs=[
            pl.BlockSpec((MP, PROJ_TN), lambda i, j: (i, j)),
            pl.BlockSpec((1, MS, PROJ_TN), lambda i, j: (i, 0, j)),
        ],
        out_shape=[
            jax.ShapeDtypeStruct((N_MIX_BLOCKS * MP, 4 * D_A), F32),
            jax.ShapeDtypeStruct((N_MIX_BLOCKS, MS, 4 * D_A), F32),
        ],
        compiler_params=pltpu.CompilerParams(
            dimension_semantics=("parallel", "arbitrary"), vmem_limit_bytes=VMEM_LIMIT),
        name="proj_qkvo",
    )(xn_c, w_t)


def _group_norm(yb, nrm):
    gw = D_B // G_B
    parts = []
    for g in range(yb.shape[1] // gw):
        seg = yb[:, g * gw:(g + 1) * gw]
        parts.append(seg * lax.rsqrt(jnp.mean(seg * seg, axis=-1, keepdims=True) + EPS))
    return jnp.concatenate(parts, axis=1) * nrm


def _conv_kernel(xn_ref, wgb_ref, wgc_ref, wxc_ref, cw_ref, cb_ref, nrm_ref, buf_ref, q_ref, c_ref,
                 yb_ref, cp_ref, cs_ref, qc_ref):
    q_t = jnp.concatenate([q_ref[...], jnp.zeros((LANE - SAMPLE_BS, D_A), F32)], axis=0).T
    for h in range(H_A):
        hs = slice(h * DK, (h + 1) * DK)
        for j in range(SAMPLE_BS):
            qc_ref[j:j + 1, hs] = jnp.sum(q_t[hs, j:j + 1] * c_ref[0, j, h], axis=0, keepdims=True)

    wgb = wgb_ref[...].astype(BF16)
    wgc = wgc_ref[...].astype(BF16)
    wxc = wxc_ref[...].astype(BF16)
    cw0 = cw_ref[0:1, :]
    cw1 = cw_ref[1:2, :]
    cw2 = cw_ref[2:3, :]
    cb = cb_ref[...]
    nrm = nrm_ref[...]
    row = lax.broadcasted_iota(jnp.int32, (TM_P, wgb.shape[0]), 0)
    zero_row = jnp.zeros((1, wgb.shape[0]), F32)
    um2, um1 = zero_row, zero_row

    for t in range(2):
        xn = xn_ref[t]
        u = _bdot_t(xn, wgc) * _bdot_t(xn, wxc)
        gb = _bdot_t(xn, wgb)

        up = u[:TM_P]
        u1 = jnp.where(row < 1, um1, pltpu.roll(up, 1, 0))
        u2 = jnp.where(row < 1, um2, jnp.where(row < 2, um1, pltpu.roll(up, 2, 0)))
        yc = cw0 * u2 + cw1 * u1 + cw2 * up + cb
        yb_ref[0, t * TM_P:(t + 1) * TM_P, :] = _group_norm(gb[:TM_P] * yc, nrm).astype(yb_ref.dtype)
        um2, um1 = up[TM_P - 2:TM_P - 1, :], up[TM_P - 1:TM_P, :]

        ss = slice(t * TM_S, (t + 1) * TM_S)
        us = u[TM_P:]
        b0 = buf_ref[0, ss, 0, :]
        b1 = buf_ref[0, ss, 1, :]
        ycs = cw0 * b0 + cw1 * b1 + cw2 * us + cb
        yb_ref[0, MP + t * TM_S:MP + (t + 1) * TM_S, :] = _group_norm(gb[TM_P:] * ycs, nrm).astype(yb_ref.dtype)
        cs_ref[0, ss, 0, :] = b1
        cs_ref[0, ss, 1, :] = us

    cp_ref[0, 0, 0:1, :] = um2
    cp_ref[0, 0, 1:2, :] = um1


def _conv(xn_c, w_t, cw, cb_row, nrm_row, buf, za_s, c0):
    tc = CONV_TC
    n_c = D_B // tc
    assert N_MIX_BLOCKS * n_c * SAMPLE_BS == za_s.shape[0], "one sample group per grid step"
    wspec = lambda base: pl.BlockSpec(
        (pl.Element(tc), pl.Element(D_MODEL)), lambda i, c: (pl.multiple_of(base + c * tc, 8), 0))
    grp = lambda i, c: i * n_c + c
    return pl.pallas_call(
        _conv_kernel,
        grid=(N_MIX_BLOCKS, n_c),
        in_specs=[
            pl.BlockSpec((2, TM, D_MODEL), lambda i, c: (i, 0, 0)),
            wspec(CONV_ROW0),
            wspec(CONV_ROW0 + D_B),
            wspec(CONV_ROW0 + 2 * D_B),
            pl.BlockSpec((3, tc), lambda i, c: (0, c)),
            pl.BlockSpec((1, tc), lambda i, c: (0, c)),
            pl.BlockSpec((1, tc), lambda i, c: (0, c)),
            pl.BlockSpec((1, MS, 2, tc), lambda i, c: (i, 0, 0, c)),
            pl.BlockSpec((SAMPLE_BS, D_A), lambda i, c: (grp(i, c), 0)),
            pl.BlockSpec((1, SAMPLE_BS, H_A, DK, DV), lambda i, c: (0, grp(i, c), 0, 0, 0)),
        ],
        out_specs=[
            pl.BlockSpec((1, MP + MS, tc), lambda i, c: (i, 0, c)),
            pl.BlockSpec((1, 1, 2, tc), lambda i, c: (0, i, 0, c)),
            pl.BlockSpec((1, MS, 2, tc), lambda i, c: (i, 0, 0, c)),
            pl.BlockSpec((SAMPLE_BS, D_A), lambda i, c: (grp(i, c), 0)),
        ],
        out_shape=[
            jax.ShapeDtypeStruct((N_MIX_BLOCKS, MP + MS, D_B), BF16),
            jax.ShapeDtypeStruct((1, N_MIX_BLOCKS, 2, D_B), F32),
            jax.ShapeDtypeStruct((N_MIX_BLOCKS, MS, 2, D_B), F32),
            jax.ShapeDtypeStruct((za_s.shape[0], D_A), F32),
        ],
        compiler_params=pltpu.CompilerParams(
            dimension_semantics=("parallel", "arbitrary"), vmem_limit_bytes=VMEM_LIMIT),
        name="proj_conv",
    )(xn_c, w_t, w_t, w_t, cw, cb_row, nrm_row, buf, za_s, c0)


def _mlstm_prompt_kernel(z_ref, g_ref, bias_ref, nrm_ref,
                         ha_ref, c_ref, n_ref, m_ref, cx_s, m_s, *, n_chunks):
    L = MLSTM_CHUNK
    c = pl.program_id(1)

    @pl.when(c == 0)
    def _():
        cx_s[...] = jnp.zeros_like(cx_s)
        m_s[...] = jnp.zeros_like(m_s)

    row = lax.broadcasted_iota(jnp.int32, (L, L), 0)
    col = lax.broadcasted_iota(jnp.int32, (L, L), 1)
    causal = row >= col
    tri = causal.astype(BF16)

    for b in range(MLSTM_SEQS):
        _mlstm_prompt_seq(b, causal, tri, z_ref, g_ref, bias_ref, nrm_ref, ha_ref, cx_s, m_s)

    @pl.when(c == n_chunks - 1)
    def _():
        for b in range(MLSTM_SEQS):
            m_ref[b] = m_s[b]
            for h in range(H_A):
                cx = cx_s[b * H_A + h]
                c_ref[0, b, h] = cx[:, :DV]
                n_ref[0, b, h:h + 1, :] = cx[:, DV:].T[h:h + 1, :]


def _prefix_max_rows(x):
    rows = x.shape[0]
    row = lax.broadcasted_iota(jnp.int32, x.shape, 0)
    k = 1
    while k < rows:
        x = jnp.where(row >= k, jnp.maximum(x, pltpu.roll(x, k, 0)), x)
        k *= 2
    return x


def _mlstm_prompt_seq(b, causal, tri, z_ref, g_ref, bias_ref, nrm_ref, ha_ref, cx_s, m_s):
    L = MLSTM_CHUNK
    lane = lax.broadcasted_iota(jnp.int32, (L, LANE), 1)
    lane_dv = lax.broadcasted_iota(jnp.int32, (DV, LANE), 1)
    capped = _soft_cap(g_ref[b] + bias_ref[...])
    lf = _log_sigmoid(capped)
    hi = lf.astype(BF16)
    r1 = lf - hi.astype(F32)
    mid = r1.astype(BF16)
    lo = (r1 - mid.astype(F32)).astype(BF16)
    parts = _bdot(tri, jnp.concatenate([hi, mid, lo], axis=1))
    bc = parts[:, :LANE] + parts[:, LANE:2 * LANE] + parts[:, 2 * LANE:]
    bh = pltpu.roll(bc, LANE - H_A, 1)
    a = capped - bh
    a_t = a.T
    m_prev = m_s[b]
    big_m = jnp.maximum(_prefix_max_rows(a), m_prev[0:1, :])
    sc_all = jnp.exp(m_prev[0:1, :] - big_m)
    emt_all = jnp.exp(-(bh + big_m))
    m_last = big_m[L - 1:L, :]
    w_all = jnp.exp(a - m_last)
    decay_all = sc_all[L - 1:L, :]
    m_s[b] = jnp.broadcast_to(bh[L - 1:L, :] + m_last, (8, LANE))

    nums = []
    nd_tail = jnp.zeros((L, LANE), F32)
    sq_tail = jnp.zeros((L, LANE), F32)
    for h in range(H_A):
        hs = slice(h * DK, (h + 1) * DK)
        onehot = (lane == h).astype(BF16)
        qb = z_ref[b, :, hs].astype(BF16)
        kf = z_ref[b, :, slice(D_A + h * DK, D_A + (h + 1) * DK)] * (DK ** -0.5)
        kb = kf.astype(BF16)
        v_h = z_ref[b, :, slice(2 * D_A + h * DK, 2 * D_A + (h + 1) * DK)]
        vx = jnp.concatenate([v_h.astype(BF16), onehot], axis=1)
        cx_prev = cx_s[b * H_A + h]

        p = jnp.exp(jnp.where(causal, a_t[h:h + 1, :] - big_m[:, h:h + 1], -jnp.inf))
        s = _bdot_t(qb, kb) * p
        nd = sc_all[:, h:h + 1] * _bdot(qb, cx_prev.astype(BF16)) + _bdot(s.astype(BF16), vx)
        num = nd[:, :DV]
        nums.append(num)
        nd_tail = nd_tail + nd[:, DV:]
        sq_tail = sq_tail + _bdot((num * num).astype(BF16), (lane_dv == h).astype(BF16))

        wk = (w_all[:, h:h + 1] * kf).astype(BF16)
        cx_s[b * H_A + h] = decay_all[:, h:h + 1] * cx_prev + lax.dot_general(
            wk, vx, (((0,), (0,)), ((), ())), preferred_element_type=F32)

    inv = 1.0 / jnp.maximum(jnp.abs(nd_tail), emt_all)
    ms = inv * inv * sq_tail * (1.0 / DV)
    scale_all = inv * lax.rsqrt(ms + EPS)
    for h in range(H_A):
        hs = slice(h * DK, (h + 1) * DK)
        hn = nums[h] * scale_all[:, h:h + 1]
        o_h = z_ref[b, :, slice(3 * D_A + h * DK, 3 * D_A + (h + 1) * DK)]
        ha_ref[b, :, hs] = (hn * nrm_ref[:, hs] * jax.nn.sigmoid(o_h)).astype(ha_ref.dtype)


def _mlstm_prompt(za_p, g_p, bias_row, nrm_row, batch, seq):
    L = MLSTM_CHUNK
    nsq = MLSTM_SEQS
    n_chunks = seq // L
    kern = functools.partial(_mlstm_prompt_kernel, n_chunks=n_chunks)
    za3 = za_p.reshape(batch, seq, 4 * D_A)
    g3 = g_p.reshape(batch, seq, LANE)
    return pl.pallas_call(
        kern,
        grid=(batch // nsq, n_chunks),
        in_specs=[
            pl.BlockSpec((nsq, L, 4 * D_A), lambda b, c: (b, c, 0)),
            pl.BlockSpec((nsq, L, LANE), lambda b, c: (b, c, 0)),
            pl.BlockSpec((1, LANE), lambda b, c: (0, 0)),
            pl.BlockSpec((1, D_A), lambda b, c: (0, 0)),
        ],
        out_specs=[
            pl.BlockSpec((nsq, L, D_A), lambda b, c: (b, c, 0)),
            pl.BlockSpec((1, nsq, H_A, DK, DV), lambda b, c: (0, b, 0, 0, 0)),
            pl.BlockSpec((1, nsq, H_A, DK), lambda b, c: (0, b, 0, 0)),
            pl.BlockSpec((nsq, 8, LANE), lambda b, c: (b, 0, 0)),
        ],
        out_shape=[
            jax.ShapeDtypeStruct((batch, seq, D_A), BF16),
            jax.ShapeDtypeStruct((1, batch, H_A, DK, DV), F32),
            jax.ShapeDtypeStruct((1, batch, H_A, DK), F32),
            jax.ShapeDtypeStruct((batch, 8, LANE), F32),
        ],
        scratch_shapes=[pltpu.VMEM((nsq * H_A, DK, DV + LANE), F32),
                        pltpu.VMEM((nsq, 8, LANE), F32)],
        compiler_params=pltpu.CompilerParams(
            dimension_semantics=("parallel", "arbitrary"), vmem_limit_bytes=VMEM_LIMIT),
        name="mlstm_prompt",
    )(za3, g3, bias_row, nrm_row)


def _expand_heads(x, width):
    rows = x.shape[0]
    return jnp.concatenate([jnp.broadcast_to(x[:, h:h + 1], (rows, width)) for h in range(H_A)], axis=1)


def _head_sums(x):
    return jnp.concatenate(
        [jnp.sum(x[:, h * DK:(h + 1) * DK], axis=-1, keepdims=True) for h in range(H_A)], axis=1)


def _mlstm_sample_kernel(za_ref, qc_ref, g_ref, bias_ref, nrm_ref, m_ref, n_ref,
                         ha_ref, m_out, n_out, wk_out, dec_out):
    nb = za_ref.shape[0]
    capped = _soft_cap(g_ref[...] + bias_ref[...])
    logi = capped[:, 0:H_A]
    logf = _log_sigmoid(capped)[:, H_A:2 * H_A]
    m_prev = m_ref[...]
    m_inter = logf + m_prev
    m_t = jnp.maximum(m_inter, logi)
    sc = jnp.exp(m_inter - m_t)
    ei = jnp.exp(logi - m_t)
    emt = jnp.exp(-m_t)

    q = za_ref[:, 0:D_A]
    k = za_ref[:, D_A:2 * D_A] * (DK ** -0.5)
    v = za_ref[:, 2 * D_A:3 * D_A]
    o = za_ref[:, 3 * D_A:4 * D_A]
    n_prev = n_ref[...]

    s = _head_sums(q * k) * ei
    den = sc * _head_sums(q * n_prev) + s
    denom = jnp.maximum(jnp.abs(den), emt)
    num = _expand_heads(sc, DV) * qc_ref[...] + _expand_heads(s, DV) * v
    hh = num / _expand_heads(denom, DV)
    ms = _head_sums(hh * hh) * (1.0 / DV)
    hn = hh * lax.rsqrt(_expand_heads(ms, DV) + EPS)
    ha_ref[...] = (hn * nrm_ref[...] * jax.nn.sigmoid(o)).astype(ha_ref.dtype)
    n_out[...] = _expand_heads(sc, DK) * n_prev + _expand_heads(ei, DK) * k
    m_out[...] = m_t
    pad = jnp.zeros((nb, LANE - H_A), F32)
    dec_out[...] = jnp.concatenate([sc, pad], axis=1)

    k_t = k.T
    ei_t = jnp.concatenate([ei, pad], axis=1).T
    for h in range(H_A):
        hs = slice(h * DK, (h + 1) * DK)
        wk_out[hs, :] = k_t[hs, :] * ei_t[h:h + 1, :]


def _mlstm_sample(za_s, qc, g_s, bias_row, nrm_row, m0, n0):
    nb = za_s.shape[0]
    assert nb == LANE, "the sequence axis is transposed onto the lanes"
    return pl.pallas_call(
        _mlstm_sample_kernel,
        out_shape=[
            jax.ShapeDtypeStruct((nb, D_A), BF16),
            jax.ShapeDtypeStruct((nb, H_A), F32),
            jax.ShapeDtypeStruct((nb, H_A * DK), F32),
            jax.ShapeDtypeStruct((D_A, nb), F32),
            jax.ShapeDtypeStruct((nb, LANE), F32),
        ],
        compiler_params=pltpu.CompilerParams(vmem_limit_bytes=VMEM_LIMIT),
        name="mlstm_sample",
    )(za_s, qc, g_s, bias_row, nrm_row, m0, n0)


def _outproj_kernel(hp_ref, hs_ref, ap_ref, as_ref, yb_ref, w_ref, op_ref, os_ref, wc_ref):
    j = pl.program_id(1)

    @pl.when(pl.program_id(0) == 0)
    def _():
        wc_ref[j] = w_ref[...].astype(BF16)

    wa = wc_ref[j, :D_A, :]
    wb = wc_ref[j, D_A:, :]
    op_ref[...] = hp_ref[...] + _bdot(ap_ref[...], wa) + _bdot(yb_ref[0, :MP, :], wb)
    os_ref[0] = hs_ref[0] + _bdot(as_ref[0], wa) + _bdot(yb_ref[0, MP:, :], wb)


def _outproj(hp, hs, ap, a_s, yb_c, w3):
    tn = OUT_TN
    n_n = D_MODEL // tn
    return pl.pallas_call(
        _outproj_kernel,
        grid=(N_MIX_BLOCKS, n_n),
        in_specs=[
            pl.BlockSpec((MP, tn), lambda i, j: (i, j)),
            pl.BlockSpec((1, MS, tn), lambda i, j: (i, 0, j)),
            pl.BlockSpec((MP, D_A), lambda i, j: (i, 0)),
            pl.BlockSpec((1, MS, D_A), lambda i, j: (i, 0, 0)),
            pl.BlockSpec((1, MP + MS, D_B), lambda i, j: (i, 0, 0)),
            pl.BlockSpec((None, D_A + D_B, tn), lambda i, j: (0, 0, jnp.where(i == 0, j, n_n - 1))),
        ],
        out_specs=[
            pl.BlockSpec((MP, tn), lambda i, j: (i, j)),
            pl.BlockSpec((1, MS, tn), lambda i, j: (i, 0, j)),
        ],
        out_shape=[
            jax.ShapeDtypeStruct((N_MIX_BLOCKS * MP, D_MODEL), F32),
            jax.ShapeDtypeStruct((N_MIX_BLOCKS, MS, D_MODEL), F32),
        ],
        scratch_shapes=[pltpu.VMEM((n_n, D_A + D_B, tn), BF16)],
        compiler_params=pltpu.CompilerParams(
            dimension_semantics=("arbitrary", "arbitrary"), vmem_limit_bytes=VMEM_LIMIT),
        name="outproj",
    )(hp, hs, ap, a_s, yb_c, w3)


def kernel(x_prompt, x_sample, state_mlstm_C, state_mlstm_n, state_mlstm_m, state_conv, norm_ffn1, ffn1_gate, ffn1_up, ffn1_down, norm_mix, w_in, b_gates, conv_w, conv_b, norm_mlstm, norm_conv, w_out, norm_ffn2, ffn2_gate, ffn2_up, ffn2_down, norm_final):
    batch, seq, _ = x_prompt.shape
    nb = x_sample.shape[0]
    assert batch == N_MIX_BLOCKS and seq == MP and nb == N_MIX_BLOCKS * MS
    assert norm_ffn1.shape[0] == 1, "single-layer trunk"

    xp = x_prompt.reshape(batch * seq, D_MODEL)

    w_t = jnp.swapaxes(w_in[0], 0, 1)

    h1p, h1s, xn_c, g_p, g_s = _ffn(xp, x_sample, norm_ffn1, ffn1_gate, ffn1_up, ffn1_down, norm_mix, TF1,
                                    w_gate_t=w_t)
    za_p, za_s = _proj(xn_c, w_t)
    za_s = za_s.reshape(nb, 4 * D_A)
    yb_c, conv_p, conv_s, qc_s = _conv(xn_c, w_t, conv_w[0], conv_b, norm_conv,
                                       state_conv.reshape(N_MIX_BLOCKS, MS, 2, D_B), za_s, state_mlstm_C)

    bias_row = jnp.zeros((1, LANE), F32).at[0, :2 * H_A].set(b_gates[0].astype(F32))
    ha_p, c_p, n_p, m_p = _mlstm_prompt(za_p, g_p, bias_row, norm_mlstm, batch, seq)
    ha_s, m_s, n_s, wk_s, dec_s = _mlstm_sample(
        za_s, qc_s, g_s.reshape(nb, LANE), bias_row, norm_mlstm,
        state_mlstm_m.reshape(nb, H_A), state_mlstm_n.reshape(nb, H_A * DK))

    h2p, h2s = _outproj(h1p, h1s.reshape(N_MIX_BLOCKS, MS, D_MODEL), ha_p.reshape(batch * seq, D_A),
                        ha_s.reshape(N_MIX_BLOCKS, MS, D_A), yb_c, w_out)

    yp, ys, c_s = _ffn(h2p, h2s.reshape(N_FFN_BLOCKS, TM_S, D_MODEL), norm_ffn2, ffn2_gate, ffn2_up,
                       ffn2_down, norm_final.reshape(1, D_MODEL), TF2,
                       state=(state_mlstm_C, wk_s, za_s, dec_s))

    return (
        yp.reshape(batch, seq, D_MODEL),
        ys,
        c_p,
        n_p,
        m_p[:, 0, :H_A].reshape(1, batch, H_A),
        conv_p,
        c_s,
        n_s.reshape(1, nb, H_A, DK),
        m_s.reshape(1, nb, H_A),
        conv_s.reshape(1, nb, 2, D_B),
    )
```

```python
import functools

import jax
import jax.numpy as jnp
from jax import lax
from jax.experimental import pallas as pl
from jax.experimental.pallas import tpu as pltpu

F32 = jnp.float32
BF16 = jnp.bfloat16

D_MODEL = 2048
D_A = 1024
D_B = 1024
H_A = 4
DK = 256
DV = 256
G_B = 8
D_FF = 5504
GATE_CAP = 15.0
EPS = 1e-6
GATE_ROW0 = 4 * D_A
CONV_ROW0 = 4 * D_A + 2 * H_A

LANE = 128
N_FFN_BLOCKS = 8
TM_P = 1024
TM_S = 16
TM = TM_P + TM_S
TF1 = 512
TF2 = 512
STATE_SEQS = 2
X_CHUNK = 128
N_X_CHUNKS = TM_P // X_CHUNK
X_AHEAD0 = 2
N_MIX_BLOCKS = N_FFN_BLOCKS // 2
MP = 2 * TM_P
MS = 2 * TM_S
PROJ_TN = 1024
CONV_TC = 256
OUT_TN = 512
MLSTM_CHUNK = 256
MLSTM_SEQS = 2
SAMPLE_BS = 8
VMEM_LIMIT = 62 * 1024 * 1024


def _rms(x, g):
    return x * lax.rsqrt(jnp.mean(x * x, axis=-1, keepdims=True) + EPS) * g


def _soft_cap(x):
    return GATE_CAP * jnp.tanh(x / GATE_CAP)


def _log_sigmoid(x):
    return -jax.nn.softplus(-x)


def _bdot(a, b):
    return jnp.dot(a, b, preferred_element_type=F32)


def _bdot_t(a, b):
    return lax.dot_general(a, b, (((1,), (1,)), ((), ())), preferred_element_type=F32)


def _sample_state_update(f, c_ref, wk_ref, v_ref, dec_ref, c_out):
    grp = jnp.minimum(f, TM_S // STATE_SEQS - 1)
    lane = lax.broadcasted_iota(jnp.int32, (D_A, LANE), 1)
    for u in range(STATE_SEQS):
        r = grp * STATE_SEQS + u
        seq = pl.program_id(0) * TM_S + r
        wk_col = jnp.sum(jnp.where(lane == seq, wk_ref[...], 0.0), axis=1, keepdims=True)
        v_row = v_ref[pl.ds(r, 1), :]
        dec = dec_ref[pl.ds(r, 1), :]
        for h in range(H_A):
            hs = slice(h * DK, (h + 1) * DK)
            c_out[0, u, h] = dec[:, h:h + 1] * c_ref[0, u, h] + wk_col[hs, :] * v_row[:, hs]


def _ffn_kernel(x_hbm, xs_ref, g_ref, wg_ref, wu_ref, wd_ref, g2_ref, *rest,
                tf, n_f, last_valid, final_norm):
    if final_norm:
        c_ref, wk_ref, v_ref, dec_ref, out_hbm, os_ref, c_out, xn_ref, acc_ref, x_sem, o_sem = rest
    else:
        wgate_ref, out_hbm, os_ref, nxt_ref, gp_ref, gs_ref, xn_ref, acc_ref, x_sem, o_sem = rest
    i = pl.program_id(0)
    f = pl.program_id(1)
    slot = lax.rem(i, 2)
    op_ref = acc_ref.at[slot]

    def x_copy(blk, s, c):
        dst0 = pl.multiple_of(c * X_CHUNK, X_CHUNK)
        src0 = pl.multiple_of(blk * TM_P + c * X_CHUNK, X_CHUNK)
        return pltpu.make_async_copy(x_hbm.at[pl.ds(src0, X_CHUNK), :],
                                     acc_ref.at[s, pl.ds(dst0, X_CHUNK), :], x_sem.at[c])

    def out_copy(blk, s):
        row0 = pl.multiple_of(blk * TM_P, TM_P)
        return pltpu.make_async_copy(acc_ref.at[s], out_hbm.at[pl.ds(row0, TM_P), :], o_sem.at[s])

    def first_step_dma():
        @pl.when(i == 0)
        def _():
            for c in range(N_X_CHUNKS):
                x_copy(0, 0, c).start()

        for c in range(N_X_CHUNKS):
            x_copy(i, slot, c).wait()

    def middle_step_dma():
        @pl.when(jnp.logical_and(i > 0, f == X_AHEAD0 - 1))
        def _():
            out_copy(i - 1, 1 - slot).wait()

        @pl.when(jnp.logical_and(jnp.logical_and(f >= X_AHEAD0, f < X_AHEAD0 + N_X_CHUNKS),
                                 i < N_FFN_BLOCKS - 1))
        def _():
            x_copy(i + 1, 1 - slot, f - X_AHEAD0).start()

    xs_idx = (0,) if final_norm else (slice(None), 0)
    os_idx = (slice(None), 0) if final_norm else (0,)

    def step(valid, first, with_state):
        if first:
            g = g_ref[...]
            xn_ref[:TM_P, :] = _rms(op_ref[...], g).astype(BF16)
            xn_ref[TM_P:, :] = _rms(xs_ref[xs_idx], g).astype(BF16)
        xn = xn_ref[...]
        a = _bdot(xn, wg_ref[:, :valid].astype(BF16))
        b = _bdot(xn, wu_ref[:, :valid].astype(BF16))
        hid = (a * jax.nn.sigmoid(a) * b * 0.5).astype(BF16)
        r = _bdot(hid, wd_ref[:valid, :].astype(BF16))
        op_ref[...] += r[:TM_P]
        if first:
            os_ref[os_idx] = xs_ref[xs_idx] + r[TM_P:]
        else:
            os_ref[os_idx] += r[TM_P:]
        if with_state:
            _sample_state_update(f, c_ref, wk_ref, v_ref, dec_ref, c_out)

    assert last_valid < tf and (not final_norm or (n_f - 1) * STATE_SEQS >= TM_S), \
        "state updates ride the full-width steps"

    assert X_AHEAD0 >= 2 and X_AHEAD0 + N_X_CHUNKS <= n_f - 1, "hand DMAs are issued from the middle steps"

    @pl.when(f == 0)
    def _():
        first_step_dma()
        step(tf, True, final_norm)

    @pl.when(jnp.logical_and(f > 0, f < n_f - 1))
    def _():
        middle_step_dma()
        step(tf, False, final_norm)

    @pl.when(f == n_f - 1)
    def _():
        step(last_valid, False, False)
        g2 = g2_ref[...]
        if final_norm:
            op_ref[...] = _rms(op_ref[...], g2)
            os_ref[os_idx] = _rms(os_ref[os_idx], g2)
        else:
            nx_p = _rms(op_ref[...], g2).astype(BF16)
            nx_s = _rms(os_ref[os_idx], g2).astype(BF16)
            nxt_ref[0, :TM_P, :] = nx_p
            nxt_ref[0, TM_P:, :] = nx_s
            wgt = wgate_ref[...].astype(BF16)
            half = TM_P // 2
            gp_ref[:half, :] = _bdot_t(nx_p[:half], wgt)
            gp_ref[half:, :] = _bdot_t(nx_p[half:], wgt)
            gs_ref[0] = _bdot_t(nx_s, wgt)

        out_copy(i, slot).start()

        @pl.when(i == N_FFN_BLOCKS - 1)
        def _():
            out_copy(i, slot).wait()


def _ffn(xp, xs, g, wg, wu, wd, g2, tf, state=None, w_gate_t=None):
    final_norm = state is not None
    n_f = pl.cdiv(D_FF, tf)
    last_valid = D_FF - (n_f - 1) * tf
    kern = functools.partial(_ffn_kernel, tf=tf, n_f=n_f, last_valid=last_valid, final_norm=final_norm)
    assert X_AHEAD0 + N_X_CHUNKS <= n_f and N_X_CHUNKS * X_CHUNK == TM_P
    blocked_s = pl.BlockSpec((1, TM_S, D_MODEL), lambda i, f: (i, 0, 0))
    native_s = pl.BlockSpec((TM_S, 1, D_MODEL), lambda i, f: (i, 0, 0))
    in_specs = [
        pl.BlockSpec(memory_space=pl.ANY),
        blocked_s if final_norm else native_s,
        pl.BlockSpec((1, D_MODEL), lambda i, f: (0, 0)),
        pl.BlockSpec((None, D_MODEL, tf), lambda i, f: (0, 0, f)),
        pl.BlockSpec((None, D_MODEL, tf), lambda i, f: (0, 0, f)),
        pl.BlockSpec((None, tf, D_MODEL), lambda i, f: (0, f, 0)),
        pl.BlockSpec((1, D_MODEL), lambda i, f: (0, 0)),
    ]
    out_specs = [
        pl.BlockSpec(memory_space=pl.ANY),
        native_s if final_norm else blocked_s,
    ]
    out_shape = [
        jax.ShapeDtypeStruct((N_FFN_BLOCKS * TM_P, D_MODEL), F32),
        jax.ShapeDtypeStruct((N_FFN_BLOCKS * TM_S, 1, D_MODEL) if final_norm
                             else (N_FFN_BLOCKS, TM_S, D_MODEL), F32),
    ]
    args = [xp, xs, g, wg, wu, wd, g2]
    if final_norm:
        c0, wk, za_s, dec = state
        groups = TM_S // STATE_SEQS
        grp = lambda i, f: i * groups + jnp.minimum(f, groups - 1)
        cspec = pl.BlockSpec((1, STATE_SEQS, H_A, DK, DV), lambda i, f: (0, grp(i, f), 0, 0, 0))
        in_specs += [
            cspec,
            pl.BlockSpec((D_A, LANE), lambda i, f: (0, 0)),
            pl.BlockSpec((TM_S, D_A), lambda i, f: (i, 2)),
            pl.BlockSpec((TM_S, LANE), lambda i, f: (i, 0)),
        ]
        out_specs.append(cspec)
        out_shape.append(jax.ShapeDtypeStruct(c0.shape, F32))
        args += [c0, wk, za_s, dec]
    else:
        in_specs.append(pl.BlockSpec((LANE, D_MODEL), lambda i, f: (GATE_ROW0 // LANE, 0)))
        args.append(w_gate_t)
        out_specs += [
            pl.BlockSpec((1, TM, D_MODEL), lambda i, f: (i, 0, 0)),
            pl.BlockSpec((TM_P, LANE), lambda i, f: (i, 0)),
            pl.BlockSpec((1, TM_S, LANE), lambda i, f: (i, 0, 0)),
        ]
        out_shape += [
            jax.ShapeDtypeStruct((N_FFN_BLOCKS, TM, D_MODEL), BF16),
            jax.ShapeDtypeStruct((N_FFN_BLOCKS * TM_P, LANE), F32),
            jax.ShapeDtypeStruct((N_FFN_BLOCKS, TM_S, LANE), F32),
        ]
    return pl.pallas_call(
        kern,
        grid=(N_FFN_BLOCKS, n_f),
        in_specs=in_specs,
        out_specs=out_specs,
        out_shape=out_shape,
        scratch_shapes=[pltpu.VMEM((TM, D_MODEL), BF16),
                        pltpu.VMEM((2, TM_P, D_MODEL), F32),
                        pltpu.SemaphoreType.DMA((N_X_CHUNKS,)),
                        pltpu.SemaphoreType.DMA((2,))],
        compiler_params=pltpu.CompilerParams(
            dimension_semantics=("arbitrary", "arbitrary"), vmem_limit_bytes=VMEM_LIMIT),
        name="ffn_final" if final_norm else "ffn",
    )(*args)


def _split_rows(z0, z1, p_ref, s_ref):
    p_ref[:TM_P, :] = z0[:TM_P]
    p_ref[TM_P:, :] = z1[:TM_P]
    s_ref[0, :TM_S, :] = z0[TM_P:]
    s_ref[0, TM_S:, :] = z1[TM_P:]


def _proj_kernel(xn_ref, w_ref, zp_ref, zs_ref):
    w = w_ref[...].astype(BF16)
    _split_rows(_bdot_t(xn_ref[0], w), _bdot_t(xn_ref[1], w), zp_ref, zs_ref)


def _proj(xn_c, w_t):
    n_n = (4 * D_A) // PROJ_TN
    return pl.pallas_call(
        _proj_kernel,
        grid=(N_MIX_BLOCKS, n_n),
        in_specs=[
            pl.BlockSpec((2, TM, D_MODEL), lambda i, j: (i, 0, 0)),
            pl.BlockSpec((PROJ_TN, D_MODEL), lambda i, j: (j, 0)),
        ],
        out_specs=[
            pl.BlockSpec((MP, PROJ_TN), lambda i, j: (i, j)),
            pl.BlockSpec((1, MS, PROJ_TN), lambda i, j: (i, 0, j)),
        ],
        out_shape=[
            jax.ShapeDtypeStruct((N_MIX_BLOCKS * MP, 4 * D_A), F32),
            jax.ShapeDtypeStruct((N_MIX_BLOCKS, MS, 4 * D_A), F32),
        ],
        compiler_params=pltpu.CompilerParams(
            dimension_semantics=("parallel", "arbitrary"), vmem_limit_bytes=VMEM_LIMIT),
        name="proj_qkvo",
    )(xn_c, w_t)


def _group_norm(yb, nrm):
    gw = D_B // G_B
    parts = []
    for g in range(yb.shape[1] // gw):
        seg = yb[:, g * gw:(g + 1) * gw]
        parts.append(seg * lax.rsqrt(jnp.mean(seg * seg, axis=-1, keepdims=True) + EPS))
    return jnp.concatenate(parts, axis=1) * nrm


def _conv_kernel(xn_ref, wgb_ref, wgc_ref, wxc_ref, cw_ref, cb_ref, nrm_ref, buf_ref, q_ref, c_ref,
                 yb_ref, cp_ref, cs_ref, qc_ref):
    q_t = jnp.concatenate([q_ref[...], jnp.zeros((LANE - SAMPLE_BS, D_A), F32)], axis=0).T
    for h in range(H_A):
        hs = slice(h * DK, (h + 1) * DK)
        for j in range(SAMPLE_BS):
            qc_ref[j:j + 1, hs] = jnp.sum(q_t[hs, j:j + 1] * c_ref[0, j, h], axis=0, keepdims=True)

    wgb = wgb_ref[...].astype(BF16)
    wgc = wgc_ref[...].astype(BF16)
    wxc = wxc_ref[...].astype(BF16)
    cw0 = cw_ref[0:1, :]
    cw1 = cw_ref[1:2, :]
    cw2 = cw_ref[2:3, :]
    cb = cb_ref[...]
    nrm = nrm_ref[...]
    row = lax.broadcasted_iota(jnp.int32, (TM_P, wgb.shape[0]), 0)
    zero_row = jnp.zeros((1, wgb.shape[0]), F32)
    um2, um1 = zero_row, zero_row

    for t in range(2):
        xn = xn_ref[t]
        u = _bdot_t(xn, wgc) * _bdot_t(xn, wxc)
        gb = _bdot_t(xn, wgb)

        up = u[:TM_P]
        u1 = jnp.where(row < 1, um1, pltpu.roll(up, 1, 0))
        u2 = jnp.where(row < 1, um2, jnp.where(row < 2, um1, pltpu.roll(up, 2, 0)))
        yc = cw0 * u2 + cw1 * u1 + cw2 * up + cb
        yb_ref[0, t * TM_P:(t + 1) * TM_P, :] = _group_norm(gb[:TM_P] * yc, nrm).astype(yb_ref.dtype)
        um2, um1 = up[TM_P - 2:TM_P - 1, :], up[TM_P - 1:TM_P, :]

        ss = slice(t * TM_S, (t + 1) * TM_S)
        us = u[TM_P:]
        b0 = buf_ref[0, ss, 0, :]
        b1 = buf_ref[0, ss, 1, :]
        ycs = cw0 * b0 + cw1 * b1 + cw2 * us + cb
        yb_ref[0, MP + t * TM_S:MP + (t + 1) * TM_S, :] = _group_norm(gb[TM_P:] * ycs, nrm).astype(yb_ref.dtype)
        cs_ref[0, ss, 0, :] = b1
        cs_ref[0, ss, 1, :] = us

    cp_ref[0, 0, 0:1, :] = um2
    cp_ref[0, 0, 1:2, :] = um1


def _conv(xn_c, w_t, cw, cb_row, nrm_row, buf, za_s, c0):
    tc = CONV_TC
    n_c = D_B // tc
    assert N_MIX_BLOCKS * n_c * SAMPLE_BS == za_s.shape[0], "one sample group per grid step"
    wspec = lambda base: pl.BlockSpec(
        (pl.Element(tc), pl.Element(D_MODEL)), lambda i, c: (pl.multiple_of(base + c * tc, 8), 0))
    grp = lambda i, c: i * n_c + c
    return pl.pallas_call(
        _conv_kernel,
        grid=(N_MIX_BLOCKS, n_c),
        in_specs=[
            pl.BlockSpec((2, TM, D_MODEL), lambda i, c: (i, 0, 0)),
            wspec(CONV_ROW0),
            wspec(CONV_ROW0 + D_B),
            wspec(CONV_ROW0 + 2 * D_B),
            pl.BlockSpec((None, 3, tc), lambda i, c: (0, 0, c)),
            pl.BlockSpec((1, tc), lambda i, c: (0, c)),
            pl.BlockSpec((1, tc), lambda i, c: (0, c)),
            pl.BlockSpec((1, MS, 2, tc), lambda i, c: (i, 0, 0, c)),
            pl.BlockSpec((SAMPLE_BS, D_A), lambda i, c: (grp(i, c), 0)),
            pl.BlockSpec((1, SAMPLE_BS, H_A, DK, DV), lambda i, c: (0, grp(i, c), 0, 0, 0)),
        ],
        out_specs=[
            pl.BlockSpec((1, MP + MS, tc), lambda i, c: (i, 0, c)),
            pl.BlockSpec((1, 1, 2, tc), lambda i, c: (0, i, 0, c)),
            pl.BlockSpec((1, MS, 2, tc), lambda i, c: (i, 0, 0, c)),
            pl.BlockSpec((SAMPLE_BS, D_A), lambda i, c: (grp(i, c), 0)),
        ],
        out_shape=[
            jax.ShapeDtypeStruct((N_MIX_BLOCKS, MP + MS, D_B), BF16),
            jax.ShapeDtypeStruct((1, N_MIX_BLOCKS, 2, D_B), F32),
            jax.ShapeDtypeStruct((N_MIX_BLOCKS, MS, 2, D_B), F32),
            jax.ShapeDtypeStruct((za_s.shape[0], D_A), F32),
        ],
        compiler_params=pltpu.CompilerParams(
            dimension_semantics=("parallel", "arbitrary"), vmem_limit_bytes=VMEM_LIMIT),
        name="proj_conv",
    )(xn_c, w_t, w_t, w_t, cw, cb_row, nrm_row, buf, za_s, c0)


def _mlstm_prompt_kernel(z_ref, g_ref, bias_ref, nrm_ref,
                         ha_ref, c_ref, n_ref, m_ref, cx_s, m_s, *, n_chunks):
    L = MLSTM_CHUNK
    c = pl.program_id(1)

    @pl.when(c == 0)
    def _():
        cx_s[...] = jnp.zeros_like(cx_s)
        m_s[...] = jnp.zeros_like(m_s)

    row = lax.broadcasted_iota(jnp.int32, (L, L), 0)
    col = lax.broadcasted_iota(jnp.int32, (L, L), 1)
    causal = row >= col
    tri = causal.astype(BF16)

    for b in range(MLSTM_SEQS):
        _mlstm_prompt_seq(b, causal, tri, z_ref, g_ref, bias_ref, nrm_ref, ha_ref, cx_s, m_s)

    @pl.when(c == n_chunks - 1)
    def _():
        for b in range(MLSTM_SEQS):
            m_ref[b] = m_s[b]
            for h in range(H_A):
                cx = cx_s[b * H_A + h]
                c_ref[0, b, h] = cx[:, :DV]
                n_ref[0, b, h:h + 1, :] = cx[:, DV:].T[h:h + 1, :]


def _prefix_max_rows(x):
    rows = x.shape[0]
    row = lax.broadcasted_iota(jnp.int32, x.shape, 0)
    k = 1
    while k < rows:
        x = jnp.where(row >= k, jnp.maximum(x, pltpu.roll(x, k, 0)), x)
        k *= 2
    return x


def _mlstm_prompt_seq(b, causal, tri, z_ref, g_ref, bias_ref, nrm_ref, ha_ref, cx_s, m_s):
    L = MLSTM_CHUNK
    lane = lax.broadcasted_iota(jnp.int32, (L, LANE), 1)
    lane_dv = lax.broadcasted_iota(jnp.int32, (DV, LANE), 1)
    capped = _soft_cap(g_ref[b] + bias_ref[...])
    lf = _log_sigmoid(capped)
    hi = lf.astype(BF16)
    r1 = lf - hi.astype(F32)
    mid = r1.astype(BF16)
    lo = (r1 - mid.astype(F32)).astype(BF16)
    parts = _bdot(tri, jnp.concatenate([hi, mid, lo], axis=1))
    bc = parts[:, :LANE] + parts[:, LANE:2 * LANE] + parts[:, 2 * LANE:]
    bh = pltpu.roll(bc, LANE - H_A, 1)
    a = capped - bh
    a_t = a.T
    m_prev = m_s[b]
    big_m = jnp.maximum(_prefix_max_rows(a), m_prev[0:1, :])
    sc_all = jnp.exp(m_prev[0:1, :] - big_m)
    emt_all = jnp.exp(-(bh + big_m))
    m_last = big_m[L - 1:L, :]
    w_all = jnp.exp(a - m_last)
    decay_all = sc_all[L - 1:L, :]
    m_s[b] = jnp.broadcast_to(bh[L - 1:L, :] + m_last, (8, LANE))

    nums = []
    nd_tail = jnp.zeros((L, LANE), F32)
    sq_tail = jnp.zeros((L, LANE), F32)
    for h in range(H_A):
        hs = slice(h * DK, (h + 1) * DK)
        onehot = (lane == h).astype(BF16)
        qb = z_ref[b, :, hs].astype(BF16)
        kf = z_ref[b, :, slice(D_A + h * DK, D_A + (h + 1) * DK)] * (DK ** -0.5)
        kb = kf.astype(BF16)
        v_h = z_ref[b, :, slice(2 * D_A + h * DK, 2 * D_A + (h + 1) * DK)]
        vx = jnp.concatenate([v_h.astype(BF16), onehot], axis=1)
        cx_prev = cx_s[b * H_A + h]

        p = jnp.exp(jnp.where(causal, a_t[h:h + 1, :] - big_m[:, h:h + 1], -jnp.inf))
        s = _bdot_t(qb, kb) * p
        nd = sc_all[:, h:h + 1] * _bdot(qb, cx_prev.astype(BF16)) + _bdot(s.astype(BF16), vx)
        num = nd[:, :DV]
        nums.append(num)
        nd_tail = nd_tail + nd[:, DV:]
        sq_tail = sq_tail + _bdot((num * num).astype(BF16), (lane_dv == h).astype(BF16))

        wk = (w_all[:, h:h + 1] * kf).astype(BF16)
        cx_s[b * H_A + h] = decay_all[:, h:h + 1] * cx_prev + lax.dot_general(
            wk, vx, (((0,), (0,)), ((), ())), preferred_element_type=F32)

    inv = 1.0 / jnp.maximum(jnp.abs(nd_tail), emt_all)
    ms = inv * inv * sq_tail * (1.0 / DV)
    scale_all = inv * lax.rsqrt(ms + EPS)
    for h in range(H_A):
        hs = slice(h * DK, (h + 1) * DK)
        hn = nums[h] * scale_all[:, h:h + 1]
        o_h = z_ref[b, :, slice(3 * D_A + h * DK, 3 * D_A + (h + 1) * DK)]
        ha_ref[b, :, hs] = (hn * nrm_ref[:, hs] * jax.nn.sigmoid(o_h)).astype(ha_ref.dtype)


def _mlstm_prompt(za_p, g_p, bias_row, nrm_row, batch, seq):
    L = MLSTM_CHUNK
    nsq = MLSTM_SEQS
    n_chunks = seq // L
    kern = functools.partial(_mlstm_prompt_kernel, n_chunks=n_chunks)
    za3 = za_p.reshape(batch, seq, 4 * D_A)
    g3 = g_p.reshape(batch, seq, LANE)
    return pl.pallas_call(
        kern,
        grid=(batch // nsq, n_chunks),
        in_specs=[
            pl.BlockSpec((nsq, L, 4 * D_A), lambda b, c: (b, c, 0)),
            pl.BlockSpec((nsq, L, LANE), lambda b, c: (b, c, 0)),
            pl.BlockSpec((1, LANE), lambda b, c: (0, 0)),
            pl.BlockSpec((1, D_A), lambda b, c: (0, 0)),
        ],
        out_specs=[
            pl.BlockSpec((nsq, L, D_A), lambda b, c: (b, c, 0)),
            pl.BlockSpec((1, nsq, H_A, DK, DV), lambda b, c: (0, b, 0, 0, 0)),
            pl.BlockSpec((1, nsq, H_A, DK), lambda b, c: (0, b, 0, 0)),
            pl.BlockSpec((nsq, 8, LANE), lambda b, c: (b, 0, 0)),
        ],
        out_shape=[
            jax.ShapeDtypeStruct((batch, seq, D_A), BF16),
            jax.ShapeDtypeStruct((1, batch, H_A, DK, DV), F32),
            jax.ShapeDtypeStruct((1, batch, H_A, DK), F32),
            jax.ShapeDtypeStruct((batch, 8, LANE), F32),
        ],
        scratch_shapes=[pltpu.VMEM((nsq * H_A, DK, DV + LANE), F32),
                        pltpu.VMEM((nsq, 8, LANE), F32)],
        compiler_params=pltpu.CompilerParams(
            dimension_semantics=("parallel", "arbitrary"), vmem_limit_bytes=VMEM_LIMIT),
        name="mlstm_prompt",
    )(za3, g3, bias_row, nrm_row)


def _expand_heads(x, width):
    rows = x.shape[0]
    return jnp.concatenate([jnp.broadcast_to(x[:, h:h + 1], (rows, width)) for h in range(H_A)], axis=1)


def _head_sums(x):
    return jnp.concatenate(
        [jnp.sum(x[:, h * DK:(h + 1) * DK], axis=-1, keepdims=True) for h in range(H_A)], axis=1)


def _mlstm_sample_kernel(za_ref, qc_ref, g_ref, bias_ref, nrm_ref, m_ref, n_ref,
                         ha_ref, m_out, n_out, wk_out, dec_out):
    nb = za_ref.shape[0]
    capped = _soft_cap(g_ref[...] + bias_ref[...])
    logi = capped[:, 0:H_A]
    logf = _log_sigmoid(capped)[:, H_A:2 * H_A]
    m_prev = m_ref[...]
    m_inter = logf + m_prev
    m_t = jnp.maximum(m_inter, logi)
    sc = jnp.exp(m_inter - m_t)
    ei = jnp.exp(logi - m_t)
    emt = jnp.exp(-m_t)

    q = za_ref[:, 0:D_A]
    k = za_ref[:, D_A:2 * D_A] * (DK ** -0.5)
    v = za_ref[:, 2 * D_A:3 * D_A]
    o = za_ref[:, 3 * D_A:4 * D_A]
    n_prev = n_ref[...]

    s = _head_sums(q * k) * ei
    den = sc * _head_sums(q * n_prev) + s
    denom = jnp.maximum(jnp.abs(den), emt)
    num = _expand_heads(sc, DV) * qc_ref[...] + _expand_heads(s, DV) * v
    hh = num / _expand_heads(denom, DV)
    ms = _head_sums(hh * hh) * (1.0 / DV)
    hn = hh * lax.rsqrt(_expand_heads(ms, DV) + EPS)
    ha_ref[...] = (hn * nrm_ref[...] * jax.nn.sigmoid(o)).astype(ha_ref.dtype)
    n_out[...] = _expand_heads(sc, DK) * n_prev + _expand_heads(ei, DK) * k
    m_out[...] = m_t
    pad = jnp.zeros((nb, LANE - H_A), F32)
    dec_out[...] = jnp.concatenate([sc, pad], axis=1)

    k_t = k.T
    ei_t = jnp.concatenate([ei, pad], axis=1).T
    for h in range(H_A):
        hs = slice(h * DK, (h + 1) * DK)
        wk_out[hs, :] = k_t[hs, :] * ei_t[h:h + 1, :]


def _mlstm_sample(za_s, qc, g_s, bias_row, nrm_row, m0, n0):
    nb = za_s.shape[0]
    assert nb == LANE, "the sequence axis is transposed onto the lanes"
    return pl.pallas_call(
        _mlstm_sample_kernel,
        out_shape=[
            jax.ShapeDtypeStruct((nb, D_A), BF16),
            jax.ShapeDtypeStruct((nb, H_A), F32),
            jax.ShapeDtypeStruct((nb, H_A * DK), F32),
            jax.ShapeDtypeStruct((D_A, nb), F32),
            jax.ShapeDtypeStruct((nb, LANE), F32),
        ],
        compiler_params=pltpu.CompilerParams(vmem_limit_bytes=VMEM_LIMIT),
        name="mlstm_sample",
    )(za_s, qc, g_s, bias_row, nrm_row, m0, n0)


def _outproj_kernel(hp_ref, hs_ref, ap_ref, as_ref, yb_ref, w_ref, op_ref, os_ref, wc_ref):
    j = pl.program_id(1)

    @pl.when(pl.program_id(0) == 0)
    def _():
        wc_ref[j] = w_ref[...].astype(BF16)

    wa = wc_ref[j, :D_A, :]
    wb = wc_ref[j, D_A:, :]
    op_ref[...] = hp_ref[...] + _bdot(ap_ref[...], wa) + _bdot(yb_ref[0, :MP, :], wb)
    os_ref[0] = hs_ref[0] + _bdot(as_ref[0], wa) + _bdot(yb_ref[0, MP:, :], wb)


def _outproj(hp, hs, ap, a_s, yb_c, w3):
    tn = OUT_TN
    n_n = D_MODEL // tn
    return pl.pallas_call(
        _outproj_kernel,
        grid=(N_MIX_BLOCKS, n_n),
        in_specs=[
            pl.BlockSpec((MP, tn), lambda i, j: (i, j)),
            pl.BlockSpec((1, MS, tn), lambda i, j: (i, 0, j)),
            pl.BlockSpec((MP, D_A), lambda i, j: (i, 0)),
            pl.BlockSpec((1, MS, D_A), lambda i, j: (i, 0, 0)),
            pl.BlockSpec((1, MP + MS, D_B), lambda i, j: (i, 0, 0)),
            pl.BlockSpec((None, D_A + D_B, tn), lambda i, j: (0, 0, jnp.where(i == 0, j, n_n - 1))),
        ],
        out_specs=[
            pl.BlockSpec((MP, tn), lambda i, j: (i, j)),
            pl.BlockSpec((1, MS, tn), lambda i, j: (i, 0, j)),
        ],
        out_shape=[
            jax.ShapeDtypeStruct((N_MIX_BLOCKS * MP, D_MODEL), F32),
            jax.ShapeDtypeStruct((N_MIX_BLOCKS, MS, D_MODEL), F32),
        ],
        scratch_shapes=[pltpu.VMEM((n_n, D_A + D_B, tn), BF16)],
        compiler_params=pltpu.CompilerParams(
            dimension_semantics=("arbitrary", "arbitrary"), vmem_limit_bytes=VMEM_LIMIT),
        name="outproj",
    )(hp, hs, ap, a_s, yb_c, w3)


def kernel(x_prompt, x_sample, state_mlstm_C, state_mlstm_n, state_mlstm_m, state_conv, norm_ffn1, ffn1_gate, ffn1_up, ffn1_down, norm_mix, w_in, b_gates, conv_w, conv_b, norm_mlstm, norm_conv, w_out, norm_ffn2, ffn2_gate, ffn2_up, ffn2_down, norm_final):
    batch, seq, _ = x_prompt.shape
    nb = x_sample.shape[0]
    assert batch == N_MIX_BLOCKS and seq == MP and nb == N_MIX_BLOCKS * MS
    assert norm_ffn1.shape[0] == 1, "single-layer trunk"

    xp = x_prompt.reshape(batch * seq, D_MODEL)

    w_t = jnp.swapaxes(w_in[0], 0, 1)

    h1p, h1s, xn_c, g_p, g_s = _ffn(xp, x_sample, norm_ffn1, ffn1_gate, ffn1_up, ffn1_down, norm_mix, TF1,
                                    w_gate_t=w_t)
    za_p, za_s = _proj(xn_c, w_t)
    za_s = za_s.reshape(nb, 4 * D_A)
    yb_c, conv_p, conv_s, qc_s = _conv(xn_c, w_t, conv_w, conv_b, norm_conv,
                                       state_conv.reshape(N_MIX_BLOCKS, MS, 2, D_B), za_s, state_mlstm_C)

    bias_row = jnp.pad(b_gates.astype(F32), ((0, 0), (0, LANE - 2 * H_A)))
    ha_p, c_p, n_p, m_p = _mlstm_prompt(za_p, g_p, bias_row, norm_mlstm, batch, seq)
    ha_s, m_s, n_s, wk_s, dec_s = _mlstm_sample(
        za_s, qc_s, g_s.reshape(nb, LANE), bias_row, norm_mlstm,
        state_mlstm_m.reshape(nb, H_A), state_mlstm_n.reshape(nb, H_A * DK))

    h2p, h2s = _outproj(h1p, h1s.reshape(N_MIX_BLOCKS, MS, D_MODEL), ha_p.reshape(batch * seq, D_A),
                        ha_s.reshape(N_MIX_BLOCKS, MS, D_A), yb_c, w_out)

    yp, ys, c_s = _ffn(h2p, h2s.reshape(N_FFN_BLOCKS, TM_S, D_MODEL), norm_ffn2, ffn2_gate, ffn2_up,
                       ffn2_down, norm_final.reshape(1, D_MODEL), TF2,
                       state=(state_mlstm_C, wk_s, za_s, dec_s))

    return (
        yp.reshape(batch, seq, D_MODEL),
        ys,
        c_p,
        n_p,
        m_p[:, 0, :H_A].reshape(1, batch, H_A),
        conv_p,
        c_s,
        n_s.reshape(1, nb, H_A, DK),
        m_s.reshape(1, nb, H_A),
        conv_s.reshape(1, nb, 2, D_B),
    )
```

```python
import functools

import jax
import jax.numpy as jnp
from jax import lax
from jax.experimental import pallas as pl
from jax.experimental.pallas import tpu as pltpu

F32 = jnp.float32
BF16 = jnp.bfloat16

D_MODEL = 2048
D_A = 1024
D_B = 1024
H_A = 4
DK = 256
DV = 256
G_B = 8
D_FF = 5504
GATE_CAP = 15.0
EPS = 1e-6
GATE_ROW0 = 4 * D_A
CONV_ROW0 = 4 * D_A + 2 * H_A

LANE = 128
N_FFN_BLOCKS = 8
TM_P = 1024
TM_S = 16
TM = TM_P + TM_S
TF1 = 512
TF2 = 512
STATE_SEQS = 2
X_CHUNK = 128
N_X_CHUNKS = TM_P // X_CHUNK
X_AHEAD0 = 2
N_MIX_BLOCKS = N_FFN_BLOCKS // 2
MP = 2 * TM_P
MS = 2 * TM_S
PROJ_TN = 1024
CONV_TC = 256
OUT_TN = 512
MLSTM_CHUNK = 256
MLSTM_SEQS = 2
SAMPLE_BS = 8
VMEM_LIMIT = 62 * 1024 * 1024


def _rms(x, g):
    return x * lax.rsqrt(jnp.mean(x * x, axis=-1, keepdims=True) + EPS) * g


def _soft_cap(x):
    return GATE_CAP * jnp.tanh(x / GATE_CAP)


def _log_sigmoid(x):
    return -jax.nn.softplus(-x)


def _bdot(a, b):
    return jnp.dot(a, b, preferred_element_type=F32)


def _bdot_t(a, b):
    return lax.dot_general(a, b, (((1,), (1,)), ((), ())), preferred_element_type=F32)


def _sample_state_update(f, c_ref, wk_ref, v_ref, dec_ref, c_out):
    grp = jnp.minimum(f, TM_S // STATE_SEQS - 1)
    lane = lax.broadcasted_iota(jnp.int32, (D_A, LANE), 1)
    for u in range(STATE_SEQS):
        r = grp * STATE_SEQS + u
        seq = pl.program_id(0) * TM_S + r
        wk_col = jnp.sum(jnp.where(lane == seq, wk_ref[...], 0.0), axis=1, keepdims=True)
        v_row = v_ref[pl.ds(r, 1), :]
        dec = dec_ref[pl.ds(r, 1), :]
        for h in range(H_A):
            hs = slice(h * DK, (h + 1) * DK)
            c_out[0, u, h] = dec[:, h:h + 1] * c_ref[0, u, h] + wk_col[hs, :] * v_row[:, hs]


def _ffn_kernel(x_hbm, xs_ref, g_ref, wg_ref, wu_ref, wd_ref, g2_ref, *rest,
                tf, n_f, last_valid, final_norm):
    if final_norm:
        c_ref, wk_ref, v_ref, dec_ref, out_hbm, os_ref, c_out, xn_ref, acc_ref, x_sem, o_sem = rest
    else:
        wgate_ref, out_hbm, os_ref, nxt_ref, gp_ref, gs_ref, xn_ref, acc_ref, x_sem, o_sem = rest
    i = pl.program_id(0)
    f = pl.program_id(1)
    slot = lax.rem(i, 2)
    op_ref = acc_ref.at[slot]

    def x_copy(blk, s, c):
        dst0 = pl.multiple_of(c * X_CHUNK, X_CHUNK)
        src0 = pl.multiple_of(blk * TM_P + c * X_CHUNK, X_CHUNK)
        return pltpu.make_async_copy(x_hbm.at[pl.ds(src0, X_CHUNK), :],
                                     acc_ref.at[s, pl.ds(dst0, X_CHUNK), :], x_sem.at[c])

    def out_copy(blk, s):
        row0 = pl.multiple_of(blk * TM_P, TM_P)
        return pltpu.make_async_copy(acc_ref.at[s], out_hbm.at[pl.ds(row0, TM_P), :], o_sem.at[s])

    def first_step_dma():
        @pl.when(i == 0)
        def _():
            for c in range(N_X_CHUNKS):
                x_copy(0, 0, c).start()

        for c in range(N_X_CHUNKS):
            x_copy(i, slot, c).wait()

    def middle_step_dma():
        @pl.when(jnp.logical_and(i > 0, f == X_AHEAD0 - 1))
        def _():
            out_copy(i - 1, 1 - slot).wait()

        @pl.when(jnp.logical_and(jnp.logical_and(f >= X_AHEAD0, f < X_AHEAD0 + N_X_CHUNKS),
                                 i < N_FFN_BLOCKS - 1))
        def _():
            x_copy(i + 1, 1 - slot, f - X_AHEAD0).start()

    xs_idx = (0,) if final_norm else (slice(None), 0)
    os_idx = (slice(None), 0) if final_norm else (0,)

    def step(valid, first, with_state):
        if first:
            g = g_ref[...]
            xn_ref[:TM_P, :] = _rms(op_ref[...], g).astype(BF16)
            xn_ref[TM_P:, :] = _rms(xs_ref[xs_idx], g).astype(BF16)
        xn = xn_ref[...]
        a = _bdot(xn, wg_ref[:, :valid].astype(BF16))
        b = _bdot(xn, wu_ref[:, :valid].astype(BF16))
        hid = (a * jax.nn.sigmoid(a) * b * 0.5).astype(BF16)
        r = _bdot(hid, wd_ref[:valid, :].astype(BF16))
        op_ref[...] += r[:TM_P]
        if first:
            os_ref[os_idx] = xs_ref[xs_idx] + r[TM_P:]
        else:
            os_ref[os_idx] += r[TM_P:]
        if with_state:
            _sample_state_update(f, c_ref, wk_ref, v_ref, dec_ref, c_out)

    assert last_valid < tf and (not final_norm or (n_f - 1) * STATE_SEQS >= TM_S), \
        "state updates ride the full-width steps"

    assert X_AHEAD0 >= 2 and X_AHEAD0 + N_X_CHUNKS <= n_f - 1, "hand DMAs are issued from the middle steps"

    @pl.when(f == 0)
    def _():
        first_step_dma()
        step(tf, True, final_norm)

    @pl.when(jnp.logical_and(f > 0, f < n_f - 1))
    def _():
        middle_step_dma()
        step(tf, False, final_norm)

    @pl.when(f == n_f - 1)
    def _():
        step(last_valid, False, False)
        g2 = g2_ref[...]
        if final_norm:
            op_ref[...] = _rms(op_ref[...], g2)
            os_ref[os_idx] = _rms(os_ref[os_idx], g2)
        else:
            nx_p = _rms(op_ref[...], g2).astype(BF16)
            nx_s = _rms(os_ref[os_idx], g2).astype(BF16)
            nxt_ref[0, :TM_P, :] = nx_p
            nxt_ref[0, TM_P:, :] = nx_s
            wgt = wgate_ref[...].astype(BF16)
            half = TM_P // 2
            gp_ref[:half, :] = _bdot_t(nx_p[:half], wgt)
            gp_ref[half:, :] = _bdot_t(nx_p[half:], wgt)
            gs_ref[0] = _bdot_t(nx_s, wgt)

        out_copy(i, slot).start()

        @pl.when(i == N_FFN_BLOCKS - 1)
        def _():
            out_copy(i, slot).wait()


def _ffn(xp, xs, g, wg, wu, wd, g2, tf, state=None, w_gate_t=None):
    final_norm = state is not None
    n_f = pl.cdiv(D_FF, tf)
    last_valid = D_FF - (n_f - 1) * tf
    kern = functools.partial(_ffn_kernel, tf=tf, n_f=n_f, last_valid=last_valid, final_norm=final_norm)
    assert X_AHEAD0 + N_X_CHUNKS <= n_f and N_X_CHUNKS * X_CHUNK == TM_P
    blocked_s = pl.BlockSpec((1, TM_S, D_MODEL), lambda i, f: (i, 0, 0))
    native_s = pl.BlockSpec((TM_S, 1, D_MODEL), lambda i, f: (i, 0, 0))
    in_specs = [
        pl.BlockSpec(memory_space=pl.ANY),
        blocked_s if final_norm else native_s,
        pl.BlockSpec((1, D_MODEL), lambda i, f: (0, 0)),
        pl.BlockSpec((None, D_MODEL, tf), lambda i, f: (0, 0, f)),
        pl.BlockSpec((None, D_MODEL, tf), lambda i, f: (0, 0, f)),
        pl.BlockSpec((None, tf, D_MODEL), lambda i, f: (0, f, 0)),
        pl.BlockSpec((1, D_MODEL), lambda i, f: (0, 0)),
    ]
    out_specs = [
        pl.BlockSpec(memory_space=pl.ANY),
        native_s if final_norm else blocked_s,
    ]
    out_shape = [
        jax.ShapeDtypeStruct((N_FFN_BLOCKS * TM_P, D_MODEL), F32),
        jax.ShapeDtypeStruct((N_FFN_BLOCKS * TM_S, 1, D_MODEL) if final_norm
                             else (N_FFN_BLOCKS, TM_S, D_MODEL), F32),
    ]
    args = [xp, xs, g, wg, wu, wd, g2]
    if final_norm:
        c0, wk, za_s, dec = state
        groups = TM_S // STATE_SEQS
        grp = lambda i, f: i * groups + jnp.minimum(f, groups - 1)
        cspec = pl.BlockSpec((1, STATE_SEQS, H_A, DK, DV), lambda i, f: (0, grp(i, f), 0, 0, 0))
        in_specs += [
            cspec,
            pl.BlockSpec((D_A, LANE), lambda i, f: (0, 0)),
            pl.BlockSpec((TM_S, D_A), lambda i, f: (i, 2)),
            pl.BlockSpec((TM_S, LANE), lambda i, f: (i, 0)),
        ]
        out_specs.append(cspec)
        out_shape.append(jax.ShapeDtypeStruct(c0.shape, F32))
        args += [c0, wk, za_s, dec]
    else:
        in_specs.append(pl.BlockSpec((LANE, D_MODEL), lambda i, f: (GATE_ROW0 // LANE, 0)))
        args.append(w_gate_t)
        out_specs += [
            pl.BlockSpec((1, TM, D_MODEL), lambda i, f: (i, 0, 0)),
            pl.BlockSpec((TM_P, LANE), lambda i, f: (i, 0)),
            pl.BlockSpec((1, TM_S, LANE), lambda i, f: (i, 0, 0)),
        ]
        out_shape += [
            jax.ShapeDtypeStruct((N_FFN_BLOCKS, TM, D_MODEL), BF16),
            jax.ShapeDtypeStruct((N_FFN_BLOCKS * TM_P, LANE), F32),
            jax.ShapeDtypeStruct((N_FFN_BLOCKS, TM_S, LANE), F32),
        ]
    return pl.pallas_call(
        kern,
        grid=(N_FFN_BLOCKS, n_f),
        in_specs=in_specs,
        out_specs=out_specs,
        out_shape=out_shape,
        scratch_shapes=[pltpu.VMEM((TM, D_MODEL), BF16),
                        pltpu.VMEM((2, TM_P, D_MODEL), F32),
                        pltpu.SemaphoreType.DMA((N_X_CHUNKS,)),
                        pltpu.SemaphoreType.DMA((2,))],
        compiler_params=pltpu.CompilerParams(
            dimension_semantics=("arbitrary", "arbitrary"), vmem_limit_bytes=VMEM_LIMIT),
        name="ffn_final" if final_norm else "ffn",
    )(*args)


def _split_rows(z0, z1, p_ref, s_ref):
    p_ref[:TM_P, :] = z0[:TM_P]
    p_ref[TM_P:, :] = z1[:TM_P]
    s_ref[0, :TM_S, :] = z0[TM_P:]
    s_ref[0, TM_S:, :] = z1[TM_P:]


def _proj_kernel(xn_ref, w_ref, zp_ref, zs_ref):
    w = w_ref[...].astype(BF16)
    _split_rows(_bdot_t(xn_ref[0], w), _bdot_t(xn_ref[1], w), zp_ref, zs_ref)


def _proj(xn_c, w_t):
    n_n = (4 * D_A) // PROJ_TN
    return pl.pallas_call(
        _proj_kernel,
        grid=(N_MIX_BLOCKS, n_n),
        in_specs=[
            pl.BlockSpec((2, TM, D_MODEL), lambda i, j: (i, 0, 0)),
            pl.BlockSpec((PROJ_TN, D_MODEL), lambda i, j: (j, 0)),
        ],
        out_specs=[
            pl.BlockSpec((MP, PROJ_TN), lambda i, j: (i, j)),
            pl.BlockSpec((1, MS, PROJ_TN), lambda i, j: (i, 0, j)),
        ],
        out_shape=[
            jax.ShapeDtypeStruct((N_MIX_BLOCKS * MP, 4 * D_A), F32),
            jax.ShapeDtypeStruct((N_MIX_BLOCKS, MS, 4 * D_A), F32),
        ],
        compiler_params=pltpu.CompilerParams(
            dimension_semantics=("parallel", "arbitrary"), vmem_limit_bytes=VMEM_LIMIT),
        name="proj_qkvo",
    )(xn_c, w_t)


def _group_norm(yb, nrm):
    gw = D_B // G_B
    parts = []
    for g in range(yb.shape[1] // gw):
        seg = yb[:, g * gw:(g + 1) * gw]
        parts.append(seg * lax.rsqrt(jnp.mean(seg * seg, axis=-1, keepdims=True) + EPS))
    return jnp.concatenate(parts, axis=1) * nrm


def _conv_kernel(xn_ref, wgb_ref, wgc_ref, wxc_ref, cw_ref, cb_ref, nrm_ref, buf_ref, q_ref, c_ref,
                 yb_ref, cp_ref, cs_ref, qc_ref):
    q_t = jnp.concatenate([q_ref[...], jnp.zeros((LANE - SAMPLE_BS, D_A), F32)], axis=0).T
    for h in range(H_A):
        hs = slice(h * DK, (h + 1) * DK)
        for j in range(SAMPLE_BS):
            qc_ref[j:j + 1, hs] = jnp.sum(q_t[hs, j:j + 1] * c_ref[0, j, h], axis=0, keepdims=True)

    wgb = wgb_ref[...].astype(BF16)
    wgc = wgc_ref[...].astype(BF16)
    wxc = wxc_ref[...].astype(BF16)
    cw0 = cw_ref[0:1, :]
    cw1 = cw_ref[1:2, :]
    cw2 = cw_ref[2:3, :]
    cb = cb_ref[...]
    nrm = nrm_ref[...]
    row = lax.broadcasted_iota(jnp.int32, (TM_P, wgb.shape[0]), 0)
    zero_row = jnp.zeros((1, wgb.shape[0]), F32)
    um2, um1 = zero_row, zero_row

    for t in range(2):
        xn = xn_ref[t]
        u = _bdot_t(xn, wgc) * _bdot_t(xn, wxc)
        gb = _bdot_t(xn, wgb)

        up = u[:TM_P]
        u1 = jnp.where(row < 1, um1, pltpu.roll(up, 1, 0))
        u2 = jnp.where(row < 1, um2, jnp.where(row < 2, um1, pltpu.roll(up, 2, 0)))
        yc = cw0 * u2 + cw1 * u1 + cw2 * up + cb
        yb_ref[0, t * TM_P:(t + 1) * TM_P, :] = _group_norm(gb[:TM_P] * yc, nrm).astype(yb_ref.dtype)
        um2, um1 = up[TM_P - 2:TM_P - 1, :], up[TM_P - 1:TM_P, :]

        ss = slice(t * TM_S, (t + 1) * TM_S)
        us = u[TM_P:]
        b0 = buf_ref[0, ss, 0, :]
        b1 = buf_ref[0, ss, 1, :]
        ycs = cw0 * b0 + cw1 * b1 + cw2 * us + cb
        yb_ref[0, MP + t * TM_S:MP + (t + 1) * TM_S, :] = _group_norm(gb[TM_P:] * ycs, nrm).astype(yb_ref.dtype)
        cs_ref[0, ss, 0, :] = b1
        cs_ref[0, ss, 1, :] = us

    cp_ref[0, 0, 0:1, :] = um2
    cp_ref[0, 0, 1:2, :] = um1


def _conv(xn_c, w_t, cw, cb_row, nrm_row, buf, za_s, c0):
    tc = CONV_TC
    n_c = D_B // tc
    assert N_MIX_BLOCKS * n_c * SAMPLE_BS == za_s.shape[0], "one sample group per grid step"
    wspec = lambda base: pl.BlockSpec(
        (pl.Element(tc), pl.Element(D_MODEL)), lambda i, c: (pl.multiple_of(base + c * tc, 8), 0))
    grp = lambda i, c: i * n_c + c
    return pl.pallas_call(
        _conv_kernel,
        grid=(N_MIX_BLOCKS, n_c),
        in_specs=[
            pl.BlockSpec((2, TM, D_MODEL), lambda i, c: (i, 0, 0)),
            wspec(CONV_ROW0),
            wspec(CONV_ROW0 + D_B),
            wspec(CONV_ROW0 + 2 * D_B),
            pl.BlockSpec((None, 3, tc), lambda i, c: (0, 0, c)),
            pl.BlockSpec((1, tc), lambda i, c: (0, c)),
            pl.BlockSpec((1, tc), lambda i, c: (0, c)),
            pl.BlockSpec((1, MS, 2, tc), lambda i, c: (i, 0, 0, c)),
            pl.BlockSpec((SAMPLE_BS, D_A), lambda i, c: (grp(i, c), 0)),
            pl.BlockSpec((1, SAMPLE_BS, H_A, DK, DV), lambda i, c: (0, grp(i, c), 0, 0, 0)),
        ],
        out_specs=[
            pl.BlockSpec((1, MP + MS, tc), lambda i, c: (i, 0, c)),
            pl.BlockSpec((1, 1, 2, tc), lambda i, c: (0, i, 0, c)),
            pl.BlockSpec((1, MS, 2, tc), lambda i, c: (i, 0, 0, c)),
            pl.BlockSpec((SAMPLE_BS, D_A), lambda i, c: (grp(i, c), 0)),
        ],
        out_shape=[
            jax.ShapeDtypeStruct((N_MIX_BLOCKS, MP + MS, D_B), BF16),
            jax.ShapeDtypeStruct((1, N_MIX_BLOCKS, 2, D_B), F32),
            jax.ShapeDtypeStruct((N_MIX_BLOCKS, MS, 2, D_B), F32),
            jax.ShapeDtypeStruct((za_s.shape[0], D_A), F32),
        ],
        compiler_params=pltpu.CompilerParams(
            dimension_semantics=("parallel", "arbitrary"), vmem_limit_bytes=VMEM_LIMIT),
        name="proj_conv",
    )(xn_c, w_t, w_t, w_t, cw, cb_row, nrm_row, buf, za_s, c0)


def _mlstm_prompt_kernel(z_ref, g_ref, bias_ref, nrm_ref,
                         ha_ref, c_ref, n_ref, m_ref, cx_s, m_s, *, n_chunks):
    L = MLSTM_CHUNK
    c = pl.program_id(1)

    @pl.when(c == 0)
    def _():
        cx_s[...] = jnp.zeros_like(cx_s)
        m_s[...] = jnp.zeros_like(m_s)

    row = lax.broadcasted_iota(jnp.int32, (L, L), 0)
    col = lax.broadcasted_iota(jnp.int32, (L, L), 1)
    causal = row >= col
    tri = causal.astype(BF16)

    for b in range(MLSTM_SEQS):
        _mlstm_prompt_seq(b, causal, tri, z_ref, g_ref, bias_ref, nrm_ref, ha_ref, cx_s, m_s)

    @pl.when(c == n_chunks - 1)
    def _():
        for b in range(MLSTM_SEQS):
            m_ref[b] = m_s[b]
            for h in range(H_A):
                cx = cx_s[b * H_A + h]
                c_ref[0, b, h] = cx[:, :DV]
                n_ref[0, b, h:h + 1, :] = cx[:, DV:].T[h:h + 1, :]


def _prefix_max_rows(x):
    rows = x.shape[0]
    row = lax.broadcasted_iota(jnp.int32, x.shape, 0)
    k = 1
    while k < rows:
        x = jnp.where(row >= k, jnp.maximum(x, pltpu.roll(x, k, 0)), x)
        k *= 2
    return x


def _mlstm_prompt_seq(b, causal, tri, z_ref, g_ref, bias_ref, nrm_ref, ha_ref, cx_s, m_s):
    L = MLSTM_CHUNK
    lane = lax.broadcasted_iota(jnp.int32, (L, LANE), 1)
    lane_dv = lax.broadcasted_iota(jnp.int32, (DV, LANE), 1)
    capped = _soft_cap(g_ref[b] + bias_ref[...])
    lf = _log_sigmoid(capped)
    hi = lf.astype(BF16)
    r1 = lf - hi.astype(F32)
    mid = r1.astype(BF16)
    lo = (r1 - mid.astype(F32)).astype(BF16)
    parts = _bdot(tri, jnp.concatenate([hi, mid, lo], axis=1))
    bc = parts[:, :LANE] + parts[:, LANE:2 * LANE] + parts[:, 2 * LANE:]
    bh = pltpu.roll(bc, LANE - H_A, 1)
    a = capped - bh
    a_t = a.T
    m_prev = m_s[b]
    big_m = jnp.maximum(_prefix_max_rows(a), m_prev[0:1, :])
    sc_all = jnp.exp(m_prev[0:1, :] - big_m)
    emt_all = jnp.exp(-(bh + big_m))
    m_last = big_m[L - 1:L, :]
    w_all = jnp.exp(a - m_last)
    decay_all = sc_all[L - 1:L, :]
    m_s[b] = jnp.broadcast_to(bh[L - 1:L, :] + m_last, (8, LANE))

    nums = []
    nd_tail = jnp.zeros((L, LANE), F32)
    sq_tail = jnp.zeros((L, LANE), F32)
    for h in range(H_A):
        hs = slice(h * DK, (h + 1) * DK)
        onehot = (lane == h).astype(BF16)
        qb = z_ref[b, :, hs].astype(BF16)
        kf = z_ref[b, :, slice(D_A + h * DK, D_A + (h + 1) * DK)] * (DK ** -0.5)
        kb = kf.astype(BF16)
        v_h = z_ref[b, :, slice(2 * D_A + h * DK, 2 * D_A + (h + 1) * DK)]
        vx = jnp.concatenate([v_h.astype(BF16), onehot], axis=1)
        cx_prev = cx_s[b * H_A + h]

        p = jnp.exp(jnp.where(causal, a_t[h:h + 1, :] - big_m[:, h:h + 1], -jnp.inf))
        s = _bdot_t(qb, kb) * p
        nd = sc_all[:, h:h + 1] * _bdot(qb, cx_prev.astype(BF16)) + _bdot(s.astype(BF16), vx)
        num = nd[:, :DV]
        nums.append(num)
        nd_tail = nd_tail + nd[:, DV:]
        sq_tail = sq_tail + _bdot((num * num).astype(BF16), (lane_dv == h).astype(BF16))

        wk = (w_all[:, h:h + 1] * kf).astype(BF16)
        cx_s[b * H_A + h] = decay_all[:, h:h + 1] * cx_prev + lax.dot_general(
            wk, vx, (((0,), (0,)), ((), ())), preferred_element_type=F32)

    inv = 1.0 / jnp.maximum(jnp.abs(nd_tail), emt_all)
    ms = inv * inv * sq_tail * (1.0 / DV)
    scale_all = inv * lax.rsqrt(ms + EPS)
    for h in range(H_A):
        hs = slice(h * DK, (h + 1) * DK)
        hn = nums[h] * scale_all[:, h:h + 1]
        o_h = z_ref[b, :, slice(3 * D_A + h * DK, 3 * D_A + (h + 1) * DK)]
        ha_ref[b, :, hs] = (hn * nrm_ref[:, hs] * jax.nn.sigmoid(o_h)).astype(ha_ref.dtype)


def _mlstm_prompt(za_p, g_p, bias_row, nrm_row, batch, seq):
    L = MLSTM_CHUNK
    nsq = MLSTM_SEQS
    n_chunks = seq // L
    kern = functools.partial(_mlstm_prompt_kernel, n_chunks=n_chunks)
    za3 = za_p.reshape(batch, seq, 4 * D_A)
    g3 = g_p.reshape(batch, seq, LANE)
    return pl.pallas_call(
        kern,
        grid=(batch // nsq, n_chunks),
        in_specs=[
            pl.BlockSpec((nsq, L, 4 * D_A), lambda b, c: (b, c, 0)),
            pl.BlockSpec((nsq, L, LANE), lambda b, c: (b, c, 0)),
            pl.BlockSpec((1, LANE), lambda b, c: (0, 0)),
            pl.BlockSpec((1, D_A), lambda b, c: (0, 0)),
        ],
        out_specs=[
            pl.BlockSpec((nsq, L, D_A), lambda b, c: (b, c, 0)),
            pl.BlockSpec((1, nsq, H_A, DK, DV), lambda b, c: (0, b, 0, 0, 0)),
            pl.BlockSpec((1, nsq, H_A, DK), lambda b, c: (0, b, 0, 0)),
            pl.BlockSpec((nsq, 8, LANE), lambda b, c: (b, 0, 0)),
        ],
        out_shape=[
            jax.ShapeDtypeStruct((batch, seq, D_A), BF16),
            jax.ShapeDtypeStruct((1, batch, H_A, DK, DV), F32),
            jax.ShapeDtypeStruct((1, batch, H_A, DK), F32),
            jax.ShapeDtypeStruct((batch, 8, LANE), F32),
        ],
        scratch_shapes=[pltpu.VMEM((nsq * H_A, DK, DV + LANE), F32),
                        pltpu.VMEM((nsq, 8, LANE), F32)],
        compiler_params=pltpu.CompilerParams(
            dimension_semantics=("parallel", "arbitrary"), vmem_limit_bytes=VMEM_LIMIT),
        name="mlstm_prompt",
    )(za3, g3, bias_row, nrm_row)


def _expand_heads(x, width):
    rows = x.shape[0]
    return jnp.concatenate([jnp.broadcast_to(x[:, h:h + 1], (rows, width)) for h in range(H_A)], axis=1)


def _head_sums(x):
    return jnp.concatenate(
        [jnp.sum(x[:, h * DK:(h + 1) * DK], axis=-1, keepdims=True) for h in range(H_A)], axis=1)


def _mlstm_sample_kernel(za_ref, qc_ref, g_ref, bias_ref, nrm_ref, m_ref, n_ref,
                         ha_ref, m_out, n_out, wk_out, dec_out):
    nb = za_ref.shape[0]
    capped = _soft_cap(g_ref[...] + bias_ref[...])
    logi = capped[:, 0:H_A]
    logf = _log_sigmoid(capped)[:, H_A:2 * H_A]
    m_prev = m_ref[0]
    m_inter = logf + m_prev
    m_t = jnp.maximum(m_inter, logi)
    sc = jnp.exp(m_inter - m_t)
    ei = jnp.exp(logi - m_t)
    emt = jnp.exp(-m_t)

    q = za_ref[:, 0:D_A]
    k = za_ref[:, D_A:2 * D_A] * (DK ** -0.5)
    v = za_ref[:, 2 * D_A:3 * D_A]
    o = za_ref[:, 3 * D_A:4 * D_A]
    n_prev = jnp.concatenate([n_ref[0, :, h, :] for h in range(H_A)], axis=1)

    s = _head_sums(q * k) * ei
    den = sc * _head_sums(q * n_prev) + s
    denom = jnp.maximum(jnp.abs(den), emt)
    num = _expand_heads(sc, DV) * qc_ref[...] + _expand_heads(s, DV) * v
    hh = num / _expand_heads(denom, DV)
    ms = _head_sums(hh * hh) * (1.0 / DV)
    hn = hh * lax.rsqrt(_expand_heads(ms, DV) + EPS)
    ha_ref[...] = (hn * nrm_ref[...] * jax.nn.sigmoid(o)).astype(ha_ref.dtype)
    n_new = _expand_heads(sc, DK) * n_prev + _expand_heads(ei, DK) * k
    for h in range(H_A):
        n_out[0, :, h, :] = n_new[:, h * DK:(h + 1) * DK]
    m_out[0] = m_t
    pad = jnp.zeros((nb, LANE - H_A), F32)
    dec_out[...] = jnp.concatenate([sc, pad], axis=1)

    k_t = k.T
    ei_t = jnp.concatenate([ei, pad], axis=1).T
    for h in range(H_A):
        hs = slice(h * DK, (h + 1) * DK)
        wk_out[hs, :] = k_t[hs, :] * ei_t[h:h + 1, :]


def _mlstm_sample(za_s, qc, g_s, bias_row, nrm_row, m0, n0):
    nb = za_s.shape[0]
    assert nb == LANE, "the sequence axis is transposed onto the lanes"
    return pl.pallas_call(
        _mlstm_sample_kernel,
        out_shape=[
            jax.ShapeDtypeStruct((nb, D_A), BF16),
            jax.ShapeDtypeStruct((1, nb, H_A), F32),
            jax.ShapeDtypeStruct((1, nb, H_A, DK), F32),
            jax.ShapeDtypeStruct((D_A, nb), F32),
            jax.ShapeDtypeStruct((nb, LANE), F32),
        ],
        compiler_params=pltpu.CompilerParams(vmem_limit_bytes=VMEM_LIMIT),
        name="mlstm_sample",
    )(za_s, qc, g_s, bias_row, nrm_row, m0, n0)


def _outproj_kernel(hp_ref, hs_ref, ap_ref, as_ref, yb_ref, w_ref, op_ref, os_ref, wc_ref):
    j = pl.program_id(1)

    @pl.when(pl.program_id(0) == 0)
    def _():
        wc_ref[j] = w_ref[...].astype(BF16)

    wa = wc_ref[j, :D_A, :]
    wb = wc_ref[j, D_A:, :]
    op_ref[...] = hp_ref[...] + _bdot(ap_ref[...], wa) + _bdot(yb_ref[0, :MP, :], wb)
    os_ref[0] = hs_ref[0] + _bdot(as_ref[0], wa) + _bdot(yb_ref[0, MP:, :], wb)


def _outproj(hp, hs, ap, a_s, yb_c, w3):
    tn = OUT_TN
    n_n = D_MODEL // tn
    return pl.pallas_call(
        _outproj_kernel,
        grid=(N_MIX_BLOCKS, n_n),
        in_specs=[
            pl.BlockSpec((MP, tn), lambda i, j: (i, j)),
            pl.BlockSpec((1, MS, tn), lambda i, j: (i, 0, j)),
            pl.BlockSpec((MP, D_A), lambda i, j: (i, 0)),
            pl.BlockSpec((1, MS, D_A), lambda i, j: (i, 0, 0)),
            pl.BlockSpec((1, MP + MS, D_B), lambda i, j: (i, 0, 0)),
            pl.BlockSpec((None, D_A + D_B, tn), lambda i, j: (0, 0, jnp.where(i == 0, j, n_n - 1))),
        ],
        out_specs=[
            pl.BlockSpec((MP, tn), lambda i, j: (i, j)),
            pl.BlockSpec((1, MS, tn), lambda i, j: (i, 0, j)),
        ],
        out_shape=[
            jax.ShapeDtypeStruct((N_MIX_BLOCKS * MP, D_MODEL), F32),
            jax.ShapeDtypeStruct((N_MIX_BLOCKS, MS, D_MODEL), F32),
        ],
        scratch_shapes=[pltpu.VMEM((n_n, D_A + D_B, tn), BF16)],
        compiler_params=pltpu.CompilerParams(
            dimension_semantics=("arbitrary", "arbitrary"), vmem_limit_bytes=VMEM_LIMIT),
        name="outproj",
    )(hp, hs, ap, a_s, yb_c, w3)


def kernel(x_prompt, x_sample, state_mlstm_C, state_mlstm_n, state_mlstm_m, state_conv, norm_ffn1, ffn1_gate, ffn1_up, ffn1_down, norm_mix, w_in, b_gates, conv_w, conv_b, norm_mlstm, norm_conv, w_out, norm_ffn2, ffn2_gate, ffn2_up, ffn2_down, norm_final):
    batch, seq, _ = x_prompt.shape
    nb = x_sample.shape[0]
    assert batch == N_MIX_BLOCKS and seq == MP and nb == N_MIX_BLOCKS * MS
    assert norm_ffn1.shape[0] == 1, "single-layer trunk"

    xp = x_prompt.reshape(batch * seq, D_MODEL)

    w_t = jnp.swapaxes(w_in[0], 0, 1)

    h1p, h1s, xn_c, g_p, g_s = _ffn(xp, x_sample, norm_ffn1, ffn1_gate, ffn1_up, ffn1_down, norm_mix, TF1,
                                    w_gate_t=w_t)
    za_p, za_s = _proj(xn_c, w_t)
    za_s = za_s.reshape(nb, 4 * D_A)
    yb_c, conv_p, conv_s, qc_s = _conv(xn_c, w_t, conv_w, conv_b, norm_conv,
                                       state_conv.reshape(N_MIX_BLOCKS, MS, 2, D_B), za_s, state_mlstm_C)

    bias_row = jnp.pad(b_gates.astype(F32), ((0, 0), (0, LANE - 2 * H_A)))
    ha_p, c_p, n_p, m_p = _mlstm_prompt(za_p, g_p, bias_row, norm_mlstm, batch, seq)
    ha_s, m_s, n_s, wk_s, dec_s = _mlstm_sample(
        za_s, qc_s, g_s.reshape(nb, LANE), bias_row, norm_mlstm,
        state_mlstm_m, state_mlstm_n)

    h2p, h2s = _outproj(h1p, h1s.reshape(N_MIX_BLOCKS, MS, D_MODEL), ha_p.reshape(batch * seq, D_A),
                        ha_s.reshape(N_MIX_BLOCKS, MS, D_A), yb_c, w_out)

    yp, ys, c_s = _ffn(h2p, h2s.reshape(N_FFN_BLOCKS, TM_S, D_MODEL), norm_ffn2, ffn2_gate, ffn2_up,
                       ffn2_down, norm_final.reshape(1, D_MODEL), TF2,
                       state=(state_mlstm_C, wk_s, za_s, dec_s))

    return (
        yp.reshape(batch, seq, D_MODEL),
        ys,
        c_p,
        n_p,
        m_p[:, 0, :H_A].reshape(1, batch, H_A),
        conv_p,
        c_s,
        n_s,
        m_s,
        conv_s.reshape(1, nb, 2, D_B),
    )
```
